```python
import jax, jax.numpy as jnp
from jax import lax
import numpy as np

D_MODEL = 1024
BATCH = 16
SEQ = 256
DEPTH = 2
DEC_BATCH = 4
DEC_SEQ = 1024
PAST_LEN = 512

GRID_W = 64
D_FF = 2816
FOUR_GROUPS = 4
FOUR_GROUP_DIM = 64
D_FOUR = FOUR_GROUPS * FOUR_GROUP_DIM
N_HEADS = 8
QK_NOPE = 64
QK_ROPE = 32
V_DIM = 64
Q_RANK = 384
KV_RANK = 256
N_BRANCH = 2
IN_W = D_FOUR + Q_RANK + KV_RANK + QK_ROPE + N_BRANCH * D_MODEL
N_MOD = 9
ROPE_BASE = 10000.0
EPS = 1e-6
Q_BLOCK = 128

kernel_name = "hybrid_fnet_mla_macaron_step"


def _rmsnorm(x, g):
    xf = x.astype(jnp.float32)
    y = xf * lax.rsqrt(jnp.mean(xf * xf, axis=-1, keepdims=True) + EPS)
    return (y * g.astype(jnp.float32)).astype(x.dtype)


def _swiglu(h, wg, wu, wd):
    return (jax.nn.silu(h @ wg) * (h @ wu)) @ wd


def _axial_tables(n_tok):
    rows = n_tok // GRID_W
    row = jnp.repeat(jnp.arange(rows), GRID_W).astype(jnp.float32)
    col = jnp.tile(jnp.arange(GRID_W), rows).astype(jnp.float32)
    axis_dim = QK_ROPE // 2
    inv = ROPE_BASE ** (-jnp.arange(0, axis_dim, 2, dtype=jnp.float32) / axis_dim)
    ar = row[:, None] * inv
    ac = col[:, None] * inv
    return (jnp.cos(ar), jnp.sin(ar), jnp.cos(ac), jnp.sin(ac))


def _rot_half(x, cos, sin):
    m = x.shape[-1] // 2
    x1, x2 = x[..., :m], x[..., m:]
    return jnp.concatenate([x1 * cos - x2 * sin, x2 * cos + x1 * sin], axis=-1)


def _axial_rope(x, tables):
    cr, sr, cc, sc = tables
    if x.ndim == 4:
        cr, sr, cc, sc = cr[:, None], sr[:, None], cc[:, None], sc[:, None]
    xf = x.astype(jnp.float32)
    half = QK_ROPE // 2
    out = jnp.concatenate([_rot_half(xf[..., :half], cr, sr), _rot_half(xf[..., half:], cc, sc)], axis=-1)
    return out.astype(x.dtype)


def _attend(q, k, v):
    B, N, H, Dh = q.shape
    scale = Dh ** -0.5
    nb = N // Q_BLOCK
    qb = q.reshape(B, nb, Q_BLOCK, H, Dh).swapaxes(0, 1)

    def one(qi):
        s = jnp.einsum("bqhd,bkhd->bhqk", qi, k).astype(jnp.float32) * scale
        p = jax.nn.softmax(s, axis=-1).astype(v.dtype)
        return jnp.einsum("bhqk,bkhd->bqhd", p, v)

    o = lax.map(one, qb)
    return o.swapaxes(0, 1).reshape(B, N, H, v.shape[-1])


def _mixer(h, p, tables, ctx_ckv, ctx_krope):
    B, N, _ = h.shape
    proj = h @ p["w_in"]
    cuts = [D_FOUR, D_FOUR + Q_RANK, D_FOUR + Q_RANK + KV_RANK, D_FOUR + Q_RANK + KV_RANK + QK_ROPE,
            D_FOUR + Q_RANK + KV_RANK + QK_ROPE + D_MODEL]
    f_in, q_lat, ckv, krope, g_f, g_a = jnp.split(proj, cuts, axis=-1)

    fg = f_in.reshape(B, N, FOUR_GROUPS, FOUR_GROUP_DIM).astype(jnp.float32)
    fr = jnp.real(jnp.fft.fft2(fg, axes=(1, 3), norm="ortho")).astype(h.dtype).reshape(B, N, D_FOUR)
    a_out = fr @ p["w_four"]

    q = (_rmsnorm(q_lat, p["q_norm"]) @ p["w_uq"]).reshape(B, N, N_HEADS, QK_NOPE + QK_ROPE)
    q_nope, q_rope = q[..., :QK_NOPE], q[..., QK_NOPE:]
    ckv_n = _rmsnorm(ckv, p["kv_norm"])
    if tables is not None:
        q_rope = _axial_rope(q_rope, tables)
        k_rope_own = _axial_rope(krope, tables)
    else:
        k_rope_own = krope
    key_ckv, key_rope = ckv_n, k_rope_own
    if ctx_ckv is not None:
        key_ckv = jnp.concatenate([ckv_n, ctx_ckv], axis=1)
        key_rope = jnp.concatenate([k_rope_own, ctx_krope], axis=1)
    M = key_ckv.shape[1]
    kv = (key_ckv @ p["w_ukv"]).reshape(B, M, N_HEADS, QK_NOPE + V_DIM)
    k_nope, v = kv[..., :QK_NOPE], kv[..., QK_NOPE:]
    q_full = jnp.concatenate([q_nope, q_rope], axis=-1)
    k_full = jnp.concatenate([k_nope, jnp.broadcast_to(key_rope[:, :, None, :], (B, M, N_HEADS, QK_ROPE))], axis=-1)
    o = _attend(q_full, k_full, v).reshape(B, N, N_HEADS * V_DIM) @ p["w_attn_proj"]

    merged = jax.nn.sigmoid(g_f) * a_out + jax.nn.sigmoid(g_a) * o
    return merged @ p["w_out"], ckv_n, krope


def _layer(x, mod, p, tables, ctx_ckv, ctx_krope):
    sh1, sc1, g1, sh2, sc2, g2, sh3, sc3, g3 = jnp.split(mod, N_MOD, axis=-1)
    h = _rmsnorm(x, p["norm_ffn1"]) * (1 + sc1) + sh1
    x = x + 0.5 * g1 * _swiglu(h, p["w_ffn1_gate"], p["w_ffn1_up"], p["w_ffn1_down"])
    h = _rmsnorm(x, p["norm_mix"]) * (1 + sc2) + sh2
    m, ckv_n, krope = _mixer(h, p, tables, ctx_ckv, ctx_krope)
    x = x + g2 * m
    h = _rmsnorm(x, p["norm_ffn2"]) * (1 + sc3) + sh3
    x = x + 0.5 * g3 * _swiglu(h, p["w_ffn2_gate"], p["w_ffn2_up"], p["w_ffn2_down"])
    return x, ckv_n, krope


def setup_inputs(seed: int = 0) -> dict:
    key = jax.random.key(seed)
    ks = iter(jax.random.split(key, 40))

    def nrm(shape, scale=1.0):
        return jax.random.normal(next(ks), shape, jnp.float32) * scale

    def gain(shape):
        return 1.0 + 0.05 * jax.random.normal(next(ks), shape, jnp.float32)

    L, D = DEPTH, D_MODEL
    return {
        "x_prompt": nrm((BATCH, SEQ, D)),
        "x_sample": nrm((DEC_BATCH, DEC_SEQ, D)),
        "cache_ckv": nrm((DEC_BATCH, DEPTH, PAST_LEN, KV_RANK)),
        "cache_krope": nrm((DEC_BATCH, DEPTH, PAST_LEN, QK_ROPE)),
        "c": nrm((DEC_BATCH, D)),
        "c_ctx": nrm((D,)),
        "w_mod": nrm((L, D, N_MOD * D), 0.5 * D ** -0.5),
        "b_mod": nrm((L, N_MOD * D), 0.01),
        "norm_ffn1": gain((L, D)),
        "w_ffn1_gate": nrm((L, D, D_FF), D ** -0.5),
        "w_ffn1_up": nrm((L, D, D_FF), D ** -0.5),
        "w_ffn1_down": nrm((L, D_FF, D), D_FF ** -0.5),
        "norm_mix": gain((L, D)),
        "w_in": nrm((L, D, IN_W), D ** -0.5),
        "w_four": nrm((L, D_FOUR, D), D_FOUR ** -0.5),
        "q_norm": gain((L, Q_RANK)),
        "w_uq": nrm((L, Q_RANK, N_HEADS * (QK_NOPE + QK_ROPE)), Q_RANK ** -0.5),
        "kv_norm": gain((L, KV_RANK)),
        "w_ukv": nrm((L, KV_RANK, N_HEADS * (QK_NOPE + V_DIM)), KV_RANK ** -0.5),
        "w_attn_proj": nrm((L, N_HEADS * V_DIM, D), (N_HEADS * V_DIM) ** -0.5),
        "w_out": nrm((L, D, D), D ** -0.5),
        "norm_ffn2": gain((L, D)),
        "w_ffn2_gate": nrm((L, D, D_FF), D ** -0.5),
        "w_ffn2_up": nrm((L, D, D_FF), D ** -0.5),
        "w_ffn2_down": nrm((L, D_FF, D), D_FF ** -0.5),
        "final_norm": gain((D,)),
    }


def reference(x_prompt, x_sample, cache_ckv, cache_krope, c, c_ctx, w_mod, b_mod, norm_ffn1, w_ffn1_gate,
              w_ffn1_up, w_ffn1_down, norm_mix, w_in, w_four, q_norm, w_uq, kv_norm, w_ukv, w_attn_proj,
              w_out, norm_ffn2, w_ffn2_gate, w_ffn2_up, w_ffn2_down, final_norm):
    tables = _axial_tables(x_sample.shape[1])
    xp, xs = x_prompt, x_sample
    ckvs, krs = [], []
    for l in range(DEPTH):
        p = {"norm_ffn1": norm_ffn1[l], "w_ffn1_gate": w_ffn1_gate[l], "w_ffn1_up": w_ffn1_up[l],
             "w_ffn1_down": w_ffn1_down[l], "norm_mix": norm_mix[l], "w_in": w_in[l], "w_four": w_four[l],
             "q_norm": q_norm[l], "w_uq": w_uq[l], "kv_norm": kv_norm[l], "w_ukv": w_ukv[l],
             "w_attn_proj": w_attn_proj[l], "w_out": w_out[l], "norm_ffn2": norm_ffn2[l],
             "w_ffn2_gate": w_ffn2_gate[l], "w_ffn2_up": w_ffn2_up[l], "w_ffn2_down": w_ffn2_down[l]}
        mod_ctx = (jax.nn.silu(c_ctx) @ w_mod[l] + b_mod[l])[None, None, :]
        mod_lat = (jax.nn.silu(c) @ w_mod[l] + b_mod[l])[:, None, :]
        xp, ckv_l, kr_l = _layer(xp, mod_ctx, p, None, None, None)
        ckvs.append(ckv_l)
        krs.append(kr_l)
        xs, _, _ = _layer(xs, mod_lat, p, tables, cache_ckv[:, l], cache_krope[:, l])
    y_prompt = _rmsnorm(xp, final_norm)
    y_sample = _rmsnorm(xs, final_norm)
    new_ckv = jnp.stack(ckvs, axis=1)
    new_krope = jnp.stack(krs, axis=1)
    return (y_prompt, y_sample, new_ckv, new_krope)
```

```python
import functools

import numpy as np
import jax
import jax.numpy as jnp
from jax import lax
from jax.experimental import pallas as pl
from jax.experimental.pallas import tpu as pltpu

D_MODEL = 1024
BATCH = 16
SEQ = 256
DEPTH = 2
DEC_BATCH = 4
DEC_SEQ = 1024
PAST_LEN = 512
GRID_W = 64
D_FF = 2816
FOUR_GROUPS = 4
FOUR_GROUP_DIM = 64
D_FOUR = FOUR_GROUPS * FOUR_GROUP_DIM
N_HEADS = 8
QK_NOPE = 64
QK_ROPE = 32
V_DIM = 64
Q_RANK = 384
KV_RANK = 256
N_MOD = 9
ROPE_BASE = 10000.0
EPS = 1e-6

N_PROMPT = BATCH * SEQ
N_SAMPLE = DEC_BATCH * DEC_SEQ
N_TOK = N_PROMPT + N_SAMPLE
MOD_ROWS = 8
HEAD_PAD = 128
KEYS_SAMPLE = DEC_SEQ + PAST_LEN

VMEM_LIMIT = 52 * 1024 * 1024

F32 = jnp.float32
BF16 = jnp.bfloat16


def _dot(a, b):
    return jnp.dot(a, b, preferred_element_type=F32)


def _dot_nt(a, b):
    return lax.dot_general(a, b, (((1,), (1,)), ((), ())), preferred_element_type=F32)


def _mod_row(i, tm):
    n_prompt_tiles = N_PROMPT // tm
    per_batch = DEC_SEQ // tm
    return jnp.where(i < n_prompt_tiles, 0, 1 + (i - n_prompt_tiles) // per_batch)


def _mod_spec(layer, which, tm, grid_rank):
    def idx(*g):
        return ((layer * MOD_ROWS + _mod_row(g[0], tm)) * N_MOD + which, 0, 0)
    return pl.BlockSpec((None, 1, D_MODEL), idx)


def _norm_mod(x, nw, sc, sh):
    ms = jnp.mean(x * x, axis=-1, keepdims=True)
    y = x * lax.rsqrt(ms + EPS) * nw
    return y * (1.0 + sc) + sh


def _mod_kernel(c_ref, w_ref, b_ref, o_ref):
    c = c_ref[...]
    a = (c * jax.nn.sigmoid(c)).astype(BF16)
    o_ref[...] = _dot(a, w_ref[...].astype(BF16)) + b_ref[...]


def _modulation(c_all, w_mod, b_mod):
    tn = 1024
    n_out = N_MOD * D_MODEL
    return pl.pallas_call(
        _mod_kernel,
        grid=(DEPTH, n_out // tn),
        in_specs=[
            pl.BlockSpec((MOD_ROWS, D_MODEL), lambda l, j: (0, 0)),
            pl.BlockSpec((None, D_MODEL, tn), lambda l, j: (l, 0, j)),
            pl.BlockSpec((None, 1, tn), lambda l, j: (l, 0, j)),
        ],
        out_specs=pl.BlockSpec((None, MOD_ROWS, tn), lambda l, j: (l, 0, j)),
        out_shape=jax.ShapeDtypeStruct((DEPTH, MOD_ROWS, n_out), F32),
        compiler_params=pltpu.CompilerParams(
            dimension_semantics=("arbitrary", "arbitrary"), vmem_limit_bytes=VMEM_LIMIT),
        name="modulation",
    )(c_all, w_mod, b_mod.reshape(DEPTH, 1, n_out))


def _ffn_kernel(x_ref, sh_ref, sc_ref, g_ref, nw_ref, wg_ref, wu_ref, wd_ref, fn_ref, o_ref,
                h_scr, acc_scr, *, final):
    j = pl.program_id(1)

    @pl.when(j == 0)
    def _():
        h = _norm_mod(x_ref[...], nw_ref[...], sc_ref[...], sh_ref[...])
        h_scr[...] = h.astype(BF16)
        acc_scr[...] = jnp.zeros_like(acc_scr)

    h = h_scr[...]
    gate = _dot(h, wg_ref[...].astype(BF16))
    up = _dot(h, wu_ref[...].astype(BF16))
    act = (gate * jax.nn.sigmoid(gate)) * up
    acc_scr[...] += _dot(act.astype(BF16), wd_ref[...].astype(BF16))

    @pl.when(j == pl.num_programs(1) - 1)
    def _():
        xn = x_ref[...] + 0.5 * g_ref[...] * acc_scr[...]
        if final:
            ms = jnp.mean(xn * xn, axis=-1, keepdims=True)
            xn = xn * lax.rsqrt(ms + EPS) * fn_ref[...]
        o_ref[...] = xn


def _ffn(x, mod, layer, mod_base, nw, wg, wu, wd, final_norm, final):
    tm, tf = 1024, 256
    tok = lambda i, j: (i, 0)
    return pl.pallas_call(
        functools.partial(_ffn_kernel, final=final),
        grid=(N_TOK // tm, D_FF // tf),
        in_specs=[
            pl.BlockSpec((tm, D_MODEL), tok),
            _mod_spec(layer, mod_base + 0, tm, 2),
            _mod_spec(layer, mod_base + 1, tm, 2),
            _mod_spec(layer, mod_base + 2, tm, 2),
            pl.BlockSpec((None, 1, D_MODEL), lambda i, j: (layer, 0, 0)),
            pl.BlockSpec((None, D_MODEL, tf), lambda i, j: (layer, 0, j)),
            pl.BlockSpec((None, D_MODEL, tf), lambda i, j: (layer, 0, j)),
            pl.BlockSpec((None, tf, D_MODEL), lambda i, j: (layer, j, 0)),
            pl.BlockSpec((1, D_MODEL), lambda i, j: (0, 0)),
        ],
        out_specs=pl.BlockSpec((tm, D_MODEL), tok),
        out_shape=jax.ShapeDtypeStruct((N_TOK, D_MODEL), F32),
        scratch_shapes=[pltpu.VMEM((tm, D_MODEL), BF16), pltpu.VMEM((tm, D_MODEL), F32)],
        compiler_params=pltpu.CompilerParams(
            dimension_semantics=("arbitrary", "arbitrary"), vmem_limit_bytes=VMEM_LIMIT),
        name="ffn_final" if final else "ffn",
    )(x, mod, mod, mod, nw.reshape(DEPTH, 1, D_MODEL), wg, wu, wd, final_norm.reshape(1, D_MODEL))


PROJ_W = D_FOUR + Q_RANK + KV_RANK + 2 * HEAD_PAD


def _proj_kernel(x_ref, sh_ref, sc_ref, nw_ref, w1_ref, qn_ref, wq_ref, kvn_ref,
                 cq_ref, sq_ref, ck_ref, sk_ref,
                 f_ref, q_ref, ckv_ref, kr_ref, kraw_ref):
    h = _norm_mod(x_ref[...], nw_ref[...], sc_ref[...], sh_ref[...]).astype(BF16)
    p = _dot(h, w1_ref[...])
    f_ref[...] = p[:, :D_FOUR]

    ql = p[:, D_FOUR:D_FOUR + Q_RANK]
    qn = ql * lax.rsqrt(jnp.mean(ql * ql, axis=-1, keepdims=True) + EPS) * qn_ref[...]
    qq = _dot(qn.astype(BF16), wq_ref[...])
    cq, sq = cq_ref[...], sq_ref[...]
    for hd in range(N_HEADS):
        lo = hd * HEAD_PAD
        main = qq[:, lo:lo + HEAD_PAD]
        swap = qq[:, N_HEADS * HEAD_PAD + lo:N_HEADS * HEAD_PAD + lo + HEAD_PAD]
        q_ref[:, lo:lo + HEAD_PAD] = (main * cq + swap * sq).astype(BF16)

    c0 = D_FOUR + Q_RANK
    ckv = p[:, c0:c0 + KV_RANK]
    ckv_ref[...] = ckv * lax.rsqrt(jnp.mean(ckv * ckv, axis=-1, keepdims=True) + EPS) * kvn_ref[...]

    k0 = c0 + KV_RANK
    kr = p[:, k0:k0 + HEAD_PAD]
    krs = p[:, k0 + HEAD_PAD:k0 + 2 * HEAD_PAD]
    kraw_ref[...] = kr[:, :QK_ROPE]
    kr_ref[...] = (kr * ck_ref[...] + krs * sk_ref[...]).astype(BF16)


def _proj(x, mod, layer, nw, w1, qn, wq, kvn, tabs):
    tm = 512
    n_prompt_tiles = N_PROMPT // tm
    per_batch = DEC_SEQ // tm

    def tab_idx(i):
        return (jnp.where(i < n_prompt_tiles, per_batch, (i - n_prompt_tiles) % per_batch), 0)

    tok = lambda i: (i, 0)
    const = lambda i: (0, 0)
    tab_spec = pl.BlockSpec((tm, HEAD_PAD), tab_idx)
    return pl.pallas_call(
        _proj_kernel,
        grid=(N_TOK // tm,),
        in_specs=[
            pl.BlockSpec((tm, D_MODEL), tok),
            _mod_spec(layer, 3, tm, 1),
            _mod_spec(layer, 4, tm, 1),
            pl.BlockSpec((None, 1, D_MODEL), lambda i: (layer, 0, 0)),
            pl.BlockSpec((D_MODEL, PROJ_W), const),
            pl.BlockSpec((None, 1, Q_RANK), lambda i: (layer, 0, 0)),
            pl.BlockSpec((Q_RANK, 2 * N_HEADS * HEAD_PAD), const),
            pl.BlockSpec((None, 1, KV_RANK), lambda i: (layer, 0, 0)),
            tab_spec, tab_spec, tab_spec, tab_spec,
        ],
        out_specs=[
            pl.BlockSpec((tm, D_FOUR), tok),
            pl.BlockSpec((tm, N_HEADS * HEAD_PAD), tok),
            pl.BlockSpec((tm, KV_RANK), tok),
            pl.BlockSpec((tm, HEAD_PAD), tok),
            pl.BlockSpec((tm, QK_ROPE), tok),
        ],
        out_shape=[
            jax.ShapeDtypeStruct((N_TOK, D_FOUR), F32),
            jax.ShapeDtypeStruct((N_TOK, N_HEADS * HEAD_PAD), BF16),
            jax.ShapeDtypeStruct((N_TOK, KV_RANK), F32),
            jax.ShapeDtypeStruct((N_TOK, HEAD_PAD), BF16),
            jax.ShapeDtypeStruct((N_TOK, QK_ROPE), F32),
        ],
        compiler_params=pltpu.CompilerParams(
            dimension_semantics=("arbitrary",), vmem_limit_bytes=VMEM_LIMIT),
        name="mixer_proj",
    )(x, mod, mod, nw.reshape(DEPTH, 1, D_MODEL), w1, qn.reshape(DEPTH, 1, Q_RANK), wq,
      kvn.reshape(DEPTH, 1, KV_RANK), *tabs(tm))


def _split(x):
    hi = x.astype(BF16)
    lo = (x - hi.astype(F32)).astype(BF16)
    return hi, lo


def _dft_kernel(x_ref, gh_ref, gl_ref, th_ref, tl_ref, o_ref):
    xh, xl = _split(x_ref[...])
    gh, gl = gh_ref[...], gl_ref[...]
    y = _dot(xh, gh) + _dot(xl, gh) + _dot(xh, gl)
    yh, yl = _split(y)
    ych = jnp.concatenate([yh[:, :D_FOUR], yh[:, D_FOUR:]], axis=0)
    ycl = jnp.concatenate([yl[:, :D_FOUR], yl[:, D_FOUR:]], axis=0)
    th, tl = th_ref[...], tl_ref[...]
    fr = _dot(th, ych) + _dot(tl, ych) + _dot(th, ycl)
    o_ref[...] = fr.astype(BF16)


def _split_np(a):
    a32 = np.asarray(a, np.float32)
    hi = a32.astype(BF16)
    lo = (a32 - hi.astype(np.float32)).astype(BF16)
    return hi, lo


def _dft_tables(n):
    k = np.arange(n, dtype=np.int64)
    ang = 2.0 * np.pi * ((k[:, None] * k[None, :]) % n).astype(np.float64) / n
    t = np.concatenate([np.cos(ang), -np.sin(ang)], axis=1) / np.sqrt(n)
    c = np.arange(FOUR_GROUP_DIM, dtype=np.int64)
    ang_c = 2.0 * np.pi * ((c[:, None] * c[None, :]) % FOUR_GROUP_DIM).astype(np.float64) / FOUR_GROUP_DIM
    eye = np.eye(FOUR_GROUPS)
    g = np.concatenate([np.kron(eye, np.cos(ang_c)), np.kron(eye, np.sin(ang_c))], axis=1)
    g = g / np.sqrt(FOUR_GROUP_DIM)
    return _split_np(g) + _split_np(t)


def _dft(f_in, n_batch, n, row0):
    gh, gl, th, tl = _dft_tables(n)
    blk0 = row0 // n
    const = lambda b: (0, 0)
    return pl.pallas_call(
        _dft_kernel,
        grid=(n_batch,),
        in_specs=[
            pl.BlockSpec((n, D_FOUR), lambda b: (blk0 + b, 0)),
            pl.BlockSpec((D_FOUR, 2 * D_FOUR), const),
            pl.BlockSpec((D_FOUR, 2 * D_FOUR), const),
            pl.BlockSpec((n, 2 * n), const),
            pl.BlockSpec((n, 2 * n), const),
        ],
        out_specs=pl.BlockSpec((n, D_FOUR), lambda b: (b, 0)),
        out_shape=jax.ShapeDtypeStruct((n_batch * n, D_FOUR), BF16),
        compiler_params=pltpu.CompilerParams(
            dimension_semantics=("arbitrary",), vmem_limit_bytes=VMEM_LIMIT),
        name="fnet_dft_%d" % n,
    )(f_in, gh, gl, th, tl)


def _attn_kernel(*refs, n_own, n_ctx):
    if n_ctx:
        q_ref, ckv_ref, kr_ref, cckv_ref, ckr_ref, wkt_ref, wv_ref, o_ref, kt_scr, v_scr = refs
    else:
        q_ref, ckv_ref, kr_ref, wkt_ref, wv_ref, o_ref, kt_scr, v_scr = refs
    n_keys = n_own + n_ctx

    @pl.when(pl.program_id(1) == 0)
    def _():
        ckv = ckv_ref[...].astype(BF16)
        kr = kr_ref[...]
        if n_ctx:
            ckv = jnp.concatenate([ckv, cckv_ref[...].astype(BF16)], axis=0)
            kr = jnp.concatenate([kr, ckr_ref[...].astype(BF16)], axis=0)
        knt = _dot_nt(wkt_ref[...], ckv).astype(BF16)
        v = _dot(ckv, wv_ref[...]).astype(BF16)
        rows = lax.broadcasted_iota(jnp.int32, (HEAD_PAD, HEAD_PAD), 0)
        cols = lax.broadcasted_iota(jnp.int32, (HEAD_PAD, HEAD_PAD), 1)
        eye = (rows == cols).astype(BF16)
        krt = _dot_nt(eye, kr).astype(BF16)
        zeros = jnp.zeros((HEAD_PAD - QK_ROPE - QK_NOPE, n_keys), BF16)
        for hd in range(N_HEADS):
            lo = hd * HEAD_PAD
            kt_scr[lo:lo + QK_ROPE, :] = krt[:QK_ROPE]
            kt_scr[lo + QK_ROPE:lo + QK_ROPE + QK_NOPE, :] = knt[hd * QK_NOPE:(hd + 1) * QK_NOPE]
            kt_scr[lo + QK_ROPE + QK_NOPE:lo + HEAD_PAD, :] = zeros
            v_scr[hd] = v[:, hd * V_DIM:(hd + 1) * V_DIM]

    outs = []
    for hd in range(N_HEADS):
        lo = hd * HEAD_PAD
        s = _dot(q_ref[:, lo:lo + HEAD_PAD], kt_scr[lo:lo + HEAD_PAD, :])
        m = jnp.max(s, axis=-1, keepdims=True)
        p = jnp.exp(s - m)
        l = jnp.sum(p, axis=-1, keepdims=True)
        outs.append(_dot(p.astype(BF16), v_scr[hd]) / l)
    o_ref[...] = jnp.concatenate(outs, axis=1).astype(BF16)


def _attention(q, ckv_n, kr, wkt, wv, n_batch, n_own, row0, ctx=None):
    tq = 256
    n_ctx = 0 if ctx is None else PAST_LEN
    n_keys = n_own + n_ctx
    qt = n_own // tq
    blk0 = row0 // n_own
    const = lambda b, t: (0, 0)
    in_specs = [
        pl.BlockSpec((tq, N_HEADS * HEAD_PAD), lambda b, t: ((row0 // tq) + b * qt + t, 0)),
        pl.BlockSpec((n_own, KV_RANK), lambda b, t: (blk0 + b, 0)),
        pl.BlockSpec((n_own, HEAD_PAD), lambda b, t: (blk0 + b, 0)),
    ]
    args = [q, ckv_n, kr]
    if ctx is not None:
        cckv, ckr, layer = ctx
        in_specs += [
            pl.BlockSpec((None, None, PAST_LEN, KV_RANK), lambda b, t: (b, layer, 0, 0)),
            pl.BlockSpec((None, None, PAST_LEN, HEAD_PAD), lambda b, t: (b, layer, 0, 0)),
        ]
        args += [cckv, ckr]
    in_specs += [
        pl.BlockSpec((N_HEADS * QK_NOPE, KV_RANK), const),
        pl.BlockSpec((KV_RANK, N_HEADS * V_DIM), const),
    ]
    args += [wkt, wv]
    return pl.pallas_call(
        functools.partial(_attn_kernel, n_own=n_own, n_ctx=n_ctx),
        grid=(n_batch, qt),
        in_specs=in_specs,
        out_specs=pl.BlockSpec((tq, N_HEADS * V_DIM), lambda b, t: (b * qt + t, 0)),
        out_shape=jax.ShapeDtypeStruct((n_batch * n_own, N_HEADS * V_DIM), BF16),
        scratch_shapes=[
            pltpu.VMEM((N_HEADS * HEAD_PAD, n_keys), BF16),
            pltpu.VMEM((N_HEADS, n_keys, V_DIM), BF16),
        ],
        compiler_params=pltpu.CompilerParams(
            dimension_semantics=("arbitrary", "arbitrary"), vmem_limit_bytes=VMEM_LIMIT),
        name="mla_attention_%d" % n_keys,
    )(*args)


def _tail_kernel(x_ref, sh_ref, sc_ref, g_ref, nw_ref, fr_ref, o_ref, wg_ref, wf_ref, wa_ref, wo_ref,
                 out_ref):
    x = x_ref[...]
    h = _norm_mod(x, nw_ref[...], sc_ref[...], sh_ref[...]).astype(BF16)
    gates = _dot(h, wg_ref[...])
    a_out = _dot(fr_ref[...], wf_ref[...])
    o_out = _dot(o_ref[...], wa_ref[...])
    merged = jax.nn.sigmoid(gates[:, :D_MODEL]) * a_out + jax.nn.sigmoid(gates[:, D_MODEL:]) * o_out
    m = _dot(merged.astype(BF16), wo_ref[...])
    out_ref[...] = x + g_ref[...] * m


def _tail(x, mod, layer, nw, fr, o, wg, wf, wa, wo):
    tm = 512
    tok = lambda i: (i, 0)
    const = lambda i: (0, 0)
    return pl.pallas_call(
        _tail_kernel,
        grid=(N_TOK // tm,),
        in_specs=[
            pl.BlockSpec((tm, D_MODEL), tok),
            _mod_spec(layer, 3, tm, 1),
            _mod_spec(layer, 4, tm, 1),
            _mod_spec(layer, 5, tm, 1),
            pl.BlockSpec((None, 1, D_MODEL), lambda i: (layer, 0, 0)),
            pl.BlockSpec((tm, D_FOUR), tok),
            pl.BlockSpec((tm, N_HEADS * V_DIM), tok),
            pl.BlockSpec((D_MODEL, 2 * D_MODEL), const),
            pl.BlockSpec((D_FOUR, D_MODEL), const),
            pl.BlockSpec((N_HEADS * V_DIM, D_MODEL), const),
            pl.BlockSpec((D_MODEL, D_MODEL), const),
        ],
        out_specs=pl.BlockSpec((tm, D_MODEL), tok),
        out_shape=jax.ShapeDtypeStruct((N_TOK, D_MODEL), F32),
        compiler_params=pltpu.CompilerParams(
            dimension_semantics=("arbitrary",), vmem_limit_bytes=VMEM_LIMIT),
        name="mixer_tail",
    )(x, mod, mod, mod, nw.reshape(DEPTH, 1, D_MODEL), fr, o, wg, wf, wa, wo)


def _pair_swap():
    j = np.arange(QK_ROPE)
    return np.where((j % 16) < 8, j + 8, j - 8)


def _rope_tables(tm):
    rows = DEC_SEQ // GRID_W
    row = np.repeat(np.arange(rows), GRID_W).astype(np.float64)
    col = np.tile(np.arange(GRID_W), rows).astype(np.float64)
    axis_dim = QK_ROPE // 2
    inv = ROPE_BASE ** (-np.arange(0, axis_dim, 2, dtype=np.float64) / axis_dim)
    ar = row[:, None] * inv
    ac = col[:, None] * inv
    cr, sr, cc, sc = np.cos(ar), np.sin(ar), np.cos(ac), np.sin(ac)
    cos32 = np.concatenate([cr, cr, cc, cc], axis=1)
    sin32 = np.concatenate([-sr, sr, -sc, sc], axis=1)
    scale = np.float32((QK_NOPE + QK_ROPE) ** -0.5)

    def table(rope_part, nope_val, ident_rope):
        t = np.zeros((DEC_SEQ + tm, HEAD_PAD), np.float32)
        t[:DEC_SEQ, :QK_ROPE] = rope_part
        t[:DEC_SEQ, QK_ROPE:QK_ROPE + QK_NOPE] = nope_val
        t[DEC_SEQ:, :QK_ROPE] = ident_rope
        t[DEC_SEQ:, QK_ROPE:QK_ROPE + QK_NOPE] = nope_val
        return t

    cq = table(cos32, 1.0, 1.0) * scale
    sq = table(sin32, 0.0, 0.0) * scale
    ck = table(cos32, 0.0, 1.0)
    sk = table(sin32, 0.0, 0.0)
    return tuple(jnp.asarray(t, F32) for t in (cq, sq, ck, sk))


def _prep_layer(l, w_in, w_uq, w_ukv):
    swap = _pair_swap()
    wi = w_in[l]
    k0 = D_FOUR + Q_RANK + KV_RANK
    krope_w = wi[:, k0:k0 + QK_ROPE]
    pad = jnp.zeros((D_MODEL, HEAD_PAD - QK_ROPE), F32)
    w1 = jnp.concatenate([wi[:, :k0], krope_w, pad, krope_w[:, swap], pad], axis=1).astype(BF16)
    wgate = wi[:, k0 + QK_ROPE:].astype(BF16)

    wq = w_uq[l].reshape(Q_RANK, N_HEADS, QK_NOPE + QK_ROPE)
    nope, rope = wq[..., :QK_NOPE], wq[..., QK_NOPE:]
    zpad = jnp.zeros((Q_RANK, N_HEADS, HEAD_PAD - QK_ROPE - QK_NOPE), F32)
    main = jnp.concatenate([rope, nope, zpad], axis=-1).reshape(Q_RANK, N_HEADS * HEAD_PAD)
    zpad2 = jnp.zeros((Q_RANK, N_HEADS, HEAD_PAD - QK_ROPE), F32)
    swp = jnp.concatenate([rope[..., swap], zpad2], axis=-1).reshape(Q_RANK, N_HEADS * HEAD_PAD)
    wq2 = jnp.concatenate([main, swp], axis=1).astype(BF16)

    wkv = w_ukv[l].reshape(KV_RANK, N_HEADS, QK_NOPE + V_DIM)
    wkt = wkv[..., :QK_NOPE].reshape(KV_RANK, N_HEADS * QK_NOPE).T.astype(BF16)
    wv = wkv[..., QK_NOPE:].reshape(KV_RANK, N_HEADS * V_DIM).astype(BF16)
    return w1, wgate, wq2, wkt, wv


def kernel(x_prompt, x_sample, cache_ckv, cache_krope, c, c_ctx, w_mod, b_mod, norm_ffn1, w_ffn1_gate,
           w_ffn1_up, w_ffn1_down, norm_mix, w_in, w_four, q_norm, w_uq, kv_norm, w_ukv, w_attn_proj,
           w_out, norm_ffn2, w_ffn2_gate, w_ffn2_up, w_ffn2_down, final_norm):
    x = jnp.concatenate([x_prompt.reshape(N_PROMPT, D_MODEL), x_sample.reshape(N_SAMPLE, D_MODEL)], axis=0)
    c_all = jnp.concatenate(
        [c_ctx[None, :], c, jnp.zeros((MOD_ROWS - 1 - DEC_BATCH, D_MODEL), F32)], axis=0)
    mod = _modulation(c_all, w_mod, b_mod).reshape(DEPTH * MOD_ROWS * N_MOD, 1, D_MODEL)
    cache_kr = jnp.pad(cache_krope, ((0, 0), (0, 0), (0, 0), (0, HEAD_PAD - QK_ROPE)))

    ckvs, krs = [], []
    for l in range(DEPTH):
        w1, wgate, wq2, wkt, wv = _prep_layer(l, w_in, w_uq, w_ukv)
        x = _ffn(x, mod, l, 0, norm_ffn1, w_ffn1_gate, w_ffn1_up, w_ffn1_down, final_norm, False)
        f_in, q, ckv_n, kr, kraw = _proj(x, mod, l, norm_mix, w1, q_norm, wq2, kv_norm, _rope_tables)
        fr = jnp.concatenate([_dft(f_in, BATCH, SEQ, 0), _dft(f_in, DEC_BATCH, DEC_SEQ, N_PROMPT)], axis=0)
        o = jnp.concatenate([
            _attention(q, ckv_n, kr, wkt, wv, BATCH, SEQ, 0),
            _attention(q, ckv_n, kr, wkt, wv, DEC_BATCH, DEC_SEQ, N_PROMPT, (cache_ckv, cache_kr, l)),
        ], axis=0)
        x = _tail(x, mod, l, norm_mix, fr, o, wgate, w_four[l].astype(BF16),
                  w_attn_proj[l].astype(BF16), w_out[l].astype(BF16))
        x = _ffn(x, mod, l, 6, norm_ffn2, w_ffn2_gate, w_ffn2_up, w_ffn2_down, final_norm, l == DEPTH - 1)
        ckvs.append(ckv_n[:N_PROMPT].reshape(BATCH, SEQ, KV_RANK))
        krs.append(kraw[:N_PROMPT].reshape(BATCH, SEQ, QK_ROPE))

    y_prompt = x[:N_PROMPT].reshape(BATCH, SEQ, D_MODEL)
    y_sample = x[N_PROMPT:].reshape(DEC_BATCH, DEC_SEQ, D_MODEL)
    return y_prompt, y_sample, jnp.stack(ckvs, axis=1), jnp.stack(krs, axis=1)
```

```python
import functools

import numpy as np
import jax
import jax.numpy as jnp
from jax import lax
from jax.experimental import pallas as pl
from jax.experimental.pallas import tpu as pltpu

D_MODEL = 1024
BATCH = 16
SEQ = 256
DEPTH = 2
DEC_BATCH = 4
DEC_SEQ = 1024
PAST_LEN = 512
GRID_W = 64
D_FF = 2816
FOUR_GROUPS = 4
FOUR_GROUP_DIM = 64
D_FOUR = FOUR_GROUPS * FOUR_GROUP_DIM
N_HEADS = 8
QK_NOPE = 64
QK_ROPE = 32
V_DIM = 64
Q_RANK = 384
KV_RANK = 256
N_MOD = 9
ROPE_BASE = 10000.0
EPS = 1e-6

N_PROMPT = BATCH * SEQ
N_SAMPLE = DEC_BATCH * DEC_SEQ
N_TOK = N_PROMPT + N_SAMPLE
MOD_ROWS = 8
HEAD_PAD = 128
KEYS_SAMPLE = DEC_SEQ + PAST_LEN

VMEM_LIMIT = 52 * 1024 * 1024

F32 = jnp.float32
BF16 = jnp.bfloat16


def _dot(a, b):
    return jnp.dot(a, b, preferred_element_type=F32)


def _dot_nt(a, b):
    return lax.dot_general(a, b, (((1,), (1,)), ((), ())), preferred_element_type=F32)


def _mod_row(i, tm):
    n_prompt_tiles = N_PROMPT // tm
    per_batch = DEC_SEQ // tm
    return jnp.where(i < n_prompt_tiles, 0, 1 + (i - n_prompt_tiles) // per_batch)


def _mod_spec(layer, which, tm, tile_of_step=lambda i: i):
    def idx(*g):
        return ((layer * MOD_ROWS + _mod_row(tile_of_step(g[0]), tm)) * N_MOD + which, 0, 0)
    return pl.BlockSpec((None, 1, D_MODEL), idx)


def _norm_mod(x, nw, sc, sh):
    ms = jnp.mean(x * x, axis=-1, keepdims=True)
    y = x * lax.rsqrt(ms + EPS) * nw
    return y * (1.0 + sc) + sh


def _mod_kernel(c_ref, w_ref, b_ref, o_ref):
    c = c_ref[...]
    a = (c * jax.nn.sigmoid(c)).astype(BF16)
    o_ref[...] = _dot(a, w_ref[...].astype(BF16)) + b_ref[...]


def _modulation(c_all, w_mod, b_mod):
    tn = 1024
    n_out = N_MOD * D_MODEL
    return pl.pallas_call(
        _mod_kernel,
        grid=(DEPTH, n_out // tn),
        in_specs=[
            pl.BlockSpec((MOD_ROWS, D_MODEL), lambda l, j: (0, 0)),
            pl.BlockSpec((None, D_MODEL, tn), lambda l, j: (l, 0, j)),
            pl.BlockSpec((None, 1, tn), lambda l, j: (l, 0, j)),
        ],
        out_specs=pl.BlockSpec((None, MOD_ROWS, tn), lambda l, j: (l, 0, j)),
        out_shape=jax.ShapeDtypeStruct((DEPTH, MOD_ROWS, n_out), F32),
        compiler_params=pltpu.CompilerParams(
            dimension_semantics=("arbitrary", "arbitrary"), vmem_limit_bytes=VMEM_LIMIT),
        name="modulation",
    )(c_all, w_mod, b_mod.reshape(DEPTH, 1, n_out))


FFN_TM = 512
FFN_CHUNK = 256
FFN_NC = D_FF // FFN_CHUNK


def _ffn_kernel(x_ref, sh_ref, sc_ref, g_ref, nw_ref, wg_ref, wu_ref, wd_ref, fn_ref, o_ref,
                wg_s, wu_s, wd_s, h_scr, acc_scr, *, final):
    s = pl.program_id(0)

    def chunk_act(h, wg, wu):
        gate = _dot(h, wg)
        up = _dot(h, wu)
        return ((gate * jax.nn.sigmoid(gate)) * up).astype(BF16)

    def finish(acc):
        xn = x_ref[...] + 0.5 * g_ref[...] * acc
        if final:
            ms = jnp.mean(xn * xn, axis=-1, keepdims=True)
            xn = xn * lax.rsqrt(ms + EPS) * fn_ref[...]
        o_ref[...] = xn

    def hidden():
        return _norm_mod(x_ref[...], nw_ref[...], sc_ref[...], sh_ref[...]).astype(BF16)

    @pl.when(s == 0)
    def _():
        h_scr[...] = hidden()
        acc_scr[...] = jnp.zeros_like(acc_scr)

    @pl.when(s < FFN_NC)
    def _():
        wg = wg_ref[...].astype(BF16)
        wu = wu_ref[...].astype(BF16)
        wd = wd_ref[...].astype(BF16)
        wg_s[s] = wg
        wu_s[s] = wu
        wd_s[pl.ds(pl.multiple_of(s * FFN_CHUNK, FFN_CHUNK), FFN_CHUNK), :] = wd
        acc_scr[...] += _dot(chunk_act(h_scr[...], wg, wu), wd)

    @pl.when(s == FFN_NC - 1)
    def _():
        finish(acc_scr[...])

    @pl.when(s >= FFN_NC)
    def _():
        h = hidden()
        act = jnp.concatenate([chunk_act(h, wg_s[j], wu_s[j]) for j in range(FFN_NC)], axis=1)
        finish(_dot(act, wd_s[...]))


def _ffn(x, mod, layer, mod_base, nw, wg, wu, wd, final_norm, final):
    tm = FFN_TM
    tile = lambda s: jnp.maximum(s - (FFN_NC - 1), 0)
    chunk = lambda s: jnp.minimum(s, FFN_NC - 1)
    tok = lambda s: (tile(s), 0)
    return pl.pallas_call(
        functools.partial(_ffn_kernel, final=final),
        grid=(FFN_NC + N_TOK // tm - 1,),
        in_specs=[
            pl.BlockSpec((tm, D_MODEL), tok),
            _mod_spec(layer, mod_base + 0, tm, tile),
            _mod_spec(layer, mod_base + 1, tm, tile),
            _mod_spec(layer, mod_base + 2, tm, tile),
            pl.BlockSpec((None, 1, D_MODEL), lambda s: (layer, 0, 0)),
            pl.BlockSpec((None, D_MODEL, FFN_CHUNK), lambda s: (layer, 0, chunk(s))),
            pl.BlockSpec((None, D_MODEL, FFN_CHUNK), lambda s: (layer, 0, chunk(s))),
            pl.BlockSpec((None, FFN_CHUNK, D_MODEL), lambda s: (layer, chunk(s), 0)),
            pl.BlockSpec((1, D_MODEL), lambda s: (0, 0)),
        ],
        out_specs=pl.BlockSpec((tm, D_MODEL), tok),
        out_shape=jax.ShapeDtypeStruct((N_TOK, D_MODEL), F32),
        scratch_shapes=[
            pltpu.VMEM((FFN_NC, D_MODEL, FFN_CHUNK), BF16),
            pltpu.VMEM((FFN_NC, D_MODEL, FFN_CHUNK), BF16),
            pltpu.VMEM((D_FF, D_MODEL), BF16),
            pltpu.VMEM((tm, D_MODEL), BF16),
            pltpu.VMEM((tm, D_MODEL), F32),
        ],
        compiler_params=pltpu.CompilerParams(
            dimension_semantics=("arbitrary",), vmem_limit_bytes=VMEM_LIMIT),
        name="ffn_final" if final else "ffn",
    )(x, mod, mod, mod, nw.reshape(DEPTH, 1, D_MODEL), wg, wu, wd, final_norm.reshape(1, D_MODEL))


PROJ_W = D_FOUR + Q_RANK + KV_RANK + 2 * HEAD_PAD


def _proj_kernel(x_ref, sh_ref, sc_ref, nw_ref, w1_ref, qn_ref, wq_ref, kvn_ref,
                 cq_ref, sq_ref, ck_ref, sk_ref,
                 f_ref, q_ref, ckv_ref, kr_ref, kraw_ref):
    h = _norm_mod(x_ref[...], nw_ref[...], sc_ref[...], sh_ref[...]).astype(BF16)
    p = _dot(h, w1_ref[...])
    f_ref[...] = p[:, :D_FOUR]

    ql = p[:, D_FOUR:D_FOUR + Q_RANK]
    qn = ql * lax.rsqrt(jnp.mean(ql * ql, axis=-1, keepdims=True) + EPS) * qn_ref[...]
    qq = _dot(qn.astype(BF16), wq_ref[...])
    cq, sq = cq_ref[...], sq_ref[...]
    for hd in range(N_HEADS):
        lo = hd * HEAD_PAD
        main = qq[:, lo:lo + HEAD_PAD]
        swap = qq[:, N_HEADS * HEAD_PAD + lo:N_HEADS * HEAD_PAD + lo + HEAD_PAD]
        q_ref[:, lo:lo + HEAD_PAD] = (main * cq + swap * sq).astype(BF16)

    c0 = D_FOUR + Q_RANK
    ckv = p[:, c0:c0 + KV_RANK]
    ckv_ref[...] = ckv * lax.rsqrt(jnp.mean(ckv * ckv, axis=-1, keepdims=True) + EPS) * kvn_ref[...]

    k0 = c0 + KV_RANK
    kr = p[:, k0:k0 + HEAD_PAD]
    krs = p[:, k0 + HEAD_PAD:k0 + 2 * HEAD_PAD]
    kraw_ref[...] = kr[:, :QK_ROPE]
    kr_ref[...] = (kr * ck_ref[...] + krs * sk_ref[...]).astype(BF16)


def _proj(x, mod, layer, nw, w1, qn, wq, kvn, tabs):
    tm = 512
    n_prompt_tiles = N_PROMPT // tm
    per_batch = DEC_SEQ // tm

    def tab_idx(i):
        return (jnp.where(i < n_prompt_tiles, per_batch, (i - n_prompt_tiles) % per_batch), 0)

    tok = lambda i: (i, 0)
    const = lambda i: (0, 0)
    tab_spec = pl.BlockSpec((tm, HEAD_PAD), tab_idx)
    return pl.pallas_call(
        _proj_kernel,
        grid=(N_TOK // tm,),
        in_specs=[
            pl.BlockSpec((tm, D_MODEL), tok),
            _mod_spec(layer, 3, tm),
            _mod_spec(layer, 4, tm),
            pl.BlockSpec((None, 1, D_MODEL), lambda i: (layer, 0, 0)),
            pl.BlockSpec((D_MODEL, PROJ_W), const),
            pl.BlockSpec((None, 1, Q_RANK), lambda i: (layer, 0, 0)),
            pl.BlockSpec((Q_RANK, 2 * N_HEADS * HEAD_PAD), const),
            pl.BlockSpec((None, 1, KV_RANK), lambda i: (layer, 0, 0)),
            tab_spec, tab_spec, tab_spec, tab_spec,
        ],
        out_specs=[
            pl.BlockSpec((tm, D_FOUR), tok),
            pl.BlockSpec((tm, N_HEADS * HEAD_PAD), tok),
            pl.BlockSpec((tm, KV_RANK), tok),
            pl.BlockSpec((tm, HEAD_PAD), tok),
            pl.BlockSpec((tm, QK_ROPE), tok),
        ],
        out_shape=[
            jax.ShapeDtypeStruct((N_TOK, D_FOUR), F32),
            jax.ShapeDtypeStruct((N_TOK, N_HEADS * HEAD_PAD), BF16),
            jax.ShapeDtypeStruct((N_TOK, KV_RANK), F32),
            jax.ShapeDtypeStruct((N_TOK, HEAD_PAD), BF16),
            jax.ShapeDtypeStruct((N_TOK, QK_ROPE), F32),
        ],
        compiler_params=pltpu.CompilerParams(
            dimension_semantics=("arbitrary",), vmem_limit_bytes=VMEM_LIMIT),
        name="mixer_proj",
    )(x, mod, mod, nw.reshape(DEPTH, 1, D_MODEL), w1, qn.reshape(DEPTH, 1, Q_RANK), wq,
      kvn.reshape(DEPTH, 1, KV_RANK), *tabs(tm))


def _split(x):
    hi = x.astype(BF16)
    lo = (x - hi.astype(F32)).astype(BF16)
    return hi, lo


def _dft_kernel(x_ref, gh_ref, gl_ref, th_ref, tl_ref, o_ref):
    xh, xl = _split(x_ref[...])
    gh, gl = gh_ref[...], gl_ref[...]
    y = _dot(xh, gh) + _dot(xl, gh) + _dot(xh, gl)
    yh, yl = _split(y)
    ych = jnp.concatenate([yh[:, :D_FOUR], yh[:, D_FOUR:]], axis=0)
    ycl = jnp.concatenate([yl[:, :D_FOUR], yl[:, D_FOUR:]], axis=0)
    th, tl = th_ref[...], tl_ref[...]
    fr = _dot(th, ych) + _dot(tl, ych) + _dot(th, ycl)
    o_ref[...] = fr.astype(BF16)


def _split_np(a):
    a32 = np.asarray(a, np.float32)
    hi = a32.astype(BF16)
    lo = (a32 - hi.astype(np.float32)).astype(BF16)
    return hi, lo


def _dft_tables(n):
    k = np.arange(n, dtype=np.int64)
    ang = 2.0 * np.pi * ((k[:, None] * k[None, :]) % n).astype(np.float64) / n
    t = np.concatenate([np.cos(ang), -np.sin(ang)], axis=1) / np.sqrt(n)
    c = np.arange(FOUR_GROUP_DIM, dtype=np.int64)
    ang_c = 2.0 * np.pi * ((c[:, None] * c[None, :]) % FOUR_GROUP_DIM).astype(np.float64) / FOUR_GROUP_DIM
    eye = np.eye(FOUR_GROUPS)
    g = np.concatenate([np.kron(eye, np.cos(ang_c)), np.kron(eye, np.sin(ang_c))], axis=1)
    g = g / np.sqrt(FOUR_GROUP_DIM)
    return _split_np(g) + _split_np(t)


def _dft(f_in, n_batch, n, row0):
    gh, gl, th, tl = _dft_tables(n)
    blk0 = row0 // n
    const = lambda b: (0, 0)
    return pl.pallas_call(
        _dft_kernel,
        grid=(n_batch,),
        in_specs=[
            pl.BlockSpec((n, D_FOUR), lambda b: (blk0 + b, 0)),
            pl.BlockSpec((D_FOUR, 2 * D_FOUR), const),
            pl.BlockSpec((D_FOUR, 2 * D_FOUR), const),
            pl.BlockSpec((n, 2 * n), const),
            pl.BlockSpec((n, 2 * n), const),
        ],
        out_specs=pl.BlockSpec((n, D_FOUR), lambda b: (b, 0)),
        out_shape=jax.ShapeDtypeStruct((n_batch * n, D_FOUR), BF16),
        compiler_params=pltpu.CompilerParams(
            dimension_semantics=("arbitrary",), vmem_limit_bytes=VMEM_LIMIT),
        name="fnet_dft_%d" % n,
    )(f_in, gh, gl, th, tl)


def _attn_kernel(*refs, n_own, n_ctx):
    if n_ctx:
        q_ref, ckv_ref, kr_ref, cckv_ref, ckr_ref, wkt_ref, wv_ref, o_ref, kt_scr, v_scr = refs
    else:
        q_ref, ckv_ref, kr_ref, wkt_ref, wv_ref, o_ref, kt_scr, v_scr = refs
    n_keys = n_own + n_ctx

    @pl.when(pl.program_id(1) == 0)
    def _():
        ckv = ckv_ref[...].astype(BF16)
        kr = kr_ref[...]
        if n_ctx:
            ckv = jnp.concatenate([ckv, cckv_ref[...].astype(BF16)], axis=0)
            kr = jnp.concatenate([kr, ckr_ref[...].astype(BF16)], axis=0)
        knt = _dot_nt(wkt_ref[...], ckv).astype(BF16)
        v = _dot(ckv, wv_ref[...]).astype(BF16)
        rows = lax.broadcasted_iota(jnp.int32, (HEAD_PAD, HEAD_PAD), 0)
        cols = lax.broadcasted_iota(jnp.int32, (HEAD_PAD, HEAD_PAD), 1)
        eye = (rows == cols).astype(BF16)
        krt = _dot_nt(eye, kr).astype(BF16)
        zeros = jnp.zeros((HEAD_PAD - QK_ROPE - QK_NOPE, n_keys), BF16)
        for hd in range(N_HEADS):
            lo = hd * HEAD_PAD
            kt_scr[lo:lo + QK_ROPE, :] = krt[:QK_ROPE]
            kt_scr[lo + QK_ROPE:lo + QK_ROPE + QK_NOPE, :] = knt[hd * QK_NOPE:(hd + 1) * QK_NOPE]
            kt_scr[lo + QK_ROPE + QK_NOPE:lo + HEAD_PAD, :] = zeros
            v_scr[hd] = v[:, hd * V_DIM:(hd + 1) * V_DIM]

    outs = []
    for hd in range(N_HEADS):
        lo = hd * HEAD_PAD
        s = _dot(q_ref[:, lo:lo + HEAD_PAD], kt_scr[lo:lo + HEAD_PAD, :])
        m = jnp.max(s, axis=-1, keepdims=True)
        p = jnp.exp(s - m)
        l = jnp.sum(p, axis=-1, keepdims=True)
        outs.append(_dot(p.astype(BF16), v_scr[hd]) / l)
    o_ref[...] = jnp.concatenate(outs, axis=1).astype(BF16)


def _attention(q, ckv_n, kr, wkt, wv, n_batch, n_own, row0, ctx=None):
    tq = 256
    n_ctx = 0 if ctx is None else PAST_LEN
    n_keys = n_own + n_ctx
    qt = n_own // tq
    blk0 = row0 // n_own
    const = lambda b, t: (0, 0)
    in_specs = [
        pl.BlockSpec((tq, N_HEADS * HEAD_PAD), lambda b, t: ((row0 // tq) + b * qt + t, 0)),
        pl.BlockSpec((n_own, KV_RANK), lambda b, t: (blk0 + b, 0)),
        pl.BlockSpec((n_own, HEAD_PAD), lambda b, t: (blk0 + b, 0)),
    ]
    args = [q, ckv_n, kr]
    if ctx is not None:
        cckv, ckr, layer = ctx
        in_specs += [
            pl.BlockSpec((None, None, PAST_LEN, KV_RANK), lambda b, t: (b, layer, 0, 0)),
            pl.BlockSpec((None, None, PAST_LEN, HEAD_PAD), lambda b, t: (b, layer, 0, 0)),
        ]
        args += [cckv, ckr]
    in_specs += [
        pl.BlockSpec((N_HEADS * QK_NOPE, KV_RANK), const),
        pl.BlockSpec((KV_RANK, N_HEADS * V_DIM), const),
    ]
    args += [wkt, wv]
    return pl.pallas_call(
        functools.partial(_attn_kernel, n_own=n_own, n_ctx=n_ctx),
        grid=(n_batch, qt),
        in_specs=in_specs,
        out_specs=pl.BlockSpec((tq, N_HEADS * V_DIM), lambda b, t: (b * qt + t, 0)),
        out_shape=jax.ShapeDtypeStruct((n_batch * n_own, N_HEADS * V_DIM), BF16),
        scratch_shapes=[
            pltpu.VMEM((N_HEADS * HEAD_PAD, n_keys), BF16),
            pltpu.VMEM((N_HEADS, n_keys, V_DIM), BF16),
        ],
        compiler_params=pltpu.CompilerParams(
            dimension_semantics=("arbitrary", "arbitrary"), vmem_limit_bytes=VMEM_LIMIT),
        name="mla_attention_%d" % n_keys,
    )(*args)


def _tail_kernel(x_ref, sh_ref, sc_ref, g_ref, nw_ref, fr_ref, o_ref, wg_ref, wf_ref, wa_ref, wo_ref,
                 out_ref):
    x = x_ref[...]
    h = _norm_mod(x, nw_ref[...], sc_ref[...], sh_ref[...]).astype(BF16)
    gates = _dot(h, wg_ref[...])
    a_out = _dot(fr_ref[...], wf_ref[...])
    o_out = _dot(o_ref[...], wa_ref[...])
    merged = jax.nn.sigmoid(gates[:, :D_MODEL]) * a_out + jax.nn.sigmoid(gates[:, D_MODEL:]) * o_out
    m = _dot(merged.astype(BF16), wo_ref[...])
    out_ref[...] = x + g_ref[...] * m


def _tail(x, mod, layer, nw, fr, o, wg, wf, wa, wo):
    tm = 512
    tok = lambda i: (i, 0)
    const = lambda i: (0, 0)
    return pl.pallas_call(
        _tail_kernel,
        grid=(N_TOK // tm,),
        in_specs=[
            pl.BlockSpec((tm, D_MODEL), tok),
            _mod_spec(layer, 3, tm),
            _mod_spec(layer, 4, tm),
            _mod_spec(layer, 5, tm),
            pl.BlockSpec((None, 1, D_MODEL), lambda i: (layer, 0, 0)),
            pl.BlockSpec((tm, D_FOUR), tok),
            pl.BlockSpec((tm, N_HEADS * V_DIM), tok),
            pl.BlockSpec((D_MODEL, 2 * D_MODEL), const),
            pl.BlockSpec((D_FOUR, D_MODEL), const),
            pl.BlockSpec((N_HEADS * V_DIM, D_MODEL), const),
            pl.BlockSpec((D_MODEL, D_MODEL), const),
        ],
        out_specs=pl.BlockSpec((tm, D_MODEL), tok),
        out_shape=jax.ShapeDtypeStruct((N_TOK, D_MODEL), F32),
        compiler_params=pltpu.CompilerParams(
            dimension_semantics=("arbitrary",), vmem_limit_bytes=VMEM_LIMIT),
        name="mixer_tail",
    )(x, mod, mod, mod, nw.reshape(DEPTH, 1, D_MODEL), fr, o, wg, wf, wa, wo)


def _pair_swap():
    j = np.arange(QK_ROPE)
    return np.where((j % 16) < 8, j + 8, j - 8)


def _rope_tables(tm):
    rows = DEC_SEQ // GRID_W
    row = np.repeat(np.arange(rows), GRID_W).astype(np.float64)
    col = np.tile(np.arange(GRID_W), rows).astype(np.float64)
    axis_dim = QK_ROPE // 2
    inv = ROPE_BASE ** (-np.arange(0, axis_dim, 2, dtype=np.float64) / axis_dim)
    ar = row[:, None] * inv
    ac = col[:, None] * inv
    cr, sr, cc, sc = np.cos(ar), np.sin(ar), np.cos(ac), np.sin(ac)
    cos32 = np.concatenate([cr, cr, cc, cc], axis=1)
    sin32 = np.concatenate([-sr, sr, -sc, sc], axis=1)
    scale = np.float32((QK_NOPE + QK_ROPE) ** -0.5)

    def table(rope_part, nope_val, ident_rope):
        t = np.zeros((DEC_SEQ + tm, HEAD_PAD), np.float32)
        t[:DEC_SEQ, :QK_ROPE] = rope_part
        t[:DEC_SEQ, QK_ROPE:QK_ROPE + QK_NOPE] = nope_val
        t[DEC_SEQ:, :QK_ROPE] = ident_rope
        t[DEC_SEQ:, QK_ROPE:QK_ROPE + QK_NOPE] = nope_val
        return t

    cq = table(cos32, 1.0, 1.0) * scale
    sq = table(sin32, 0.0, 0.0) * scale
    ck = table(cos32, 0.0, 1.0)
    sk = table(sin32, 0.0, 0.0)
    return tuple(jnp.asarray(t, F32) for t in (cq, sq, ck, sk))


def _prep_layer(l, w_in, w_uq, w_ukv):
    swap = _pair_swap()
    wi = w_in[l]
    k0 = D_FOUR + Q_RANK + KV_RANK
    krope_w = wi[:, k0:k0 + QK_ROPE]
    pad = jnp.zeros((D_MODEL, HEAD_PAD - QK_ROPE), F32)
    w1 = jnp.concatenate([wi[:, :k0], krope_w, pad, krope_w[:, swap], pad], axis=1).astype(BF16)
    wgate = wi[:, k0 + QK_ROPE:].astype(BF16)

    wq = w_uq[l].reshape(Q_RANK, N_HEADS, QK_NOPE + QK_ROPE)
    nope, rope = wq[..., :QK_NOPE], wq[..., QK_NOPE:]
    zpad = jnp.zeros((Q_RANK, N_HEADS, HEAD_PAD - QK_ROPE - QK_NOPE), F32)
    main = jnp.concatenate([rope, nope, zpad], axis=-1).reshape(Q_RANK, N_HEADS * HEAD_PAD)
    zpad2 = jnp.zeros((Q_RANK, N_HEADS, HEAD_PAD - QK_ROPE), F32)
    swp = jnp.concatenate([rope[..., swap], zpad2], axis=-1).reshape(Q_RANK, N_HEADS * HEAD_PAD)
    wq2 = jnp.concatenate([main, swp], axis=1).astype(BF16)

    wkv = w_ukv[l].reshape(KV_RANK, N_HEADS, QK_NOPE + V_DIM)
    wkt = wkv[..., :QK_NOPE].reshape(KV_RANK, N_HEADS * QK_NOPE).T.astype(BF16)
    wv = wkv[..., QK_NOPE:].reshape(KV_RANK, N_HEADS * V_DIM).astype(BF16)
    return w1, wgate, wq2, wkt, wv


def kernel(x_prompt, x_sample, cache_ckv, cache_krope, c, c_ctx, w_mod, b_mod, norm_ffn1, w_ffn1_gate,
           w_ffn1_up, w_ffn1_down, norm_mix, w_in, w_four, q_norm, w_uq, kv_norm, w_ukv, w_attn_proj,
           w_out, norm_ffn2, w_ffn2_gate, w_ffn2_up, w_ffn2_down, final_norm):
    x = jnp.concatenate([x_prompt.reshape(N_PROMPT, D_MODEL), x_sample.reshape(N_SAMPLE, D_MODEL)], axis=0)
    c_all = jnp.concatenate(
        [c_ctx[None, :], c, jnp.zeros((MOD_ROWS - 1 - DEC_BATCH, D_MODEL), F32)], axis=0)
    mod = _modulation(c_all, w_mod, b_mod).reshape(DEPTH * MOD_ROWS * N_MOD, 1, D_MODEL)
    cache_kr = jnp.pad(cache_krope, ((0, 0), (0, 0), (0, 0), (0, HEAD_PAD - QK_ROPE)))

    ckvs, krs = [], []
    for l in range(DEPTH):
        w1, wgate, wq2, wkt, wv = _prep_layer(l, w_in, w_uq, w_ukv)
        x = _ffn(x, mod, l, 0, norm_ffn1, w_ffn1_gate, w_ffn1_up, w_ffn1_down, final_norm, False)
        f_in, q, ckv_n, kr, kraw = _proj(x, mod, l, norm_mix, w1, q_norm, wq2, kv_norm, _rope_tables)
        fr = jnp.concatenate([_dft(f_in, BATCH, SEQ, 0), _dft(f_in, DEC_BATCH, DEC_SEQ, N_PROMPT)], axis=0)
        o = jnp.concatenate([
            _attention(q, ckv_n, kr, wkt, wv, BATCH, SEQ, 0),
            _attention(q, ckv_n, kr, wkt, wv, DEC_BATCH, DEC_SEQ, N_PROMPT, (cache_ckv, cache_kr, l)),
        ], axis=0)
        x = _tail(x, mod, l, norm_mix, fr, o, wgate, w_four[l].astype(BF16),
                  w_attn_proj[l].astype(BF16), w_out[l].astype(BF16))
        x = _ffn(x, mod, l, 6, norm_ffn2, w_ffn2_gate, w_ffn2_up, w_ffn2_down, final_norm, l == DEPTH - 1)
        ckvs.append(ckv_n[:N_PROMPT].reshape(BATCH, SEQ, KV_RANK))
        krs.append(kraw[:N_PROMPT].reshape(BATCH, SEQ, QK_ROPE))

    y_prompt = x[:N_PROMPT].reshape(BATCH, SEQ, D_MODEL)
    y_sample = x[N_PROMPT:].reshape(DEC_BATCH, DEC_SEQ, D_MODEL)
    return y_prompt, y_sample, jnp.stack(ckvs, axis=1), jnp.stack(krs, axis=1)
```

```python
import functools

import numpy as np
import jax
import jax.numpy as jnp
from jax import lax
from jax.experimental import pallas as pl
from jax.experimental.pallas import tpu as pltpu

D_MODEL = 1024
BATCH = 16
SEQ = 256
DEPTH = 2
DEC_BATCH = 4
DEC_SEQ = 1024
PAST_LEN = 512
GRID_W = 64
D_FF = 2816
FOUR_GROUPS = 4
FOUR_GROUP_DIM = 64
D_FOUR = FOUR_GROUPS * FOUR_GROUP_DIM
N_HEADS = 8
QK_NOPE = 64
QK_ROPE = 32
V_DIM = 64
Q_RANK = 384
KV_RANK = 256
N_MOD = 9
ROPE_BASE = 10000.0
EPS = 1e-6

N_PROMPT = BATCH * SEQ
N_SAMPLE = DEC_BATCH * DEC_SEQ
N_TOK = N_PROMPT + N_SAMPLE
MOD_ROWS = 8
HEAD_PAD = 128
KEYS_SAMPLE = DEC_SEQ + PAST_LEN

VMEM_LIMIT = 52 * 1024 * 1024

F32 = jnp.float32
BF16 = jnp.bfloat16


def _dot(a, b):
    return jnp.dot(a, b, preferred_element_type=F32)


def _dot_nt(a, b):
    return lax.dot_general(a, b, (((1,), (1,)), ((), ())), preferred_element_type=F32)


def _mod_row(i, tm):
    n_prompt_tiles = N_PROMPT // tm
    per_batch = DEC_SEQ // tm
    return jnp.where(i < n_prompt_tiles, 0, 1 + (i - n_prompt_tiles) // per_batch)


def _mod_spec(layer, which, tm, tile_of_step=lambda i: i):
    def idx(*g):
        return ((layer * MOD_ROWS + _mod_row(tile_of_step(g[0]), tm)) * N_MOD + which, 0, 0)
    return pl.BlockSpec((None, 1, D_MODEL), idx)


def _norm_mod(x, nw, sc, sh):
    ms = jnp.mean(x * x, axis=-1, keepdims=True)
    y = x * lax.rsqrt(ms + EPS) * nw
    return y * (1.0 + sc) + sh


def _mod_kernel(c_ref, w_ref, b_ref, o_ref):
    c = c_ref[...]
    a = (c * jax.nn.sigmoid(c)).astype(BF16)
    o_ref[...] = _dot(a, w_ref[...].astype(BF16)) + b_ref[...]


def _modulation(c_all, w_mod, b_mod):
    tn = 1024
    n_out = N_MOD * D_MODEL
    return pl.pallas_call(
        _mod_kernel,
        grid=(DEPTH, n_out // tn),
        in_specs=[
            pl.BlockSpec((MOD_ROWS, D_MODEL), lambda l, j: (0, 0)),
            pl.BlockSpec((None, D_MODEL, tn), lambda l, j: (l, 0, j)),
            pl.BlockSpec((None, 1, tn), lambda l, j: (l, 0, j)),
        ],
        out_specs=pl.BlockSpec((None, MOD_ROWS, tn), lambda l, j: (l, 0, j)),
        out_shape=jax.ShapeDtypeStruct((DEPTH, MOD_ROWS, n_out), F32),
        compiler_params=pltpu.CompilerParams(
            dimension_semantics=("arbitrary", "arbitrary"), vmem_limit_bytes=VMEM_LIMIT),
        name="modulation",
    )(c_all, w_mod, b_mod.reshape(DEPTH, 1, n_out))


FFN_TM = 512
FFN_CHUNK = 256
FFN_NC = D_FF // FFN_CHUNK


def _ffn_kernel(*refs, final, split_in):
    refs = list(refs)
    x_refs = [refs.pop(0) for _ in range(2 if split_in else 1)]
    sh_ref, sc_ref, g_ref, nw_ref, wg_ref, wu_ref, wd_ref, fn_ref = refs[:8]
    o_refs = refs[8:10] if final else refs[8:9]
    wg_s, wu_s, wd_s, h_scr, acc_scr = refs[-5:]
    s = pl.program_id(0)
    is_prompt = jnp.maximum(s - (FFN_NC - 1), 0) < N_PROMPT // FFN_TM

    def x_tile():
        if split_in:
            return jnp.where(is_prompt, x_refs[0][...], x_refs[1][...])
        return x_refs[0][...]

    def chunk_act(h, wg, wu):
        gate = _dot(h, wg)
        up = _dot(h, wu)
        return ((gate * jax.nn.sigmoid(gate)) * up).astype(BF16)

    def finish(acc):
        xn = x_tile() + 0.5 * g_ref[...] * acc
        if not final:
            o_refs[0][...] = xn
            return
        ms = jnp.mean(xn * xn, axis=-1, keepdims=True)
        y = xn * lax.rsqrt(ms + EPS) * fn_ref[...]

        @pl.when(is_prompt)
        def _():
            o_refs[0][...] = y

        @pl.when(jnp.logical_not(is_prompt))
        def _():
            o_refs[1][...] = y

    def hidden():
        return _norm_mod(x_tile(), nw_ref[...], sc_ref[...], sh_ref[...]).astype(BF16)

    @pl.when(s == 0)
    def _():
        h_scr[...] = hidden()
        acc_scr[...] = jnp.zeros_like(acc_scr)

    @pl.when(s < FFN_NC)
    def _():
        wg = wg_ref[...].astype(BF16)
        wu = wu_ref[...].astype(BF16)
        wd = wd_ref[...].astype(BF16)
        wg_s[s] = wg
        wu_s[s] = wu
        wd_s[pl.ds(pl.multiple_of(s * FFN_CHUNK, FFN_CHUNK), FFN_CHUNK), :] = wd
        acc_scr[...] += _dot(chunk_act(h_scr[...], wg, wu), wd)

    @pl.when(s == FFN_NC - 1)
    def _():
        finish(acc_scr[...])

    @pl.when(s >= FFN_NC)
    def _():
        h = hidden()
        act = jnp.concatenate([chunk_act(h, wg_s[j], wu_s[j]) for j in range(FFN_NC)], axis=1)
        finish(_dot(act, wd_s[...]))


def _ffn(xs, mod, layer, mod_base, nw, wg, wu, wd, final_norm, final):
    tm = FFN_TM
    npt = N_PROMPT // tm
    split_in = len(xs) == 2
    tile = lambda s: jnp.maximum(s - (FFN_NC - 1), 0)
    chunk = lambda s: jnp.minimum(s, FFN_NC - 1)
    tok = pl.BlockSpec((tm, D_MODEL), lambda s: (tile(s), 0))
    tok_p = pl.BlockSpec((tm, D_MODEL), lambda s: (jnp.minimum(tile(s), npt - 1), 0))
    tok_s = pl.BlockSpec((tm, D_MODEL), lambda s: (jnp.maximum(tile(s) - npt, 0), 0))
    half = jax.ShapeDtypeStruct((N_PROMPT, D_MODEL), F32)
    return pl.pallas_call(
        functools.partial(_ffn_kernel, final=final, split_in=split_in),
        grid=(FFN_NC + N_TOK // tm - 1,),
        in_specs=([tok_p, tok_s] if split_in else [tok]) + [
            _mod_spec(layer, mod_base + 0, tm, tile),
            _mod_spec(layer, mod_base + 1, tm, tile),
            _mod_spec(layer, mod_base + 2, tm, tile),
            pl.BlockSpec((None, 1, D_MODEL), lambda s: (layer, 0, 0)),
            pl.BlockSpec((None, D_MODEL, FFN_CHUNK), lambda s: (layer, 0, chunk(s))),
            pl.BlockSpec((None, D_MODEL, FFN_CHUNK), lambda s: (layer, 0, chunk(s))),
            pl.BlockSpec((None, FFN_CHUNK, D_MODEL), lambda s: (layer, chunk(s), 0)),
            pl.BlockSpec((1, D_MODEL), lambda s: (0, 0)),
        ],
        out_specs=[tok_p, tok_s] if final else tok,
        out_shape=[half, half] if final else jax.ShapeDtypeStruct((N_TOK, D_MODEL), F32),
        scratch_shapes=[
            pltpu.VMEM((FFN_NC, D_MODEL, FFN_CHUNK), BF16),
            pltpu.VMEM((FFN_NC, D_MODEL, FFN_CHUNK), BF16),
            pltpu.VMEM((D_FF, D_MODEL), BF16),
            pltpu.VMEM((tm, D_MODEL), BF16),
            pltpu.VMEM((tm, D_MODEL), F32),
        ],
        compiler_params=pltpu.CompilerParams(
            dimension_semantics=("arbitrary",), vmem_limit_bytes=VMEM_LIMIT),
        name="ffn_final" if final else ("ffn_first" if split_in else "ffn"),
    )(*xs, mod, mod, mod, nw.reshape(DEPTH, 1, D_MODEL), wg, wu, wd, final_norm.reshape(1, D_MODEL))


PROJ_W = D_FOUR + Q_RANK + KV_RANK + 2 * HEAD_PAD


PROJ_TM = 512
PROJ_SEQS = PROJ_TM // SEQ


def _proj_kernel(*refs):
    (x_ref, sh_ref, sc_ref, nw_ref, w1_ref, qn_ref, wq_ref, kvn_ref,
     cq_ref, sq_ref, ck_ref, sk_ref) = refs[:12]
    f_ref, q_ref, ckv_ref, kr_ref, nckv_ref, nkr_ref = refs[-6:]
    is_prompt = pl.program_id(0) < N_PROMPT // PROJ_TM
    h = _norm_mod(x_ref[...], nw_ref[...], sc_ref[...], sh_ref[...]).astype(BF16)
    p = _dot(h, w1_ref[...])
    f_ref[...] = p[:, :D_FOUR]

    ql = p[:, D_FOUR:D_FOUR + Q_RANK]
    qn = ql * lax.rsqrt(jnp.mean(ql * ql, axis=-1, keepdims=True) + EPS) * qn_ref[...]
    qq = _dot(qn.astype(BF16), wq_ref[...])
    cq, sq = cq_ref[...], sq_ref[...]
    for hd in range(N_HEADS):
        lo = hd * HEAD_PAD
        main = qq[:, lo:lo + HEAD_PAD]
        swap = qq[:, N_HEADS * HEAD_PAD + lo:N_HEADS * HEAD_PAD + lo + HEAD_PAD]
        q_ref[:, lo:lo + HEAD_PAD] = (main * cq + swap * sq).astype(BF16)

    c0 = D_FOUR + Q_RANK
    ckv = p[:, c0:c0 + KV_RANK]
    ckv_n = ckv * lax.rsqrt(jnp.mean(ckv * ckv, axis=-1, keepdims=True) + EPS) * kvn_ref[...]
    ckv_ref[...] = ckv_n

    k0 = c0 + KV_RANK
    kr = p[:, k0:k0 + HEAD_PAD]
    krs = p[:, k0 + HEAD_PAD:k0 + 2 * HEAD_PAD]
    kr_ref[...] = (kr * ck_ref[...] + krs * sk_ref[...]).astype(BF16)

    @pl.when(is_prompt)
    def _():
        nckv_ref[...] = ckv_n.reshape(PROJ_SEQS, SEQ, KV_RANK)
        nkr_ref[...] = kr[:, :QK_ROPE].reshape(PROJ_SEQS, SEQ, QK_ROPE)


def _proj(x, mod, layer, nw, w1, qn, wq, kvn, tabs, prev_caches):
    tm = PROJ_TM
    n_prompt_tiles = N_PROMPT // tm
    per_batch = DEC_SEQ // tm

    def tab_idx(i):
        return (jnp.where(i < n_prompt_tiles, per_batch, (i - n_prompt_tiles) % per_batch), 0)

    tok = lambda i: (i, 0)
    const = lambda i: (0, 0)
    cache_idx = lambda i: (jnp.minimum(i, n_prompt_tiles - 1), layer, 0, 0)
    tab_spec = pl.BlockSpec((tm, HEAD_PAD), tab_idx)
    return pl.pallas_call(
        _proj_kernel,
        grid=(N_TOK // tm,),
        in_specs=[
            pl.BlockSpec((tm, D_MODEL), tok),
            _mod_spec(layer, 3, tm),
            _mod_spec(layer, 4, tm),
            pl.BlockSpec((None, 1, D_MODEL), lambda i: (layer, 0, 0)),
            pl.BlockSpec((D_MODEL, PROJ_W), const),
            pl.BlockSpec((None, 1, Q_RANK), lambda i: (layer, 0, 0)),
            pl.BlockSpec((Q_RANK, 2 * N_HEADS * HEAD_PAD), const),
            pl.BlockSpec((None, 1, KV_RANK), lambda i: (layer, 0, 0)),
            tab_spec, tab_spec, tab_spec, tab_spec,
        ] + [pl.BlockSpec(memory_space=pl.ANY)] * len(prev_caches),
        out_specs=[
            pl.BlockSpec((tm, D_FOUR), tok),
            pl.BlockSpec((tm, N_HEADS * HEAD_PAD), tok),
            pl.BlockSpec((tm, KV_RANK), tok),
            pl.BlockSpec((tm, HEAD_PAD), tok),
            pl.BlockSpec((PROJ_SEQS, None, SEQ, KV_RANK), cache_idx),
            pl.BlockSpec((PROJ_SEQS, None, SEQ, QK_ROPE), cache_idx),
        ],
        out_shape=[
            jax.ShapeDtypeStruct((N_TOK, D_FOUR), F32),
            jax.ShapeDtypeStruct((N_TOK, N_HEADS * HEAD_PAD), BF16),
            jax.ShapeDtypeStruct((N_TOK, KV_RANK), F32),
            jax.ShapeDtypeStruct((N_TOK, HEAD_PAD), BF16),
            jax.ShapeDtypeStruct((BATCH, DEPTH, SEQ, KV_RANK), F32),
            jax.ShapeDtypeStruct((BATCH, DEPTH, SEQ, QK_ROPE), F32),
        ],
        input_output_aliases={12 + k: 4 + k for k in range(len(prev_caches))},
        compiler_params=pltpu.CompilerParams(
            dimension_semantics=("arbitrary",), vmem_limit_bytes=VMEM_LIMIT),
        name="mixer_proj",
    )(x, mod, mod, nw.reshape(DEPTH, 1, D_MODEL), w1, qn.reshape(DEPTH, 1, Q_RANK), wq,
      kvn.reshape(DEPTH, 1, KV_RANK), *tabs(tm), *prev_caches)


def _split(x):
    hi = x.astype(BF16)
    lo = (x - hi.astype(F32)).astype(BF16)
    return hi, lo


def _dft_kernel(x_ref, gh_ref, gl_ref, th_ref, tl_ref, *rest):
    o_ref = rest[-1]
    xh, xl = _split(x_ref[...])
    gh, gl = gh_ref[...], gl_ref[...]
    y = _dot(xh, gh) + _dot(xl, gh) + _dot(xh, gl)
    yh, yl = _split(y)
    ych = jnp.concatenate([yh[:, :D_FOUR], yh[:, D_FOUR:]], axis=0)
    ycl = jnp.concatenate([yl[:, :D_FOUR], yl[:, D_FOUR:]], axis=0)
    th, tl = th_ref[...], tl_ref[...]
    fr = _dot(th, ych) + _dot(tl, ych) + _dot(th, ycl)
    o_ref[...] = fr.astype(BF16)


def _split_np(a):
    a32 = np.asarray(a, np.float32)
    hi = a32.astype(BF16)
    lo = (a32 - hi.astype(np.float32)).astype(BF16)
    return hi, lo


def _dft_tables(n):
    k = np.arange(n, dtype=np.int64)
    ang = 2.0 * np.pi * ((k[:, None] * k[None, :]) % n).astype(np.float64) / n
    t = np.concatenate([np.cos(ang), -np.sin(ang)], axis=1) / np.sqrt(n)
    c = np.arange(FOUR_GROUP_DIM, dtype=np.int64)
    ang_c = 2.0 * np.pi * ((c[:, None] * c[None, :]) % FOUR_GROUP_DIM).astype(np.float64) / FOUR_GROUP_DIM
    eye = np.eye(FOUR_GROUPS)
    g = np.concatenate([np.kron(eye, np.cos(ang_c)), np.kron(eye, np.sin(ang_c))], axis=1)
    g = g / np.sqrt(FOUR_GROUP_DIM)
    return _split_np(g) + _split_np(t)


def _alias_prev(prev, n_inputs):
    if prev is None:
        return [], {}, []
    return [pl.BlockSpec(memory_space=pl.ANY)], {n_inputs: 0}, [prev]


def _dft(f_in, n_batch, n, row0, prev=None):
    gh, gl, th, tl = _dft_tables(n)
    blk0 = row0 // n
    const = lambda b: (0, 0)
    prev_spec, aliases, prev_arg = _alias_prev(prev, 5)
    return pl.pallas_call(
        _dft_kernel,
        grid=(n_batch,),
        in_specs=[
            pl.BlockSpec((n, D_FOUR), lambda b: (blk0 + b, 0)),
            pl.BlockSpec((D_FOUR, 2 * D_FOUR), const),
            pl.BlockSpec((D_FOUR, 2 * D_FOUR), const),
            pl.BlockSpec((n, 2 * n), const),
            pl.BlockSpec((n, 2 * n), const),
        ] + prev_spec,
        out_specs=pl.BlockSpec((n, D_FOUR), lambda b: (blk0 + b, 0)),
        out_shape=jax.ShapeDtypeStruct((N_TOK, D_FOUR), BF16),
        input_output_aliases=aliases,
        compiler_params=pltpu.CompilerParams(
            dimension_semantics=("arbitrary",), vmem_limit_bytes=VMEM_LIMIT),
        name="fnet_dft_%d" % n,
    )(f_in, gh, gl, th, tl, *prev_arg)


def _attn_kernel(*refs, n_own, n_ctx):
    if n_ctx:
        q_ref, ckv_ref, kr_ref, cckv_ref, ckr_ref, wkt_ref, wv_ref = refs[:7]
    else:
        q_ref, ckv_ref, kr_ref, wkt_ref, wv_ref = refs[:5]
    o_ref, kt_scr, v_scr = refs[-3:]
    n_keys = n_own + n_ctx

    @pl.when(pl.program_id(1) == 0)
    def _():
        ckv = ckv_ref[...].astype(BF16)
        kr = kr_ref[...]
        if n_ctx:
            ckv = jnp.concatenate([ckv, cckv_ref[...].astype(BF16)], axis=0)
            kr = jnp.concatenate([kr, ckr_ref[...].astype(BF16)], axis=0)
        knt = _dot_nt(wkt_ref[...], ckv).astype(BF16)
        v = _dot(ckv, wv_ref[...]).astype(BF16)
        rows = lax.broadcasted_iota(jnp.int32, (HEAD_PAD, HEAD_PAD), 0)
        cols = lax.broadcasted_iota(jnp.int32, (HEAD_PAD, HEAD_PAD), 1)
        eye = (rows == cols).astype(BF16)
        krt = _dot_nt(eye, kr).astype(BF16)
        zeros = jnp.zeros((HEAD_PAD - QK_ROPE - QK_NOPE, n_keys), BF16)
        for hd in range(N_HEADS):
            lo = hd * HEAD_PAD
            kt_scr[lo:lo + QK_ROPE, :] = krt[:QK_ROPE]
            kt_scr[lo + QK_ROPE:lo + QK_ROPE + QK_NOPE, :] = knt[hd * QK_NOPE:(hd + 1) * QK_NOPE]
            kt_scr[lo + QK_ROPE + QK_NOPE:lo + HEAD_PAD, :] = zeros
            v_scr[hd] = v[:, hd * V_DIM:(hd + 1) * V_DIM]

    outs = []
    for hd in range(N_HEADS):
        lo = hd * HEAD_PAD
        s = _dot(q_ref[:, lo:lo + HEAD_PAD], kt_scr[lo:lo + HEAD_PAD, :])
        m = jnp.max(s, axis=-1, keepdims=True)
        p = jnp.exp(s - m)
        l = jnp.sum(p, axis=-1, keepdims=True)
        outs.append(_dot(p.astype(BF16), v_scr[hd]) / l)
    o_ref[...] = jnp.concatenate(outs, axis=1).astype(BF16)


def _attention(q, ckv_n, kr, wkt, wv, n_batch, n_own, row0, ctx=None, prev=None):
    tq = 256
    n_ctx = 0 if ctx is None else PAST_LEN
    n_keys = n_own + n_ctx
    qt = n_own // tq
    blk0 = row0 // n_own
    const = lambda b, t: (0, 0)
    in_specs = [
        pl.BlockSpec((tq, N_HEADS * HEAD_PAD), lambda b, t: ((row0 // tq) + b * qt + t, 0)),
        pl.BlockSpec((n_own, KV_RANK), lambda b, t: (blk0 + b, 0)),
        pl.BlockSpec((n_own, HEAD_PAD), lambda b, t: (blk0 + b, 0)),
    ]
    args = [q, ckv_n, kr]
    if ctx is not None:
        cckv, ckr, layer = ctx
        in_specs += [
            pl.BlockSpec((None, None, PAST_LEN, KV_RANK), lambda b, t: (b, layer, 0, 0)),
            pl.BlockSpec((None, None, PAST_LEN, HEAD_PAD), lambda b, t: (b, layer, 0, 0)),
        ]
        args += [cckv, ckr]
    in_specs += [
        pl.BlockSpec((N_HEADS * QK_NOPE, KV_RANK), const),
        pl.BlockSpec((KV_RANK, N_HEADS * V_DIM), const),
    ]
    args += [wkt, wv]
    prev_spec, aliases, prev_arg = _alias_prev(prev, len(args))
    return pl.pallas_call(
        functools.partial(_attn_kernel, n_own=n_own, n_ctx=n_ctx),
        grid=(n_batch, qt),
        in_specs=in_specs + prev_spec,
        out_specs=pl.BlockSpec((tq, N_HEADS * V_DIM), lambda b, t: ((row0 // tq) + b * qt + t, 0)),
        out_shape=jax.ShapeDtypeStruct((N_TOK, N_HEADS * V_DIM), BF16),
        input_output_aliases=aliases,
        scratch_shapes=[
            pltpu.VMEM((N_HEADS * HEAD_PAD, n_keys), BF16),
            pltpu.VMEM((N_HEADS, n_keys, V_DIM), BF16),
        ],
        compiler_params=pltpu.CompilerParams(
            dimension_semantics=("arbitrary", "arbitrary"), vmem_limit_bytes=VMEM_LIMIT),
        name="mla_attention_%d" % n_keys,
    )(*args, *prev_arg)


def _tail_kernel(x_ref, sh_ref, sc_ref, g_ref, nw_ref, fr_ref, o_ref, wg_ref, wf_ref, wa_ref, wo_ref,
                 out_ref):
    x = x_ref[...]
    h = _norm_mod(x, nw_ref[...], sc_ref[...], sh_ref[...]).astype(BF16)
    gates = _dot(h, wg_ref[...])
    a_out = _dot(fr_ref[...], wf_ref[...])
    o_out = _dot(o_ref[...], wa_ref[...])
    merged = jax.nn.sigmoid(gates[:, :D_MODEL]) * a_out + jax.nn.sigmoid(gates[:, D_MODEL:]) * o_out
    m = _dot(merged.astype(BF16), wo_ref[...])
    out_ref[...] = x + g_ref[...] * m


def _tail(x, mod, layer, nw, fr, o, wg, wf, wa, wo):
    tm = 512
    tok = lambda i: (i, 0)
    const = lambda i: (0, 0)
    return pl.pallas_call(
        _tail_kernel,
        grid=(N_TOK // tm,),
        in_specs=[
            pl.BlockSpec((tm, D_MODEL), tok),
            _mod_spec(layer, 3, tm),
            _mod_spec(layer, 4, tm),
            _mod_spec(layer, 5, tm),
            pl.BlockSpec((None, 1, D_MODEL), lambda i: (layer, 0, 0)),
            pl.BlockSpec((tm, D_FOUR), tok),
            pl.BlockSpec((tm, N_HEADS * V_DIM), tok),
            pl.BlockSpec((D_MODEL, 2 * D_MODEL), const),
            pl.BlockSpec((D_FOUR, D_MODEL), const),
            pl.BlockSpec((N_HEADS * V_DIM, D_MODEL), const),
            pl.BlockSpec((D_MODEL, D_MODEL), const),
        ],
        out_specs=pl.BlockSpec((tm, D_MODEL), tok),
        out_shape=jax.ShapeDtypeStruct((N_TOK, D_MODEL), F32),
        compiler_params=pltpu.CompilerParams(
            dimension_semantics=("arbitrary",), vmem_limit_bytes=VMEM_LIMIT),
        name="mixer_tail",
    )(x, mod, mod, mod, nw.reshape(DEPTH, 1, D_MODEL), fr, o, wg, wf, wa, wo)


def _pair_swap():
    j = np.arange(QK_ROPE)
    return np.where((j % 16) < 8, j + 8, j - 8)


def _rope_tables(tm):
    rows = DEC_SEQ // GRID_W
    row = np.repeat(np.arange(rows), GRID_W).astype(np.float64)
    col = np.tile(np.arange(GRID_W), rows).astype(np.float64)
    axis_dim = QK_ROPE // 2
    inv = ROPE_BASE ** (-np.arange(0, axis_dim, 2, dtype=np.float64) / axis_dim)
    ar = row[:, None] * inv
    ac = col[:, None] * inv
    cr, sr, cc, sc = np.cos(ar), np.sin(ar), np.cos(ac), np.sin(ac)
    cos32 = np.concatenate([cr, cr, cc, cc], axis=1)
    sin32 = np.concatenate([-sr, sr, -sc, sc], axis=1)
    scale = np.float32((QK_NOPE + QK_ROPE) ** -0.5)

    def table(rope_part, nope_val, ident_rope):
        t = np.zeros((DEC_SEQ + tm, HEAD_PAD), np.float32)
        t[:DEC_SEQ, :QK_ROPE] = rope_part
        t[:DEC_SEQ, QK_ROPE:QK_ROPE + QK_NOPE] = nope_val
        t[DEC_SEQ:, :QK_ROPE] = ident_rope
        t[DEC_SEQ:, QK_ROPE:QK_ROPE + QK_NOPE] = nope_val
        return t

    cq = table(cos32, 1.0, 1.0) * scale
    sq = table(sin32, 0.0, 0.0) * scale
    ck = table(cos32, 0.0, 1.0)
    sk = table(sin32, 0.0, 0.0)
    return tuple(jnp.asarray(t, F32) for t in (cq, sq, ck, sk))


def _prep_layer(l, w_in, w_uq, w_ukv):
    swap = _pair_swap()
    wi = w_in[l]
    k0 = D_FOUR + Q_RANK + KV_RANK
    krope_w = wi[:, k0:k0 + QK_ROPE]
    pad = jnp.zeros((D_MODEL, HEAD_PAD - QK_ROPE), F32)
    w1 = jnp.concatenate([wi[:, :k0], krope_w, pad, krope_w[:, swap], pad], axis=1).astype(BF16)
    wgate = wi[:, k0 + QK_ROPE:].astype(BF16)

    wq = w_uq[l].reshape(Q_RANK, N_HEADS, QK_NOPE + QK_ROPE)
    nope, rope = wq[..., :QK_NOPE], wq[..., QK_NOPE:]
    zpad = jnp.zeros((Q_RANK, N_HEADS, HEAD_PAD - QK_ROPE - QK_NOPE), F32)
    main = jnp.concatenate([rope, nope, zpad], axis=-1).reshape(Q_RANK, N_HEADS * HEAD_PAD)
    zpad2 = jnp.zeros((Q_RANK, N_HEADS, HEAD_PAD - QK_ROPE), F32)
    swp = jnp.concatenate([rope[..., swap], zpad2], axis=-1).reshape(Q_RANK, N_HEADS * HEAD_PAD)
    wq2 = jnp.concatenate([main, swp], axis=1).astype(BF16)

    wkv = w_ukv[l].reshape(KV_RANK, N_HEADS, QK_NOPE + V_DIM)
    wkt = wkv[..., :QK_NOPE].reshape(KV_RANK, N_HEADS * QK_NOPE).T.astype(BF16)
    wv = wkv[..., QK_NOPE:].reshape(KV_RANK, N_HEADS * V_DIM).astype(BF16)
    return w1, wgate, wq2, wkt, wv


def kernel(x_prompt, x_sample, cache_ckv, cache_krope, c, c_ctx, w_mod, b_mod, norm_ffn1, w_ffn1_gate,
           w_ffn1_up, w_ffn1_down, norm_mix, w_in, w_four, q_norm, w_uq, kv_norm, w_ukv, w_attn_proj,
           w_out, norm_ffn2, w_ffn2_gate, w_ffn2_up, w_ffn2_down, final_norm):
    xs = (x_prompt.reshape(N_PROMPT, D_MODEL), x_sample.reshape(N_SAMPLE, D_MODEL))
    c_all = jnp.concatenate(
        [c_ctx[None, :], c, jnp.zeros((MOD_ROWS - 1 - DEC_BATCH, D_MODEL), F32)], axis=0)
    mod = _modulation(c_all, w_mod, b_mod).reshape(DEPTH * MOD_ROWS * N_MOD, 1, D_MODEL)
    cache_kr = jnp.pad(cache_krope, ((0, 0), (0, 0), (0, 0), (0, HEAD_PAD - QK_ROPE)))

    caches = ()
    for l in range(DEPTH):
        w1, wgate, wq2, wkt, wv = _prep_layer(l, w_in, w_uq, w_ukv)
        x = _ffn(xs, mod, l, 0, norm_ffn1, w_ffn1_gate, w_ffn1_up, w_ffn1_down, final_norm, False)
        f_in, q, ckv_n, kr, new_ckv, new_krope = _proj(
            x, mod, l, norm_mix, w1, q_norm, wq2, kv_norm, _rope_tables, caches)
        caches = (new_ckv, new_krope)
        fr = _dft(f_in, BATCH, SEQ, 0)
        fr = _dft(f_in, DEC_BATCH, DEC_SEQ, N_PROMPT, prev=fr)
        o = _attention(q, ckv_n, kr, wkt, wv, BATCH, SEQ, 0)
        o = _attention(q, ckv_n, kr, wkt, wv, DEC_BATCH, DEC_SEQ, N_PROMPT, (cache_ckv, cache_kr, l), prev=o)
        x = _tail(x, mod, l, norm_mix, fr, o, wgate, w_four[l].astype(BF16),
                  w_attn_proj[l].astype(BF16), w_out[l].astype(BF16))
        xs = _ffn((x,), mod, l, 6, norm_ffn2, w_ffn2_gate, w_ffn2_up, w_ffn2_down, final_norm, l == DEPTH - 1)
        if l < DEPTH - 1:
            xs = (xs,)

    y_prompt = xs[0].reshape(BATCH, SEQ, D_MODEL)
    y_sample = xs[1].reshape(DEC_BATCH, DEC_SEQ, D_MODEL)
    return y_prompt, y_sample, caches[0], caches[1]
```

```python
import functools

import numpy as np
import jax
import jax.numpy as jnp
from jax import lax
from jax.experimental import pallas as pl
from jax.experimental.pallas import tpu as pltpu

D_MODEL = 1024
BATCH = 16
SEQ = 256
DEPTH = 2
DEC_BATCH = 4
DEC_SEQ = 1024
PAST_LEN = 512
GRID_W = 64
D_FF = 2816
FOUR_GROUPS = 4
FOUR_GROUP_DIM = 64
D_FOUR = FOUR_GROUPS * FOUR_GROUP_DIM
N_HEADS = 8
QK_NOPE = 64
QK_ROPE = 32
V_DIM = 64
Q_RANK = 384
KV_RANK = 256
N_MOD = 9
ROPE_BASE = 10000.0
EPS = 1e-6

N_PROMPT = BATCH * SEQ
N_SAMPLE = DEC_BATCH * DEC_SEQ
N_TOK = N_PROMPT + N_SAMPLE
MOD_ROWS = 8
HEAD_PAD = 128
KEYS_SAMPLE = DEC_SEQ + PAST_LEN

VMEM_LIMIT = 52 * 1024 * 1024

F32 = jnp.float32
BF16 = jnp.bfloat16


def _dot(a, b):
    return jnp.dot(a, b, preferred_element_type=F32)


def _dot_nt(a, b):
    return lax.dot_general(a, b, (((1,), (1,)), ((), ())), preferred_element_type=F32)


def _mod_row(i, tm):
    n_prompt_tiles = N_PROMPT // tm
    per_batch = DEC_SEQ // tm
    return jnp.where(i < n_prompt_tiles, 0, 1 + (i - n_prompt_tiles) // per_batch)


def _mod_spec(layer, which, tm, tile_of_step=lambda i: i):
    def idx(*g):
        return ((layer * MOD_ROWS + _mod_row(tile_of_step(g[0]), tm)) * N_MOD + which, 0, 0)
    return pl.BlockSpec((None, 1, D_MODEL), idx)


def _norm_mod(x, nw, sc, sh):
    ms = jnp.mean(x * x, axis=-1, keepdims=True)
    y = x * lax.rsqrt(ms + EPS) * nw
    return y * (1.0 + sc) + sh


def _mod_kernel(c_ref, w_ref, b_ref, o_ref):
    c = c_ref[...]
    a = (c * jax.nn.sigmoid(c)).astype(BF16)
    o_ref[...] = _dot(a, w_ref[...].astype(BF16)) + b_ref[...]


def _modulation(c_all, w_mod, b_mod):
    tn = 1024
    n_out = N_MOD * D_MODEL
    return pl.pallas_call(
        _mod_kernel,
        grid=(DEPTH, n_out // tn),
        in_specs=[
            pl.BlockSpec((MOD_ROWS, D_MODEL), lambda l, j: (0, 0)),
            pl.BlockSpec((None, D_MODEL, tn), lambda l, j: (l, 0, j)),
            pl.BlockSpec((None, 1, tn), lambda l, j: (l, 0, j)),
        ],
        out_specs=pl.BlockSpec((None, MOD_ROWS, tn), lambda l, j: (l, 0, j)),
        out_shape=jax.ShapeDtypeStruct((DEPTH, MOD_ROWS, n_out), F32),
        compiler_params=pltpu.CompilerParams(
            dimension_semantics=("arbitrary", "arbitrary"), vmem_limit_bytes=VMEM_LIMIT),
        name="modulation",
    )(c_all, w_mod, b_mod.reshape(DEPTH, 1, n_out))


FFN_TM = 512
FFN_CHUNK = 256
FFN_NC = D_FF // FFN_CHUNK


def _ffn_kernel(*refs, final, split_in):
    refs = list(refs)
    x_refs = [refs.pop(0) for _ in range(2 if split_in else 1)]
    sh_ref, sc_ref, g_ref, nw_ref, wg_ref, wu_ref, wd_ref, fn_ref = refs[:8]
    o_refs = refs[8:10] if final else refs[8:9]
    wg_s, wu_s, wd_s, h_scr, acc_scr = refs[-5:]
    s = pl.program_id(0)
    is_prompt = jnp.maximum(s - (FFN_NC - 1), 0) < N_PROMPT // FFN_TM

    def x_tile():
        if split_in:
            return jnp.where(is_prompt, x_refs[0][...], x_refs[1][...])
        return x_refs[0][...]

    def chunk_act(h, wg, wu):
        gate = _dot(h, wg)
        up = _dot(h, wu)
        return ((gate * jax.nn.sigmoid(gate)) * up).astype(BF16)

    def finish(acc):
        xn = x_tile() + 0.5 * g_ref[...] * acc
        if not final:
            o_refs[0][...] = xn
            return
        ms = jnp.mean(xn * xn, axis=-1, keepdims=True)
        y = xn * lax.rsqrt(ms + EPS) * fn_ref[...]

        @pl.when(is_prompt)
        def _():
            o_refs[0][...] = y

        @pl.when(jnp.logical_not(is_prompt))
        def _():
            o_refs[1][...] = y

    def hidden():
        return _norm_mod(x_tile(), nw_ref[...], sc_ref[...], sh_ref[...]).astype(BF16)

    @pl.when(s == 0)
    def _():
        h_scr[...] = hidden()
        acc_scr[...] = jnp.zeros_like(acc_scr)

    @pl.when(s < FFN_NC)
    def _():
        wg = wg_ref[...].astype(BF16)
        wu = wu_ref[...].astype(BF16)
        wd = wd_ref[...].astype(BF16)
        wg_s[s] = wg
        wu_s[s] = wu
        wd_s[pl.ds(pl.multiple_of(s * FFN_CHUNK, FFN_CHUNK), FFN_CHUNK), :] = wd
        acc_scr[...] += _dot(chunk_act(h_scr[...], wg, wu), wd)

    @pl.when(s == FFN_NC - 1)
    def _():
        finish(acc_scr[...])

    @pl.when(s >= FFN_NC)
    def _():
        h = hidden()
        act = jnp.concatenate([chunk_act(h, wg_s[j], wu_s[j]) for j in range(FFN_NC)], axis=1)
        finish(_dot(act, wd_s[...]))


def _ffn(xs, mod, layer, mod_base, nw, wg, wu, wd, final_norm, final):
    tm = FFN_TM
    npt = N_PROMPT // tm
    split_in = len(xs) == 2
    tile = lambda s: jnp.maximum(s - (FFN_NC - 1), 0)
    chunk = lambda s: jnp.minimum(s, FFN_NC - 1)
    tok = pl.BlockSpec((tm, D_MODEL), lambda s: (tile(s), 0))
    tok_p = pl.BlockSpec((tm, D_MODEL), lambda s: (jnp.minimum(tile(s), npt - 1), 0))
    tok_s = pl.BlockSpec((tm, D_MODEL), lambda s: (jnp.maximum(tile(s) - npt, 0), 0))
    half = jax.ShapeDtypeStruct((N_PROMPT, D_MODEL), F32)
    return pl.pallas_call(
        functools.partial(_ffn_kernel, final=final, split_in=split_in),
        grid=(FFN_NC + N_TOK // tm - 1,),
        in_specs=([tok_p, tok_s] if split_in else [tok]) + [
            _mod_spec(layer, mod_base + 0, tm, tile),
            _mod_spec(layer, mod_base + 1, tm, tile),
            _mod_spec(layer, mod_base + 2, tm, tile),
            pl.BlockSpec((None, 1, D_MODEL), lambda s: (layer, 0, 0)),
            pl.BlockSpec((None, D_MODEL, FFN_CHUNK), lambda s: (layer, 0, chunk(s))),
            pl.BlockSpec((None, D_MODEL, FFN_CHUNK), lambda s: (layer, 0, chunk(s))),
            pl.BlockSpec((None, FFN_CHUNK, D_MODEL), lambda s: (layer, chunk(s), 0)),
            pl.BlockSpec((1, D_MODEL), lambda s: (0, 0)),
        ],
        out_specs=[tok_p, tok_s] if final else tok,
        out_shape=[half, half] if final else jax.ShapeDtypeStruct((N_TOK, D_MODEL), F32),
        scratch_shapes=[
            pltpu.VMEM((FFN_NC, D_MODEL, FFN_CHUNK), BF16),
            pltpu.VMEM((FFN_NC, D_MODEL, FFN_CHUNK), BF16),
            pltpu.VMEM((D_FF, D_MODEL), BF16),
            pltpu.VMEM((tm, D_MODEL), BF16),
            pltpu.VMEM((tm, D_MODEL), F32),
        ],
        compiler_params=pltpu.CompilerParams(
            dimension_semantics=("arbitrary",), vmem_limit_bytes=VMEM_LIMIT),
        name="ffn_final" if final else ("ffn_first" if split_in else "ffn"),
    )(*xs, mod, mod, mod, nw.reshape(DEPTH, 1, D_MODEL), wg, wu, wd, final_norm.reshape(1, D_MODEL))


PROJ_TM = 512
PROJ_SEQS = PROJ_TM // SEQ
PROJ_W = 1024
ROPE_COL = D_FOUR + Q_RANK + KV_RANK


def _pair_swap(x):
    lane = lax.broadcasted_iota(jnp.int32, x.shape, 1)
    return jnp.where(lane % 16 < 8, pltpu.roll(x, HEAD_PAD - 8, 1), pltpu.roll(x, 8, 1))


def _proj_kernel(*refs):
    (x_ref, sh_ref, sc_ref, nw_ref, w1_ref, qn_ref, wq_ref, pq_ref, kvn_ref,
     cq_ref, sq_ref, ck_ref, sk_ref) = refs[:13]
    f_ref, q_ref, ckv_ref, kr_ref, nckv_ref, nkr_ref, w1_s, wq_s = refs[-8:]
    is_prompt = pl.program_id(0) < N_PROMPT // PROJ_TM

    @pl.when(pl.program_id(0) == 0)
    def _():
        w1_s[...] = w1_ref[...].astype(BF16)
        wq_s[...] = _dot(wq_ref[...].astype(BF16), pq_ref[...]).astype(BF16)

    h = _norm_mod(x_ref[...], nw_ref[...], sc_ref[...], sh_ref[...]).astype(BF16)
    p = _dot(h, w1_s[...])
    f_ref[...] = p[:, :D_FOUR]

    ql = p[:, D_FOUR:D_FOUR + Q_RANK]
    qn = ql * lax.rsqrt(jnp.mean(ql * ql, axis=-1, keepdims=True) + EPS) * qn_ref[...]
    qn = qn.astype(BF16)
    n_qp = N_HEADS * HEAD_PAD
    qq = _dot(qn, wq_s[:, :n_qp])

    c0 = D_FOUR + Q_RANK
    ckv = p[:, c0:c0 + KV_RANK]
    ckv_n = ckv * lax.rsqrt(jnp.mean(ckv * ckv, axis=-1, keepdims=True) + EPS) * kvn_ref[...]
    ckv_ref[...] = ckv_n

    kr = p[:, ROPE_COL:ROPE_COL + HEAD_PAD]
    cq, ck = cq_ref[...], ck_ref[...]

    @pl.when(is_prompt)
    def _():
        for hd in range(N_HEADS):
            lo = hd * HEAD_PAD
            q_ref[:, lo:lo + HEAD_PAD] = (qq[:, lo:lo + HEAD_PAD] * cq).astype(BF16)
        kr_ref[...] = (kr * ck).astype(BF16)
        nckv_ref[...] = ckv_n.reshape(PROJ_SEQS, SEQ, KV_RANK)
        nkr_ref[...] = kr[:, :QK_ROPE].reshape(PROJ_SEQS, SEQ, QK_ROPE)

    @pl.when(jnp.logical_not(is_prompt))
    def _():
        sq, sk = sq_ref[...], sk_ref[...]
        qs = _dot(qn, wq_s[:, n_qp:])
        for hd in range(N_HEADS):
            lo = hd * HEAD_PAD
            q_ref[:, lo:lo + HEAD_PAD] = (qq[:, lo:lo + HEAD_PAD] * cq + qs[:, lo:lo + HEAD_PAD] * sq).astype(BF16)
        kr_ref[...] = (kr * ck + _pair_swap(kr) * sk).astype(BF16)


def _q_select():
    n_qp = N_HEADS * HEAD_PAD
    p = np.zeros((N_HEADS * (QK_NOPE + QK_ROPE), 2 * n_qp), np.float32)
    for hd in range(N_HEADS):
        src, dst = hd * (QK_NOPE + QK_ROPE), hd * HEAD_PAD
        for j in range(QK_ROPE):
            partner = j + 8 if j % 16 < 8 else j - 8
            p[src + QK_NOPE + j, dst + j] = 1.0
            p[src + QK_NOPE + partner, n_qp + dst + j] = 1.0
        for j in range(QK_NOPE):
            p[src + j, dst + QK_ROPE + j] = 1.0
    return p.astype(BF16)


def _proj(x, mod, layer, nw, w_in, qn, w_uq, kvn, tabs, prev_caches):
    tm = PROJ_TM
    n_prompt_tiles = N_PROMPT // tm
    per_batch = DEC_SEQ // tm
    n_q = N_HEADS * (QK_NOPE + QK_ROPE)

    def tab_idx(i):
        return (jnp.where(i < n_prompt_tiles, per_batch, (i - n_prompt_tiles) % per_batch), 0)

    tok = lambda i: (i, 0)
    const = lambda i: (0, 0)
    cache_idx = lambda i: (jnp.minimum(i, n_prompt_tiles - 1), layer, 0, 0)
    tab_spec = pl.BlockSpec((tm, HEAD_PAD), tab_idx)
    return pl.pallas_call(
        _proj_kernel,
        grid=(N_TOK // tm,),
        in_specs=[
            pl.BlockSpec((tm, D_MODEL), tok),
            _mod_spec(layer, 3, tm),
            _mod_spec(layer, 4, tm),
            pl.BlockSpec((None, 1, D_MODEL), lambda i: (layer, 0, 0)),
            pl.BlockSpec((None, D_MODEL, PROJ_W), lambda i: (layer, 0, 0)),
            pl.BlockSpec((None, 1, Q_RANK), lambda i: (layer, 0, 0)),
            pl.BlockSpec((None, Q_RANK, n_q), lambda i: (layer, 0, 0)),
            pl.BlockSpec((n_q, 2 * N_HEADS * HEAD_PAD), const),
            pl.BlockSpec((None, 1, KV_RANK), lambda i: (layer, 0, 0)),
            tab_spec, tab_spec, tab_spec, tab_spec,
        ] + [pl.BlockSpec(memory_space=pl.ANY)] * len(prev_caches),
        out_specs=[
            pl.BlockSpec((tm, D_FOUR), tok),
            pl.BlockSpec((tm, N_HEADS * HEAD_PAD), tok),
            pl.BlockSpec((tm, KV_RANK), tok),
            pl.BlockSpec((tm, HEAD_PAD), tok),
            pl.BlockSpec((PROJ_SEQS, None, SEQ, KV_RANK), cache_idx),
            pl.BlockSpec((PROJ_SEQS, None, SEQ, QK_ROPE), cache_idx),
        ],
        out_shape=[
            jax.ShapeDtypeStruct((N_TOK, D_FOUR), F32),
            jax.ShapeDtypeStruct((N_TOK, N_HEADS * HEAD_PAD), BF16),
            jax.ShapeDtypeStruct((N_TOK, KV_RANK), F32),
            jax.ShapeDtypeStruct((N_TOK, HEAD_PAD), BF16),
            jax.ShapeDtypeStruct((BATCH, DEPTH, SEQ, KV_RANK), F32),
            jax.ShapeDtypeStruct((BATCH, DEPTH, SEQ, QK_ROPE), F32),
        ],
        input_output_aliases={13 + k: 4 + k for k in range(len(prev_caches))},
        scratch_shapes=[
            pltpu.VMEM((D_MODEL, PROJ_W), BF16),
            pltpu.VMEM((Q_RANK, 2 * N_HEADS * HEAD_PAD), BF16),
        ],
        compiler_params=pltpu.CompilerParams(
            dimension_semantics=("arbitrary",), vmem_limit_bytes=VMEM_LIMIT),
        name="mixer_proj",
    )(x, mod, mod, nw.reshape(DEPTH, 1, D_MODEL), w_in, qn.reshape(DEPTH, 1, Q_RANK), w_uq, _q_select(),
      kvn.reshape(DEPTH, 1, KV_RANK), *tabs(tm), *prev_caches)


DFT_ROWS = 1024


def _split(x):
    hi = x.astype(BF16)
    lo = (x - hi.astype(F32)).astype(BF16)
    return hi, lo


def _dft_kernel(x_ref, gh_ref, gl_ref, th_ref, tl_ref, *rest, n):
    o_ref = rest[-1]
    xh, xl = _split(x_ref[...])
    gh, gl = gh_ref[...], gl_ref[...]
    y = _dot(xh, gh) + _dot(xl, gh) + _dot(xh, gl)
    yh, yl = _split(y)
    th, tl = th_ref[...], tl_ref[...]
    for b in range(x_ref.shape[0] // n):
        r = slice(b * n, (b + 1) * n)
        ych = jnp.concatenate([yh[r, :D_FOUR], yh[r, D_FOUR:]], axis=0)
        ycl = jnp.concatenate([yl[r, :D_FOUR], yl[r, D_FOUR:]], axis=0)
        fr = _dot(th, ych) + _dot(tl, ych) + _dot(th, ycl)
        o_ref[r, :] = fr.astype(BF16)


def _split_np(a):
    a32 = np.asarray(a, np.float32)
    hi = a32.astype(BF16)
    lo = (a32 - hi.astype(np.float32)).astype(BF16)
    return hi, lo


def _dft_tables(n):
    k = np.arange(n, dtype=np.int64)
    ang = 2.0 * np.pi * ((k[:, None] * k[None, :]) % n).astype(np.float64) / n
    t = np.concatenate([np.cos(ang), -np.sin(ang)], axis=1) / np.sqrt(n)
    c = np.arange(FOUR_GROUP_DIM, dtype=np.int64)
    ang_c = 2.0 * np.pi * ((c[:, None] * c[None, :]) % FOUR_GROUP_DIM).astype(np.float64) / FOUR_GROUP_DIM
    eye = np.eye(FOUR_GROUPS)
    g = np.concatenate([np.kron(eye, np.cos(ang_c)), np.kron(eye, np.sin(ang_c))], axis=1)
    g = g / np.sqrt(FOUR_GROUP_DIM)
    return _split_np(g) + _split_np(t)


def _alias_prev(prev, n_inputs):
    if prev is None:
        return [], {}, []
    return [pl.BlockSpec(memory_space=pl.ANY)], {n_inputs: 0}, [prev]


def _dft(f_in, n_batch, n, row0, prev=None):
    gh, gl, th, tl = _dft_tables(n)
    rows = max(n, DFT_ROWS)
    blk0 = row0 // rows
    const = lambda b: (0, 0)
    prev_spec, aliases, prev_arg = _alias_prev(prev, 5)
    return pl.pallas_call(
        functools.partial(_dft_kernel, n=n),
        grid=(n_batch * n // rows,),
        in_specs=[
            pl.BlockSpec((rows, D_FOUR), lambda b: (blk0 + b, 0)),
            pl.BlockSpec((D_FOUR, 2 * D_FOUR), const),
            pl.BlockSpec((D_FOUR, 2 * D_FOUR), const),
            pl.BlockSpec((n, 2 * n), const),
            pl.BlockSpec((n, 2 * n), const),
        ] + prev_spec,
        out_specs=pl.BlockSpec((rows, D_FOUR), lambda b: (blk0 + b, 0)),
        out_shape=jax.ShapeDtypeStruct((N_TOK, D_FOUR), BF16),
        input_output_aliases=aliases,
        compiler_params=pltpu.CompilerParams(
            dimension_semantics=("arbitrary",), vmem_limit_bytes=VMEM_LIMIT),
        name="fnet_dft_%d" % n,
    )(f_in, gh, gl, th, tl, *prev_arg)


def _attn_kernel(*refs, n_own, n_ctx):
    if n_ctx:
        q_ref, ckv_ref, kr_ref, cckv_ref, ckr_ref, wkv_ref, pkt_ref, pv_ref = refs[:8]
    else:
        q_ref, ckv_ref, kr_ref, wkv_ref, pkt_ref, pv_ref = refs[:6]
    o_ref, kt_scr, v_scr, wkt_s, wv_s = refs[-5:]
    n_keys = n_own + n_ctx

    @pl.when(jnp.logical_and(pl.program_id(0) == 0, pl.program_id(1) == 0))
    def _():
        wkv = wkv_ref[...].astype(BF16)
        wkt_s[...] = _dot_nt(pkt_ref[...], wkv).astype(BF16)
        wv_s[...] = _dot(wkv, pv_ref[...]).astype(BF16)

    @pl.when(pl.program_id(1) == 0)
    def _():
        ckv = ckv_ref[...].astype(BF16)
        kr = kr_ref[...]
        if n_ctx:
            ckv = jnp.concatenate([ckv, cckv_ref[...].astype(BF16)], axis=0)
            kr = jnp.concatenate([kr, ckr_ref[...].astype(BF16)], axis=0)
        knt = _dot_nt(wkt_s[...], ckv).astype(BF16)
        v = _dot(ckv, wv_s[...]).astype(BF16)
        rows = lax.broadcasted_iota(jnp.int32, (HEAD_PAD, HEAD_PAD), 0)
        cols = lax.broadcasted_iota(jnp.int32, (HEAD_PAD, HEAD_PAD), 1)
        eye = (rows == cols).astype(BF16)
        krt = _dot_nt(eye, kr).astype(BF16)
        zeros = jnp.zeros((HEAD_PAD - QK_ROPE - QK_NOPE, n_keys), BF16)
        for hd in range(N_HEADS):
            lo = hd * HEAD_PAD
            kt_scr[lo:lo + QK_ROPE, :] = krt[:QK_ROPE]
            kt_scr[lo + QK_ROPE:lo + QK_ROPE + QK_NOPE, :] = knt[hd * QK_NOPE:(hd + 1) * QK_NOPE]
            kt_scr[lo + QK_ROPE + QK_NOPE:lo + HEAD_PAD, :] = zeros
            v_scr[hd] = v[:, hd * V_DIM:(hd + 1) * V_DIM]

    outs = []
    for hd in range(N_HEADS):
        lo = hd * HEAD_PAD
        s = _dot(q_ref[:, lo:lo + HEAD_PAD], kt_scr[lo:lo + HEAD_PAD, :])
        m = jnp.max(s, axis=-1, keepdims=True)
        p = jnp.exp(s - m)
        l = jnp.sum(p, axis=-1, keepdims=True)
        outs.append(_dot(p.astype(BF16), v_scr[hd]) / l)
    o_ref[...] = jnp.concatenate(outs, axis=1).astype(BF16)


def _kv_select():
    pkt = np.zeros((N_HEADS * QK_NOPE, N_HEADS * (QK_NOPE + V_DIM)), np.float32)
    pv = np.zeros((N_HEADS * (QK_NOPE + V_DIM), N_HEADS * V_DIM), np.float32)
    for hd in range(N_HEADS):
        src = hd * (QK_NOPE + V_DIM)
        for j in range(QK_NOPE):
            pkt[hd * QK_NOPE + j, src + j] = 1.0
        for j in range(V_DIM):
            pv[src + QK_NOPE + j, hd * V_DIM + j] = 1.0
    return pkt.astype(BF16), pv.astype(BF16)


def _attention(q, ckv_n, kr, w_ukv, layer, n_batch, n_own, row0, ctx=None, prev=None):
    tq = 256
    n_kv = N_HEADS * (QK_NOPE + V_DIM)
    n_ctx = 0 if ctx is None else PAST_LEN
    n_keys = n_own + n_ctx
    qt = n_own // tq
    blk0 = row0 // n_own
    const = lambda b, t: (0, 0)
    in_specs = [
        pl.BlockSpec((tq, N_HEADS * HEAD_PAD), lambda b, t: ((row0 // tq) + b * qt + t, 0)),
        pl.BlockSpec((n_own, KV_RANK), lambda b, t: (blk0 + b, 0)),
        pl.BlockSpec((n_own, HEAD_PAD), lambda b, t: (blk0 + b, 0)),
    ]
    args = [q, ckv_n, kr]
    if ctx is not None:
        cckv, ckr = ctx
        in_specs += [
            pl.BlockSpec((None, None, PAST_LEN, KV_RANK), lambda b, t: (b, layer, 0, 0)),
            pl.BlockSpec((None, None, PAST_LEN, HEAD_PAD), lambda b, t: (b, layer, 0, 0)),
        ]
        args += [cckv, ckr]
    in_specs += [
        pl.BlockSpec((None, KV_RANK, n_kv), lambda b, t: (layer, 0, 0)),
        pl.BlockSpec((N_HEADS * QK_NOPE, n_kv), const),
        pl.BlockSpec((n_kv, N_HEADS * V_DIM), const),
    ]
    args += [w_ukv, *_kv_select()]
    prev_spec, aliases, prev_arg = _alias_prev(prev, len(args))
    return pl.pallas_call(
        functools.partial(_attn_kernel, n_own=n_own, n_ctx=n_ctx),
        grid=(n_batch, qt),
        in_specs=in_specs + prev_spec,
        out_specs=pl.BlockSpec((tq, N_HEADS * V_DIM), lambda b, t: ((row0 // tq) + b * qt + t, 0)),
        out_shape=jax.ShapeDtypeStruct((N_TOK, N_HEADS * V_DIM), BF16),
        input_output_aliases=aliases,
        scratch_shapes=[
            pltpu.VMEM((N_HEADS * HEAD_PAD, n_keys), BF16),
            pltpu.VMEM((N_HEADS, n_keys, V_DIM), BF16),
            pltpu.VMEM((N_HEADS * QK_NOPE, KV_RANK), BF16),
            pltpu.VMEM((KV_RANK, N_HEADS * V_DIM), BF16),
        ],
        compiler_params=pltpu.CompilerParams(
            dimension_semantics=("arbitrary", "arbitrary"), vmem_limit_bytes=VMEM_LIMIT),
        name="mla_attention_%d" % n_keys,
    )(*args, *prev_arg)


TAIL_TM = 512
TAIL_SLABS = 4
GATE_COL = ROPE_COL + QK_ROPE
W_IN_BLOCKS = 3


def _tail_kernel(x_ref, sh_ref, sc_ref, g_ref, nw_ref, fr_ref, o_ref, wi0_ref, wi1_ref, wi2_ref,
                 wf_ref, wa_ref, wo_ref, out_ref, wg_s, wf_s, wa_s, wo_s):
    s = pl.program_id(0)

    @pl.when(s < TAIL_SLABS)
    def _():
        def put(dst, src):
            rows = src.shape[0]
            dst[pl.ds(pl.multiple_of(s * rows, rows), rows), :] = src.astype(BF16)

        wi = jnp.concatenate([wi0_ref[...], wi1_ref[...], wi2_ref[...]], axis=1)
        put(wg_s, wi[:, GATE_COL:GATE_COL + 2 * D_MODEL])
        put(wf_s, wf_ref[...])
        put(wa_s, wa_ref[...])
        put(wo_s, wo_ref[...])

    @pl.when(s >= TAIL_SLABS)
    def _():
        x = x_ref[...]
        h = _norm_mod(x, nw_ref[...], sc_ref[...], sh_ref[...]).astype(BF16)
        gates = _dot(h, wg_s[...])
        a_out = _dot(fr_ref[...], wf_s[...])
        o_out = _dot(o_ref[...], wa_s[...])
        merged = jax.nn.sigmoid(gates[:, :D_MODEL]) * a_out + jax.nn.sigmoid(gates[:, D_MODEL:]) * o_out
        m = _dot(merged.astype(BF16), wo_s[...])
        out_ref[...] = x + g_ref[...] * m


def _tail(x, mod, layer, nw, fr, o, w_in, w_four, w_attn_proj, w_out):
    tm = TAIL_TM
    tile = lambda s: jnp.maximum(s - TAIL_SLABS, 0)
    slab = lambda s: jnp.minimum(s, TAIL_SLABS - 1)
    tok = lambda s: (tile(s), 0)
    n_attn = N_HEADS * V_DIM

    def slab_spec(rows, cols, col_block=0):
        return pl.BlockSpec((None, rows // TAIL_SLABS, cols), lambda s: (layer, slab(s), col_block))

    return pl.pallas_call(
        _tail_kernel,
        grid=(TAIL_SLABS + N_TOK // tm,),
        in_specs=[
            pl.BlockSpec((tm, D_MODEL), tok),
            _mod_spec(layer, 3, tm, tile),
            _mod_spec(layer, 4, tm, tile),
            _mod_spec(layer, 5, tm, tile),
            pl.BlockSpec((None, 1, D_MODEL), lambda s: (layer, 0, 0)),
            pl.BlockSpec((tm, D_FOUR), tok),
            pl.BlockSpec((tm, n_attn), tok),
        ] + [slab_spec(D_MODEL, D_MODEL, cb) for cb in range(W_IN_BLOCKS)] + [
            slab_spec(D_FOUR, D_MODEL),
            slab_spec(n_attn, D_MODEL),
            slab_spec(D_MODEL, D_MODEL),
        ],
        out_specs=pl.BlockSpec((tm, D_MODEL), tok),
        out_shape=jax.ShapeDtypeStruct((N_TOK, D_MODEL), F32),
        scratch_shapes=[
            pltpu.VMEM((D_MODEL, 2 * D_MODEL), BF16),
            pltpu.VMEM((D_FOUR, D_MODEL), BF16),
            pltpu.VMEM((n_attn, D_MODEL), BF16),
            pltpu.VMEM((D_MODEL, D_MODEL), BF16),
        ],
        compiler_params=pltpu.CompilerParams(
            dimension_semantics=("arbitrary",), vmem_limit_bytes=VMEM_LIMIT),
        name="mixer_tail",
    )(x, mod, mod, mod, nw.reshape(DEPTH, 1, D_MODEL), fr, o, w_in, w_in, w_in, w_four, w_attn_proj, w_out)


def _rope_tables(tm):
    rows = DEC_SEQ // GRID_W
    row = np.repeat(np.arange(rows), GRID_W).astype(np.float64)
    col = np.tile(np.arange(GRID_W), rows).astype(np.float64)
    axis_dim = QK_ROPE // 2
    inv = ROPE_BASE ** (-np.arange(0, axis_dim, 2, dtype=np.float64) / axis_dim)
    ar = row[:, None] * inv
    ac = col[:, None] * inv
    cr, sr, cc, sc = np.cos(ar), np.sin(ar), np.cos(ac), np.sin(ac)
    cos32 = np.concatenate([cr, cr, cc, cc], axis=1)
    sin32 = np.concatenate([-sr, sr, -sc, sc], axis=1)
    scale = np.float32((QK_NOPE + QK_ROPE) ** -0.5)

    def table(rope_part, nope_val, ident_rope):
        t = np.zeros((DEC_SEQ + tm, HEAD_PAD), np.float32)
        t[:DEC_SEQ, :QK_ROPE] = rope_part
        t[:DEC_SEQ, QK_ROPE:QK_ROPE + QK_NOPE] = nope_val
        t[DEC_SEQ:, :QK_ROPE] = ident_rope
        t[DEC_SEQ:, QK_ROPE:QK_ROPE + QK_NOPE] = nope_val
        return t

    cq = table(cos32, 1.0, 1.0) * scale
    sq = table(sin32, 0.0, 0.0) * scale
    ck = table(cos32, 0.0, 1.0)
    sk = table(sin32, 0.0, 0.0)
    return tuple(jnp.asarray(t, F32) for t in (cq, sq, ck, sk))


def kernel(x_prompt, x_sample, cache_ckv, cache_krope, c, c_ctx, w_mod, b_mod, norm_ffn1, w_ffn1_gate,
           w_ffn1_up, w_ffn1_down, norm_mix, w_in, w_four, q_norm, w_uq, kv_norm, w_ukv, w_attn_proj,
           w_out, norm_ffn2, w_ffn2_gate, w_ffn2_up, w_ffn2_down, final_norm):
    xs = (x_prompt.reshape(N_PROMPT, D_MODEL), x_sample.reshape(N_SAMPLE, D_MODEL))
    c_all = jnp.concatenate(
        [c_ctx[None, :], c, jnp.zeros((MOD_ROWS - 1 - DEC_BATCH, D_MODEL), F32)], axis=0)
    mod = _modulation(c_all, w_mod, b_mod).reshape(DEPTH * MOD_ROWS * N_MOD, 1, D_MODEL)
    cache_kr = jnp.pad(cache_krope, ((0, 0), (0, 0), (0, 0), (0, HEAD_PAD - QK_ROPE)))

    caches = ()
    for l in range(DEPTH):
        x = _ffn(xs, mod, l, 0, norm_ffn1, w_ffn1_gate, w_ffn1_up, w_ffn1_down, final_norm, False)
        f_in, q, ckv_n, kr, new_ckv, new_krope = _proj(
            x, mod, l, norm_mix, w_in, q_norm, w_uq, kv_norm, _rope_tables, caches)
        caches = (new_ckv, new_krope)
        fr = _dft(f_in, BATCH, SEQ, 0)
        fr = _dft(f_in, DEC_BATCH, DEC_SEQ, N_PROMPT, prev=fr)
        o = _attention(q, ckv_n, kr, w_ukv, l, BATCH, SEQ, 0)
        o = _attention(q, ckv_n, kr, w_ukv, l, DEC_BATCH, DEC_SEQ, N_PROMPT, (cache_ckv, cache_kr), prev=o)
        x = _tail(x, mod, l, norm_mix, fr, o, w_in, w_four, w_attn_proj, w_out)
        xs = _ffn((x,), mod, l, 6, norm_ffn2, w_ffn2_gate, w_ffn2_up, w_ffn2_down, final_norm, l == DEPTH - 1)
        if l < DEPTH - 1:
            xs = (xs,)

    y_prompt = xs[0].reshape(BATCH, SEQ, D_MODEL)
    y_sample = xs[1].reshape(DEC_BATCH, DEC_SEQ, D_MODEL)
    return y_prompt, y_sample, caches[0], caches[1]
```

```python
import functools

import numpy as np
import jax
import jax.numpy as jnp
from jax import lax
from jax.experimental import pallas as pl
from jax.experimental.pallas import tpu as pltpu

D_MODEL = 1024
BATCH = 16
SEQ = 256
DEPTH = 2
DEC_BATCH = 4
DEC_SEQ = 1024
PAST_LEN = 512
GRID_W = 64
D_FF = 2816
FOUR_GROUPS = 4
FOUR_GROUP_DIM = 64
D_FOUR = FOUR_GROUPS * FOUR_GROUP_DIM
N_HEADS = 8
QK_NOPE = 64
QK_ROPE = 32
V_DIM = 64
Q_RANK = 384
KV_RANK = 256
N_MOD = 9
ROPE_BASE = 10000.0
EPS = 1e-6

N_PROMPT = BATCH * SEQ
N_SAMPLE = DEC_BATCH * DEC_SEQ
N_TOK = N_PROMPT + N_SAMPLE
MOD_ROWS = 8
HEAD_PAD = 128
KEYS_SAMPLE = DEC_SEQ + PAST_LEN

VMEM_LIMIT = 52 * 1024 * 1024

F32 = jnp.float32
BF16 = jnp.bfloat16


def _dot(a, b):
    return jnp.dot(a, b, preferred_element_type=F32)


def _dot_nt(a, b):
    return lax.dot_general(a, b, (((1,), (1,)), ((), ())), preferred_element_type=F32)


def _mod_row(i, tm):
    n_prompt_tiles = N_PROMPT // tm
    per_batch = DEC_SEQ // tm
    return jnp.where(i < n_prompt_tiles, 0, 1 + (i - n_prompt_tiles) // per_batch)


def _mod_spec(layer, which, tm, tile_of_step=lambda i: i):
    def idx(*g):
        return ((layer * MOD_ROWS + _mod_row(tile_of_step(g[0]), tm)) * N_MOD + which, 0, 0)
    return pl.BlockSpec((None, 1, D_MODEL), idx)


def _norm_mod(x, nw, sc, sh):
    ms = jnp.mean(x * x, axis=-1, keepdims=True)
    y = x * lax.rsqrt(ms + EPS) * nw
    return y * (1.0 + sc) + sh


def _mod_kernel(c_ref, w_ref, b_ref, o_ref):
    c = c_ref[...]
    a = (c * jax.nn.sigmoid(c)).astype(BF16)
    o_ref[...] = _dot(a, w_ref[...].astype(BF16)) + b_ref[...]


def _modulation(c_all, w_mod, b_mod):
    tn = 1024
    n_out = N_MOD * D_MODEL
    return pl.pallas_call(
        _mod_kernel,
        grid=(DEPTH, n_out // tn),
        in_specs=[
            pl.BlockSpec((MOD_ROWS, D_MODEL), lambda l, j: (0, 0)),
            pl.BlockSpec((None, D_MODEL, tn), lambda l, j: (l, 0, j)),
            pl.BlockSpec((None, 1, tn), lambda l, j: (l, 0, j)),
        ],
        out_specs=pl.BlockSpec((None, MOD_ROWS, tn), lambda l, j: (l, 0, j)),
        out_shape=jax.ShapeDtypeStruct((DEPTH, MOD_ROWS, n_out), F32),
        compiler_params=pltpu.CompilerParams(
            dimension_semantics=("arbitrary", "arbitrary"), vmem_limit_bytes=VMEM_LIMIT),
        name="modulation",
    )(c_all, w_mod, b_mod.reshape(DEPTH, 1, n_out))


FFN_TM = 512
FFN_CHUNK = 256
FFN_NC = D_FF // FFN_CHUNK


def _ffn_kernel(*refs, final, split_in):
    refs = list(refs)
    x_refs = [refs.pop(0) for _ in range(2 if split_in else 1)]
    sh_ref, sc_ref, g_ref, nw_ref, wg_ref, wu_ref, wd_ref, fn_ref = refs[:8]
    o_refs = refs[8:10] if final else refs[8:9]
    wg_s, wu_s, wd_s, h_scr, acc_scr = refs[-5:]
    s = pl.program_id(0)
    is_prompt = jnp.maximum(s - (FFN_NC - 1), 0) < N_PROMPT // FFN_TM

    def x_tile():
        if split_in:
            return jnp.where(is_prompt, x_refs[0][...], x_refs[1][...])
        return x_refs[0][...]

    def chunk_act(h, wg, wu):
        gate = _dot(h, wg)
        up = _dot(h, wu)
        return ((gate * jax.nn.sigmoid(gate)) * up).astype(BF16)

    def finish(acc):
        xn = x_tile() + 0.5 * g_ref[...] * acc
        if not final:
            o_refs[0][...] = xn
            return
        ms = jnp.mean(xn * xn, axis=-1, keepdims=True)
        y = xn * lax.rsqrt(ms + EPS) * fn_ref[...]

        @pl.when(is_prompt)
        def _():
            o_refs[0][...] = y

        @pl.when(jnp.logical_not(is_prompt))
        def _():
            o_refs[1][...] = y

    def hidden():
        return _norm_mod(x_tile(), nw_ref[...], sc_ref[...], sh_ref[...]).astype(BF16)

    @pl.when(s == 0)
    def _():
        h_scr[...] = hidden()
        acc_scr[...] = jnp.zeros_like(acc_scr)

    @pl.when(s < FFN_NC)
    def _():
        wg = wg_ref[...].astype(BF16)
        wu = wu_ref[...].astype(BF16)
        wd = wd_ref[...].astype(BF16)
        wg_s[s] = wg
        wu_s[s] = wu
        wd_s[pl.ds(pl.multiple_of(s * FFN_CHUNK, FFN_CHUNK), FFN_CHUNK), :] = wd
        acc_scr[...] += _dot(chunk_act(h_scr[...], wg, wu), wd)

    @pl.when(s == FFN_NC - 1)
    def _():
        finish(acc_scr[...])

    @pl.when(s >= FFN_NC)
    def _():
        h = hidden()
        act = jnp.concatenate([chunk_act(h, wg_s[j], wu_s[j]) for j in range(FFN_NC)], axis=1)
        finish(_dot(act, wd_s[...]))


def _ffn(xs, mod, layer, mod_base, nw, wg, wu, wd, final_norm, final):
    tm = FFN_TM
    npt = N_PROMPT // tm
    split_in = len(xs) == 2
    tile = lambda s: jnp.maximum(s - (FFN_NC - 1), 0)
    chunk = lambda s: jnp.minimum(s, FFN_NC - 1)
    tok = pl.BlockSpec((tm, D_MODEL), lambda s: (tile(s), 0))
    tok_p = pl.BlockSpec((tm, D_MODEL), lambda s: (jnp.minimum(tile(s), npt - 1), 0))
    tok_s = pl.BlockSpec((tm, D_MODEL), lambda s: (jnp.maximum(tile(s) - npt, 0), 0))
    half = jax.ShapeDtypeStruct((N_PROMPT, D_MODEL), F32)
    return pl.pallas_call(
        functools.partial(_ffn_kernel, final=final, split_in=split_in),
        grid=(FFN_NC + N_TOK // tm - 1,),
        in_specs=([tok_p, tok_s] if split_in else [tok]) + [
            _mod_spec(layer, mod_base + 0, tm, tile),
            _mod_spec(layer, mod_base + 1, tm, tile),
            _mod_spec(layer, mod_base + 2, tm, tile),
            pl.BlockSpec((None, 1, D_MODEL), lambda s: (layer, 0, 0)),
            pl.BlockSpec((None, D_MODEL, FFN_CHUNK), lambda s: (layer, 0, chunk(s))),
            pl.BlockSpec((None, D_MODEL, FFN_CHUNK), lambda s: (layer, 0, chunk(s))),
            pl.BlockSpec((None, FFN_CHUNK, D_MODEL), lambda s: (layer, chunk(s), 0)),
            pl.BlockSpec((1, D_MODEL), lambda s: (0, 0)),
        ],
        out_specs=[tok_p, tok_s] if final else tok,
        out_shape=[half, half] if final else jax.ShapeDtypeStruct((N_TOK, D_MODEL), F32),
        scratch_shapes=[
            pltpu.VMEM((FFN_NC, D_MODEL, FFN_CHUNK), BF16),
            pltpu.VMEM((FFN_NC, D_MODEL, FFN_CHUNK), BF16),
            pltpu.VMEM((D_FF, D_MODEL), BF16),
            pltpu.VMEM((tm, D_MODEL), BF16),
            pltpu.VMEM((tm, D_MODEL), F32),
        ],
        compiler_params=pltpu.CompilerParams(
            dimension_semantics=("arbitrary",), vmem_limit_bytes=VMEM_LIMIT),
        name="ffn_final" if final else ("ffn_first" if split_in else "ffn"),
    )(*xs, mod, mod, mod, nw.reshape(DEPTH, 1, D_MODEL), wg, wu, wd, final_norm.reshape(1, D_MODEL))


PROJ_TM = 512
PROJ_SEQS = PROJ_TM // SEQ
PROJ_W = 1024
ROPE_COL = D_FOUR + Q_RANK + KV_RANK


def _pair_swap(x):
    lane = lax.broadcasted_iota(jnp.int32, x.shape, 1)
    return jnp.where(lane % 16 < 8, pltpu.roll(x, HEAD_PAD - 8, 1), pltpu.roll(x, 8, 1))


def _proj_kernel(*refs):
    (x_ref, sh_ref, sc_ref, nw_ref, w1_ref, qn_ref, wq_ref, pq_ref, kvn_ref,
     cq_ref, sq_ref, ck_ref, sk_ref) = refs[:13]
    f_ref, q_ref, ckv_ref, kr_ref, nckv_ref, nkr_ref, w1_s, wq_s = refs[-8:]
    is_prompt = pl.program_id(0) < N_PROMPT // PROJ_TM

    @pl.when(pl.program_id(0) == 0)
    def _():
        w1_s[...] = w1_ref[...].astype(BF16)
        wq_s[...] = _dot_nt(pq_ref[...], wq_ref[...].astype(BF16)).astype(BF16)

    h = _norm_mod(x_ref[...], nw_ref[...], sc_ref[...], sh_ref[...]).astype(BF16)
    p = _dot(h, w1_s[...])
    f_ref[...] = p[:, :D_FOUR]

    ql = p[:, D_FOUR:D_FOUR + Q_RANK]
    qn = ql * lax.rsqrt(jnp.mean(ql * ql, axis=-1, keepdims=True) + EPS) * qn_ref[...]
    qn = qn.astype(BF16)
    n_qp = N_HEADS * HEAD_PAD
    qq = _dot_nt(wq_s[:n_qp, :], qn)

    c0 = D_FOUR + Q_RANK
    ckv = p[:, c0:c0 + KV_RANK]
    ckv_n = ckv * lax.rsqrt(jnp.mean(ckv * ckv, axis=-1, keepdims=True) + EPS) * kvn_ref[...]
    ckv_ref[...] = ckv_n

    kr = p[:, ROPE_COL:ROPE_COL + HEAD_PAD]
    cq, ck = cq_ref[...], ck_ref[...]

    @pl.when(is_prompt)
    def _():
        for hd in range(N_HEADS):
            lo = hd * HEAD_PAD
            q_ref[lo:lo + HEAD_PAD, :] = (qq[lo:lo + HEAD_PAD, :] * cq).astype(BF16)
        kr_ref[...] = (kr * ck).astype(BF16)
        nckv_ref[...] = ckv_n.reshape(PROJ_SEQS, SEQ, KV_RANK)
        nkr_ref[...] = kr[:, :QK_ROPE].reshape(PROJ_SEQS, SEQ, QK_ROPE)

    @pl.when(jnp.logical_not(is_prompt))
    def _():
        sq, sk = sq_ref[...], sk_ref[...]
        qs = _dot_nt(wq_s[n_qp:, :], qn)
        for hd in range(N_HEADS):
            lo = hd * HEAD_PAD
            q_ref[lo:lo + HEAD_PAD, :] = (qq[lo:lo + HEAD_PAD, :] * cq + qs[lo:lo + HEAD_PAD, :] * sq).astype(BF16)
        kr_ref[...] = (kr * ck + _pair_swap(kr) * sk).astype(BF16)


def _q_select():
    n_qp = N_HEADS * HEAD_PAD
    p = np.zeros((2 * n_qp, N_HEADS * (QK_NOPE + QK_ROPE)), np.float32)
    for hd in range(N_HEADS):
        src, dst = hd * (QK_NOPE + QK_ROPE), hd * HEAD_PAD
        for j in range(QK_ROPE):
            partner = j + 8 if j % 16 < 8 else j - 8
            p[dst + j, src + QK_NOPE + j] = 1.0
            p[n_qp + dst + j, src + QK_NOPE + partner] = 1.0
        for j in range(QK_NOPE):
            p[dst + QK_ROPE + j, src + j] = 1.0
    return p.astype(BF16)


def _proj(x, mod, layer, nw, w_in, qn, w_uq, kvn, tabs, prev_caches):
    tm = PROJ_TM
    n_prompt_tiles = N_PROMPT // tm
    per_batch = DEC_SEQ // tm
    n_q = N_HEADS * (QK_NOPE + QK_ROPE)

    def tab_blk(i):
        return jnp.where(i < n_prompt_tiles, per_batch, (i - n_prompt_tiles) % per_batch)

    tok = lambda i: (i, 0)
    const = lambda i: (0, 0)
    cache_idx = lambda i: (jnp.minimum(i, n_prompt_tiles - 1), layer, 0, 0)
    tab_spec = pl.BlockSpec((tm, HEAD_PAD), lambda i: (tab_blk(i), 0))
    tab_t_spec = pl.BlockSpec((HEAD_PAD, tm), lambda i: (0, tab_blk(i)))
    return pl.pallas_call(
        _proj_kernel,
        grid=(N_TOK // tm,),
        in_specs=[
            pl.BlockSpec((tm, D_MODEL), tok),
            _mod_spec(layer, 3, tm),
            _mod_spec(layer, 4, tm),
            pl.BlockSpec((None, 1, D_MODEL), lambda i: (layer, 0, 0)),
            pl.BlockSpec((None, D_MODEL, PROJ_W), lambda i: (layer, 0, 0)),
            pl.BlockSpec((None, 1, Q_RANK), lambda i: (layer, 0, 0)),
            pl.BlockSpec((None, Q_RANK, n_q), lambda i: (layer, 0, 0)),
            pl.BlockSpec((2 * N_HEADS * HEAD_PAD, n_q), const),
            pl.BlockSpec((None, 1, KV_RANK), lambda i: (layer, 0, 0)),
            tab_t_spec, tab_t_spec, tab_spec, tab_spec,
        ] + [pl.BlockSpec(memory_space=pl.ANY)] * len(prev_caches),
        out_specs=[
            pl.BlockSpec((tm, D_FOUR), tok),
            pl.BlockSpec((N_HEADS * HEAD_PAD, tm), lambda i: (0, i)),
            pl.BlockSpec((tm, KV_RANK), tok),
            pl.BlockSpec((tm, HEAD_PAD), tok),
            pl.BlockSpec((PROJ_SEQS, None, SEQ, KV_RANK), cache_idx),
            pl.BlockSpec((PROJ_SEQS, None, SEQ, QK_ROPE), cache_idx),
        ],
        out_shape=[
            jax.ShapeDtypeStruct((N_TOK, D_FOUR), F32),
            jax.ShapeDtypeStruct((N_HEADS * HEAD_PAD, N_TOK), BF16),
            jax.ShapeDtypeStruct((N_TOK, KV_RANK), F32),
            jax.ShapeDtypeStruct((N_TOK, HEAD_PAD), BF16),
            jax.ShapeDtypeStruct((BATCH, DEPTH, SEQ, KV_RANK), F32),
            jax.ShapeDtypeStruct((BATCH, DEPTH, SEQ, QK_ROPE), F32),
        ],
        input_output_aliases={13 + k: 4 + k for k in range(len(prev_caches))},
        scratch_shapes=[
            pltpu.VMEM((D_MODEL, PROJ_W), BF16),
            pltpu.VMEM((2 * N_HEADS * HEAD_PAD, Q_RANK), BF16),
        ],
        compiler_params=pltpu.CompilerParams(
            dimension_semantics=("arbitrary",), vmem_limit_bytes=VMEM_LIMIT),
        name="mixer_proj",
    )(x, mod, mod, nw.reshape(DEPTH, 1, D_MODEL), w_in, qn.reshape(DEPTH, 1, Q_RANK), w_uq, _q_select(),
      kvn.reshape(DEPTH, 1, KV_RANK), *tabs(tm), *prev_caches)


DFT_ROWS = 1024


def _split(x):
    hi = x.astype(BF16)
    lo = (x - hi.astype(F32)).astype(BF16)
    return hi, lo


def _dft_kernel(x_ref, gh_ref, gl_ref, th_ref, tl_ref, *rest, n):
    o_ref = rest[-1]
    xh, xl = _split(x_ref[...])
    gh, gl = gh_ref[...], gl_ref[...]
    y = _dot(xh, gh) + _dot(xl, gh) + _dot(xh, gl)
    yh, yl = _split(y)
    th, tl = th_ref[...], tl_ref[...]
    for b in range(x_ref.shape[0] // n):
        r = slice(b * n, (b + 1) * n)
        ych = jnp.concatenate([yh[r, :D_FOUR], yh[r, D_FOUR:]], axis=0)
        ycl = jnp.concatenate([yl[r, :D_FOUR], yl[r, D_FOUR:]], axis=0)
        fr = _dot(th, ych) + _dot(tl, ych) + _dot(th, ycl)
        o_ref[r, :] = fr.astype(BF16)


def _split_np(a):
    a32 = np.asarray(a, np.float32)
    hi = a32.astype(BF16)
    lo = (a32 - hi.astype(np.float32)).astype(BF16)
    return hi, lo


def _dft_tables(n):
    k = np.arange(n, dtype=np.int64)
    ang = 2.0 * np.pi * ((k[:, None] * k[None, :]) % n).astype(np.float64) / n
    t = np.concatenate([np.cos(ang), -np.sin(ang)], axis=1) / np.sqrt(n)
    c = np.arange(FOUR_GROUP_DIM, dtype=np.int64)
    ang_c = 2.0 * np.pi * ((c[:, None] * c[None, :]) % FOUR_GROUP_DIM).astype(np.float64) / FOUR_GROUP_DIM
    eye = np.eye(FOUR_GROUPS)
    g = np.concatenate([np.kron(eye, np.cos(ang_c)), np.kron(eye, np.sin(ang_c))], axis=1)
    g = g / np.sqrt(FOUR_GROUP_DIM)
    return _split_np(g) + _split_np(t)


def _alias_prev(prev, n_inputs):
    if prev is None:
        return [], {}, []
    return [pl.BlockSpec(memory_space=pl.ANY)], {n_inputs: 0}, [prev]


def _dft(f_in, n_batch, n, row0, prev=None):
    gh, gl, th, tl = _dft_tables(n)
    rows = max(n, DFT_ROWS)
    blk0 = row0 // rows
    const = lambda b: (0, 0)
    prev_spec, aliases, prev_arg = _alias_prev(prev, 5)
    return pl.pallas_call(
        functools.partial(_dft_kernel, n=n),
        grid=(n_batch * n // rows,),
        in_specs=[
            pl.BlockSpec((rows, D_FOUR), lambda b: (blk0 + b, 0)),
            pl.BlockSpec((D_FOUR, 2 * D_FOUR), const),
            pl.BlockSpec((D_FOUR, 2 * D_FOUR), const),
            pl.BlockSpec((n, 2 * n), const),
            pl.BlockSpec((n, 2 * n), const),
        ] + prev_spec,
        out_specs=pl.BlockSpec((rows, D_FOUR), lambda b: (blk0 + b, 0)),
        out_shape=jax.ShapeDtypeStruct((N_TOK, D_FOUR), BF16),
        input_output_aliases=aliases,
        compiler_params=pltpu.CompilerParams(
            dimension_semantics=("arbitrary",), vmem_limit_bytes=VMEM_LIMIT),
        name="fnet_dft_%d" % n,
    )(f_in, gh, gl, th, tl, *prev_arg)


def _attn_kernel(*refs, n_own, n_ctx):
    if n_ctx:
        qt_ref, ckv_ref, kr_ref, cckv_ref, ckr_ref, wkv_ref, pk_ref, pvt_ref = refs[:8]
    else:
        qt_ref, ckv_ref, kr_ref, wkv_ref, pk_ref, pvt_ref = refs[:6]
    o_ref, k_scr, vt_scr, wk_s, wvt_s = refs[-5:]

    @pl.when(jnp.logical_and(pl.program_id(0) == 0, pl.program_id(1) == 0))
    def _():
        wkv = wkv_ref[...].astype(BF16)
        wk_s[...] = _dot(wkv, pk_ref[...]).astype(BF16)
        wvt_s[...] = _dot_nt(pvt_ref[...], wkv).astype(BF16)

    @pl.when(pl.program_id(1) == 0)
    def _():
        ckv = ckv_ref[...].astype(BF16)
        kr = kr_ref[...]
        if n_ctx:
            ckv = jnp.concatenate([ckv, cckv_ref[...].astype(BF16)], axis=0)
            kr = jnp.concatenate([kr, ckr_ref[...].astype(BF16)], axis=0)
        k_nope = _dot(ckv, wk_s[...])
        kr = kr.astype(F32)
        for hd in range(N_HEADS):
            lo = hd * HEAD_PAD
            k_scr[:, lo:lo + HEAD_PAD] = (k_nope[:, lo:lo + HEAD_PAD] + kr).astype(BF16)
        vt_scr[...] = _dot_nt(wvt_s[...], ckv).astype(BF16)

    def scores(hd):
        lo = hd * HEAD_PAD
        return _dot(k_scr[:, lo:lo + HEAD_PAD], qt_ref[lo:lo + HEAD_PAD, :])

    outs = []
    ahead = 3
    queue = [scores(hd) for hd in range(ahead)]
    for hd in range(N_HEADS):
        s = queue.pop(0)
        if hd + ahead < N_HEADS:
            queue.append(scores(hd + ahead))
        m = jnp.max(s, axis=0, keepdims=True)
        p = jnp.exp(s - m)
        l = jnp.sum(p, axis=0, keepdims=True)
        outs.append(_dot(vt_scr[hd * V_DIM:(hd + 1) * V_DIM, :], p.astype(BF16)) / l)
    o_ref[...] = jnp.concatenate(outs, axis=0).T.astype(BF16)


def _kv_select():
    n_kv = N_HEADS * (QK_NOPE + V_DIM)
    pk = np.zeros((n_kv, N_HEADS * HEAD_PAD), np.float32)
    pvt = np.zeros((N_HEADS * V_DIM, n_kv), np.float32)
    for hd in range(N_HEADS):
        src = hd * (QK_NOPE + V_DIM)
        for j in range(QK_NOPE):
            pk[src + j, hd * HEAD_PAD + QK_ROPE + j] = 1.0
        for j in range(V_DIM):
            pvt[hd * V_DIM + j, src + QK_NOPE + j] = 1.0
    return pk.astype(BF16), pvt.astype(BF16)


def _attention(q, ckv_n, kr, w_ukv, layer, n_batch, n_own, row0, ctx=None, prev=None):
    tq = 256
    n_kv = N_HEADS * (QK_NOPE + V_DIM)
    n_ctx = 0 if ctx is None else PAST_LEN
    n_keys = n_own + n_ctx
    qt = n_own // tq
    blk0 = row0 // n_own
    const = lambda b, t: (0, 0)
    in_specs = [
        pl.BlockSpec((N_HEADS * HEAD_PAD, tq), lambda b, t: (0, (row0 // tq) + b * qt + t)),
        pl.BlockSpec((n_own, KV_RANK), lambda b, t: (blk0 + b, 0)),
        pl.BlockSpec((n_own, HEAD_PAD), lambda b, t: (blk0 + b, 0)),
    ]
    args = [q, ckv_n, kr]
    if ctx is not None:
        cckv, ckr = ctx
        in_specs += [
            pl.BlockSpec((None, None, PAST_LEN, KV_RANK), lambda b, t: (b, layer, 0, 0)),
            pl.BlockSpec((None, None, PAST_LEN, HEAD_PAD), lambda b, t: (b, layer, 0, 0)),
        ]
        args += [cckv, ckr]
    in_specs += [
        pl.BlockSpec((None, KV_RANK, n_kv), lambda b, t: (layer, 0, 0)),
        pl.BlockSpec((n_kv, N_HEADS * HEAD_PAD), const),
        pl.BlockSpec((N_HEADS * V_DIM, n_kv), const),
    ]
    args += [w_ukv, *_kv_select()]
    prev_spec, aliases, prev_arg = _alias_prev(prev, len(args))
    return pl.pallas_call(
        functools.partial(_attn_kernel, n_own=n_own, n_ctx=n_ctx),
        grid=(n_batch, qt),
        in_specs=in_specs + prev_spec,
        out_specs=pl.BlockSpec((tq, N_HEADS * V_DIM), lambda b, t: ((row0 // tq) + b * qt + t, 0)),
        out_shape=jax.ShapeDtypeStruct((N_TOK, N_HEADS * V_DIM), BF16),
        input_output_aliases=aliases,
        scratch_shapes=[
            pltpu.VMEM((n_keys, N_HEADS * HEAD_PAD), BF16),
            pltpu.VMEM((N_HEADS * V_DIM, n_keys), BF16),
            pltpu.VMEM((KV_RANK, N_HEADS * HEAD_PAD), BF16),
            pltpu.VMEM((N_HEADS * V_DIM, KV_RANK), BF16),
        ],
        compiler_params=pltpu.CompilerParams(
            dimension_semantics=("arbitrary", "arbitrary"), vmem_limit_bytes=VMEM_LIMIT),
        name="mla_attention_%d" % n_keys,
    )(*args, *prev_arg)


TAIL_TM = 512
TAIL_SLABS = 4
GATE_COL = ROPE_COL + QK_ROPE
W_IN_BLOCKS = 3


def _tail_kernel(x_ref, sh_ref, sc_ref, g_ref, nw_ref, fr_ref, o_ref, wi0_ref, wi1_ref, wi2_ref,
                 wf_ref, wa_ref, wo_ref, out_ref, wg_s, wf_s, wa_s, wo_s):
    s = pl.program_id(0)

    @pl.when(s < TAIL_SLABS)
    def _():
        def put(dst, src):
            rows = src.shape[0]
            dst[pl.ds(pl.multiple_of(s * rows, rows), rows), :] = src.astype(BF16)

        wi = jnp.concatenate([wi0_ref[...], wi1_ref[...], wi2_ref[...]], axis=1)
        put(wg_s, wi[:, GATE_COL:GATE_COL + 2 * D_MODEL])
        put(wf_s, wf_ref[...])
        put(wa_s, wa_ref[...])
        put(wo_s, wo_ref[...])

    @pl.when(s >= TAIL_SLABS)
    def _():
        x = x_ref[...]
        h = _norm_mod(x, nw_ref[...], sc_ref[...], sh_ref[...]).astype(BF16)
        gates = _dot(h, wg_s[...])
        a_out = _dot(fr_ref[...], wf_s[...])
        o_out = _dot(o_ref[...], wa_s[...])
        merged = jax.nn.sigmoid(gates[:, :D_MODEL]) * a_out + jax.nn.sigmoid(gates[:, D_MODEL:]) * o_out
        m = _dot(merged.astype(BF16), wo_s[...])
        out_ref[...] = x + g_ref[...] * m


def _tail(x, mod, layer, nw, fr, o, w_in, w_four, w_attn_proj, w_out):
    tm = TAIL_TM
    tile = lambda s: jnp.maximum(s - TAIL_SLABS, 0)
    slab = lambda s: jnp.minimum(s, TAIL_SLABS - 1)
    tok = lambda s: (tile(s), 0)
    n_attn = N_HEADS * V_DIM

    def slab_spec(rows, cols, col_block=0):
        return pl.BlockSpec((None, rows // TAIL_SLABS, cols), lambda s: (layer, slab(s), col_block))

    return pl.pallas_call(
        _tail_kernel,
        grid=(TAIL_SLABS + N_TOK // tm,),
        in_specs=[
            pl.BlockSpec((tm, D_MODEL), tok),
            _mod_spec(layer, 3, tm, tile),
            _mod_spec(layer, 4, tm, tile),
            _mod_spec(layer, 5, tm, tile),
            pl.BlockSpec((None, 1, D_MODEL), lambda s: (layer, 0, 0)),
            pl.BlockSpec((tm, D_FOUR), tok),
            pl.BlockSpec((tm, n_attn), tok),
        ] + [slab_spec(D_MODEL, D_MODEL, cb) for cb in range(W_IN_BLOCKS)] + [
            slab_spec(D_FOUR, D_MODEL),
            slab_spec(n_attn, D_MODEL),
            slab_spec(D_MODEL, D_MODEL),
        ],
        out_specs=pl.BlockSpec((tm, D_MODEL), tok),
        out_shape=jax.ShapeDtypeStruct((N_TOK, D_MODEL), F32),
        scratch_shapes=[
            pltpu.VMEM((D_MODEL, 2 * D_MODEL), BF16),
            pltpu.VMEM((D_FOUR, D_MODEL), BF16),
            pltpu.VMEM((n_attn, D_MODEL), BF16),
            pltpu.VMEM((D_MODEL, D_MODEL), BF16),
        ],
        compiler_params=pltpu.CompilerParams(
            dimension_semantics=("arbitrary",), vmem_limit_bytes=VMEM_LIMIT),
        name="mixer_tail",
    )(x, mod, mod, mod, nw.reshape(DEPTH, 1, D_MODEL), fr, o, w_in, w_in, w_in, w_four, w_attn_proj, w_out)


def _rope_tables(tm):
    rows = DEC_SEQ // GRID_W
    row = np.repeat(np.arange(rows), GRID_W).astype(np.float64)
    col = np.tile(np.arange(GRID_W), rows).astype(np.float64)
    axis_dim = QK_ROPE // 2
    inv = ROPE_BASE ** (-np.arange(0, axis_dim, 2, dtype=np.float64) / axis_dim)
    ar = row[:, None] * inv
    ac = col[:, None] * inv
    cr, sr, cc, sc = np.cos(ar), np.sin(ar), np.cos(ac), np.sin(ac)
    cos32 = np.concatenate([cr, cr, cc, cc], axis=1)
    sin32 = np.concatenate([-sr, sr, -sc, sc], axis=1)
    scale = np.float32((QK_NOPE + QK_ROPE) ** -0.5)

    def table(rope_part, nope_val, ident_rope):
        t = np.zeros((DEC_SEQ + tm, HEAD_PAD), np.float32)
        t[:DEC_SEQ, :QK_ROPE] = rope_part
        t[:DEC_SEQ, QK_ROPE:QK_ROPE + QK_NOPE] = nope_val
        t[DEC_SEQ:, :QK_ROPE] = ident_rope
        t[DEC_SEQ:, QK_ROPE:QK_ROPE + QK_NOPE] = nope_val
        return t

    cq = (table(cos32, 1.0, 1.0) * scale).T
    sq = (table(sin32, 0.0, 0.0) * scale).T
    ck = table(cos32, 0.0, 1.0)
    sk = table(sin32, 0.0, 0.0)
    return tuple(jnp.asarray(np.ascontiguousarray(t), F32) for t in (cq, sq, ck, sk))


def kernel(x_prompt, x_sample, cache_ckv, cache_krope, c, c_ctx, w_mod, b_mod, norm_ffn1, w_ffn1_gate,
           w_ffn1_up, w_ffn1_down, norm_mix, w_in, w_four, q_norm, w_uq, kv_norm, w_ukv, w_attn_proj,
           w_out, norm_ffn2, w_ffn2_gate, w_ffn2_up, w_ffn2_down, final_norm):
    xs = (x_prompt.reshape(N_PROMPT, D_MODEL), x_sample.reshape(N_SAMPLE, D_MODEL))
    c_all = jnp.concatenate(
        [c_ctx[None, :], c, jnp.zeros((MOD_ROWS - 1 - DEC_BATCH, D_MODEL), F32)], axis=0)
    mod = _modulation(c_all, w_mod, b_mod).reshape(DEPTH * MOD_ROWS * N_MOD, 1, D_MODEL)
    cache_kr = jnp.pad(cache_krope, ((0, 0), (0, 0), (0, 0), (0, HEAD_PAD - QK_ROPE)))

    caches = ()
    for l in range(DEPTH):
        x = _ffn(xs, mod, l, 0, norm_ffn1, w_ffn1_gate, w_ffn1_up, w_ffn1_down, final_norm, False)
        f_in, q, ckv_n, kr, new_ckv, new_krope = _proj(
            x, mod, l, norm_mix, w_in, q_norm, w_uq, kv_norm, _rope_tables, caches)
        caches = (new_ckv, new_krope)
        fr = _dft(f_in, BATCH, SEQ, 0)
        fr = _dft(f_in, DEC_BATCH, DEC_SEQ, N_PROMPT, prev=fr)
        o = _attention(q, ckv_n, kr, w_ukv, l, BATCH, SEQ, 0)
        o = _attention(q, ckv_n, kr, w_ukv, l, DEC_BATCH, DEC_SEQ, N_PROMPT, (cache_ckv, cache_kr), prev=o)
        x = _tail(x, mod, l, norm_mix, fr, o, w_in, w_four, w_attn_proj, w_out)
        xs = _ffn((x,), mod, l, 6, norm_ffn2, w_ffn2_gate, w_ffn2_up, w_ffn2_down, final_norm, l == DEPTH - 1)
        if l < DEPTH - 1:
            xs = (xs,)

    y_prompt = xs[0].reshape(BATCH, SEQ, D_MODEL)
    y_sample = xs[1].reshape(DEC_BATCH, DEC_SEQ, D_MODEL)
    return y_prompt, y_sample, caches[0], caches[1]
```

```python
import functools

import numpy as np
import jax
import jax.numpy as jnp
from jax import lax
from jax.experimental import pallas as pl
from jax.experimental.pallas import tpu as pltpu

D_MODEL = 1024
BATCH = 16
SEQ = 256
DEPTH = 2
DEC_BATCH = 4
DEC_SEQ = 1024
PAST_LEN = 512
GRID_W = 64
D_FF = 2816
FOUR_GROUPS = 4
FOUR_GROUP_DIM = 64
D_FOUR = FOUR_GROUPS * FOUR_GROUP_DIM
N_HEADS = 8
QK_NOPE = 64
QK_ROPE = 32
V_DIM = 64
Q_RANK = 384
KV_RANK = 256
N_MOD = 9
ROPE_BASE = 10000.0
EPS = 1e-6

N_PROMPT = BATCH * SEQ
N_SAMPLE = DEC_BATCH * DEC_SEQ
N_TOK = N_PROMPT + N_SAMPLE
MOD_ROWS = 8
HEAD_PAD = 128
KEYS_SAMPLE = DEC_SEQ + PAST_LEN

VMEM_LIMIT = 52 * 1024 * 1024

F32 = jnp.float32
BF16 = jnp.bfloat16


def _dot(a, b):
    return jnp.dot(a, b, preferred_element_type=F32)


def _dot_nt(a, b):
    return lax.dot_general(a, b, (((1,), (1,)), ((), ())), preferred_element_type=F32)


def _mod_row(i, tm):
    n_prompt_tiles = N_PROMPT // tm
    per_batch = DEC_SEQ // tm
    return jnp.where(i < n_prompt_tiles, 0, 1 + (i - n_prompt_tiles) // per_batch)


def _mod_spec(layer, which, tm, tile_of_step=lambda i: i):
    def idx(*g):
        return ((layer * MOD_ROWS + _mod_row(tile_of_step(g[0]), tm)) * N_MOD + which, 0, 0)
    return pl.BlockSpec((None, 1, D_MODEL), idx)


def _norm_mod(x, nw, sc, sh):
    ms = jnp.mean(x * x, axis=-1, keepdims=True)
    y = x * lax.rsqrt(ms + EPS) * nw
    return y * (1.0 + sc) + sh


def _mod_kernel(c_ref, w_ref, b_ref, o_ref):
    c = c_ref[...]
    a = (c * jax.nn.sigmoid(c)).astype(BF16)
    o_ref[...] = _dot(a, w_ref[...].astype(BF16)) + b_ref[...]


def _modulation(c_all, w_mod, b_mod):
    tn = 1024
    n_out = N_MOD * D_MODEL
    return pl.pallas_call(
        _mod_kernel,
        grid=(DEPTH, n_out // tn),
        in_specs=[
            pl.BlockSpec((MOD_ROWS, D_MODEL), lambda l, j: (0, 0)),
            pl.BlockSpec((None, D_MODEL, tn), lambda l, j: (l, 0, j)),
            pl.BlockSpec((None, 1, tn), lambda l, j: (l, 0, j)),
        ],
        out_specs=pl.BlockSpec((None, MOD_ROWS, tn), lambda l, j: (l, 0, j)),
        out_shape=jax.ShapeDtypeStruct((DEPTH, MOD_ROWS, n_out), F32),
        compiler_params=pltpu.CompilerParams(
            dimension_semantics=("arbitrary", "arbitrary"), vmem_limit_bytes=VMEM_LIMIT),
        name="modulation",
    )(c_all, w_mod, b_mod.reshape(DEPTH, 1, n_out))


FFN_TM = 512
FFN_CHUNK = 256
FFN_NC = D_FF // FFN_CHUNK


def _ffn_kernel(*refs, final, split_in):
    refs = list(refs)
    x_refs = [refs.pop(0) for _ in range(2 if split_in else 1)]
    sh_ref, sc_ref, g_ref, nw_ref, wg_ref, wu_ref, wd_ref, fn_ref = refs[:8]
    o_refs = refs[8:10] if final else refs[8:9]
    wg_s, wu_s, wd_s, h_scr, acc_scr = refs[-5:]
    s = pl.program_id(0)
    is_prompt = jnp.maximum(s - (FFN_NC - 1), 0) < N_PROMPT // FFN_TM

    def x_tile():
        if split_in:
            return jnp.where(is_prompt, x_refs[0][...], x_refs[1][...])
        return x_refs[0][...]

    def chunk_act(h, wg, wu):
        gate = _dot(h, wg)
        up = _dot(h, wu)
        return ((gate * jax.nn.sigmoid(gate)) * up).astype(BF16)

    def finish(acc):
        xn = x_tile() + 0.5 * g_ref[...] * acc
        if not final:
            o_refs[0][...] = xn
            return
        ms = jnp.mean(xn * xn, axis=-1, keepdims=True)
        y = xn * lax.rsqrt(ms + EPS) * fn_ref[...]

        @pl.when(is_prompt)
        def _():
            o_refs[0][...] = y

        @pl.when(jnp.logical_not(is_prompt))
        def _():
            o_refs[1][...] = y

    def hidden():
        return _norm_mod(x_tile(), nw_ref[...], sc_ref[...], sh_ref[...]).astype(BF16)

    @pl.when(s == 0)
    def _():
        h_scr[...] = hidden()
        acc_scr[...] = jnp.zeros_like(acc_scr)

    @pl.when(s < FFN_NC)
    def _():
        wg = wg_ref[...].astype(BF16)
        wu = wu_ref[...].astype(BF16)
        wd = wd_ref[...].astype(BF16)
        wg_s[s] = wg
        wu_s[s] = wu
        wd_s[pl.ds(pl.multiple_of(s * FFN_CHUNK, FFN_CHUNK), FFN_CHUNK), :] = wd
        acc_scr[...] += _dot(chunk_act(h_scr[...], wg, wu), wd)

    @pl.when(s == FFN_NC - 1)
    def _():
        finish(acc_scr[...])

    @pl.when(s >= FFN_NC)
    def _():
        h = hidden()
        act = jnp.concatenate([chunk_act(h, wg_s[j], wu_s[j]) for j in range(FFN_NC)], axis=1)
        finish(_dot(act, wd_s[...]))


def _ffn(xs, mod, layer, mod_base, nw, wg, wu, wd, final_norm, final):
    tm = FFN_TM
    npt = N_PROMPT // tm
    split_in = len(xs) == 2
    tile = lambda s: jnp.maximum(s - (FFN_NC - 1), 0)
    chunk = lambda s: jnp.minimum(s, FFN_NC - 1)
    tok = pl.BlockSpec((tm, D_MODEL), lambda s: (tile(s), 0))
    tok_p = pl.BlockSpec((tm, D_MODEL), lambda s: (jnp.minimum(tile(s), npt - 1), 0))
    tok_s = pl.BlockSpec((tm, D_MODEL), lambda s: (jnp.maximum(tile(s) - npt, 0), 0))
    half = jax.ShapeDtypeStruct((N_PROMPT, D_MODEL), F32)
    return pl.pallas_call(
        functools.partial(_ffn_kernel, final=final, split_in=split_in),
        grid=(FFN_NC + N_TOK // tm - 1,),
        in_specs=([tok_p, tok_s] if split_in else [tok]) + [
            _mod_spec(layer, mod_base + 0, tm, tile),
            _mod_spec(layer, mod_base + 1, tm, tile),
            _mod_spec(layer, mod_base + 2, tm, tile),
            pl.BlockSpec((None, 1, D_MODEL), lambda s: (layer, 0, 0)),
            pl.BlockSpec((None, D_MODEL, FFN_CHUNK), lambda s: (layer, 0, chunk(s))),
            pl.BlockSpec((None, D_MODEL, FFN_CHUNK), lambda s: (layer, 0, chunk(s))),
            pl.BlockSpec((None, FFN_CHUNK, D_MODEL), lambda s: (layer, chunk(s), 0)),
            pl.BlockSpec((1, D_MODEL), lambda s: (0, 0)),
        ],
        out_specs=[tok_p, tok_s] if final else tok,
        out_shape=[half, half] if final else jax.ShapeDtypeStruct((N_TOK, D_MODEL), F32),
        scratch_shapes=[
            pltpu.VMEM((FFN_NC, D_MODEL, FFN_CHUNK), BF16),
            pltpu.VMEM((FFN_NC, D_MODEL, FFN_CHUNK), BF16),
            pltpu.VMEM((D_FF, D_MODEL), BF16),
            pltpu.VMEM((tm, D_MODEL), BF16),
            pltpu.VMEM((tm, D_MODEL), F32),
        ],
        compiler_params=pltpu.CompilerParams(
            dimension_semantics=("arbitrary",), vmem_limit_bytes=VMEM_LIMIT),
        name="ffn_final" if final else ("ffn_first" if split_in else "ffn"),
    )(*xs, mod, mod, mod, nw.reshape(DEPTH, 1, D_MODEL), wg, wu, wd, final_norm.reshape(1, D_MODEL))


PROJ_TM = 512
PROJ_SEQS = PROJ_TM // SEQ
PROJ_W = 1024
ROPE_COL = D_FOUR + Q_RANK + KV_RANK


def _pair_swap(x):
    lane = lax.broadcasted_iota(jnp.int32, x.shape, 1)
    return jnp.where(lane % 16 < 8, pltpu.roll(x, HEAD_PAD - 8, 1), pltpu.roll(x, 8, 1))


def _proj_kernel(*refs, first_layer):
    (x_ref, sh_ref, sc_ref, nw_ref, w1_ref, qn_ref, wq_ref, pq_ref, kvn_ref,
     cq_ref, sq_ref, ck_ref, sk_ref) = refs[:13]
    f_ref, q_ref, ckv_ref, kr_ref, nckv_ref, nkr_ref, w1_s, wq_s = refs[-8:]
    is_prompt = pl.program_id(0) < N_PROMPT // PROJ_TM

    @pl.when(pl.program_id(0) == 0)
    def _():
        w1_s[...] = w1_ref[...].astype(BF16)
        wq_s[...] = _dot_nt(pq_ref[...], wq_ref[...].astype(BF16)).astype(BF16)

    n_qp = N_HEADS * HEAD_PAD
    c0 = D_FOUR + Q_RANK
    groups = [slice(r * SEQ, (r + 1) * SEQ) for r in range(PROJ_SEQS)]
    ps = []
    for rows in groups:
        h = _norm_mod(x_ref[rows, :], nw_ref[...], sc_ref[...], sh_ref[...]).astype(BF16)
        ps.append(_dot_nt(h, w1_s[...]))
    staged = []
    for rows, p in zip(groups, ps):
        f_ref[rows, :] = p[:, :D_FOUR]
        ql = p[:, D_FOUR:c0]
        qn = ql * lax.rsqrt(jnp.mean(ql * ql, axis=-1, keepdims=True) + EPS) * qn_ref[...]
        qn = qn.astype(BF16)
        qq = _dot_nt(wq_s[:n_qp, :], qn)
        ckv = p[:, c0:c0 + KV_RANK]
        ckv_n = ckv * lax.rsqrt(jnp.mean(ckv * ckv, axis=-1, keepdims=True) + EPS) * kvn_ref[...]
        ckv_ref[rows, :] = ckv_n
        kr = p[:, ROPE_COL:ROPE_COL + HEAD_PAD]
        staged.append((qn, qq, ckv_n, kr))

    @pl.when(is_prompt)
    def _():
        for r, (rows, (qn, qq, ckv_n, kr)) in enumerate(zip(groups, staged)):
            cq = cq_ref[:, rows]
            for hd in range(N_HEADS):
                lo = hd * HEAD_PAD
                q_ref[lo:lo + HEAD_PAD, rows] = (qq[lo:lo + HEAD_PAD, :] * cq).astype(BF16)
            kr_ref[rows, :] = (kr * ck_ref[rows, :]).astype(BF16)
            if first_layer:
                nckv_ref[r, 0] = ckv_n
                nkr_ref[r, 0] = kr[:, :QK_ROPE]
                for later in range(1, DEPTH):
                    nckv_ref[r, later] = jnp.zeros_like(ckv_n)
                    nkr_ref[r, later] = jnp.zeros((SEQ, QK_ROPE), F32)
            else:
                nckv_ref[r] = ckv_n
                nkr_ref[r] = kr[:, :QK_ROPE]

    @pl.when(jnp.logical_not(is_prompt))
    def _():
        for rows, (qn, qq, ckv_n, kr) in zip(groups, staged):
            cq, sq = cq_ref[:, rows], sq_ref[:, rows]
            qs = _dot_nt(wq_s[n_qp:, :], qn)
            for hd in range(N_HEADS):
                lo, mid = hd * HEAD_PAD, hd * HEAD_PAD + QK_ROPE
                rope = qq[lo:mid, :] * cq[:QK_ROPE] + qs[hd * QK_ROPE:(hd + 1) * QK_ROPE, :] * sq[:QK_ROPE]
                q_ref[lo:mid, rows] = rope.astype(BF16)
                q_ref[mid:lo + HEAD_PAD, rows] = (qq[mid:lo + HEAD_PAD, :] * cq[QK_ROPE:]).astype(BF16)
            kr_ref[rows, :] = (kr * ck_ref[rows, :] + _pair_swap(kr) * sk_ref[rows, :]).astype(BF16)


def _q_select():
    n_qp = N_HEADS * HEAD_PAD
    p = np.zeros((n_qp + N_HEADS * QK_ROPE, N_HEADS * (QK_NOPE + QK_ROPE)), np.float32)
    for hd in range(N_HEADS):
        src, dst = hd * (QK_NOPE + QK_ROPE), hd * HEAD_PAD
        for j in range(QK_ROPE):
            partner = j + 8 if j % 16 < 8 else j - 8
            p[dst + j, src + QK_NOPE + j] = 1.0
            p[n_qp + hd * QK_ROPE + j, src + QK_NOPE + partner] = 1.0
        for j in range(QK_NOPE):
            p[dst + QK_ROPE + j, src + j] = 1.0
    return p.astype(BF16)


def _proj(x, mod, layer, nw, w_in_t, qn, w_uq, kvn, tabs, prev_caches):
    tm = PROJ_TM
    n_prompt_tiles = N_PROMPT // tm
    per_batch = DEC_SEQ // tm
    n_q = N_HEADS * (QK_NOPE + QK_ROPE)

    def tab_blk(i):
        return jnp.where(i < n_prompt_tiles, per_batch, (i - n_prompt_tiles) % per_batch)

    tok = lambda i: (i, 0)
    const = lambda i: (0, 0)
    first_layer = not prev_caches
    assert first_layer == (layer == 0)
    cache_layers = DEPTH if first_layer else None
    cache_idx = lambda i: (jnp.minimum(i, n_prompt_tiles - 1), 0 if first_layer else layer, 0, 0)
    tab_spec = pl.BlockSpec((tm, HEAD_PAD), lambda i: (tab_blk(i), 0))
    tab_t_spec = pl.BlockSpec((HEAD_PAD, tm), lambda i: (0, tab_blk(i)))
    return pl.pallas_call(
        functools.partial(_proj_kernel, first_layer=first_layer),
        grid=(N_TOK // tm,),
        in_specs=[
            pl.BlockSpec((tm, D_MODEL), tok),
            _mod_spec(layer, 3, tm),
            _mod_spec(layer, 4, tm),
            pl.BlockSpec((None, 1, D_MODEL), lambda i: (layer, 0, 0)),
            pl.BlockSpec((None, PROJ_W, D_MODEL), lambda i: (layer, 0, 0)),
            pl.BlockSpec((None, 1, Q_RANK), lambda i: (layer, 0, 0)),
            pl.BlockSpec((None, Q_RANK, n_q), lambda i: (layer, 0, 0)),
            pl.BlockSpec((N_HEADS * (HEAD_PAD + QK_ROPE), n_q), const),
            pl.BlockSpec((None, 1, KV_RANK), lambda i: (layer, 0, 0)),
            tab_t_spec, tab_t_spec, tab_spec, tab_spec,
        ] + [pl.BlockSpec(memory_space=pl.ANY)] * len(prev_caches),
        out_specs=[
            pl.BlockSpec((tm, D_FOUR), tok),
            pl.BlockSpec((N_HEADS * HEAD_PAD, tm), lambda i: (0, i)),
            pl.BlockSpec((tm, KV_RANK), tok),
            pl.BlockSpec((tm, HEAD_PAD), tok),
            pl.BlockSpec((PROJ_SEQS, cache_layers, SEQ, KV_RANK), cache_idx),
            pl.BlockSpec((PROJ_SEQS, cache_layers, SEQ, QK_ROPE), cache_idx),
        ],
        out_shape=[
            jax.ShapeDtypeStruct((N_TOK, D_FOUR), F32),
            jax.ShapeDtypeStruct((N_HEADS * HEAD_PAD, N_TOK), BF16),
            jax.ShapeDtypeStruct((N_TOK, KV_RANK), F32),
            jax.ShapeDtypeStruct((N_TOK, HEAD_PAD), BF16),
            jax.ShapeDtypeStruct((BATCH, DEPTH, SEQ, KV_RANK), F32),
            jax.ShapeDtypeStruct((BATCH, DEPTH, SEQ, QK_ROPE), F32),
        ],
        input_output_aliases={13 + k: 4 + k for k in range(len(prev_caches))},
        scratch_shapes=[
            pltpu.VMEM((PROJ_W, D_MODEL), BF16),
            pltpu.VMEM((N_HEADS * (HEAD_PAD + QK_ROPE), Q_RANK), BF16),
        ],
        compiler_params=pltpu.CompilerParams(
            dimension_semantics=("arbitrary",), vmem_limit_bytes=VMEM_LIMIT),
        name="mixer_proj",
    )(x, mod, mod, nw.reshape(DEPTH, 1, D_MODEL), w_in_t, qn.reshape(DEPTH, 1, Q_RANK), w_uq, _q_select(),
      kvn.reshape(DEPTH, 1, KV_RANK), *tabs(tm), *prev_caches)


DFT_ROWS = 1024


def _split(x):
    hi = x.astype(BF16)
    lo = (x - hi.astype(F32)).astype(BF16)
    return hi, lo


def _dft_kernel(x_ref, gh_ref, gl_ref, th_ref, tl_ref, o_ref, *, n):
    xh, xl = _split(x_ref[...])
    gh, gl = gh_ref[...], gl_ref[...]
    y = _dot(xh, gh) + _dot(xl, gh) + _dot(xh, gl)
    yh, yl = _split(y)
    th, tl = th_ref[...], tl_ref[...]
    for b in range(x_ref.shape[0] // n):
        r = slice(b * n, (b + 1) * n)
        ych = jnp.concatenate([yh[r, :D_FOUR], yh[r, D_FOUR:]], axis=0)
        ycl = jnp.concatenate([yl[r, :D_FOUR], yl[r, D_FOUR:]], axis=0)
        fr = _dot(th, ych) + _dot(tl, ych) + _dot(th, ycl)
        o_ref[r, :] = fr.astype(BF16)


def _split_np(a):
    a32 = np.asarray(a, np.float32)
    hi = a32.astype(BF16)
    lo = (a32 - hi.astype(np.float32)).astype(BF16)
    return hi, lo


def _dft_tables(n):
    k = np.arange(n, dtype=np.int64)
    ang = 2.0 * np.pi * ((k[:, None] * k[None, :]) % n).astype(np.float64) / n
    t = np.concatenate([np.cos(ang), -np.sin(ang)], axis=1) / np.sqrt(n)
    c = np.arange(FOUR_GROUP_DIM, dtype=np.int64)
    ang_c = 2.0 * np.pi * ((c[:, None] * c[None, :]) % FOUR_GROUP_DIM).astype(np.float64) / FOUR_GROUP_DIM
    eye = np.eye(FOUR_GROUPS)
    g = np.concatenate([np.kron(eye, np.cos(ang_c)), np.kron(eye, np.sin(ang_c))], axis=1)
    g = g / np.sqrt(FOUR_GROUP_DIM)
    return _split_np(g) + _split_np(t)


def _dft(f_in, n_batch, n, row0):
    gh, gl, th, tl = _dft_tables(n)
    rows = max(n, DFT_ROWS)
    blk0 = row0 // rows
    const = lambda b: (0, 0)
    return pl.pallas_call(
        functools.partial(_dft_kernel, n=n),
        grid=(n_batch * n // rows,),
        in_specs=[
            pl.BlockSpec((rows, D_FOUR), lambda b: (blk0 + b, 0)),
            pl.BlockSpec((D_FOUR, 2 * D_FOUR), const),
            pl.BlockSpec((D_FOUR, 2 * D_FOUR), const),
            pl.BlockSpec((n, 2 * n), const),
            pl.BlockSpec((n, 2 * n), const),
        ],
        out_specs=pl.BlockSpec((rows, D_FOUR), lambda b: (b, 0)),
        out_shape=jax.ShapeDtypeStruct((n_batch * n, D_FOUR), BF16),
        compiler_params=pltpu.CompilerParams(
            dimension_semantics=("arbitrary",), vmem_limit_bytes=VMEM_LIMIT),
        name="fnet_dft_%d" % n,
    )(f_in, gh, gl, th, tl)


def _attn_kernel(*refs, n_own, n_ctx):
    if n_ctx:
        qt_ref, ckv_ref, kr_ref, cckv_ref, ckr_ref, wkv_ref, pk_ref, pvt_ref = refs[:8]
    else:
        qt_ref, ckv_ref, kr_ref, wkv_ref, pk_ref, pvt_ref = refs[:6]
    o_ref, k_scr, vt_scr, wk_s, wvt_s = refs[-5:]

    @pl.when(jnp.logical_and(pl.program_id(0) == 0, pl.program_id(1) == 0))
    def _():
        wkv = wkv_ref[...].astype(BF16)
        wk_s[...] = _dot(wkv, pk_ref[...]).astype(BF16)
        wvt_s[...] = _dot_nt(pvt_ref[...], wkv).astype(BF16)

    @pl.when(pl.program_id(1) == 0)
    def _():
        ckv = ckv_ref[...].astype(BF16)
        kr = kr_ref[...]
        if n_ctx:
            ckv = jnp.concatenate([ckv, cckv_ref[...].astype(BF16)], axis=0)
            kr = jnp.concatenate([kr, ckr_ref[...].astype(BF16)], axis=0)
        k_nope = _dot(ckv, wk_s[...])
        kr = kr.astype(F32)
        for hd in range(N_HEADS):
            lo = hd * HEAD_PAD
            k_scr[:, lo:lo + HEAD_PAD] = (k_nope[:, lo:lo + HEAD_PAD] + kr).astype(BF16)
        vt_scr[...] = _dot_nt(wvt_s[...], ckv).astype(BF16)

    def scores(hd):
        lo = hd * HEAD_PAD
        return _dot(k_scr[:, lo:lo + HEAD_PAD], qt_ref[lo:lo + HEAD_PAD, :])

    outs = []
    ahead = 3
    queue = [scores(hd) for hd in range(ahead)]
    for hd in range(N_HEADS):
        s = queue.pop(0)
        if hd + ahead < N_HEADS:
            queue.append(scores(hd + ahead))
        m = jnp.max(s, axis=0, keepdims=True)
        p = jnp.exp(s - m)
        l = jnp.sum(p, axis=0, keepdims=True)
        outs.append(_dot(vt_scr[hd * V_DIM:(hd + 1) * V_DIM, :], p.astype(BF16)) / l)
    o_ref[...] = jnp.concatenate(outs, axis=0).T.astype(BF16)


def _kv_select():
    n_kv = N_HEADS * (QK_NOPE + V_DIM)
    pk = np.zeros((n_kv, N_HEADS * HEAD_PAD), np.float32)
    pvt = np.zeros((N_HEADS * V_DIM, n_kv), np.float32)
    for hd in range(N_HEADS):
        src = hd * (QK_NOPE + V_DIM)
        for j in range(QK_NOPE):
            pk[src + j, hd * HEAD_PAD + QK_ROPE + j] = 1.0
        for j in range(V_DIM):
            pvt[hd * V_DIM + j, src + QK_NOPE + j] = 1.0
    return pk.astype(BF16), pvt.astype(BF16)


def _attention(q, ckv_n, kr, w_ukv, layer, n_batch, n_own, row0, ctx=None):
    tq = 256
    n_kv = N_HEADS * (QK_NOPE + V_DIM)
    n_ctx = 0 if ctx is None else PAST_LEN
    n_keys = n_own + n_ctx
    qt = n_own // tq
    blk0 = row0 // n_own
    const = lambda b, t: (0, 0)
    in_specs = [
        pl.BlockSpec((N_HEADS * HEAD_PAD, tq), lambda b, t: (0, (row0 // tq) + b * qt + t)),
        pl.BlockSpec((n_own, KV_RANK), lambda b, t: (blk0 + b, 0)),
        pl.BlockSpec((n_own, HEAD_PAD), lambda b, t: (blk0 + b, 0)),
    ]
    args = [q, ckv_n, kr]
    if ctx is not None:
        cckv, ckr = ctx
        in_specs += [
            pl.BlockSpec((None, None, PAST_LEN, KV_RANK), lambda b, t: (b, layer, 0, 0)),
            pl.BlockSpec((None, None, PAST_LEN, HEAD_PAD), lambda b, t: (b, layer, 0, 0)),
        ]
        args += [cckv, ckr]
    in_specs += [
        pl.BlockSpec((None, KV_RANK, n_kv), lambda b, t: (layer, 0, 0)),
        pl.BlockSpec((n_kv, N_HEADS * HEAD_PAD), const),
        pl.BlockSpec((N_HEADS * V_DIM, n_kv), const),
    ]
    args += [w_ukv, *_kv_select()]
    return pl.pallas_call(
        functools.partial(_attn_kernel, n_own=n_own, n_ctx=n_ctx),
        grid=(n_batch, qt),
        in_specs=in_specs,
        out_specs=pl.BlockSpec((tq, N_HEADS * V_DIM), lambda b, t: (b * qt + t, 0)),
        out_shape=jax.ShapeDtypeStruct((n_batch * n_own, N_HEADS * V_DIM), BF16),
        scratch_shapes=[
            pltpu.VMEM((n_keys, N_HEADS * HEAD_PAD), BF16),
            pltpu.VMEM((N_HEADS * V_DIM, n_keys), BF16),
            pltpu.VMEM((KV_RANK, N_HEADS * HEAD_PAD), BF16),
            pltpu.VMEM((N_HEADS * V_DIM, KV_RANK), BF16),
        ],
        compiler_params=pltpu.CompilerParams(
            dimension_semantics=("arbitrary", "arbitrary"), vmem_limit_bytes=VMEM_LIMIT),
        name="mla_attention_%d" % n_keys,
    )(*args)


TAIL_TM = 512
TAIL_GROUP = 256
TAIL_SLABS = 4
GATE_ROW = ROPE_COL + QK_ROPE
GATE_CHUNK = 512
GATE_CHUNKS = 2 * D_MODEL // GATE_CHUNK


def _tail_kernel(x_ref, sh_ref, sc_ref, g_ref, nw_ref, frp_ref, frs_ref, op_ref, os_ref, wi_hbm,
                 wf_ref, wa_ref, wo_ref, out_ref, wg_s, wf_s, wa_s, wo_s, stage, sem, *, layer):
    s = pl.program_id(0)
    is_prompt = jnp.maximum(s - TAIL_SLABS, 0) < N_PROMPT // TAIL_TM

    def gate_copy(c):
        src = wi_hbm.at[layer, pl.ds(GATE_ROW + c * GATE_CHUNK, GATE_CHUNK), :]
        return pltpu.make_async_copy(src, stage.at[c % 2], sem.at[c % 2])

    @pl.when(s == 0)
    def _():
        gate_copy(0).start()
        for c in range(GATE_CHUNKS):
            if c + 1 < GATE_CHUNKS:
                gate_copy(c + 1).start()
            gate_copy(c).wait()
            wg_s[c * GATE_CHUNK:(c + 1) * GATE_CHUNK, :] = stage[c % 2].astype(BF16)

    @pl.when(s < TAIL_SLABS)
    def _():
        def put(dst, src):
            rows = src.shape[0]
            dst[pl.ds(pl.multiple_of(s * rows, rows), rows), :] = src.astype(BF16)

        put(wf_s, wf_ref[...])
        put(wa_s, wa_ref[...])
        put(wo_s, wo_ref[...])

    @pl.when(s >= TAIL_SLABS)
    def _():
        groups = [slice(r * TAIL_GROUP, (r + 1) * TAIL_GROUP) for r in range(TAIL_TM // TAIL_GROUP)]
        staged = []
        for rows in groups:
            x = x_ref[rows, :]
            h = _norm_mod(x, nw_ref[...], sc_ref[...], sh_ref[...]).astype(BF16)
            gates = _dot_nt(h, wg_s[...])
            a_out = _dot(jnp.where(is_prompt, frp_ref[rows, :], frs_ref[rows, :]), wf_s[...])
            o_out = _dot(jnp.where(is_prompt, op_ref[rows, :], os_ref[rows, :]), wa_s[...])
            staged.append((x, gates, a_out, o_out))
        for rows, (x, gates, a_out, o_out) in zip(groups, staged):
            merged = jax.nn.sigmoid(gates[:, :D_MODEL]) * a_out + jax.nn.sigmoid(gates[:, D_MODEL:]) * o_out
            m = _dot(merged.astype(BF16), wo_s[...])
            out_ref[rows, :] = x + g_ref[...] * m


def _tail(x, mod, layer, nw, frs, os_, w_in_t, w_four, w_attn_proj, w_out):
    tm = TAIL_TM
    npt = N_PROMPT // tm
    tile = lambda s: jnp.maximum(s - TAIL_SLABS, 0)
    slab = lambda s: jnp.minimum(s, TAIL_SLABS - 1)
    tok = lambda s: (tile(s), 0)
    tok_p = lambda s: (jnp.minimum(tile(s), npt - 1), 0)
    tok_s = lambda s: (jnp.maximum(tile(s) - npt, 0), 0)
    n_attn = N_HEADS * V_DIM

    def slab_spec(rows, cols):
        return pl.BlockSpec((None, rows // TAIL_SLABS, cols), lambda s: (layer, slab(s), 0))

    return pl.pallas_call(
        functools.partial(_tail_kernel, layer=layer),
        grid=(TAIL_SLABS + N_TOK // tm,),
        in_specs=[
            pl.BlockSpec((tm, D_MODEL), tok),
            _mod_spec(layer, 3, tm, tile),
            _mod_spec(layer, 4, tm, tile),
            _mod_spec(layer, 5, tm, tile),
            pl.BlockSpec((None, 1, D_MODEL), lambda s: (layer, 0, 0)),
            pl.BlockSpec((tm, D_FOUR), tok_p),
            pl.BlockSpec((tm, D_FOUR), tok_s),
            pl.BlockSpec((tm, n_attn), tok_p),
            pl.BlockSpec((tm, n_attn), tok_s),
            pl.BlockSpec(memory_space=pl.ANY),
            slab_spec(D_FOUR, D_MODEL),
            slab_spec(n_attn, D_MODEL),
            slab_spec(D_MODEL, D_MODEL),
        ],
        out_specs=pl.BlockSpec((tm, D_MODEL), tok),
        out_shape=jax.ShapeDtypeStruct((N_TOK, D_MODEL), F32),
        scratch_shapes=[
            pltpu.VMEM((2 * D_MODEL, D_MODEL), BF16),
            pltpu.VMEM((D_FOUR, D_MODEL), BF16),
            pltpu.VMEM((n_attn, D_MODEL), BF16),
            pltpu.VMEM((D_MODEL, D_MODEL), BF16),
            pltpu.VMEM((2, GATE_CHUNK, D_MODEL), F32),
            pltpu.SemaphoreType.DMA((2,)),
        ],
        compiler_params=pltpu.CompilerParams(
            dimension_semantics=("arbitrary",), vmem_limit_bytes=VMEM_LIMIT),
        name="mixer_tail",
    )(x, mod, mod, mod, nw.reshape(DEPTH, 1, D_MODEL), *frs, *os_, w_in_t, w_four, w_attn_proj, w_out)


def _rope_tables(tm):
    rows = DEC_SEQ // GRID_W
    row = np.repeat(np.arange(rows), GRID_W).astype(np.float64)
    col = np.tile(np.arange(GRID_W), rows).astype(np.float64)
    axis_dim = QK_ROPE // 2
    inv = ROPE_BASE ** (-np.arange(0, axis_dim, 2, dtype=np.float64) / axis_dim)
    ar = row[:, None] * inv
    ac = col[:, None] * inv
    cr, sr, cc, sc = np.cos(ar), np.sin(ar), np.cos(ac), np.sin(ac)
    cos32 = np.concatenate([cr, cr, cc, cc], axis=1)
    sin32 = np.concatenate([-sr, sr, -sc, sc], axis=1)
    scale = np.float32((QK_NOPE + QK_ROPE) ** -0.5)

    def table(rope_part, nope_val, ident_rope):
        t = np.zeros((DEC_SEQ + tm, HEAD_PAD), np.float32)
        t[:DEC_SEQ, :QK_ROPE] = rope_part
        t[:DEC_SEQ, QK_ROPE:QK_ROPE + QK_NOPE] = nope_val
        t[DEC_SEQ:, :QK_ROPE] = ident_rope
        t[DEC_SEQ:, QK_ROPE:QK_ROPE + QK_NOPE] = nope_val
        return t

    cq = (table(cos32, 1.0, 1.0) * scale).T
    sq = (table(sin32, 0.0, 0.0) * scale).T
    ck = table(cos32, 0.0, 1.0)
    sk = table(sin32, 0.0, 0.0)
    return tuple(jnp.asarray(np.ascontiguousarray(t), F32) for t in (cq, sq, ck, sk))


def kernel(x_prompt, x_sample, cache_ckv, cache_krope, c, c_ctx, w_mod, b_mod, norm_ffn1, w_ffn1_gate,
           w_ffn1_up, w_ffn1_down, norm_mix, w_in, w_four, q_norm, w_uq, kv_norm, w_ukv, w_attn_proj,
           w_out, norm_ffn2, w_ffn2_gate, w_ffn2_up, w_ffn2_down, final_norm):
    xs = (x_prompt.reshape(N_PROMPT, D_MODEL), x_sample.reshape(N_SAMPLE, D_MODEL))
    c_all = jnp.concatenate(
        [c_ctx[None, :], c, jnp.zeros((MOD_ROWS - 1 - DEC_BATCH, D_MODEL), F32)], axis=0)
    mod = _modulation(c_all, w_mod, b_mod).reshape(DEPTH * MOD_ROWS * N_MOD, 1, D_MODEL)
    cache_kr = jnp.pad(cache_krope, ((0, 0), (0, 0), (0, 0), (0, HEAD_PAD - QK_ROPE)))
    w_in_t = jnp.swapaxes(w_in, 1, 2)

    caches = ()
    for l in range(DEPTH):
        x = _ffn(xs, mod, l, 0, norm_ffn1, w_ffn1_gate, w_ffn1_up, w_ffn1_down, final_norm, False)
        f_in, q, ckv_n, kr, new_ckv, new_krope = _proj(
            x, mod, l, norm_mix, w_in_t, q_norm, w_uq, kv_norm, _rope_tables, caches)
        caches = (new_ckv, new_krope)
        frs = (_dft(f_in, BATCH, SEQ, 0), _dft(f_in, DEC_BATCH, DEC_SEQ, N_PROMPT))
        os_ = (_attention(q, ckv_n, kr, w_ukv, l, BATCH, SEQ, 0),
               _attention(q, ckv_n, kr, w_ukv, l, DEC_BATCH, DEC_SEQ, N_PROMPT, (cache_ckv, cache_kr)))
        x = _tail(x, mod, l, norm_mix, frs, os_, w_in_t, w_four, w_attn_proj, w_out)
        xs = _ffn((x,), mod, l, 6, norm_ffn2, w_ffn2_gate, w_ffn2_up, w_ffn2_down, final_norm, l == DEPTH - 1)
        if l < DEPTH - 1:
            xs = (xs,)

    y_prompt = xs[0].reshape(BATCH, SEQ, D_MODEL)
    y_sample = xs[1].reshape(DEC_BATCH, DEC_SEQ, D_MODEL)
    return y_prompt, y_sample, caches[0], caches[1]
```

```python
import functools

import numpy as np
import jax
import jax.numpy as jnp
from jax import lax
from jax.experimental import pallas as pl
from jax.experimental.pallas import tpu as pltpu

D_MODEL = 1024
BATCH = 16
SEQ = 256
DEPTH = 2
DEC_BATCH = 4
DEC_SEQ = 1024
PAST_LEN = 512
GRID_W = 64
D_FF = 2816
FOUR_GROUPS = 4
FOUR_GROUP_DIM = 64
D_FOUR = FOUR_GROUPS * FOUR_GROUP_DIM
N_HEADS = 8
QK_NOPE = 64
QK_ROPE = 32
V_DIM = 64
Q_RANK = 384
KV_RANK = 256
N_MOD = 9
ROPE_BASE = 10000.0
EPS = 1e-6

N_PROMPT = BATCH * SEQ
N_SAMPLE = DEC_BATCH * DEC_SEQ
N_TOK = N_PROMPT + N_SAMPLE
MOD_ROWS = 8
HEAD_PAD = 128
KEYS_SAMPLE = DEC_SEQ + PAST_LEN

VMEM_LIMIT = 52 * 1024 * 1024

F32 = jnp.float32
BF16 = jnp.bfloat16


def _dot(a, b):
    return jnp.dot(a, b, preferred_element_type=F32)


def _dot_nt(a, b):
    return lax.dot_general(a, b, (((1,), (1,)), ((), ())), preferred_element_type=F32)


def _mod_row(i, tm):
    n_prompt_tiles = N_PROMPT // tm
    per_batch = DEC_SEQ // tm
    return jnp.where(i < n_prompt_tiles, 0, 1 + (i - n_prompt_tiles) // per_batch)


def _mod_spec(layer, which, tm, tile_of_step=lambda i: i):
    def idx(*g):
        return ((layer * MOD_ROWS + _mod_row(tile_of_step(g[0]), tm)) * N_MOD + which, 0, 0)
    return pl.BlockSpec((None, 1, D_MODEL), idx)


def _norm_mod(x, nw, sc, sh):
    ms = jnp.mean(x * x, axis=-1, keepdims=True)
    y = x * lax.rsqrt(ms + EPS) * nw
    return y * (1.0 + sc) + sh


def _mod_kernel(c_ref, w_ref, b_ref, o_ref):
    c = c_ref[...]
    a = (c * jax.nn.sigmoid(c)).astype(BF16)
    o_ref[...] = _dot(a, w_ref[...].astype(BF16)) + b_ref[...]


def _modulation(c_all, w_mod, b_mod):
    tn = 1024
    n_out = N_MOD * D_MODEL
    return pl.pallas_call(
        _mod_kernel,
        grid=(DEPTH, n_out // tn),
        in_specs=[
            pl.BlockSpec((MOD_ROWS, D_MODEL), lambda l, j: (0, 0)),
            pl.BlockSpec((None, D_MODEL, tn), lambda l, j: (l, 0, j)),
            pl.BlockSpec((None, 1, tn), lambda l, j: (l, 0, j)),
        ],
        out_specs=pl.BlockSpec((None, MOD_ROWS, tn), lambda l, j: (l, 0, j)),
        out_shape=jax.ShapeDtypeStruct((DEPTH, MOD_ROWS, n_out), F32),
        compiler_params=pltpu.CompilerParams(
            dimension_semantics=("arbitrary", "arbitrary"), vmem_limit_bytes=VMEM_LIMIT),
        name="modulation",
    )(c_all, w_mod, b_mod.reshape(DEPTH, 1, n_out))


FFN_TM = 512
FFN_CHUNK = 256
FFN_NC = D_FF // FFN_CHUNK


def _ffn_kernel(*refs, final, split_in):
    refs = list(refs)
    x_refs = [refs.pop(0) for _ in range(2 if split_in else 1)]
    sh_ref, sc_ref, g_ref, nw_ref, wg_ref, wu_ref, wd_ref, fn_ref = refs[:8]
    o_refs = refs[8:10] if final else refs[8:9]
    wg_s, wu_s, wd_s, h_scr, acc_scr = refs[-5:]
    s = pl.program_id(0)
    is_prompt = jnp.maximum(s - (FFN_NC - 1), 0) < N_PROMPT // FFN_TM

    def x_tile():
        if split_in:
            return jnp.where(is_prompt, x_refs[0][...], x_refs[1][...])
        return x_refs[0][...]

    def chunk_act(h, wg, wu):
        gate = _dot(h, wg)
        up = _dot(h, wu)
        return ((gate * jax.nn.sigmoid(gate)) * up).astype(BF16)

    def finish(acc):
        xn = x_tile() + 0.5 * g_ref[...] * acc
        if not final:
            o_refs[0][...] = xn
            return
        ms = jnp.mean(xn * xn, axis=-1, keepdims=True)
        y = xn * lax.rsqrt(ms + EPS) * fn_ref[...]

        @pl.when(is_prompt)
        def _():
            o_refs[0][...] = y

        @pl.when(jnp.logical_not(is_prompt))
        def _():
            o_refs[1][...] = y

    def hidden():
        return _norm_mod(x_tile(), nw_ref[...], sc_ref[...], sh_ref[...]).astype(BF16)

    @pl.when(s == 0)
    def _():
        h_scr[...] = hidden()
        acc_scr[...] = jnp.zeros_like(acc_scr)

    @pl.when(s < FFN_NC)
    def _():
        wg = wg_ref[...].astype(BF16)
        wu = wu_ref[...].astype(BF16)
        wd = wd_ref[...].astype(BF16)
        wg_s[s] = wg
        wu_s[s] = wu
        wd_s[pl.ds(pl.multiple_of(s * FFN_CHUNK, FFN_CHUNK), FFN_CHUNK), :] = wd
        acc_scr[...] += _dot(chunk_act(h_scr[...], wg, wu), wd)

    @pl.when(s == FFN_NC - 1)
    def _():
        finish(acc_scr[...])

    @pl.when(s >= FFN_NC)
    def _():
        h = hidden()
        act = jnp.concatenate([chunk_act(h, wg_s[j], wu_s[j]) for j in range(FFN_NC)], axis=1)
        finish(_dot(act, wd_s[...]))


def _ffn(xs, mod, layer, mod_base, nw, wg, wu, wd, final_norm, final):
    tm = FFN_TM
    npt = N_PROMPT // tm
    split_in = len(xs) == 2
    tile = lambda s: jnp.maximum(s - (FFN_NC - 1), 0)
    chunk = lambda s: jnp.minimum(s, FFN_NC - 1)
    tok = pl.BlockSpec((tm, D_MODEL), lambda s: (tile(s), 0))
    tok_p = pl.BlockSpec((tm, D_MODEL), lambda s: (jnp.minimum(tile(s), npt - 1), 0))
    tok_s = pl.BlockSpec((tm, D_MODEL), lambda s: (jnp.maximum(tile(s) - npt, 0), 0))
    half = jax.ShapeDtypeStruct((N_PROMPT, D_MODEL), F32)
    return pl.pallas_call(
        functools.partial(_ffn_kernel, final=final, split_in=split_in),
        grid=(FFN_NC + N_TOK // tm - 1,),
        in_specs=([tok_p, tok_s] if split_in else [tok]) + [
            _mod_spec(layer, mod_base + 0, tm, tile),
            _mod_spec(layer, mod_base + 1, tm, tile),
            _mod_spec(layer, mod_base + 2, tm, tile),
            pl.BlockSpec((None, 1, D_MODEL), lambda s: (layer, 0, 0)),
            pl.BlockSpec((None, D_MODEL, FFN_CHUNK), lambda s: (layer, 0, chunk(s))),
            pl.BlockSpec((None, D_MODEL, FFN_CHUNK), lambda s: (layer, 0, chunk(s))),
            pl.BlockSpec((None, FFN_CHUNK, D_MODEL), lambda s: (layer, chunk(s), 0)),
            pl.BlockSpec((1, D_MODEL), lambda s: (0, 0)),
        ],
        out_specs=[tok_p, tok_s] if final else tok,
        out_shape=[half, half] if final else jax.ShapeDtypeStruct((N_TOK, D_MODEL), F32),
        scratch_shapes=[
            pltpu.VMEM((FFN_NC, D_MODEL, FFN_CHUNK), BF16),
            pltpu.VMEM((FFN_NC, D_MODEL, FFN_CHUNK), BF16),
            pltpu.VMEM((D_FF, D_MODEL), BF16),
            pltpu.VMEM((tm, D_MODEL), BF16),
            pltpu.VMEM((tm, D_MODEL), F32),
        ],
        compiler_params=pltpu.CompilerParams(
            dimension_semantics=("arbitrary",), vmem_limit_bytes=VMEM_LIMIT),
        name="ffn_final" if final else ("ffn_first" if split_in else "ffn"),
    )(*xs, mod, mod, mod, nw.reshape(DEPTH, 1, D_MODEL), wg, wu, wd, final_norm.reshape(1, D_MODEL))


PROJ_TM = 512
PROJ_SEQS = PROJ_TM // SEQ
PROJ_W = 1024
ROPE_COL = D_FOUR + Q_RANK + KV_RANK


def _pair_swap(x):
    lane = lax.broadcasted_iota(jnp.int32, x.shape, 1)
    return jnp.where(lane % 16 < 8, pltpu.roll(x, HEAD_PAD - 8, 1), pltpu.roll(x, 8, 1))


def _proj_kernel(*refs, first_layer):
    (x_ref, sh_ref, sc_ref, nw_ref, w1_ref, qn_ref, wq_ref, pq_ref, kvn_ref,
     cq_ref, sq_ref, ck_ref, sk_ref) = refs[:13]
    f_ref, q_ref, ckv_ref, kr_ref, nckv_ref, nkr_ref, w1_s, wq_s = refs[-8:]
    is_prompt = pl.program_id(0) < N_PROMPT // PROJ_TM

    @pl.when(pl.program_id(0) == 0)
    def _():
        w1_s[...] = w1_ref[...].astype(BF16)
        wq_s[...] = _dot_nt(pq_ref[...], wq_ref[...].astype(BF16)).astype(BF16)

    n_qp = N_HEADS * HEAD_PAD
    c0 = D_FOUR + Q_RANK
    groups = [slice(r * SEQ, (r + 1) * SEQ) for r in range(PROJ_SEQS)]
    ps = []
    for rows in groups:
        h = _norm_mod(x_ref[rows, :], nw_ref[...], sc_ref[...], sh_ref[...]).astype(BF16)
        ps.append(_dot_nt(h, w1_s[...]))
    staged = []
    for rows, p in zip(groups, ps):
        f_ref[rows, :] = p[:, :D_FOUR]
        ql = p[:, D_FOUR:c0]
        qn = ql * lax.rsqrt(jnp.mean(ql * ql, axis=-1, keepdims=True) + EPS) * qn_ref[...]
        qn = qn.astype(BF16)
        qq = _dot_nt(wq_s[:n_qp, :], qn)
        ckv = p[:, c0:c0 + KV_RANK]
        ckv_n = ckv * lax.rsqrt(jnp.mean(ckv * ckv, axis=-1, keepdims=True) + EPS) * kvn_ref[...]
        ckv_ref[rows, :] = ckv_n
        kr = p[:, ROPE_COL:ROPE_COL + HEAD_PAD]
        staged.append((qn, qq, ckv_n, kr))

    @pl.when(is_prompt)
    def _():
        for r, (rows, (qn, qq, ckv_n, kr)) in enumerate(zip(groups, staged)):
            cq = cq_ref[:, rows]
            for hd in range(N_HEADS):
                lo = hd * HEAD_PAD
                q_ref[lo:lo + HEAD_PAD, rows] = (qq[lo:lo + HEAD_PAD, :] * cq).astype(BF16)
            kr_ref[rows, :] = (kr * ck_ref[rows, :]).astype(BF16)
            if first_layer:
                nckv_ref[r, 0] = ckv_n
                nkr_ref[r, 0] = kr[:, :QK_ROPE]
                for later in range(1, DEPTH):
                    nckv_ref[r, later] = jnp.zeros_like(ckv_n)
                    nkr_ref[r, later] = jnp.zeros((SEQ, QK_ROPE), F32)
            else:
                nckv_ref[r] = ckv_n
                nkr_ref[r] = kr[:, :QK_ROPE]

    @pl.when(jnp.logical_not(is_prompt))
    def _():
        for rows, (qn, qq, ckv_n, kr) in zip(groups, staged):
            cq, sq = cq_ref[:, rows], sq_ref[:, rows]
            qs = _dot_nt(wq_s[n_qp:, :], qn)
            for hd in range(N_HEADS):
                lo, mid = hd * HEAD_PAD, hd * HEAD_PAD + QK_ROPE
                rope = qq[lo:mid, :] * cq[:QK_ROPE] + qs[hd * QK_ROPE:(hd + 1) * QK_ROPE, :] * sq[:QK_ROPE]
                q_ref[lo:mid, rows] = rope.astype(BF16)
                q_ref[mid:lo + HEAD_PAD, rows] = (qq[mid:lo + HEAD_PAD, :] * cq[QK_ROPE:]).astype(BF16)
            kr_ref[rows, :] = (kr * ck_ref[rows, :] + _pair_swap(kr) * sk_ref[rows, :]).astype(BF16)


def _q_select():
    n_qp = N_HEADS * HEAD_PAD
    p = np.zeros((n_qp + N_HEADS * QK_ROPE, N_HEADS * (QK_NOPE + QK_ROPE)), np.float32)
    for hd in range(N_HEADS):
        src, dst = hd * (QK_NOPE + QK_ROPE), hd * HEAD_PAD
        for j in range(QK_ROPE):
            partner = j + 8 if j % 16 < 8 else j - 8
            p[dst + j, src + QK_NOPE + j] = 1.0
            p[n_qp + hd * QK_ROPE + j, src + QK_NOPE + partner] = 1.0
        for j in range(QK_NOPE):
            p[dst + QK_ROPE + j, src + j] = 1.0
    return p.astype(BF16)


def _proj(x, mod, layer, nw, w_in_t, qn, w_uq, kvn, tabs, prev_caches):
    tm = PROJ_TM
    n_prompt_tiles = N_PROMPT // tm
    per_batch = DEC_SEQ // tm
    n_q = N_HEADS * (QK_NOPE + QK_ROPE)

    def tab_blk(i):
        return jnp.where(i < n_prompt_tiles, per_batch, (i - n_prompt_tiles) % per_batch)

    tok = lambda i: (i, 0)
    const = lambda i: (0, 0)
    first_layer = not prev_caches
    assert first_layer == (layer == 0)
    cache_layers = DEPTH if first_layer else None
    cache_idx = lambda i: (jnp.minimum(i, n_prompt_tiles - 1), 0 if first_layer else layer, 0, 0)
    tab_spec = pl.BlockSpec((tm, HEAD_PAD), lambda i: (tab_blk(i), 0))
    tab_t_spec = pl.BlockSpec((HEAD_PAD, tm), lambda i: (0, tab_blk(i)))
    return pl.pallas_call(
        functools.partial(_proj_kernel, first_layer=first_layer),
        grid=(N_TOK // tm,),
        in_specs=[
            pl.BlockSpec((tm, D_MODEL), tok),
            _mod_spec(layer, 3, tm),
            _mod_spec(layer, 4, tm),
            pl.BlockSpec((None, 1, D_MODEL), lambda i: (layer, 0, 0)),
            pl.BlockSpec((None, PROJ_W, D_MODEL), lambda i: (layer, 0, 0)),
            pl.BlockSpec((None, 1, Q_RANK), lambda i: (layer, 0, 0)),
            pl.BlockSpec((None, Q_RANK, n_q), lambda i: (layer, 0, 0)),
            pl.BlockSpec((N_HEADS * (HEAD_PAD + QK_ROPE), n_q), const),
            pl.BlockSpec((None, 1, KV_RANK), lambda i: (layer, 0, 0)),
            tab_t_spec, tab_t_spec, tab_spec, tab_spec,
        ] + [pl.BlockSpec(memory_space=pl.ANY)] * len(prev_caches),
        out_specs=[
            pl.BlockSpec((tm, D_FOUR), tok),
            pl.BlockSpec((N_HEADS * HEAD_PAD, tm), lambda i: (0, i)),
            pl.BlockSpec((tm, KV_RANK), tok),
            pl.BlockSpec((tm, HEAD_PAD), tok),
            pl.BlockSpec((PROJ_SEQS, cache_layers, SEQ, KV_RANK), cache_idx),
            pl.BlockSpec((PROJ_SEQS, cache_layers, SEQ, QK_ROPE), cache_idx),
        ],
        out_shape=[
            jax.ShapeDtypeStruct((N_TOK, D_FOUR), F32),
            jax.ShapeDtypeStruct((N_HEADS * HEAD_PAD, N_TOK), BF16),
            jax.ShapeDtypeStruct((N_TOK, KV_RANK), F32),
            jax.ShapeDtypeStruct((N_TOK, HEAD_PAD), BF16),
            jax.ShapeDtypeStruct((BATCH, DEPTH, SEQ, KV_RANK), F32),
            jax.ShapeDtypeStruct((BATCH, DEPTH, SEQ, QK_ROPE), F32),
        ],
        input_output_aliases={13 + k: 4 + k for k in range(len(prev_caches))},
        scratch_shapes=[
            pltpu.VMEM((PROJ_W, D_MODEL), BF16),
            pltpu.VMEM((N_HEADS * (HEAD_PAD + QK_ROPE), Q_RANK), BF16),
        ],
        compiler_params=pltpu.CompilerParams(
            dimension_semantics=("arbitrary",), vmem_limit_bytes=VMEM_LIMIT),
        name="mixer_proj",
    )(x, mod, mod, nw.reshape(DEPTH, 1, D_MODEL), w_in_t, qn.reshape(DEPTH, 1, Q_RANK), w_uq, _q_select(),
      kvn.reshape(DEPTH, 1, KV_RANK), *tabs(tm), *prev_caches)


DFT_ROWS = 1024


def _split(x):
    hi = x.astype(BF16)
    lo = (x - hi.astype(F32)).astype(BF16)
    return hi, lo


def _dft_kernel(x_ref, gh_ref, gl_ref, th_ref, tl_ref, o_ref, *, n):
    xh, xl = _split(x_ref[...])
    gh, gl = gh_ref[...], gl_ref[...]
    y = _dot(xh, gh) + _dot(xl, gh) + _dot(xh, gl)
    yh, yl = _split(y)
    th, tl = th_ref[...], tl_ref[...]
    for b in range(x_ref.shape[0] // n):
        r = slice(b * n, (b + 1) * n)
        ych = jnp.concatenate([yh[r, :D_FOUR], yh[r, D_FOUR:]], axis=0)
        ycl = jnp.concatenate([yl[r, :D_FOUR], yl[r, D_FOUR:]], axis=0)
        fr = _dot(th, ych) + _dot(tl, ych) + _dot(th, ycl)
        o_ref[r, :] = fr.astype(BF16)


def _split_np(a):
    a32 = np.asarray(a, np.float32)
    hi = a32.astype(BF16)
    lo = (a32 - hi.astype(np.float32)).astype(BF16)
    return hi, lo


def _dft_tables(n):
    k = np.arange(n, dtype=np.int64)
    ang = 2.0 * np.pi * ((k[:, None] * k[None, :]) % n).astype(np.float64) / n
    t = np.concatenate([np.cos(ang), -np.sin(ang)], axis=1) / np.sqrt(n)
    c = np.arange(FOUR_GROUP_DIM, dtype=np.int64)
    ang_c = 2.0 * np.pi * ((c[:, None] * c[None, :]) % FOUR_GROUP_DIM).astype(np.float64) / FOUR_GROUP_DIM
    eye = np.eye(FOUR_GROUPS)
    g = np.concatenate([np.kron(eye, np.cos(ang_c)), np.kron(eye, np.sin(ang_c))], axis=1)
    g = g / np.sqrt(FOUR_GROUP_DIM)
    return _split_np(g) + _split_np(t)


def _dft(f_in, n_batch, n, row0):
    gh, gl, th, tl = _dft_tables(n)
    rows = max(n, DFT_ROWS)
    blk0 = row0 // rows
    const = lambda b: (0, 0)
    return pl.pallas_call(
        functools.partial(_dft_kernel, n=n),
        grid=(n_batch * n // rows,),
        in_specs=[
            pl.BlockSpec((rows, D_FOUR), lambda b: (blk0 + b, 0)),
            pl.BlockSpec((D_FOUR, 2 * D_FOUR), const),
            pl.BlockSpec((D_FOUR, 2 * D_FOUR), const),
            pl.BlockSpec((n, 2 * n), const),
            pl.BlockSpec((n, 2 * n), const),
        ],
        out_specs=pl.BlockSpec((rows, D_FOUR), lambda b: (b, 0)),
        out_shape=jax.ShapeDtypeStruct((n_batch * n, D_FOUR), BF16),
        compiler_params=pltpu.CompilerParams(
            dimension_semantics=("arbitrary",), vmem_limit_bytes=VMEM_LIMIT),
        name="fnet_dft_%d" % n,
    )(f_in, gh, gl, th, tl)


ATTN_TQ = 512
ATTN_SEQS = 4
ATTN_AHEAD = 3


def _attn_kernel(*refs, n_own, n_ctx, seqs):
    if n_ctx:
        qt_ref, ckv_ref, kr_ref, cckv_ref, ckr_ref, wkv_ref, pk_ref, pvt_ref = refs[:8]
    else:
        qt_ref, ckv_ref, kr_ref, wkv_ref, pk_ref, pvt_ref = refs[:6]
    o_ref, k_scr, vt_scr, wk_s, wvt_s = refs[-5:]

    @pl.when(jnp.logical_and(pl.program_id(0) == 0, pl.program_id(1) == 0))
    def _():
        wkv = wkv_ref[...].astype(BF16)
        wk_s[...] = _dot(wkv, pk_ref[...]).astype(BF16)
        wvt_s[...] = _dot_nt(pvt_ref[...], wkv).astype(BF16)

    @pl.when(pl.program_id(1) == 0)
    def _():
        ckv = ckv_ref[...].astype(BF16)
        kr = kr_ref[...]
        if n_ctx:
            ckv = jnp.concatenate([ckv, cckv_ref[...].astype(BF16)], axis=0)
            kr = jnp.concatenate([kr, ckr_ref[...].astype(BF16)], axis=0)
        k_nope = _dot(ckv, wk_s[...])
        kr = kr.astype(F32)
        for hd in range(N_HEADS):
            lo = hd * HEAD_PAD
            k_scr[:, lo:lo + HEAD_PAD] = (k_nope[:, lo:lo + HEAD_PAD] + kr).astype(BF16)
        vt_scr[...] = _dot_nt(wvt_s[...], ckv).astype(BF16)

    n_keys = n_own + n_ctx
    tq = qt_ref.shape[1] // seqs
    units = [(j, hd) for j in range(seqs) for hd in range(N_HEADS)]

    def scores(j, hd):
        lo = hd * HEAD_PAD
        return _dot(k_scr[j * n_keys:(j + 1) * n_keys, lo:lo + HEAD_PAD],
                    qt_ref[lo:lo + HEAD_PAD, j * tq:(j + 1) * tq])

    outs = []
    queue = [scores(*u) for u in units[:ATTN_AHEAD]]
    for i, (j, hd) in enumerate(units):
        s = queue.pop(0)
        if i + ATTN_AHEAD < len(units):
            queue.append(scores(*units[i + ATTN_AHEAD]))
        m = jnp.max(s, axis=0, keepdims=True)
        p = jnp.exp(s - m)
        l = jnp.sum(p, axis=0, keepdims=True)
        vt = vt_scr[hd * V_DIM:(hd + 1) * V_DIM, j * n_keys:(j + 1) * n_keys]
        outs.append(_dot(vt, p.astype(BF16)) / l)
        if hd == N_HEADS - 1:
            o_ref[j * tq:(j + 1) * tq, :] = jnp.concatenate(outs, axis=0).T.astype(BF16)
            outs = []


def _kv_select():
    n_kv = N_HEADS * (QK_NOPE + V_DIM)
    pk = np.zeros((n_kv, N_HEADS * HEAD_PAD), np.float32)
    pvt = np.zeros((N_HEADS * V_DIM, n_kv), np.float32)
    for hd in range(N_HEADS):
        src = hd * (QK_NOPE + V_DIM)
        for j in range(QK_NOPE):
            pk[src + j, hd * HEAD_PAD + QK_ROPE + j] = 1.0
        for j in range(V_DIM):
            pvt[hd * V_DIM + j, src + QK_NOPE + j] = 1.0
    return pk.astype(BF16), pvt.astype(BF16)


def _attention(q, ckv_n, kr, w_ukv, layer, n_batch, n_own, row0, ctx=None):
    tq = min(n_own, ATTN_TQ)
    n_kv = N_HEADS * (QK_NOPE + V_DIM)
    n_ctx = 0 if ctx is None else PAST_LEN
    n_keys = n_own + n_ctx
    qt = n_own // tq
    seqs = ATTN_SEQS if (ctx is None and qt == 1) else 1
    blk0 = row0 // (seqs * n_own)
    const = lambda b, t: (0, 0)
    in_specs = [
        pl.BlockSpec((N_HEADS * HEAD_PAD, seqs * tq), lambda b, t: (0, (row0 // (seqs * tq)) + b * qt + t)),
        pl.BlockSpec((seqs * n_own, KV_RANK), lambda b, t: (blk0 + b, 0)),
        pl.BlockSpec((seqs * n_own, HEAD_PAD), lambda b, t: (blk0 + b, 0)),
    ]
    args = [q, ckv_n, kr]
    if ctx is not None:
        cckv, ckr = ctx
        in_specs += [
            pl.BlockSpec((None, None, PAST_LEN, KV_RANK), lambda b, t: (b, layer, 0, 0)),
            pl.BlockSpec((None, None, PAST_LEN, HEAD_PAD), lambda b, t: (b, layer, 0, 0)),
        ]
        args += [cckv, ckr]
    in_specs += [
        pl.BlockSpec((None, KV_RANK, n_kv), lambda b, t: (layer, 0, 0)),
        pl.BlockSpec((n_kv, N_HEADS * HEAD_PAD), const),
        pl.BlockSpec((N_HEADS * V_DIM, n_kv), const),
    ]
    args += [w_ukv, *_kv_select()]
    return pl.pallas_call(
        functools.partial(_attn_kernel, n_own=n_own, n_ctx=n_ctx, seqs=seqs),
        grid=(n_batch // seqs, qt),
        in_specs=in_specs,
        out_specs=pl.BlockSpec((seqs * tq, N_HEADS * V_DIM), lambda b, t: (b * qt + t, 0)),
        out_shape=jax.ShapeDtypeStruct((n_batch * n_own, N_HEADS * V_DIM), BF16),
        scratch_shapes=[
            pltpu.VMEM((seqs * n_keys, N_HEADS * HEAD_PAD), BF16),
            pltpu.VMEM((N_HEADS * V_DIM, seqs * n_keys), BF16),
            pltpu.VMEM((KV_RANK, N_HEADS * HEAD_PAD), BF16),
            pltpu.VMEM((N_HEADS * V_DIM, KV_RANK), BF16),
        ],
        compiler_params=pltpu.CompilerParams(
            dimension_semantics=("arbitrary", "arbitrary"), vmem_limit_bytes=VMEM_LIMIT),
        name="mla_attention_%d" % n_keys,
    )(*args)


TAIL_TM = 512
TAIL_GROUP = 256
TAIL_SLABS = 4
GATE_ROW = ROPE_COL + QK_ROPE
GATE_CHUNK = 512
GATE_CHUNKS = 2 * D_MODEL // GATE_CHUNK


def _tail_kernel(x_ref, sh_ref, sc_ref, g_ref, nw_ref, frp_ref, frs_ref, op_ref, os_ref, wi_hbm,
                 wf_ref, wa_ref, wo_ref, out_ref, wg_s, wf_s, wa_s, wo_s, stage, sem, *, layer):
    s = pl.program_id(0)
    is_prompt = jnp.maximum(s - TAIL_SLABS, 0) < N_PROMPT // TAIL_TM

    def gate_copy(c):
        src = wi_hbm.at[layer, pl.ds(GATE_ROW + c * GATE_CHUNK, GATE_CHUNK), :]
        return pltpu.make_async_copy(src, stage.at[c % 2], sem.at[c % 2])

    @pl.when(s == 0)
    def _():
        gate_copy(0).start()
        for c in range(GATE_CHUNKS):
            if c + 1 < GATE_CHUNKS:
                gate_copy(c + 1).start()
            gate_copy(c).wait()
            wg_s[c * GATE_CHUNK:(c + 1) * GATE_CHUNK, :] = stage[c % 2].astype(BF16)

    @pl.when(s < TAIL_SLABS)
    def _():
        def put(dst, src):
            rows = src.shape[0]
            dst[pl.ds(pl.multiple_of(s * rows, rows), rows), :] = src.astype(BF16)

        put(wf_s, wf_ref[...])
        put(wa_s, wa_ref[...])
        put(wo_s, wo_ref[...])

    @pl.when(s >= TAIL_SLABS)
    def _():
        groups = [slice(r * TAIL_GROUP, (r + 1) * TAIL_GROUP) for r in range(TAIL_TM // TAIL_GROUP)]
        staged = []
        for rows in groups:
            x = x_ref[rows, :]
            h = _norm_mod(x, nw_ref[...], sc_ref[...], sh_ref[...]).astype(BF16)
            gates = _dot_nt(h, wg_s[...])
            a_out = _dot(jnp.where(is_prompt, frp_ref[rows, :], frs_ref[rows, :]), wf_s[...])
            o_out = _dot(jnp.where(is_prompt, op_ref[rows, :], os_ref[rows, :]), wa_s[...])
            staged.append((x, gates, a_out, o_out))
        for rows, (x, gates, a_out, o_out) in zip(groups, staged):
            merged = jax.nn.sigmoid(gates[:, :D_MODEL]) * a_out + jax.nn.sigmoid(gates[:, D_MODEL:]) * o_out
            m = _dot(merged.astype(BF16), wo_s[...])
            out_ref[rows, :] = x + g_ref[...] * m


def _tail(x, mod, layer, nw, frs, os_, w_in_t, w_four, w_attn_proj, w_out):
    tm = TAIL_TM
    npt = N_PROMPT // tm
    tile = lambda s: jnp.maximum(s - TAIL_SLABS, 0)
    slab = lambda s: jnp.minimum(s, TAIL_SLABS - 1)
    tok = lambda s: (tile(s), 0)
    tok_p = lambda s: (jnp.minimum(tile(s), npt - 1), 0)
    tok_s = lambda s: (jnp.maximum(tile(s) - npt, 0), 0)
    n_attn = N_HEADS * V_DIM

    def slab_spec(rows, cols):
        return pl.BlockSpec((None, rows // TAIL_SLABS, cols), lambda s: (layer, slab(s), 0))

    return pl.pallas_call(
        functools.partial(_tail_kernel, layer=layer),
        grid=(TAIL_SLABS + N_TOK // tm,),
        in_specs=[
            pl.BlockSpec((tm, D_MODEL), tok),
            _mod_spec(layer, 3, tm, tile),
            _mod_spec(layer, 4, tm, tile),
            _mod_spec(layer, 5, tm, tile),
            pl.BlockSpec((None, 1, D_MODEL), lambda s: (layer, 0, 0)),
            pl.BlockSpec((tm, D_FOUR), tok_p),
            pl.BlockSpec((tm, D_FOUR), tok_s),
            pl.BlockSpec((tm, n_attn), tok_p),
            pl.BlockSpec((tm, n_attn), tok_s),
            pl.BlockSpec(memory_space=pl.ANY),
            slab_spec(D_FOUR, D_MODEL),
            slab_spec(n_attn, D_MODEL),
            slab_spec(D_MODEL, D_MODEL),
        ],
        out_specs=pl.BlockSpec((tm, D_MODEL), tok),
        out_shape=jax.ShapeDtypeStruct((N_TOK, D_MODEL), F32),
        scratch_shapes=[
            pltpu.VMEM((2 * D_MODEL, D_MODEL), BF16),
            pltpu.VMEM((D_FOUR, D_MODEL), BF16),
            pltpu.VMEM((n_attn, D_MODEL), BF16),
            pltpu.VMEM((D_MODEL, D_MODEL), BF16),
            pltpu.VMEM((2, GATE_CHUNK, D_MODEL), F32),
            pltpu.SemaphoreType.DMA((2,)),
        ],
        compiler_params=pltpu.CompilerParams(
            dimension_semantics=("arbitrary",), vmem_limit_bytes=VMEM_LIMIT),
        name="mixer_tail",
    )(x, mod, mod, mod, nw.reshape(DEPTH, 1, D_MODEL), *frs, *os_, w_in_t, w_four, w_attn_proj, w_out)


def _rope_tables(tm):
    rows = DEC_SEQ // GRID_W
    row = np.repeat(np.arange(rows), GRID_W).astype(np.float64)
    col = np.tile(np.arange(GRID_W), rows).astype(np.float64)
    axis_dim = QK_ROPE // 2
    inv = ROPE_BASE ** (-np.arange(0, axis_dim, 2, dtype=np.float64) / axis_dim)
    ar = row[:, None] * inv
    ac = col[:, None] * inv
    cr, sr, cc, sc = np.cos(ar), np.sin(ar), np.cos(ac), np.sin(ac)
    cos32 = np.concatenate([cr, cr, cc, cc], axis=1)
    sin32 = np.concatenate([-sr, sr, -sc, sc], axis=1)
    scale = np.float32((QK_NOPE + QK_ROPE) ** -0.5)

    def table(rope_part, nope_val, ident_rope):
        t = np.zeros((DEC_SEQ + tm, HEAD_PAD), np.float32)
        t[:DEC_SEQ, :QK_ROPE] = rope_part
        t[:DEC_SEQ, QK_ROPE:QK_ROPE + QK_NOPE] = nope_val
        t[DEC_SEQ:, :QK_ROPE] = ident_rope
        t[DEC_SEQ:, QK_ROPE:QK_ROPE + QK_NOPE] = nope_val
        return t

    cq = (table(cos32, 1.0, 1.0) * scale).T
    sq = (table(sin32, 0.0, 0.0) * scale).T
    ck = table(cos32, 0.0, 1.0)
    sk = table(sin32, 0.0, 0.0)
    return tuple(jnp.asarray(np.ascontiguousarray(t), F32) for t in (cq, sq, ck, sk))


def kernel(x_prompt, x_sample, cache_ckv, cache_krope, c, c_ctx, w_mod, b_mod, norm_ffn1, w_ffn1_gate,
           w_ffn1_up, w_ffn1_down, norm_mix, w_in, w_four, q_norm, w_uq, kv_norm, w_ukv, w_attn_proj,
           w_out, norm_ffn2, w_ffn2_gate, w_ffn2_up, w_ffn2_down, final_norm):
    xs = (x_prompt.reshape(N_PROMPT, D_MODEL), x_sample.reshape(N_SAMPLE, D_MODEL))
    c_all = jnp.concatenate(
        [c_ctx[None, :], c, jnp.zeros((MOD_ROWS - 1 - DEC_BATCH, D_MODEL), F32)], axis=0)
    mod = _modulation(c_all, w_mod, b_mod).reshape(DEPTH * MOD_ROWS * N_MOD, 1, D_MODEL)
    cache_kr = jnp.pad(cache_krope, ((0, 0), (0, 0), (0, 0), (0, HEAD_PAD - QK_ROPE)))
    w_in_t = jnp.swapaxes(w_in, 1, 2)

    caches = ()
    for l in range(DEPTH):
        x = _ffn(xs, mod, l, 0, norm_ffn1, w_ffn1_gate, w_ffn1_up, w_ffn1_down, final_norm, False)
        f_in, q, ckv_n, kr, new_ckv, new_krope = _proj(
            x, mod, l, norm_mix, w_in_t, q_norm, w_uq, kv_norm, _rope_tables, caches)
        caches = (new_ckv, new_krope)
        frs = (_dft(f_in, BATCH, SEQ, 0), _dft(f_in, DEC_BATCH, DEC_SEQ, N_PROMPT))
        os_ = (_attention(q, ckv_n, kr, w_ukv, l, BATCH, SEQ, 0),
               _attention(q, ckv_n, kr, w_ukv, l, DEC_BATCH, DEC_SEQ, N_PROMPT, (cache_ckv, cache_kr)))
        x = _tail(x, mod, l, norm_mix, frs, os_, w_in_t, w_four, w_attn_proj, w_out)
        xs = _ffn((x,), mod, l, 6, norm_ffn2, w_ffn2_gate, w_ffn2_up, w_ffn2_down, final_norm, l == DEPTH - 1)
        if l < DEPTH - 1:
            xs = (xs,)

    y_prompt = xs[0].reshape(BATCH, SEQ, D_MODEL)
    y_sample = xs[1].reshape(DEC_BATCH, DEC_SEQ, D_MODEL)
    return y_prompt, y_sample, caches[0], caches[1]
```

```python
import functools

import numpy as np
import jax
import jax.numpy as jnp
from jax import lax
from jax.experimental import pallas as pl
from jax.experimental.pallas import tpu as pltpu

D_MODEL = 1024
BATCH = 16
SEQ = 256
DEPTH = 2
DEC_BATCH = 4
DEC_SEQ = 1024
PAST_LEN = 512
GRID_W = 64
D_FF = 2816
FOUR_GROUPS = 4
FOUR_GROUP_DIM = 64
D_FOUR = FOUR_GROUPS * FOUR_GROUP_DIM
N_HEADS = 8
QK_NOPE = 64
QK_ROPE = 32
V_DIM = 64
Q_RANK = 384
KV_RANK = 256
N_MOD = 9
ROPE_BASE = 10000.0
EPS = 1e-6

N_PROMPT = BATCH * SEQ
N_SAMPLE = DEC_BATCH * DEC_SEQ
N_TOK = N_PROMPT + N_SAMPLE
MOD_ROWS = 8
HEAD_PAD = 128
KEYS_SAMPLE = DEC_SEQ + PAST_LEN

VMEM_LIMIT = 52 * 1024 * 1024

F32 = jnp.float32
BF16 = jnp.bfloat16


def _dot(a, b):
    return jnp.dot(a, b, preferred_element_type=F32)


def _dot_nt(a, b):
    return lax.dot_general(a, b, (((1,), (1,)), ((), ())), preferred_element_type=F32)


def _mod_row(i, tm):
    n_prompt_tiles = N_PROMPT // tm
    per_batch = DEC_SEQ // tm
    return jnp.where(i < n_prompt_tiles, 0, 1 + (i - n_prompt_tiles) // per_batch)


def _mod_spec(layer, which, tm, tile_of_step=lambda i: i):
    def idx(*g):
        return ((layer * MOD_ROWS + _mod_row(tile_of_step(g[0]), tm)) * N_MOD + which, 0, 0)
    return pl.BlockSpec((None, 1, D_MODEL), idx)


def _norm_mod(x, nw, sc, sh):
    ms = jnp.mean(x * x, axis=-1, keepdims=True)
    y = x * lax.rsqrt(ms + EPS) * nw
    return y * (1.0 + sc) + sh


def _mod_kernel(c_ref, w_ref, b_ref, o_ref):
    c = c_ref[...]
    a = (c * jax.nn.sigmoid(c)).astype(BF16)
    o_ref[...] = _dot(a, w_ref[...].astype(BF16)) + b_ref[...]


def _modulation(c_all, w_mod, b_mod):
    tn = 1024
    n_out = N_MOD * D_MODEL
    return pl.pallas_call(
        _mod_kernel,
        grid=(DEPTH, n_out // tn),
        in_specs=[
            pl.BlockSpec((MOD_ROWS, D_MODEL), lambda l, j: (0, 0)),
            pl.BlockSpec((None, D_MODEL, tn), lambda l, j: (l, 0, j)),
            pl.BlockSpec((None, 1, tn), lambda l, j: (l, 0, j)),
        ],
        out_specs=pl.BlockSpec((None, MOD_ROWS, tn), lambda l, j: (l, 0, j)),
        out_shape=jax.ShapeDtypeStruct((DEPTH, MOD_ROWS, n_out), F32),
        compiler_params=pltpu.CompilerParams(
            dimension_semantics=("arbitrary", "arbitrary"), vmem_limit_bytes=VMEM_LIMIT),
        name="modulation",
    )(c_all, w_mod, b_mod.reshape(DEPTH, 1, n_out))


FFN_TM = 512
FFN_CHUNK = 256
FFN_NC = D_FF // FFN_CHUNK


def _ffn_kernel(*refs, final, split_in):
    refs = list(refs)
    x_refs = [refs.pop(0) for _ in range(2 if split_in else 1)]
    sh_ref, sc_ref, g_ref, nw_ref, wg_ref, wu_ref, wd_ref, fn_ref = refs[:8]
    o_refs = refs[8:10] if final else refs[8:9]
    wg_s, wu_s, wd_s, h_scr, acc_scr = refs[-5:]
    s = pl.program_id(0)
    is_prompt = jnp.maximum(s - (FFN_NC - 1), 0) < N_PROMPT // FFN_TM

    def x_tile():
        if split_in:
            return jnp.where(is_prompt, x_refs[0][...], x_refs[1][...])
        return x_refs[0][...]

    def chunk_act(h, wg, wu):
        gate = _dot(h, wg)
        up = _dot(h, wu)
        return ((gate * jax.nn.sigmoid(gate)) * up).astype(BF16)

    def finish(acc):
        xn = x_tile() + 0.5 * g_ref[...] * acc
        if not final:
            o_refs[0][...] = xn
            return
        ms = jnp.mean(xn * xn, axis=-1, keepdims=True)
        y = xn * lax.rsqrt(ms + EPS) * fn_ref[...]

        @pl.when(is_prompt)
        def _():
            o_refs[0][...] = y

        @pl.when(jnp.logical_not(is_prompt))
        def _():
            o_refs[1][...] = y

    def hidden():
        return _norm_mod(x_tile(), nw_ref[...], sc_ref[...], sh_ref[...]).astype(BF16)

    @pl.when(s == 0)
    def _():
        h_scr[...] = hidden()
        acc_scr[...] = jnp.zeros_like(acc_scr)

    @pl.when(s < FFN_NC)
    def _():
        wg = wg_ref[...].astype(BF16)
        wu = wu_ref[...].astype(BF16)
        wd = wd_ref[...].astype(BF16)
        wg_s[s] = wg
        wu_s[s] = wu
        wd_s[pl.ds(pl.multiple_of(s * FFN_CHUNK, FFN_CHUNK), FFN_CHUNK), :] = wd
        acc_scr[...] += _dot(chunk_act(h_scr[...], wg, wu), wd)

    @pl.when(s == FFN_NC - 1)
    def _():
        finish(acc_scr[...])

    @pl.when(s >= FFN_NC)
    def _():
        h = hidden()
        act = jnp.concatenate([chunk_act(h, wg_s[j], wu_s[j]) for j in range(FFN_NC)], axis=1)
        finish(_dot(act, wd_s[...]))


def _ffn(xs, mod, layer, mod_base, nw, wg, wu, wd, final_norm, final):
    tm = FFN_TM
    npt = N_PROMPT // tm
    split_in = len(xs) == 2
    tile = lambda s: jnp.maximum(s - (FFN_NC - 1), 0)
    chunk = lambda s: jnp.minimum(s, FFN_NC - 1)
    tok = pl.BlockSpec((tm, D_MODEL), lambda s: (tile(s), 0))
    tok_p = pl.BlockSpec((tm, D_MODEL), lambda s: (jnp.minimum(tile(s), npt - 1), 0))
    tok_s = pl.BlockSpec((tm, D_MODEL), lambda s: (jnp.maximum(tile(s) - npt, 0), 0))
    half = jax.ShapeDtypeStruct((N_PROMPT, D_MODEL), F32)
    return pl.pallas_call(
        functools.partial(_ffn_kernel, final=final, split_in=split_in),
        grid=(FFN_NC + N_TOK // tm - 1,),
        in_specs=([tok_p, tok_s] if split_in else [tok]) + [
            _mod_spec(layer, mod_base + 0, tm, tile),
            _mod_spec(layer, mod_base + 1, tm, tile),
            _mod_spec(layer, mod_base + 2, tm, tile),
            pl.BlockSpec((None, 1, D_MODEL), lambda s: (layer, 0, 0)),
            pl.BlockSpec((None, D_MODEL, FFN_CHUNK), lambda s: (layer, 0, chunk(s))),
            pl.BlockSpec((None, D_MODEL, FFN_CHUNK), lambda s: (layer, 0, chunk(s))),
            pl.BlockSpec((None, FFN_CHUNK, D_MODEL), lambda s: (layer, chunk(s), 0)),
            pl.BlockSpec((1, D_MODEL), lambda s: (0, 0)),
        ],
        out_specs=[tok_p, tok_s] if final else tok,
        out_shape=[half, half] if final else jax.ShapeDtypeStruct((N_TOK, D_MODEL), F32),
        scratch_shapes=[
            pltpu.VMEM((FFN_NC, D_MODEL, FFN_CHUNK), BF16),
            pltpu.VMEM((FFN_NC, D_MODEL, FFN_CHUNK), BF16),
            pltpu.VMEM((D_FF, D_MODEL), BF16),
            pltpu.VMEM((tm, D_MODEL), BF16),
            pltpu.VMEM((tm, D_MODEL), F32),
        ],
        compiler_params=pltpu.CompilerParams(
            dimension_semantics=("arbitrary",), vmem_limit_bytes=VMEM_LIMIT),
        name="ffn_final" if final else ("ffn_first" if split_in else "ffn"),
    )(*xs, mod, mod, mod, nw.reshape(DEPTH, 1, D_MODEL), wg, wu, wd, final_norm.reshape(1, D_MODEL))


PROJ_TM = 1024
PROJ_SEQS = PROJ_TM // SEQ
PROJ_W = 1024
ROPE_COL = D_FOUR + Q_RANK + KV_RANK


def _pair_swap(x):
    lane = lax.broadcasted_iota(jnp.int32, x.shape, 1)
    return jnp.where(lane % 16 < 8, pltpu.roll(x, HEAD_PAD - 8, 1), pltpu.roll(x, 8, 1))


def _proj_kernel(*refs, first_layer):
    (x_ref, sh_ref, sc_ref, nw_ref, w1_ref, qn_ref, wq_ref, pq_ref, kvn_ref,
     cq_ref, sq_ref, ck_ref, sk_ref) = refs[:13]
    f_ref, q_ref, ckv_ref, kr_ref, nckv_ref, nkr_ref, w1_s, wq_s = refs[-8:]
    is_prompt = pl.program_id(0) < N_PROMPT // PROJ_TM

    @pl.when(pl.program_id(0) == 0)
    def _():
        w1_s[...] = w1_ref[...].astype(BF16)
        wq_s[...] = _dot_nt(pq_ref[...], wq_ref[...].astype(BF16)).astype(BF16)

    n_qp = N_HEADS * HEAD_PAD
    c0 = D_FOUR + Q_RANK
    groups = [slice(r * SEQ, (r + 1) * SEQ) for r in range(PROJ_SEQS)]
    ps = []
    for rows in groups:
        h = _norm_mod(x_ref[rows, :], nw_ref[...], sc_ref[...], sh_ref[...]).astype(BF16)
        ps.append(_dot_nt(h, w1_s[...]))
    staged = []
    for rows, p in zip(groups, ps):
        f_ref[rows, :] = p[:, :D_FOUR]
        ql = p[:, D_FOUR:c0]
        qn = ql * lax.rsqrt(jnp.mean(ql * ql, axis=-1, keepdims=True) + EPS) * qn_ref[...]
        qn = qn.astype(BF16)
        qq = _dot_nt(wq_s[:n_qp, :], qn)
        ckv = p[:, c0:c0 + KV_RANK]
        ckv_n = ckv * lax.rsqrt(jnp.mean(ckv * ckv, axis=-1, keepdims=True) + EPS) * kvn_ref[...]
        ckv_ref[rows, :] = ckv_n
        kr = p[:, ROPE_COL:ROPE_COL + HEAD_PAD]
        staged.append((qn, qq, ckv_n, kr))

    @pl.when(is_prompt)
    def _():
        for r, (rows, (qn, qq, ckv_n, kr)) in enumerate(zip(groups, staged)):
            cq = cq_ref[:, rows]
            for hd in range(N_HEADS):
                lo = hd * HEAD_PAD
                q_ref[lo:lo + HEAD_PAD, rows] = (qq[lo:lo + HEAD_PAD, :] * cq).astype(BF16)
            kr_ref[rows, :] = (kr * ck_ref[rows, :]).astype(BF16)
            if first_layer:
                nckv_ref[r, 0] = ckv_n
                nkr_ref[r, 0] = kr[:, :QK_ROPE]
                for later in range(1, DEPTH):
                    nckv_ref[r, later] = jnp.zeros_like(ckv_n)
                    nkr_ref[r, later] = jnp.zeros((SEQ, QK_ROPE), F32)
            else:
                nckv_ref[r] = ckv_n
                nkr_ref[r] = kr[:, :QK_ROPE]

    @pl.when(jnp.logical_not(is_prompt))
    def _():
        for rows, (qn, qq, ckv_n, kr) in zip(groups, staged):
            cq, sq = cq_ref[:, rows], sq_ref[:, rows]
            qs = _dot_nt(wq_s[n_qp:, :], qn)
            for hd in range(N_HEADS):
                lo, mid = hd * HEAD_PAD, hd * HEAD_PAD + QK_ROPE
                rope = qq[lo:mid, :] * cq[:QK_ROPE] + qs[hd * QK_ROPE:(hd + 1) * QK_ROPE, :] * sq[:QK_ROPE]
                q_ref[lo:mid, rows] = rope.astype(BF16)
                q_ref[mid:lo + HEAD_PAD, rows] = (qq[mid:lo + HEAD_PAD, :] * cq[QK_ROPE:]).astype(BF16)
            kr_ref[rows, :] = (kr * ck_ref[rows, :] + _pair_swap(kr) * sk_ref[rows, :]).astype(BF16)


def _q_select():
    n_qp = N_HEADS * HEAD_PAD
    p = np.zeros((n_qp + N_HEADS * QK_ROPE, N_HEADS * (QK_NOPE + QK_ROPE)), np.float32)
    for hd in range(N_HEADS):
        src, dst = hd * (QK_NOPE + QK_ROPE), hd * HEAD_PAD
        for j in range(QK_ROPE):
            partner = j + 8 if j % 16 < 8 else j - 8
            p[dst + j, src + QK_NOPE + j] = 1.0
            p[n_qp + hd * QK_ROPE + j, src + QK_NOPE + partner] = 1.0
        for j in range(QK_NOPE):
            p[dst + QK_ROPE + j, src + j] = 1.0
    return p.astype(BF16)


def _proj(x, mod, layer, nw, w_in_t, qn, w_uq, kvn, tabs, prev_caches):
    tm = PROJ_TM
    n_prompt_tiles = N_PROMPT // tm
    per_batch = DEC_SEQ // tm
    n_q = N_HEADS * (QK_NOPE + QK_ROPE)

    def tab_blk(i):
        return jnp.where(i < n_prompt_tiles, per_batch, (i - n_prompt_tiles) % per_batch)

    tok = lambda i: (i, 0)
    const = lambda i: (0, 0)
    first_layer = not prev_caches
    assert first_layer == (layer == 0)
    cache_layers = DEPTH if first_layer else None
    cache_idx = lambda i: (jnp.minimum(i, n_prompt_tiles - 1), 0 if first_layer else layer, 0, 0)
    tab_spec = pl.BlockSpec((tm, HEAD_PAD), lambda i: (tab_blk(i), 0))
    tab_t_spec = pl.BlockSpec((HEAD_PAD, tm), lambda i: (0, tab_blk(i)))
    return pl.pallas_call(
        functools.partial(_proj_kernel, first_layer=first_layer),
        grid=(N_TOK // tm,),
        in_specs=[
            pl.BlockSpec((tm, D_MODEL), tok),
            _mod_spec(layer, 3, tm),
            _mod_spec(layer, 4, tm),
            pl.BlockSpec((None, 1, D_MODEL), lambda i: (layer, 0, 0)),
            pl.BlockSpec((None, PROJ_W, D_MODEL), lambda i: (layer, 0, 0)),
            pl.BlockSpec((None, 1, Q_RANK), lambda i: (layer, 0, 0)),
            pl.BlockSpec((None, Q_RANK, n_q), lambda i: (layer, 0, 0)),
            pl.BlockSpec((N_HEADS * (HEAD_PAD + QK_ROPE), n_q), const),
            pl.BlockSpec((None, 1, KV_RANK), lambda i: (layer, 0, 0)),
            tab_t_spec, tab_t_spec, tab_spec, tab_spec,
        ] + [pl.BlockSpec(memory_space=pl.ANY)] * len(prev_caches),
        out_specs=[
            pl.BlockSpec((tm, D_FOUR), tok),
            pl.BlockSpec((N_HEADS * HEAD_PAD, tm), lambda i: (0, i)),
            pl.BlockSpec((tm, KV_RANK), tok),
            pl.BlockSpec((tm, HEAD_PAD), tok),
            pl.BlockSpec((PROJ_SEQS, cache_layers, SEQ, KV_RANK), cache_idx),
            pl.BlockSpec((PROJ_SEQS, cache_layers, SEQ, QK_ROPE), cache_idx),
        ],
        out_shape=[
            jax.ShapeDtypeStruct((N_TOK, D_FOUR), F32),
            jax.ShapeDtypeStruct((N_HEADS * HEAD_PAD, N_TOK), BF16),
            jax.ShapeDtypeStruct((N_TOK, KV_RANK), F32),
            jax.ShapeDtypeStruct((N_TOK, HEAD_PAD), BF16),
            jax.ShapeDtypeStruct((BATCH, DEPTH, SEQ, KV_RANK), F32),
            jax.ShapeDtypeStruct((BATCH, DEPTH, SEQ, QK_ROPE), F32),
        ],
        input_output_aliases={13 + k: 4 + k for k in range(len(prev_caches))},
        scratch_shapes=[
            pltpu.VMEM((PROJ_W, D_MODEL), BF16),
            pltpu.VMEM((N_HEADS * (HEAD_PAD + QK_ROPE), Q_RANK), BF16),
        ],
        compiler_params=pltpu.CompilerParams(
            dimension_semantics=("arbitrary",), vmem_limit_bytes=VMEM_LIMIT),
        name="mixer_proj",
    )(x, mod, mod, nw.reshape(DEPTH, 1, D_MODEL), w_in_t, qn.reshape(DEPTH, 1, Q_RANK), w_uq, _q_select(),
      kvn.reshape(DEPTH, 1, KV_RANK), *tabs(tm), *prev_caches)


DFT_ROWS = 1024


def _split(x):
    hi = x.astype(BF16)
    lo = (x - hi.astype(F32)).astype(BF16)
    return hi, lo


def _dft_kernel(x_ref, gh_ref, gl_ref, th_ref, tl_ref, o_ref, *, n):
    xh, xl = _split(x_ref[...])
    gh, gl = gh_ref[...], gl_ref[...]
    y = _dot(xh, gh) + _dot(xl, gh) + _dot(xh, gl)
    yh, yl = _split(y)
    th, tl = th_ref[...], tl_ref[...]
    for b in range(x_ref.shape[0] // n):
        r = slice(b * n, (b + 1) * n)
        ych = jnp.concatenate([yh[r, :D_FOUR], yh[r, D_FOUR:]], axis=0)
        ycl = jnp.concatenate([yl[r, :D_FOUR], yl[r, D_FOUR:]], axis=0)
        fr = _dot(th, ych) + _dot(tl, ych) + _dot(th, ycl)
        o_ref[r, :] = fr.astype(BF16)


def _split_np(a):
    a32 = np.asarray(a, np.float32)
    hi = a32.astype(BF16)
    lo = (a32 - hi.astype(np.float32)).astype(BF16)
    return hi, lo


def _dft_tables(n):
    k = np.arange(n, dtype=np.int64)
    ang = 2.0 * np.pi * ((k[:, None] * k[None, :]) % n).astype(np.float64) / n
    t = np.concatenate([np.cos(ang), -np.sin(ang)], axis=1) / np.sqrt(n)
    c = np.arange(FOUR_GROUP_DIM, dtype=np.int64)
    ang_c = 2.0 * np.pi * ((c[:, None] * c[None, :]) % FOUR_GROUP_DIM).astype(np.float64) / FOUR_GROUP_DIM
    eye = np.eye(FOUR_GROUPS)
    g = np.concatenate([np.kron(eye, np.cos(ang_c)), np.kron(eye, np.sin(ang_c))], axis=1)
    g = g / np.sqrt(FOUR_GROUP_DIM)
    return _split_np(g) + _split_np(t)


def _dft(f_in, n_batch, n, row0):
    gh, gl, th, tl = _dft_tables(n)
    rows = max(n, DFT_ROWS)
    blk0 = row0 // rows
    const = lambda b: (0, 0)
    return pl.pallas_call(
        functools.partial(_dft_kernel, n=n),
        grid=(n_batch * n // rows,),
        in_specs=[
            pl.BlockSpec((rows, D_FOUR), lambda b: (blk0 + b, 0)),
            pl.BlockSpec((D_FOUR, 2 * D_FOUR), const),
            pl.BlockSpec((D_FOUR, 2 * D_FOUR), const),
            pl.BlockSpec((n, 2 * n), const),
            pl.BlockSpec((n, 2 * n), const),
        ],
        out_specs=pl.BlockSpec((rows, D_FOUR), lambda b: (b, 0)),
        out_shape=jax.ShapeDtypeStruct((n_batch * n, D_FOUR), BF16),
        compiler_params=pltpu.CompilerParams(
            dimension_semantics=("arbitrary",), vmem_limit_bytes=VMEM_LIMIT),
        name="fnet_dft_%d" % n,
    )(f_in, gh, gl, th, tl)


ATTN_TQ = 512
ATTN_SEQS = 4
ATTN_AHEAD = 3


def _attn_kernel(*refs, n_own, n_ctx, seqs):
    if n_ctx:
        qt_ref, ckv_ref, kr_ref, cckv_ref, ckr_ref, wkv_ref, pk_ref, pvt_ref = refs[:8]
    else:
        qt_ref, ckv_ref, kr_ref, wkv_ref, pk_ref, pvt_ref = refs[:6]
    o_ref, k_scr, vt_scr, wk_s, wvt_s = refs[-5:]

    @pl.when(jnp.logical_and(pl.program_id(0) == 0, pl.program_id(1) == 0))
    def _():
        wkv = wkv_ref[...].astype(BF16)
        wk_s[...] = _dot(wkv, pk_ref[...]).astype(BF16)
        wvt_s[...] = _dot_nt(pvt_ref[...], wkv).astype(BF16)

    @pl.when(pl.program_id(1) == 0)
    def _():
        ckv = ckv_ref[...].astype(BF16)
        kr = kr_ref[...]
        if n_ctx:
            ckv = jnp.concatenate([ckv, cckv_ref[...].astype(BF16)], axis=0)
            kr = jnp.concatenate([kr, ckr_ref[...].astype(BF16)], axis=0)
        k_nope = _dot(ckv, wk_s[...])
        kr = kr.astype(F32)
        for hd in range(N_HEADS):
            lo = hd * HEAD_PAD
            k_scr[:, lo:lo + HEAD_PAD] = (k_nope[:, lo:lo + HEAD_PAD] + kr).astype(BF16)
        vt_scr[...] = _dot_nt(wvt_s[...], ckv).astype(BF16)

    n_keys = n_own + n_ctx
    tq = qt_ref.shape[1] // seqs
    units = [(j, hd) for j in range(seqs) for hd in range(N_HEADS)]

    def scores(j, hd):
        lo = hd * HEAD_PAD
        return _dot(k_scr[j * n_keys:(j + 1) * n_keys, lo:lo + HEAD_PAD],
                    qt_ref[lo:lo + HEAD_PAD, j * tq:(j + 1) * tq])

    outs = []
    queue = [scores(*u) for u in units[:ATTN_AHEAD]]
    for i, (j, hd) in enumerate(units):
        s = queue.pop(0)
        if i + ATTN_AHEAD < len(units):
            queue.append(scores(*units[i + ATTN_AHEAD]))
        m = jnp.max(s, axis=0, keepdims=True)
        p = jnp.exp(s - m)
        l = jnp.sum(p, axis=0, keepdims=True)
        vt = vt_scr[hd * V_DIM:(hd + 1) * V_DIM, j * n_keys:(j + 1) * n_keys]
        outs.append(_dot(vt, p.astype(BF16)) / l)
        if hd == N_HEADS - 1:
            o_ref[j * tq:(j + 1) * tq, :] = jnp.concatenate(outs, axis=0).T.astype(BF16)
            outs = []


def _kv_select():
    n_kv = N_HEADS * (QK_NOPE + V_DIM)
    pk = np.zeros((n_kv, N_HEADS * HEAD_PAD), np.float32)
    pvt = np.zeros((N_HEADS * V_DIM, n_kv), np.float32)
    for hd in range(N_HEADS):
        src = hd * (QK_NOPE + V_DIM)
        for j in range(QK_NOPE):
            pk[src + j, hd * HEAD_PAD + QK_ROPE + j] = 1.0
        for j in range(V_DIM):
            pvt[hd * V_DIM + j, src + QK_NOPE + j] = 1.0
    return pk.astype(BF16), pvt.astype(BF16)


def _attention(q, ckv_n, kr, w_ukv, layer, n_batch, n_own, row0, ctx=None):
    tq = min(n_own, ATTN_TQ)
    n_kv = N_HEADS * (QK_NOPE + V_DIM)
    n_ctx = 0 if ctx is None else PAST_LEN
    n_keys = n_own + n_ctx
    qt = n_own // tq
    seqs = ATTN_SEQS if (ctx is None and qt == 1) else 1
    blk0 = row0 // (seqs * n_own)
    const = lambda b, t: (0, 0)
    in_specs = [
        pl.BlockSpec((N_HEADS * HEAD_PAD, seqs * tq), lambda b, t: (0, (row0 // (seqs * tq)) + b * qt + t)),
        pl.BlockSpec((seqs * n_own, KV_RANK), lambda b, t: (blk0 + b, 0)),
        pl.BlockSpec((seqs * n_own, HEAD_PAD), lambda b, t: (blk0 + b, 0)),
    ]
    args = [q, ckv_n, kr]
    if ctx is not None:
        cckv, ckr = ctx
        in_specs += [
            pl.BlockSpec((None, None, PAST_LEN, KV_RANK), lambda b, t: (b, layer, 0, 0)),
            pl.BlockSpec((None, None, PAST_LEN, HEAD_PAD), lambda b, t: (b, layer, 0, 0)),
        ]
        args += [cckv, ckr]
    in_specs += [
        pl.BlockSpec((None, KV_RANK, n_kv), lambda b, t: (layer, 0, 0)),
        pl.BlockSpec((n_kv, N_HEADS * HEAD_PAD), const),
        pl.BlockSpec((N_HEADS * V_DIM, n_kv), const),
    ]
    args += [w_ukv, *_kv_select()]
    return pl.pallas_call(
        functools.partial(_attn_kernel, n_own=n_own, n_ctx=n_ctx, seqs=seqs),
        grid=(n_batch // seqs, qt),
        in_specs=in_specs,
        out_specs=pl.BlockSpec((seqs * tq, N_HEADS * V_DIM), lambda b, t: (b * qt + t, 0)),
        out_shape=jax.ShapeDtypeStruct((n_batch * n_own, N_HEADS * V_DIM), BF16),
        scratch_shapes=[
            pltpu.VMEM((seqs * n_keys, N_HEADS * HEAD_PAD), BF16),
            pltpu.VMEM((N_HEADS * V_DIM, seqs * n_keys), BF16),
            pltpu.VMEM((KV_RANK, N_HEADS * HEAD_PAD), BF16),
            pltpu.VMEM((N_HEADS * V_DIM, KV_RANK), BF16),
        ],
        compiler_params=pltpu.CompilerParams(
            dimension_semantics=("arbitrary", "arbitrary"), vmem_limit_bytes=VMEM_LIMIT),
        name="mla_attention_%d" % n_keys,
    )(*args)


TAIL_TM = 1024
TAIL_GROUP = 256
TAIL_SLABS = 4
GATE_ROW = ROPE_COL + QK_ROPE
GATE_CHUNK = 512
GATE_CHUNKS = 2 * D_MODEL // GATE_CHUNK


def _tail_kernel(x_ref, sh_ref, sc_ref, g_ref, nw_ref, frp_ref, frs_ref, op_ref, os_ref, wi_hbm,
                 wf_ref, wa_ref, wo_ref, out_ref, wg_s, wf_s, wa_s, wo_s, stage, sem, *, layer):
    s = pl.program_id(0)
    is_prompt = jnp.maximum(s - TAIL_SLABS, 0) < N_PROMPT // TAIL_TM

    def gate_copy(c):
        src = wi_hbm.at[layer, pl.ds(GATE_ROW + c * GATE_CHUNK, GATE_CHUNK), :]
        return pltpu.make_async_copy(src, stage.at[c % 2], sem.at[c % 2])

    @pl.when(s == 0)
    def _():
        gate_copy(0).start()
        for c in range(GATE_CHUNKS):
            if c + 1 < GATE_CHUNKS:
                gate_copy(c + 1).start()
            gate_copy(c).wait()
            wg_s[c * GATE_CHUNK:(c + 1) * GATE_CHUNK, :] = stage[c % 2].astype(BF16)

    @pl.when(s < TAIL_SLABS)
    def _():
        def put(dst, src):
            rows = src.shape[0]
            dst[pl.ds(pl.multiple_of(s * rows, rows), rows), :] = src.astype(BF16)

        put(wf_s, wf_ref[...])
        put(wa_s, wa_ref[...])
        put(wo_s, wo_ref[...])

    @pl.when(s >= TAIL_SLABS)
    def _():
        groups = [slice(r * TAIL_GROUP, (r + 1) * TAIL_GROUP) for r in range(TAIL_TM // TAIL_GROUP)]

        def branches(rows):
            x = x_ref[rows, :]
            h = _norm_mod(x, nw_ref[...], sc_ref[...], sh_ref[...]).astype(BF16)
            gates = _dot_nt(h, wg_s[...])
            a_out = _dot(jnp.where(is_prompt, frp_ref[rows, :], frs_ref[rows, :]), wf_s[...])
            o_out = _dot(jnp.where(is_prompt, op_ref[rows, :], os_ref[rows, :]), wa_s[...])
            return x, gates, a_out, o_out

        def merge(rows, x, gates, a_out, o_out):
            merged = jax.nn.sigmoid(gates[:, :D_MODEL]) * a_out + jax.nn.sigmoid(gates[:, D_MODEL:]) * o_out
            m = _dot(merged.astype(BF16), wo_s[...])
            out_ref[rows, :] = x + g_ref[...] * m

        pending = branches(groups[0])
        for r in range(1, len(groups)):
            ready, pending = pending, branches(groups[r])
            merge(groups[r - 1], *ready)
        merge(groups[-1], *pending)


def _tail(x, mod, layer, nw, frs, os_, w_in_t, w_four, w_attn_proj, w_out):
    tm = TAIL_TM
    npt = N_PROMPT // tm
    tile = lambda s: jnp.maximum(s - TAIL_SLABS, 0)
    slab = lambda s: jnp.minimum(s, TAIL_SLABS - 1)
    tok = lambda s: (tile(s), 0)
    tok_p = lambda s: (jnp.minimum(tile(s), npt - 1), 0)
    tok_s = lambda s: (jnp.maximum(tile(s) - npt, 0), 0)
    n_attn = N_HEADS * V_DIM

    def slab_spec(rows, cols):
        return pl.BlockSpec((None, rows // TAIL_SLABS, cols), lambda s: (layer, slab(s), 0))

    return pl.pallas_call(
        functools.partial(_tail_kernel, layer=layer),
        grid=(TAIL_SLABS + N_TOK // tm,),
        in_specs=[
            pl.BlockSpec((tm, D_MODEL), tok),
            _mod_spec(layer, 3, tm, tile),
            _mod_spec(layer, 4, tm, tile),
            _mod_spec(layer, 5, tm, tile),
            pl.BlockSpec((None, 1, D_MODEL), lambda s: (layer, 0, 0)),
            pl.BlockSpec((tm, D_FOUR), tok_p),
            pl.BlockSpec((tm, D_FOUR), tok_s),
            pl.BlockSpec((tm, n_attn), tok_p),
            pl.BlockSpec((tm, n_attn), tok_s),
            pl.BlockSpec(memory_space=pl.ANY),
            slab_spec(D_FOUR, D_MODEL),
            slab_spec(n_attn, D_MODEL),
            slab_spec(D_MODEL, D_MODEL),
        ],
        out_specs=pl.BlockSpec((tm, D_MODEL), tok),
        out_shape=jax.ShapeDtypeStruct((N_TOK, D_MODEL), F32),
        scratch_shapes=[
            pltpu.VMEM((2 * D_MODEL, D_MODEL), BF16),
            pltpu.VMEM((D_FOUR, D_MODEL), BF16),
            pltpu.VMEM((n_attn, D_MODEL), BF16),
            pltpu.VMEM((D_MODEL, D_MODEL), BF16),
            pltpu.VMEM((2, GATE_CHUNK, D_MODEL), F32),
            pltpu.SemaphoreType.DMA((2,)),
        ],
        compiler_params=pltpu.CompilerParams(
            dimension_semantics=("arbitrary",), vmem_limit_bytes=VMEM_LIMIT),
        name="mixer_tail",
    )(x, mod, mod, mod, nw.reshape(DEPTH, 1, D_MODEL), *frs, *os_, w_in_t, w_four, w_attn_proj, w_out)


def _rope_tables(tm):
    rows = DEC_SEQ // GRID_W
    row = np.repeat(np.arange(rows), GRID_W).astype(np.float64)
    col = np.tile(np.arange(GRID_W), rows).astype(np.float64)
    axis_dim = QK_ROPE // 2
    inv = ROPE_BASE ** (-np.arange(0, axis_dim, 2, dtype=np.float64) / axis_dim)
    ar = row[:, None] * inv
    ac = col[:, None] * inv
    cr, sr, cc, sc = np.cos(ar), np.sin(ar), np.cos(ac), np.sin(ac)
    cos32 = np.concatenate([cr, cr, cc, cc], axis=1)
    sin32 = np.concatenate([-sr, sr, -sc, sc], axis=1)
    scale = np.float32((QK_NOPE + QK_ROPE) ** -0.5)

    def table(rope_part, nope_val, ident_rope):
        t = np.zeros((DEC_SEQ + tm, HEAD_PAD), np.float32)
        t[:DEC_SEQ, :QK_ROPE] = rope_part
        t[:DEC_SEQ, QK_ROPE:QK_ROPE + QK_NOPE] = nope_val
        t[DEC_SEQ:, :QK_ROPE] = ident_rope
        t[DEC_SEQ:, QK_ROPE:QK_ROPE + QK_NOPE] = nope_val
        return t

    cq = (table(cos32, 1.0, 1.0) * scale).T
    sq = (table(sin32, 0.0, 0.0) * scale).T
    ck = table(cos32, 0.0, 1.0)
    sk = table(sin32, 0.0, 0.0)
    return tuple(jnp.asarray(np.ascontiguousarray(t), F32) for t in (cq, sq, ck, sk))


def kernel(x_prompt, x_sample, cache_ckv, cache_krope, c, c_ctx, w_mod, b_mod, norm_ffn1, w_ffn1_gate,
           w_ffn1_up, w_ffn1_down, norm_mix, w_in, w_four, q_norm, w_uq, kv_norm, w_ukv, w_attn_proj,
           w_out, norm_ffn2, w_ffn2_gate, w_ffn2_up, w_ffn2_down, final_norm):
    xs = (x_prompt.reshape(N_PROMPT, D_MODEL), x_sample.reshape(N_SAMPLE, D_MODEL))
    c_all = jnp.concatenate(
        [c_ctx[None, :], c, jnp.zeros((MOD_ROWS - 1 - DEC_BATCH, D_MODEL), F32)], axis=0)
    mod = _modulation(c_all, w_mod, b_mod).reshape(DEPTH * MOD_ROWS * N_MOD, 1, D_MODEL)
    cache_kr = jnp.pad(cache_krope, ((0, 0), (0, 0), (0, 0), (0, HEAD_PAD - QK_ROPE)))
    w_in_t = jnp.swapaxes(w_in, 1, 2)

    caches = ()
    for l in range(DEPTH):
        x = _ffn(xs, mod, l, 0, norm_ffn1, w_ffn1_gate, w_ffn1_up, w_ffn1_down, final_norm, False)
        f_in, q, ckv_n, kr, new_ckv, new_krope = _proj(
            x, mod, l, norm_mix, w_in_t, q_norm, w_uq, kv_norm, _rope_tables, caches)
        caches = (new_ckv, new_krope)
        frs = (_dft(f_in, BATCH, SEQ, 0), _dft(f_in, DEC_BATCH, DEC_SEQ, N_PROMPT))
        os_ = (_attention(q, ckv_n, kr, w_ukv, l, BATCH, SEQ, 0),
               _attention(q, ckv_n, kr, w_ukv, l, DEC_BATCH, DEC_SEQ, N_PROMPT, (cache_ckv, cache_kr)))
        x = _tail(x, mod, l, norm_mix, frs, os_, w_in_t, w_four, w_attn_proj, w_out)
        xs = _ffn((x,), mod, l, 6, norm_ffn2, w_ffn2_gate, w_ffn2_up, w_ffn2_down, final_norm, l == DEPTH - 1)
        if l < DEPTH - 1:
            xs = (xs,)

    y_prompt = xs[0].reshape(BATCH, SEQ, D_MODEL)
    y_sample = xs[1].reshape(DEC_BATCH, DEC_SEQ, D_MODEL)
    return y_prompt, y_sample, caches[0], caches[1]
```

```python
import functools

import numpy as np
import jax
import jax.numpy as jnp
from jax import lax
from jax.experimental import pallas as pl
from jax.experimental.pallas import tpu as pltpu

D_MODEL = 1024
BATCH = 16
SEQ = 256
DEPTH = 2
DEC_BATCH = 4
DEC_SEQ = 1024
PAST_LEN = 512
GRID_W = 64
D_FF = 2816
FOUR_GROUPS = 4
FOUR_GROUP_DIM = 64
D_FOUR = FOUR_GROUPS * FOUR_GROUP_DIM
N_HEADS = 8
QK_NOPE = 64
QK_ROPE = 32
V_DIM = 64
Q_RANK = 384
KV_RANK = 256
N_MOD = 9
ROPE_BASE = 10000.0
EPS = 1e-6

N_PROMPT = BATCH * SEQ
N_SAMPLE = DEC_BATCH * DEC_SEQ
N_TOK = N_PROMPT + N_SAMPLE
MOD_ROWS = 8
HEAD_PAD = 128
KEYS_SAMPLE = DEC_SEQ + PAST_LEN

VMEM_LIMIT = 52 * 1024 * 1024

F32 = jnp.float32
BF16 = jnp.bfloat16


def _dot(a, b):
    return jnp.dot(a, b, preferred_element_type=F32)


def _dot_nt(a, b):
    return lax.dot_general(a, b, (((1,), (1,)), ((), ())), preferred_element_type=F32)


def _mod_row(i, tm):
    n_prompt_tiles = N_PROMPT // tm
    per_batch = DEC_SEQ // tm
    return jnp.where(i < n_prompt_tiles, 0, 1 + (i - n_prompt_tiles) // per_batch)


def _mod_spec(layer, which, tm, tile_of_step=lambda i: i):
    def idx(*g):
        return ((layer * MOD_ROWS + _mod_row(tile_of_step(g[0]), tm)) * N_MOD + which, 0, 0)
    return pl.BlockSpec((None, 1, D_MODEL), idx)


def _norm_mod(x, nw, sc, sh):
    ms = jnp.mean(x * x, axis=-1, keepdims=True)
    y = x * lax.rsqrt(ms + EPS) * nw
    return y * (1.0 + sc) + sh


def _mod_kernel(c_ref, w_ref, b_ref, o_ref):
    c = c_ref[...]
    a = (c * jax.nn.sigmoid(c)).astype(BF16)
    o_ref[...] = _dot(a, w_ref[...].astype(BF16)) + b_ref[...]


def _modulation(c_all, w_mod, b_mod):
    tn = 1024
    n_out = N_MOD * D_MODEL
    return pl.pallas_call(
        _mod_kernel,
        grid=(DEPTH, n_out // tn),
        in_specs=[
            pl.BlockSpec((MOD_ROWS, D_MODEL), lambda l, j: (0, 0)),
            pl.BlockSpec((None, D_MODEL, tn), lambda l, j: (l, 0, j)),
            pl.BlockSpec((None, 1, tn), lambda l, j: (l, 0, j)),
        ],
        out_specs=pl.BlockSpec((None, MOD_ROWS, tn), lambda l, j: (l, 0, j)),
        out_shape=jax.ShapeDtypeStruct((DEPTH, MOD_ROWS, n_out), F32),
        compiler_params=pltpu.CompilerParams(
            dimension_semantics=("arbitrary", "arbitrary"), vmem_limit_bytes=VMEM_LIMIT),
        name="modulation",
    )(c_all, w_mod, b_mod.reshape(DEPTH, 1, n_out))


FFN_TM = 512
FFN_CHUNK = 256
FFN_NC = D_FF // FFN_CHUNK


def _ffn_kernel(*refs, final, split_in):
    refs = list(refs)
    x_refs = [refs.pop(0) for _ in range(2 if split_in else 1)]
    sh_ref, sc_ref, g_ref, nw_ref, wg_ref, wu_ref, wd_ref, fn_ref = refs[:8]
    o_refs = refs[8:10] if final else refs[8:9]
    wg_s, wu_s, wd_s, h_scr, acc_scr = refs[-5:]
    s = pl.program_id(0)
    is_prompt = jnp.maximum(s - (FFN_NC - 1), 0) < N_PROMPT // FFN_TM

    def x_tile():
        if split_in:
            return jnp.where(is_prompt, x_refs[0][...], x_refs[1][...])
        return x_refs[0][...]

    def chunk_act(h, wg, wu):
        gate = _dot(h, wg)
        up = _dot(h, wu)
        return ((gate * jax.nn.sigmoid(gate)) * up).astype(BF16)

    def finish(acc):
        xn = x_tile() + 0.5 * g_ref[...] * acc
        if not final:
            o_refs[0][...] = xn
            return
        ms = jnp.mean(xn * xn, axis=-1, keepdims=True)
        y = xn * lax.rsqrt(ms + EPS) * fn_ref[...]

        @pl.when(is_prompt)
        def _():
            o_refs[0][...] = y

        @pl.when(jnp.logical_not(is_prompt))
        def _():
            o_refs[1][...] = y

    def hidden():
        return _norm_mod(x_tile(), nw_ref[...], sc_ref[...], sh_ref[...]).astype(BF16)

    @pl.when(s == 0)
    def _():
        h_scr[...] = hidden()
        acc_scr[...] = jnp.zeros_like(acc_scr)

    @pl.when(s < FFN_NC)
    def _():
        wg = wg_ref[...].astype(BF16)
        wu = wu_ref[...].astype(BF16)
        wd = wd_ref[...].astype(BF16)
        wg_s[s] = wg
        wu_s[s] = wu
        wd_s[pl.ds(pl.multiple_of(s * FFN_CHUNK, FFN_CHUNK), FFN_CHUNK), :] = wd
        acc_scr[...] += _dot(chunk_act(h_scr[...], wg, wu), wd)

    @pl.when(s == FFN_NC - 1)
    def _():
        finish(acc_scr[...])

    @pl.when(s >= FFN_NC)
    def _():
        h = hidden()
        act = jnp.concatenate([chunk_act(h, wg_s[j], wu_s[j]) for j in range(FFN_NC)], axis=1)
        finish(_dot(act, wd_s[...]))


def _ffn(xs, mod, layer, mod_base, nw, wg, wu, wd, final_norm, final):
    tm = FFN_TM
    npt = N_PROMPT // tm
    split_in = len(xs) == 2
    tile = lambda s: jnp.maximum(s - (FFN_NC - 1), 0)
    chunk = lambda s: jnp.minimum(s, FFN_NC - 1)
    tok = pl.BlockSpec((tm, D_MODEL), lambda s: (tile(s), 0))
    tok_p = pl.BlockSpec((tm, D_MODEL), lambda s: (jnp.minimum(tile(s), npt - 1), 0))
    tok_s = pl.BlockSpec((tm, D_MODEL), lambda s: (jnp.maximum(tile(s) - npt, 0), 0))
    half = jax.ShapeDtypeStruct((N_PROMPT, D_MODEL), F32)
    return pl.pallas_call(
        functools.partial(_ffn_kernel, final=final, split_in=split_in),
        grid=(FFN_NC + N_TOK // tm - 1,),
        in_specs=([tok_p, tok_s] if split_in else [tok]) + [
            _mod_spec(layer, mod_base + 0, tm, tile),
            _mod_spec(layer, mod_base + 1, tm, tile),
            _mod_spec(layer, mod_base + 2, tm, tile),
            pl.BlockSpec((None, 1, D_MODEL), lambda s: (layer, 0, 0)),
            pl.BlockSpec((None, D_MODEL, FFN_CHUNK), lambda s: (layer, 0, chunk(s))),
            pl.BlockSpec((None, D_MODEL, FFN_CHUNK), lambda s: (layer, 0, chunk(s))),
            pl.BlockSpec((None, FFN_CHUNK, D_MODEL), lambda s: (layer, chunk(s), 0)),
            pl.BlockSpec((1, D_MODEL), lambda s: (0, 0)),
        ],
        out_specs=[tok_p, tok_s] if final else tok,
        out_shape=[half, half] if final else jax.ShapeDtypeStruct((N_TOK, D_MODEL), F32),
        scratch_shapes=[
            pltpu.VMEM((FFN_NC, D_MODEL, FFN_CHUNK), BF16),
            pltpu.VMEM((FFN_NC, D_MODEL, FFN_CHUNK), BF16),
            pltpu.VMEM((D_FF, D_MODEL), BF16),
            pltpu.VMEM((tm, D_MODEL), BF16),
            pltpu.VMEM((tm, D_MODEL), F32),
        ],
        compiler_params=pltpu.CompilerParams(
            dimension_semantics=("arbitrary",), vmem_limit_bytes=VMEM_LIMIT),
        name="ffn_final" if final else ("ffn_first" if split_in else "ffn"),
    )(*xs, mod, mod, mod, nw.reshape(DEPTH, 1, D_MODEL), wg, wu, wd, final_norm.reshape(1, D_MODEL))


PROJ_TM = 1024
PROJ_SEQS = PROJ_TM // SEQ
PROJ_W = 1024
ROPE_COL = D_FOUR + Q_RANK + KV_RANK


def _pair_swap(x):
    lane = lax.broadcasted_iota(jnp.int32, x.shape, 1)
    return jnp.where(lane % 16 < 8, pltpu.roll(x, HEAD_PAD - 8, 1), pltpu.roll(x, 8, 1))


def _proj_kernel(*refs, first_layer):
    (x_ref, sh_ref, sc_ref, nw_ref, w1_ref, qn_ref, wq_ref, pq_ref, kvn_ref,
     cq_ref, sq_ref, ck_ref, sk_ref) = refs[:13]
    f_ref, q_ref, ckv_ref, kr_ref, nckv_ref, nkr_ref, w1_s, wq_s = refs[-8:]
    is_prompt = pl.program_id(0) < N_PROMPT // PROJ_TM

    @pl.when(pl.program_id(0) == 0)
    def _():
        w1_s[...] = w1_ref[...].astype(BF16)
        wq_s[...] = _dot_nt(pq_ref[...], wq_ref[...].astype(BF16)).astype(BF16)

    n_qp = N_HEADS * HEAD_PAD
    c0 = D_FOUR + Q_RANK
    groups = [slice(r * SEQ, (r + 1) * SEQ) for r in range(PROJ_SEQS)]
    ps = []
    for rows in groups:
        h = _norm_mod(x_ref[rows, :], nw_ref[...], sc_ref[...], sh_ref[...]).astype(BF16)
        ps.append(_dot_nt(h, w1_s[...]))
    staged = []
    for rows, p in zip(groups, ps):
        f_ref[rows, :] = p[:, :D_FOUR]
        ql = p[:, D_FOUR:c0]
        qn = ql * lax.rsqrt(jnp.mean(ql * ql, axis=-1, keepdims=True) + EPS) * qn_ref[...]
        qn = qn.astype(BF16)
        qq = _dot_nt(wq_s[:n_qp, :], qn)
        ckv = p[:, c0:c0 + KV_RANK]
        ckv_n = ckv * lax.rsqrt(jnp.mean(ckv * ckv, axis=-1, keepdims=True) + EPS) * kvn_ref[...]
        ckv_ref[rows, :] = ckv_n
        kr = p[:, ROPE_COL:ROPE_COL + HEAD_PAD]
        staged.append((qn, qq, ckv_n, kr))

    @pl.when(is_prompt)
    def _():
        for r, (rows, (qn, qq, ckv_n, kr)) in enumerate(zip(groups, staged)):
            cq = cq_ref[:, rows]
            for hd in range(N_HEADS):
                lo = hd * HEAD_PAD
                q_ref[lo:lo + HEAD_PAD, rows] = (qq[lo:lo + HEAD_PAD, :] * cq).astype(BF16)
            kr_ref[rows, :] = (kr * ck_ref[rows, :]).astype(BF16)
            if first_layer:
                nckv_ref[r, 0] = ckv_n
                nkr_ref[r, 0] = kr[:, :QK_ROPE]
                for later in range(1, DEPTH):
                    nckv_ref[r, later] = jnp.zeros_like(ckv_n)
                    nkr_ref[r, later] = jnp.zeros((SEQ, QK_ROPE), F32)
            else:
                nckv_ref[r] = ckv_n
                nkr_ref[r] = kr[:, :QK_ROPE]

    @pl.when(jnp.logical_not(is_prompt))
    def _():
        for rows, (qn, qq, ckv_n, kr) in zip(groups, staged):
            cq, sq = cq_ref[:, rows], sq_ref[:, rows]
            qs = _dot_nt(wq_s[n_qp:, :], qn)
            for hd in range(N_HEADS):
                lo, mid = hd * HEAD_PAD, hd * HEAD_PAD + QK_ROPE
                rope = qq[lo:mid, :] * cq[:QK_ROPE] + qs[hd * QK_ROPE:(hd + 1) * QK_ROPE, :] * sq[:QK_ROPE]
                q_ref[lo:mid, rows] = rope.astype(BF16)
                q_ref[mid:lo + HEAD_PAD, rows] = (qq[mid:lo + HEAD_PAD, :] * cq[QK_ROPE:]).astype(BF16)
            kr_ref[rows, :] = (kr * ck_ref[rows, :] + _pair_swap(kr) * sk_ref[rows, :]).astype(BF16)


def _q_select():
    n_qp = N_HEADS * HEAD_PAD
    p = np.zeros((n_qp + N_HEADS * QK_ROPE, N_HEADS * (QK_NOPE + QK_ROPE)), np.float32)
    for hd in range(N_HEADS):
        src, dst = hd * (QK_NOPE + QK_ROPE), hd * HEAD_PAD
        for j in range(QK_ROPE):
            partner = j + 8 if j % 16 < 8 else j - 8
            p[dst + j, src + QK_NOPE + j] = 1.0
            p[n_qp + hd * QK_ROPE + j, src + QK_NOPE + partner] = 1.0
        for j in range(QK_NOPE):
            p[dst + QK_ROPE + j, src + j] = 1.0
    return p.astype(BF16)


def _proj(x, mod, layer, nw, w_in_t, qn, w_uq, kvn, tabs, prev_caches):
    tm = PROJ_TM
    n_prompt_tiles = N_PROMPT // tm
    per_batch = DEC_SEQ // tm
    n_q = N_HEADS * (QK_NOPE + QK_ROPE)

    def tab_blk(i):
        return jnp.where(i < n_prompt_tiles, per_batch, (i - n_prompt_tiles) % per_batch)

    tok = lambda i: (i, 0)
    const = lambda i: (0, 0)
    first_layer = not prev_caches
    assert first_layer == (layer == 0)
    cache_layers = DEPTH if first_layer else None
    cache_idx = lambda i: (jnp.minimum(i, n_prompt_tiles - 1), 0 if first_layer else layer, 0, 0)
    tab_spec = pl.BlockSpec((tm, HEAD_PAD), lambda i: (tab_blk(i), 0))
    tab_t_spec = pl.BlockSpec((HEAD_PAD, tm), lambda i: (0, tab_blk(i)))
    return pl.pallas_call(
        functools.partial(_proj_kernel, first_layer=first_layer),
        grid=(N_TOK // tm,),
        in_specs=[
            pl.BlockSpec((tm, D_MODEL), tok),
            _mod_spec(layer, 3, tm),
            _mod_spec(layer, 4, tm),
            pl.BlockSpec((None, 1, D_MODEL), lambda i: (layer, 0, 0)),
            pl.BlockSpec((None, PROJ_W, D_MODEL), lambda i: (layer, 0, 0)),
            pl.BlockSpec((None, 1, Q_RANK), lambda i: (layer, 0, 0)),
            pl.BlockSpec((None, Q_RANK, n_q), lambda i: (layer, 0, 0)),
            pl.BlockSpec((N_HEADS * (HEAD_PAD + QK_ROPE), n_q), const),
            pl.BlockSpec((None, 1, KV_RANK), lambda i: (layer, 0, 0)),
            tab_t_spec, tab_t_spec, tab_spec, tab_spec,
        ] + [pl.BlockSpec(memory_space=pl.ANY)] * len(prev_caches),
        out_specs=[
            pl.BlockSpec((tm, D_FOUR), tok),
            pl.BlockSpec((N_HEADS * HEAD_PAD, tm), lambda i: (0, i)),
            pl.BlockSpec((tm, KV_RANK), tok),
            pl.BlockSpec((tm, HEAD_PAD), tok),
            pl.BlockSpec((PROJ_SEQS, cache_layers, SEQ, KV_RANK), cache_idx),
            pl.BlockSpec((PROJ_SEQS, cache_layers, SEQ, QK_ROPE), cache_idx),
        ],
        out_shape=[
            jax.ShapeDtypeStruct((N_TOK, D_FOUR), F32),
            jax.ShapeDtypeStruct((N_HEADS * HEAD_PAD, N_TOK), BF16),
            jax.ShapeDtypeStruct((N_TOK, KV_RANK), F32),
            jax.ShapeDtypeStruct((N_TOK, HEAD_PAD), BF16),
            jax.ShapeDtypeStruct((BATCH, DEPTH, SEQ, KV_RANK), F32),
            jax.ShapeDtypeStruct((BATCH, DEPTH, SEQ, QK_ROPE), F32),
        ],
        input_output_aliases={13 + k: 4 + k for k in range(len(prev_caches))},
        scratch_shapes=[
            pltpu.VMEM((PROJ_W, D_MODEL), BF16),
            pltpu.VMEM((N_HEADS * (HEAD_PAD + QK_ROPE), Q_RANK), BF16),
        ],
        compiler_params=pltpu.CompilerParams(
            dimension_semantics=("arbitrary",), vmem_limit_bytes=VMEM_LIMIT),
        name="mixer_proj",
    )(x, mod, mod, nw.reshape(DEPTH, 1, D_MODEL), w_in_t, qn.reshape(DEPTH, 1, Q_RANK), w_uq, _q_select(),
      kvn.reshape(DEPTH, 1, KV_RANK), *tabs(tm), *prev_caches)


DFT_ROWS = 1024


def _split(x):
    hi = x.astype(BF16)
    lo = (x - hi.astype(F32)).astype(BF16)
    return hi, lo


def _dft_kernel(x_ref, gh_ref, gl_ref, th_ref, tl_ref, o_ref, *, n):
    xh, xl = _split(x_ref[...])
    gh, gl = gh_ref[...], gl_ref[...]
    y = _dot(xh, gh) + _dot(xl, gh) + _dot(xh, gl)
    yh, yl = _split(y)
    th, tl = th_ref[...], tl_ref[...]
    for b in range(x_ref.shape[0] // n):
        r = slice(b * n, (b + 1) * n)
        ych = jnp.concatenate([yh[r, :D_FOUR], yh[r, D_FOUR:]], axis=0)
        ycl = jnp.concatenate([yl[r, :D_FOUR], yl[r, D_FOUR:]], axis=0)
        fr = _dot(th, ych) + _dot(tl, ych) + _dot(th, ycl)
        o_ref[r, :] = fr.astype(BF16)


def _split_np(a):
    a32 = np.asarray(a, np.float32)
    hi = a32.astype(BF16)
    lo = (a32 - hi.astype(np.float32)).astype(BF16)
    return hi, lo


def _dft_tables(n):
    k = np.arange(n, dtype=np.int64)
    ang = 2.0 * np.pi * ((k[:, None] * k[None, :]) % n).astype(np.float64) / n
    t = np.concatenate([np.cos(ang), -np.sin(ang)], axis=1) / np.sqrt(n)
    c = np.arange(FOUR_GROUP_DIM, dtype=np.int64)
    ang_c = 2.0 * np.pi * ((c[:, None] * c[None, :]) % FOUR_GROUP_DIM).astype(np.float64) / FOUR_GROUP_DIM
    eye = np.eye(FOUR_GROUPS)
    g = np.concatenate([np.kron(eye, np.cos(ang_c)), np.kron(eye, np.sin(ang_c))], axis=1)
    g = g / np.sqrt(FOUR_GROUP_DIM)
    return _split_np(g) + _split_np(t)


def _dft(f_in, n_batch, n, row0):
    gh, gl, th, tl = _dft_tables(n)
    rows = max(n, DFT_ROWS)
    blk0 = row0 // rows
    const = lambda b: (0, 0)
    return pl.pallas_call(
        functools.partial(_dft_kernel, n=n),
        grid=(n_batch * n // rows,),
        in_specs=[
            pl.BlockSpec((rows, D_FOUR), lambda b: (blk0 + b, 0)),
            pl.BlockSpec((D_FOUR, 2 * D_FOUR), const),
            pl.BlockSpec((D_FOUR, 2 * D_FOUR), const),
            pl.BlockSpec((n, 2 * n), const),
            pl.BlockSpec((n, 2 * n), const),
        ],
        out_specs=pl.BlockSpec((rows, D_FOUR), lambda b: (b, 0)),
        out_shape=jax.ShapeDtypeStruct((n_batch * n, D_FOUR), BF16),
        compiler_params=pltpu.CompilerParams(
            dimension_semantics=("arbitrary",), vmem_limit_bytes=VMEM_LIMIT),
        name="fnet_dft_%d" % n,
    )(f_in, gh, gl, th, tl)


ATTN_TQ = 512
ATTN_SEQS = 4
ATTN_AHEAD = 3
VT_ROWS = V_DIM + 16


def _attn_kernel(*refs, n_own, n_ctx, seqs):
    if n_ctx:
        qt_ref, ckv_ref, kr_ref, cckv_ref, ckr_ref, wkv_ref, pk_ref, pvt_ref = refs[:8]
    else:
        qt_ref, ckv_ref, kr_ref, wkv_ref, pk_ref, pvt_ref = refs[:6]
    o_ref, k_scr, vt_scr, wk_s, wvt_s = refs[-5:]

    @pl.when(jnp.logical_and(pl.program_id(0) == 0, pl.program_id(1) == 0))
    def _():
        wkv = wkv_ref[...].astype(BF16)
        wk_s[...] = _dot(wkv, pk_ref[...]).astype(BF16)
        wvt_s[...] = _dot_nt(pvt_ref[...], wkv).astype(BF16)

    @pl.when(pl.program_id(1) == 0)
    def _():
        ckv = ckv_ref[...].astype(BF16)
        kr = kr_ref[...]
        if n_ctx:
            ckv = jnp.concatenate([ckv, cckv_ref[...].astype(BF16)], axis=0)
            kr = jnp.concatenate([kr, ckr_ref[...].astype(BF16)], axis=0)
        k_nope = _dot(ckv, wk_s[...])
        kr = kr.astype(F32)
        for hd in range(N_HEADS):
            lo = hd * HEAD_PAD
            k_scr[:, lo:lo + HEAD_PAD] = (k_nope[:, lo:lo + HEAD_PAD] + kr).astype(BF16)
        vt = _dot_nt(wvt_s[...], ckv).astype(BF16)
        pad_rows = lax.broadcasted_iota(jnp.int32, (VT_ROWS - V_DIM, vt.shape[1]), 0)
        ones_row = (pad_rows == 0).astype(BF16)
        for hd in range(N_HEADS):
            vt_scr[hd * VT_ROWS:hd * VT_ROWS + V_DIM, :] = vt[hd * V_DIM:(hd + 1) * V_DIM]
            vt_scr[hd * VT_ROWS + V_DIM:(hd + 1) * VT_ROWS, :] = ones_row

    n_keys = n_own + n_ctx
    tq = qt_ref.shape[1] // seqs
    units = [(j, hd) for j in range(seqs) for hd in range(N_HEADS)]

    def scores(j, hd):
        lo = hd * HEAD_PAD
        return _dot(k_scr[j * n_keys:(j + 1) * n_keys, lo:lo + HEAD_PAD],
                    qt_ref[lo:lo + HEAD_PAD, j * tq:(j + 1) * tq])

    outs = []

    def values(j, hd, p):
        ov = _dot(vt_scr[hd * VT_ROWS:(hd + 1) * VT_ROWS, j * n_keys:(j + 1) * n_keys], p)
        outs.append(ov[:V_DIM] / ov[V_DIM:V_DIM + 1])
        if hd == N_HEADS - 1:
            o_ref[j * tq:(j + 1) * tq, :] = jnp.concatenate(outs, axis=0).T.astype(BF16)
            outs.clear()

    queue = [scores(*u) for u in units[:ATTN_AHEAD]]
    pending = None
    for i, unit in enumerate(units):
        s = queue.pop(0)
        if i + ATTN_AHEAD < len(units):
            queue.append(scores(*units[i + ATTN_AHEAD]))
        p = jnp.exp2(s - jnp.max(s, axis=0, keepdims=True)).astype(BF16)
        if pending is not None:
            values(*pending)
        pending = (*unit, p)
    values(*pending)


def _kv_select():
    n_kv = N_HEADS * (QK_NOPE + V_DIM)
    pk = np.zeros((n_kv, N_HEADS * HEAD_PAD), np.float32)
    pvt = np.zeros((N_HEADS * V_DIM, n_kv), np.float32)
    for hd in range(N_HEADS):
        src = hd * (QK_NOPE + V_DIM)
        for j in range(QK_NOPE):
            pk[src + j, hd * HEAD_PAD + QK_ROPE + j] = 1.0
        for j in range(V_DIM):
            pvt[hd * V_DIM + j, src + QK_NOPE + j] = 1.0
    return pk.astype(BF16), pvt.astype(BF16)


def _attention(q, ckv_n, kr, w_ukv, layer, n_batch, n_own, row0, ctx=None):
    tq = min(n_own, ATTN_TQ)
    n_kv = N_HEADS * (QK_NOPE + V_DIM)
    n_ctx = 0 if ctx is None else PAST_LEN
    n_keys = n_own + n_ctx
    qt = n_own // tq
    seqs = ATTN_SEQS if (ctx is None and qt == 1) else 1
    blk0 = row0 // (seqs * n_own)
    const = lambda b, t: (0, 0)
    in_specs = [
        pl.BlockSpec((N_HEADS * HEAD_PAD, seqs * tq), lambda b, t: (0, (row0 // (seqs * tq)) + b * qt + t)),
        pl.BlockSpec((seqs * n_own, KV_RANK), lambda b, t: (blk0 + b, 0)),
        pl.BlockSpec((seqs * n_own, HEAD_PAD), lambda b, t: (blk0 + b, 0)),
    ]
    args = [q, ckv_n, kr]
    if ctx is not None:
        cckv, ckr = ctx
        in_specs += [
            pl.BlockSpec((None, None, PAST_LEN, KV_RANK), lambda b, t: (b, layer, 0, 0)),
            pl.BlockSpec((None, None, PAST_LEN, HEAD_PAD), lambda b, t: (b, layer, 0, 0)),
        ]
        args += [cckv, ckr]
    in_specs += [
        pl.BlockSpec((None, KV_RANK, n_kv), lambda b, t: (layer, 0, 0)),
        pl.BlockSpec((n_kv, N_HEADS * HEAD_PAD), const),
        pl.BlockSpec((N_HEADS * V_DIM, n_kv), const),
    ]
    args += [w_ukv, *_kv_select()]
    return pl.pallas_call(
        functools.partial(_attn_kernel, n_own=n_own, n_ctx=n_ctx, seqs=seqs),
        grid=(n_batch // seqs, qt),
        in_specs=in_specs,
        out_specs=pl.BlockSpec((seqs * tq, N_HEADS * V_DIM), lambda b, t: (b * qt + t, 0)),
        out_shape=jax.ShapeDtypeStruct((n_batch * n_own, N_HEADS * V_DIM), BF16),
        scratch_shapes=[
            pltpu.VMEM((seqs * n_keys, N_HEADS * HEAD_PAD), BF16),
            pltpu.VMEM((N_HEADS * VT_ROWS, seqs * n_keys), BF16),
            pltpu.VMEM((KV_RANK, N_HEADS * HEAD_PAD), BF16),
            pltpu.VMEM((N_HEADS * V_DIM, KV_RANK), BF16),
        ],
        compiler_params=pltpu.CompilerParams(
            dimension_semantics=("arbitrary", "arbitrary"), vmem_limit_bytes=VMEM_LIMIT),
        name="mla_attention_%d" % n_keys,
    )(*args)


TAIL_TM = 1024
TAIL_GROUP = 256
TAIL_SLABS = 4
GATE_ROW = ROPE_COL + QK_ROPE
GATE_CHUNK = 512
GATE_CHUNKS = 2 * D_MODEL // GATE_CHUNK


def _tail_kernel(x_ref, sh_ref, sc_ref, g_ref, nw_ref, frp_ref, frs_ref, op_ref, os_ref, wi_hbm,
                 wf_ref, wa_ref, wo_ref, out_ref, wg_s, wf_s, wa_s, wo_s, stage, sem, *, layer):
    s = pl.program_id(0)
    is_prompt = jnp.maximum(s - TAIL_SLABS, 0) < N_PROMPT // TAIL_TM

    def gate_copy(c):
        src = wi_hbm.at[layer, pl.ds(GATE_ROW + c * GATE_CHUNK, GATE_CHUNK), :]
        return pltpu.make_async_copy(src, stage.at[c % 2], sem.at[c % 2])

    @pl.when(s == 0)
    def _():
        gate_copy(0).start()
        for c in range(GATE_CHUNKS):
            if c + 1 < GATE_CHUNKS:
                gate_copy(c + 1).start()
            gate_copy(c).wait()
            wg_s[c * GATE_CHUNK:(c + 1) * GATE_CHUNK, :] = stage[c % 2].astype(BF16)

    @pl.when(s < TAIL_SLABS)
    def _():
        def put(dst, src):
            rows = src.shape[0]
            dst[pl.ds(pl.multiple_of(s * rows, rows), rows), :] = src.astype(BF16)

        put(wf_s, wf_ref[...])
        put(wa_s, wa_ref[...])
        put(wo_s, wo_ref[...])

    @pl.when(s >= TAIL_SLABS)
    def _():
        groups = [slice(r * TAIL_GROUP, (r + 1) * TAIL_GROUP) for r in range(TAIL_TM // TAIL_GROUP)]

        def branches(rows):
            x = x_ref[rows, :]
            h = _norm_mod(x, nw_ref[...], sc_ref[...], sh_ref[...]).astype(BF16)
            gates = _dot_nt(h, wg_s[...])
            a_out = _dot(jnp.where(is_prompt, frp_ref[rows, :], frs_ref[rows, :]), wf_s[...])
            o_out = _dot(jnp.where(is_prompt, op_ref[rows, :], os_ref[rows, :]), wa_s[...])
            return x, gates, a_out, o_out

        def merge(rows, x, gates, a_out, o_out):
            merged = jax.nn.sigmoid(gates[:, :D_MODEL]) * a_out + jax.nn.sigmoid(gates[:, D_MODEL:]) * o_out
            m = _dot(merged.astype(BF16), wo_s[...])
            out_ref[rows, :] = x + g_ref[...] * m

        pending = branches(groups[0])
        for r in range(1, len(groups)):
            ready, pending = pending, branches(groups[r])
            merge(groups[r - 1], *ready)
        merge(groups[-1], *pending)


def _tail(x, mod, layer, nw, frs, os_, w_in_t, w_four, w_attn_proj, w_out):
    tm = TAIL_TM
    npt = N_PROMPT // tm
    tile = lambda s: jnp.maximum(s - TAIL_SLABS, 0)
    slab = lambda s: jnp.minimum(s, TAIL_SLABS - 1)
    tok = lambda s: (tile(s), 0)
    tok_p = lambda s: (jnp.minimum(tile(s), npt - 1), 0)
    tok_s = lambda s: (jnp.maximum(tile(s) - npt, 0), 0)
    n_attn = N_HEADS * V_DIM

    def slab_spec(rows, cols):
        return pl.BlockSpec((None, rows // TAIL_SLABS, cols), lambda s: (layer, slab(s), 0))

    return pl.pallas_call(
        functools.partial(_tail_kernel, layer=layer),
        grid=(TAIL_SLABS + N_TOK // tm,),
        in_specs=[
            pl.BlockSpec((tm, D_MODEL), tok),
            _mod_spec(layer, 3, tm, tile),
            _mod_spec(layer, 4, tm, tile),
            _mod_spec(layer, 5, tm, tile),
            pl.BlockSpec((None, 1, D_MODEL), lambda s: (layer, 0, 0)),
            pl.BlockSpec((tm, D_FOUR), tok_p),
            pl.BlockSpec((tm, D_FOUR), tok_s),
            pl.BlockSpec((tm, n_attn), tok_p),
            pl.BlockSpec((tm, n_attn), tok_s),
            pl.BlockSpec(memory_space=pl.ANY),
            slab_spec(D_FOUR, D_MODEL),
            slab_spec(n_attn, D_MODEL),
            slab_spec(D_MODEL, D_MODEL),
        ],
        out_specs=pl.BlockSpec((tm, D_MODEL), tok),
        out_shape=jax.ShapeDtypeStruct((N_TOK, D_MODEL), F32),
        scratch_shapes=[
            pltpu.VMEM((2 * D_MODEL, D_MODEL), BF16),
            pltpu.VMEM((D_FOUR, D_MODEL), BF16),
            pltpu.VMEM((n_attn, D_MODEL), BF16),
            pltpu.VMEM((D_MODEL, D_MODEL), BF16),
            pltpu.VMEM((2, GATE_CHUNK, D_MODEL), F32),
            pltpu.SemaphoreType.DMA((2,)),
        ],
        compiler_params=pltpu.CompilerParams(
            dimension_semantics=("arbitrary",), vmem_limit_bytes=VMEM_LIMIT),
        name="mixer_tail",
    )(x, mod, mod, mod, nw.reshape(DEPTH, 1, D_MODEL), *frs, *os_, w_in_t, w_four, w_attn_proj, w_out)


def _rope_tables(tm):
    rows = DEC_SEQ // GRID_W
    row = np.repeat(np.arange(rows), GRID_W).astype(np.float64)
    col = np.tile(np.arange(GRID_W), rows).astype(np.float64)
    axis_dim = QK_ROPE // 2
    inv = ROPE_BASE ** (-np.arange(0, axis_dim, 2, dtype=np.float64) / axis_dim)
    ar = row[:, None] * inv
    ac = col[:, None] * inv
    cr, sr, cc, sc = np.cos(ar), np.sin(ar), np.cos(ac), np.sin(ac)
    cos32 = np.concatenate([cr, cr, cc, cc], axis=1)
    sin32 = np.concatenate([-sr, sr, -sc, sc], axis=1)
    scale = np.float32((QK_NOPE + QK_ROPE) ** -0.5 * np.log2(np.e))

    def table(rope_part, nope_val, ident_rope):
        t = np.zeros((DEC_SEQ + tm, HEAD_PAD), np.float32)
        t[:DEC_SEQ, :QK_ROPE] = rope_part
        t[:DEC_SEQ, QK_ROPE:QK_ROPE + QK_NOPE] = nope_val
        t[DEC_SEQ:, :QK_ROPE] = ident_rope
        t[DEC_SEQ:, QK_ROPE:QK_ROPE + QK_NOPE] = nope_val
        return t

    cq = (table(cos32, 1.0, 1.0) * scale).T
    sq = (table(sin32, 0.0, 0.0) * scale).T
    ck = table(cos32, 0.0, 1.0)
    sk = table(sin32, 0.0, 0.0)
    return tuple(jnp.asarray(np.ascontiguousarray(t), F32) for t in (cq, sq, ck, sk))


def kernel(x_prompt, x_sample, cache_ckv, cache_krope, c, c_ctx, w_mod, b_mod, norm_ffn1, w_ffn1_gate,
           w_ffn1_up, w_ffn1_down, norm_mix, w_in, w_four, q_norm, w_uq, kv_norm, w_ukv, w_attn_proj,
           w_out, norm_ffn2, w_ffn2_gate, w_ffn2_up, w_ffn2_down, final_norm):
    xs = (x_prompt.reshape(N_PROMPT, D_MODEL), x_sample.reshape(N_SAMPLE, D_MODEL))
    c_all = jnp.concatenate(
        [c_ctx[None, :], c, jnp.zeros((MOD_ROWS - 1 - DEC_BATCH, D_MODEL), F32)], axis=0)
    mod = _modulation(c_all, w_mod, b_mod).reshape(DEPTH * MOD_ROWS * N_MOD, 1, D_MODEL)
    cache_kr = jnp.pad(cache_krope, ((0, 0), (0, 0), (0, 0), (0, HEAD_PAD - QK_ROPE)))
    w_in_t = jnp.swapaxes(w_in, 1, 2)

    caches = ()
    for l in range(DEPTH):
        x = _ffn(xs, mod, l, 0, norm_ffn1, w_ffn1_gate, w_ffn1_up, w_ffn1_down, final_norm, False)
        f_in, q, ckv_n, kr, new_ckv, new_krope = _proj(
            x, mod, l, norm_mix, w_in_t, q_norm, w_uq, kv_norm, _rope_tables, caches)
        caches = (new_ckv, new_krope)
        frs = (_dft(f_in, BATCH, SEQ, 0), _dft(f_in, DEC_BATCH, DEC_SEQ, N_PROMPT))
        os_ = (_attention(q, ckv_n, kr, w_ukv, l, BATCH, SEQ, 0),
               _attention(q, ckv_n, kr, w_ukv, l, DEC_BATCH, DEC_SEQ, N_PROMPT, (cache_ckv, cache_kr)))
        x = _tail(x, mod, l, norm_mix, frs, os_, w_in_t, w_four, w_attn_proj, w_out)
        xs = _ffn((x,), mod, l, 6, norm_ffn2, w_ffn2_gate, w_ffn2_up, w_ffn2_down, final_norm, l == DEPTH - 1)
        if l < DEPTH - 1:
            xs = (xs,)

    y_prompt = xs[0].reshape(BATCH, SEQ, D_MODEL)
    y_sample = xs[1].reshape(DEC_BATCH, DEC_SEQ, D_MODEL)
    return y_prompt, y_sample, caches[0], caches[1]
```

```python
import functools

import numpy as np
import jax
import jax.numpy as jnp
from jax import lax
from jax.experimental import pallas as pl
from jax.experimental.pallas import tpu as pltpu

D_MODEL = 1024
BATCH = 16
SEQ = 256
DEPTH = 2
DEC_BATCH = 4
DEC_SEQ = 1024
PAST_LEN = 512
GRID_W = 64
D_FF = 2816
FOUR_GROUPS = 4
FOUR_GROUP_DIM = 64
D_FOUR = FOUR_GROUPS * FOUR_GROUP_DIM
N_HEADS = 8
QK_NOPE = 64
QK_ROPE = 32
V_DIM = 64
Q_RANK = 384
KV_RANK = 256
N_MOD = 9
ROPE_BASE = 10000.0
EPS = 1e-6

N_PROMPT = BATCH * SEQ
N_SAMPLE = DEC_BATCH * DEC_SEQ
N_TOK = N_PROMPT + N_SAMPLE
MOD_ROWS = 8
HEAD_PAD = 128
KEYS_SAMPLE = DEC_SEQ + PAST_LEN

VMEM_LIMIT = 52 * 1024 * 1024

F32 = jnp.float32
BF16 = jnp.bfloat16


def _dot(a, b):
    return jnp.dot(a, b, preferred_element_type=F32)


def _dot_nt(a, b):
    return lax.dot_general(a, b, (((1,), (1,)), ((), ())), preferred_element_type=F32)


def _mod_row(i, tm):
    n_prompt_tiles = N_PROMPT // tm
    per_batch = DEC_SEQ // tm
    return jnp.where(i < n_prompt_tiles, 0, 1 + (i - n_prompt_tiles) // per_batch)


def _mod_spec(layer, which, tm, tile_of_step=lambda i: i):
    def idx(*g):
        return ((layer * MOD_ROWS + _mod_row(tile_of_step(g[0]), tm)) * N_MOD + which, 0, 0)
    return pl.BlockSpec((None, 1, D_MODEL), idx)


def _norm_mod(x, nw, sc, sh):
    ms = jnp.mean(x * x, axis=-1, keepdims=True)
    y = x * lax.rsqrt(ms + EPS) * nw
    return y * (1.0 + sc) + sh


def _mod_kernel(c_ref, w_ref, b_ref, o_ref):
    c = c_ref[...]
    a = (c * jax.nn.sigmoid(c)).astype(BF16)
    o_ref[...] = _dot(a, w_ref[...].astype(BF16)) + b_ref[pl.ds(pl.program_id(0), 1), :]


def _modulation(c_all, w_mod, b_mod):
    tn = 1024
    n_out = N_MOD * D_MODEL
    return pl.pallas_call(
        _mod_kernel,
        grid=(DEPTH, n_out // tn),
        in_specs=[
            pl.BlockSpec((MOD_ROWS, D_MODEL), lambda l, j: (0, 0)),
            pl.BlockSpec((None, D_MODEL, tn), lambda l, j: (l, 0, j)),
            pl.BlockSpec((DEPTH, tn), lambda l, j: (0, j)),
        ],
        out_specs=pl.BlockSpec((None, MOD_ROWS, tn), lambda l, j: (l, 0, j)),
        out_shape=jax.ShapeDtypeStruct((DEPTH, MOD_ROWS, n_out), F32),
        compiler_params=pltpu.CompilerParams(
            dimension_semantics=("arbitrary", "arbitrary"), vmem_limit_bytes=VMEM_LIMIT),
        name="modulation",
    )(c_all, w_mod, b_mod)


FFN_TM = 512
FFN_CHUNK = 256
FFN_NC = D_FF // FFN_CHUNK


def _ffn_kernel(*refs, layer, final, split_in):
    refs = list(refs)
    x_refs = [refs.pop(0) for _ in range(2 if split_in else 1)]
    sh_ref, sc_ref, g_ref, nw_ref, wg_ref, wu_ref, wd_ref, fn_ref = refs[:8]
    o_refs = refs[8:10] if final else refs[8:9]
    wg_s, wu_s, wd_s, h_scr, acc_scr = refs[-5:]
    s = pl.program_id(0)
    is_prompt = jnp.maximum(s - (FFN_NC - 1), 0) < N_PROMPT // FFN_TM

    def x_tile():
        if split_in:
            return jnp.where(is_prompt, x_refs[0][...], x_refs[1][...])
        return x_refs[0][...]

    def chunk_act(h, wg, wu):
        gate = _dot(h, wg)
        up = _dot(h, wu)
        return ((gate * jax.nn.sigmoid(gate)) * up).astype(BF16)

    def finish(acc):
        xn = x_tile() + 0.5 * g_ref[...] * acc
        if not final:
            o_refs[0][...] = xn
            return
        ms = jnp.mean(xn * xn, axis=-1, keepdims=True)
        y = xn * lax.rsqrt(ms + EPS) * fn_ref[...]

        @pl.when(is_prompt)
        def _():
            o_refs[0][...] = y

        @pl.when(jnp.logical_not(is_prompt))
        def _():
            o_refs[1][...] = y

    def hidden():
        return _norm_mod(x_tile(), nw_ref[layer:layer + 1, :], sc_ref[...], sh_ref[...]).astype(BF16)

    @pl.when(s == 0)
    def _():
        h_scr[...] = hidden()
        acc_scr[...] = jnp.zeros_like(acc_scr)

    @pl.when(s < FFN_NC)
    def _():
        wg = wg_ref[...].astype(BF16)
        wu = wu_ref[...].astype(BF16)
        wd = wd_ref[...].astype(BF16)
        wg_s[s] = wg
        wu_s[s] = wu
        wd_s[pl.ds(pl.multiple_of(s * FFN_CHUNK, FFN_CHUNK), FFN_CHUNK), :] = wd
        acc_scr[...] += _dot(chunk_act(h_scr[...], wg, wu), wd)

    @pl.when(s == FFN_NC - 1)
    def _():
        finish(acc_scr[...])

    @pl.when(s >= FFN_NC)
    def _():
        h = hidden()
        act = jnp.concatenate([chunk_act(h, wg_s[j], wu_s[j]) for j in range(FFN_NC)], axis=1)
        finish(_dot(act, wd_s[...]))


def _ffn(xs, mod, layer, mod_base, nw, wg, wu, wd, final_norm, final):
    tm = FFN_TM
    npt = N_PROMPT // tm
    split_in = len(xs) == 2
    tile = lambda s: jnp.maximum(s - (FFN_NC - 1), 0)
    chunk = lambda s: jnp.minimum(s, FFN_NC - 1)
    tok = pl.BlockSpec((tm, D_MODEL), lambda s: (tile(s), 0))
    tok_p = pl.BlockSpec((tm, D_MODEL), lambda s: (jnp.minimum(tile(s), npt - 1), 0))
    tok_s = pl.BlockSpec((tm, D_MODEL), lambda s: (jnp.maximum(tile(s) - npt, 0), 0))
    half = jax.ShapeDtypeStruct((N_PROMPT, D_MODEL), F32)
    return pl.pallas_call(
        functools.partial(_ffn_kernel, layer=layer, final=final, split_in=split_in),
        grid=(FFN_NC + N_TOK // tm - 1,),
        in_specs=([tok_p, tok_s] if split_in else [tok]) + [
            _mod_spec(layer, mod_base + 0, tm, tile),
            _mod_spec(layer, mod_base + 1, tm, tile),
            _mod_spec(layer, mod_base + 2, tm, tile),
            pl.BlockSpec((DEPTH, D_MODEL), lambda s: (0, 0)),
            pl.BlockSpec((None, D_MODEL, FFN_CHUNK), lambda s: (layer, 0, chunk(s))),
            pl.BlockSpec((None, D_MODEL, FFN_CHUNK), lambda s: (layer, 0, chunk(s))),
            pl.BlockSpec((None, FFN_CHUNK, D_MODEL), lambda s: (layer, chunk(s), 0)),
            pl.BlockSpec((1, D_MODEL), lambda s: (0, 0)),
        ],
        out_specs=[tok_p, tok_s] if final else tok,
        out_shape=[half, half] if final else jax.ShapeDtypeStruct((N_TOK, D_MODEL), F32),
        scratch_shapes=[
            pltpu.VMEM((FFN_NC, D_MODEL, FFN_CHUNK), BF16),
            pltpu.VMEM((FFN_NC, D_MODEL, FFN_CHUNK), BF16),
            pltpu.VMEM((D_FF, D_MODEL), BF16),
            pltpu.VMEM((tm, D_MODEL), BF16),
            pltpu.VMEM((tm, D_MODEL), F32),
        ],
        compiler_params=pltpu.CompilerParams(
            dimension_semantics=("arbitrary",), vmem_limit_bytes=VMEM_LIMIT),
        name="ffn_final" if final else ("ffn_first" if split_in else "ffn"),
    )(*xs, mod, mod, mod, nw, wg, wu, wd, final_norm.reshape(1, D_MODEL))


PROJ_TM = 1024
PROJ_SEQS = PROJ_TM // SEQ
PROJ_W = 1024
ROPE_COL = D_FOUR + Q_RANK + KV_RANK


def _pair_swap(x):
    lane = lax.broadcasted_iota(jnp.int32, x.shape, 1)
    return jnp.where(lane % 16 < 8, pltpu.roll(x, HEAD_PAD - 8, 1), pltpu.roll(x, 8, 1))


def _proj_kernel(*refs, layer, first_layer):
    (x_ref, sh_ref, sc_ref, nw_ref, w1_ref, qn_ref, wq_ref, pq_ref, kvn_ref,
     cq_ref, sq_ref, ck_ref, sk_ref) = refs[:13]
    f_ref, q_ref, ckv_ref, kr_ref, nckv_ref, nkr_ref, w1_s, wq_s = refs[-8:]
    is_prompt = pl.program_id(0) < N_PROMPT // PROJ_TM

    @pl.when(pl.program_id(0) == 0)
    def _():
        w1_s[...] = w1_ref[...].astype(BF16)
        wq_s[...] = _dot_nt(pq_ref[...], wq_ref[...].astype(BF16)).astype(BF16)

    n_qp = N_HEADS * HEAD_PAD
    c0 = D_FOUR + Q_RANK
    groups = [slice(r * SEQ, (r + 1) * SEQ) for r in range(PROJ_SEQS)]
    ps = []
    for rows in groups:
        h = _norm_mod(x_ref[rows, :], nw_ref[layer:layer + 1, :], sc_ref[...], sh_ref[...]).astype(BF16)
        ps.append(_dot_nt(h, w1_s[...]))
    staged = []
    for rows, p in zip(groups, ps):
        f_ref[rows, :] = p[:, :D_FOUR]
        ql = p[:, D_FOUR:c0]
        qn = ql * lax.rsqrt(jnp.mean(ql * ql, axis=-1, keepdims=True) + EPS) * qn_ref[layer:layer + 1, :]
        qn = qn.astype(BF16)
        qq = _dot_nt(wq_s[:n_qp, :], qn)
        ckv = p[:, c0:c0 + KV_RANK]
        ckv_n = ckv * lax.rsqrt(jnp.mean(ckv * ckv, axis=-1, keepdims=True) + EPS) * kvn_ref[layer:layer + 1, :]
        ckv_ref[rows, :] = ckv_n
        kr = p[:, ROPE_COL:ROPE_COL + HEAD_PAD]
        staged.append((qn, qq, ckv_n, kr))

    @pl.when(is_prompt)
    def _():
        for r, (rows, (qn, qq, ckv_n, kr)) in enumerate(zip(groups, staged)):
            cq = cq_ref[:, rows]
            for hd in range(N_HEADS):
                lo = hd * HEAD_PAD
                q_ref[lo:lo + HEAD_PAD, rows] = (qq[lo:lo + HEAD_PAD, :] * cq).astype(BF16)
            kr_ref[rows, :] = (kr * ck_ref[rows, :]).astype(BF16)
            if first_layer:
                nckv_ref[r, 0] = ckv_n
                nkr_ref[r, 0] = kr[:, :QK_ROPE]
                for later in range(1, DEPTH):
                    nckv_ref[r, later] = jnp.zeros_like(ckv_n)
                    nkr_ref[r, later] = jnp.zeros((SEQ, QK_ROPE), F32)
            else:
                nckv_ref[r] = ckv_n
                nkr_ref[r] = kr[:, :QK_ROPE]

    @pl.when(jnp.logical_not(is_prompt))
    def _():
        for rows, (qn, qq, ckv_n, kr) in zip(groups, staged):
            cq, sq = cq_ref[:, rows], sq_ref[:, rows]
            qs = _dot_nt(wq_s[n_qp:, :], qn)
            for hd in range(N_HEADS):
                lo, mid = hd * HEAD_PAD, hd * HEAD_PAD + QK_ROPE
                rope = qq[lo:mid, :] * cq[:QK_ROPE] + qs[hd * QK_ROPE:(hd + 1) * QK_ROPE, :] * sq[:QK_ROPE]
                q_ref[lo:mid, rows] = rope.astype(BF16)
                q_ref[mid:lo + HEAD_PAD, rows] = (qq[mid:lo + HEAD_PAD, :] * cq[QK_ROPE:]).astype(BF16)
            kr_ref[rows, :] = (kr * ck_ref[rows, :] + _pair_swap(kr) * sk_ref[rows, :]).astype(BF16)


def _q_select():
    n_qp = N_HEADS * HEAD_PAD
    p = np.zeros((n_qp + N_HEADS * QK_ROPE, N_HEADS * (QK_NOPE + QK_ROPE)), np.float32)
    for hd in range(N_HEADS):
        src, dst = hd * (QK_NOPE + QK_ROPE), hd * HEAD_PAD
        for j in range(QK_ROPE):
            partner = j + 8 if j % 16 < 8 else j - 8
            p[dst + j, src + QK_NOPE + j] = 1.0
            p[n_qp + hd * QK_ROPE + j, src + QK_NOPE + partner] = 1.0
        for j in range(QK_NOPE):
            p[dst + QK_ROPE + j, src + j] = 1.0
    return p.astype(BF16)


def _proj(x, mod, layer, nw, w_in_t, qn, w_uq, kvn, tabs, prev_caches):
    tm = PROJ_TM
    n_prompt_tiles = N_PROMPT // tm
    per_batch = DEC_SEQ // tm
    n_q = N_HEADS * (QK_NOPE + QK_ROPE)

    def tab_blk(i):
        return jnp.where(i < n_prompt_tiles, per_batch, (i - n_prompt_tiles) % per_batch)

    tok = lambda i: (i, 0)
    const = lambda i: (0, 0)
    first_layer = not prev_caches
    assert first_layer == (layer == 0)
    cache_layers = DEPTH if first_layer else None
    cache_idx = lambda i: (jnp.minimum(i, n_prompt_tiles - 1), 0 if first_layer else layer, 0, 0)
    tab_spec = pl.BlockSpec((tm, HEAD_PAD), lambda i: (tab_blk(i), 0))
    tab_t_spec = pl.BlockSpec((HEAD_PAD, tm), lambda i: (0, tab_blk(i)))
    return pl.pallas_call(
        functools.partial(_proj_kernel, layer=layer, first_layer=first_layer),
        grid=(N_TOK // tm,),
        in_specs=[
            pl.BlockSpec((tm, D_MODEL), tok),
            _mod_spec(layer, 3, tm),
            _mod_spec(layer, 4, tm),
            pl.BlockSpec((DEPTH, D_MODEL), const),
            pl.BlockSpec((None, PROJ_W, D_MODEL), lambda i: (layer, 0, 0)),
            pl.BlockSpec((DEPTH, Q_RANK), const),
            pl.BlockSpec((None, Q_RANK, n_q), lambda i: (layer, 0, 0)),
            pl.BlockSpec((N_HEADS * (HEAD_PAD + QK_ROPE), n_q), const),
            pl.BlockSpec((DEPTH, KV_RANK), const),
            tab_t_spec, tab_t_spec, tab_spec, tab_spec,
        ] + [pl.BlockSpec(memory_space=pl.ANY)] * len(prev_caches),
        out_specs=[
            pl.BlockSpec((tm, D_FOUR), tok),
            pl.BlockSpec((N_HEADS * HEAD_PAD, tm), lambda i: (0, i)),
            pl.BlockSpec((tm, KV_RANK), tok),
            pl.BlockSpec((tm, HEAD_PAD), tok),
            pl.BlockSpec((PROJ_SEQS, cache_layers, SEQ, KV_RANK), cache_idx),
            pl.BlockSpec((PROJ_SEQS, cache_layers, SEQ, QK_ROPE), cache_idx),
        ],
        out_shape=[
            jax.ShapeDtypeStruct((N_TOK, D_FOUR), F32),
            jax.ShapeDtypeStruct((N_HEADS * HEAD_PAD, N_TOK), BF16),
            jax.ShapeDtypeStruct((N_TOK, KV_RANK), F32),
            jax.ShapeDtypeStruct((N_TOK, HEAD_PAD), BF16),
            jax.ShapeDtypeStruct((BATCH, DEPTH, SEQ, KV_RANK), F32),
            jax.ShapeDtypeStruct((BATCH, DEPTH, SEQ, QK_ROPE), F32),
        ],
        input_output_aliases={13 + k: 4 + k for k in range(len(prev_caches))},
        scratch_shapes=[
            pltpu.VMEM((PROJ_W, D_MODEL), BF16),
            pltpu.VMEM((N_HEADS * (HEAD_PAD + QK_ROPE), Q_RANK), BF16),
        ],
        compiler_params=pltpu.CompilerParams(
            dimension_semantics=("arbitrary",), vmem_limit_bytes=VMEM_LIMIT),
        name="mixer_proj",
    )(x, mod, mod, nw, w_in_t, qn, w_uq, _q_select(), kvn, *tabs(tm), *prev_caches)


DFT_ROWS = 1024


def _split(x):
    hi = x.astype(BF16)
    lo = (x - hi.astype(F32)).astype(BF16)
    return hi, lo


def _dot3(ah, al, bh, bl):
    return _dot(ah, bh) + _dot(al, bh) + _dot(ah, bl)


def _dft_kernel(x_ref, gh_ref, gl_ref, ch_ref, cl_ref, sh_ref, sl_ref, rev_ref, o_ref, *, n):
    half = n // 2
    xh, xl = _split(x_ref[...])
    y = _dot3(xh, xl, gh_ref[...], gl_ref[...])
    yh, yl = _split(y)
    ch, cl, sh, sl = ch_ref[...], cl_ref[...], sh_ref[...], sl_ref[...]
    pos = lax.broadcasted_iota(jnp.int32, (n, D_FOUR), 0)
    alt = jnp.where(pos % 2 == 0, 1.0, -1.0).astype(F32) * (float(n) ** -0.5)
    first = lax.broadcasted_iota(jnp.int32, (half, D_FOUR), 0) == 0
    for b in range(x_ref.shape[0] // n):
        r = slice(b * n, (b + 1) * n)
        a = _dot3(ch, cl, yh[r, :D_FOUR], yl[r, :D_FOUR])
        bb = _dot3(sh, sl, yh[r, D_FOUR:], yl[r, D_FOUR:])
        o_ref[b * n:b * n + half, :] = (a - bb).astype(BF16)
        mirrored = _dot(rev_ref[...], (a + bb).astype(BF16))
        nyquist = jnp.sum(y[r, :D_FOUR] * alt, axis=0, keepdims=True)
        o_ref[b * n + half:(b + 1) * n, :] = jnp.where(first, nyquist, mirrored).astype(BF16)


def _split_np(a):
    a32 = np.asarray(a, np.float32)
    hi = a32.astype(BF16)
    lo = (a32 - hi.astype(np.float32)).astype(BF16)
    return hi, lo


def _dft_tables(n):
    half = n // 2
    k = np.arange(half, dtype=np.int64)
    pos = np.arange(n, dtype=np.int64)
    ang = 2.0 * np.pi * ((k[:, None] * pos[None, :]) % n).astype(np.float64) / n
    c = np.arange(FOUR_GROUP_DIM, dtype=np.int64)
    ang_c = 2.0 * np.pi * ((c[:, None] * c[None, :]) % FOUR_GROUP_DIM).astype(np.float64) / FOUR_GROUP_DIM
    eye = np.eye(FOUR_GROUPS)
    g = np.concatenate([np.kron(eye, np.cos(ang_c)), np.kron(eye, np.sin(ang_c))], axis=1)
    g = g / np.sqrt(FOUR_GROUP_DIM)
    rev = np.zeros((half, half), np.float32)
    rev[np.arange(1, half), half - np.arange(1, half)] = 1.0
    return (_split_np(g) + _split_np(np.cos(ang) / np.sqrt(n)) + _split_np(np.sin(ang) / np.sqrt(n))
            + (rev.astype(BF16),))


def _dft(f_in, n_batch, n, row0):
    tables = _dft_tables(n)
    rows = max(n, DFT_ROWS)
    blk0 = row0 // rows
    const = lambda b: (0, 0)
    half_spec = pl.BlockSpec((n // 2, n), const)
    return pl.pallas_call(
        functools.partial(_dft_kernel, n=n),
        grid=(n_batch * n // rows,),
        in_specs=[
            pl.BlockSpec((rows, D_FOUR), lambda b: (blk0 + b, 0)),
            pl.BlockSpec((D_FOUR, 2 * D_FOUR), const),
            pl.BlockSpec((D_FOUR, 2 * D_FOUR), const),
            half_spec, half_spec, half_spec, half_spec,
            pl.BlockSpec((n // 2, n // 2), const),
        ],
        out_specs=pl.BlockSpec((rows, D_FOUR), lambda b: (b, 0)),
        out_shape=jax.ShapeDtypeStruct((n_batch * n, D_FOUR), BF16),
        compiler_params=pltpu.CompilerParams(
            dimension_semantics=("arbitrary",), vmem_limit_bytes=VMEM_LIMIT),
        name="fnet_dft_%d" % n,
    )(f_in, *tables)


ATTN_TQ = 512
ATTN_SEQS = 4
ATTN_AHEAD = 3
VT_ROWS = V_DIM + 16


def _attn_kernel(*refs, n_own, n_ctx, seqs):
    if n_ctx:
        qt_ref, ckv_ref, kr_ref, cckv_ref, ckr_ref, wkv_ref, pk_ref, pvt_ref = refs[:8]
    else:
        qt_ref, ckv_ref, kr_ref, wkv_ref, pk_ref, pvt_ref = refs[:6]
    o_ref, k_scr, vt_scr, wk_s, wvt_s = refs[-5:]

    @pl.when(jnp.logical_and(pl.program_id(0) == 0, pl.program_id(1) == 0))
    def _():
        wkv = wkv_ref[...].astype(BF16)
        wk_s[...] = _dot(wkv, pk_ref[...]).astype(BF16)
        wvt_s[...] = _dot_nt(pvt_ref[...], wkv).astype(BF16)

    @pl.when(pl.program_id(1) == 0)
    def _():
        ckv = ckv_ref[...].astype(BF16)
        kr = kr_ref[...]
        if n_ctx:
            ckv = jnp.concatenate([ckv, cckv_ref[...].astype(BF16)], axis=0)
            kr = jnp.concatenate([kr, ckr_ref[...].astype(BF16)], axis=0)
        k_nope = _dot(ckv, wk_s[...])
        kr = kr.astype(F32)
        for hd in range(N_HEADS):
            lo = hd * HEAD_PAD
            k_scr[:, lo:lo + HEAD_PAD] = (k_nope[:, lo:lo + HEAD_PAD] + kr).astype(BF16)
        vt = _dot_nt(wvt_s[...], ckv).astype(BF16)
        pad_rows = lax.broadcasted_iota(jnp.int32, (VT_ROWS - V_DIM, vt.shape[1]), 0)
        ones_row = (pad_rows == 0).astype(BF16)
        for hd in range(N_HEADS):
            vt_scr[hd * VT_ROWS:hd * VT_ROWS + V_DIM, :] = vt[hd * V_DIM:(hd + 1) * V_DIM]
            vt_scr[hd * VT_ROWS + V_DIM:(hd + 1) * VT_ROWS, :] = ones_row

    n_keys = n_own + n_ctx
    tq = qt_ref.shape[1] // seqs
    units = [(j, hd) for j in range(seqs) for hd in range(N_HEADS)]

    def scores(j, hd):
        lo = hd * HEAD_PAD
        return _dot(k_scr[j * n_keys:(j + 1) * n_keys, lo:lo + HEAD_PAD],
                    qt_ref[lo:lo + HEAD_PAD, j * tq:(j + 1) * tq])

    outs = []

    def values(j, hd, p):
        ov = _dot(vt_scr[hd * VT_ROWS:(hd + 1) * VT_ROWS, j * n_keys:(j + 1) * n_keys], p)
        outs.append(ov[:V_DIM] / ov[V_DIM:V_DIM + 1])
        if hd == N_HEADS - 1:
            o_ref[j * tq:(j + 1) * tq, :] = jnp.concatenate(outs, axis=0).T.astype(BF16)
            outs.clear()

    queue = [scores(*u) for u in units[:ATTN_AHEAD]]
    pending = None
    for i, unit in enumerate(units):
        s = queue.pop(0)
        if i + ATTN_AHEAD < len(units):
            queue.append(scores(*units[i + ATTN_AHEAD]))
        p = jnp.exp2(s - jnp.max(s, axis=0, keepdims=True)).astype(BF16)
        if pending is not None:
            values(*pending)
        pending = (*unit, p)
    values(*pending)


def _kv_select():
    n_kv = N_HEADS * (QK_NOPE + V_DIM)
    pk = np.zeros((n_kv, N_HEADS * HEAD_PAD), np.float32)
    pvt = np.zeros((N_HEADS * V_DIM, n_kv), np.float32)
    for hd in range(N_HEADS):
        src = hd * (QK_NOPE + V_DIM)
        for j in range(QK_NOPE):
            pk[src + j, hd * HEAD_PAD + QK_ROPE + j] = 1.0
        for j in range(V_DIM):
            pvt[hd * V_DIM + j, src + QK_NOPE + j] = 1.0
    return pk.astype(BF16), pvt.astype(BF16)


def _attention(q, ckv_n, kr, w_ukv, layer, n_batch, n_own, row0, ctx=None):
    tq = min(n_own, ATTN_TQ)
    n_kv = N_HEADS * (QK_NOPE + V_DIM)
    n_ctx = 0 if ctx is None else PAST_LEN
    n_keys = n_own + n_ctx
    qt = n_own // tq
    seqs = ATTN_SEQS if (ctx is None and qt == 1) else 1
    blk0 = row0 // (seqs * n_own)
    const = lambda b, t: (0, 0)
    in_specs = [
        pl.BlockSpec((N_HEADS * HEAD_PAD, seqs * tq), lambda b, t: (0, (row0 // (seqs * tq)) + b * qt + t)),
        pl.BlockSpec((seqs * n_own, KV_RANK), lambda b, t: (blk0 + b, 0)),
        pl.BlockSpec((seqs * n_own, HEAD_PAD), lambda b, t: (blk0 + b, 0)),
    ]
    args = [q, ckv_n, kr]
    if ctx is not None:
        cckv, ckr = ctx
        in_specs += [
            pl.BlockSpec((None, None, PAST_LEN, KV_RANK), lambda b, t: (b, layer, 0, 0)),
            pl.BlockSpec((None, None, PAST_LEN, HEAD_PAD), lambda b, t: (b, layer, 0, 0)),
        ]
        args += [cckv, ckr]
    in_specs += [
        pl.BlockSpec((None, KV_RANK, n_kv), lambda b, t: (layer, 0, 0)),
        pl.BlockSpec((n_kv, N_HEADS * HEAD_PAD), const),
        pl.BlockSpec((N_HEADS * V_DIM, n_kv), const),
    ]
    args += [w_ukv, *_kv_select()]
    return pl.pallas_call(
        functools.partial(_attn_kernel, n_own=n_own, n_ctx=n_ctx, seqs=seqs),
        grid=(n_batch // seqs, qt),
        in_specs=in_specs,
        out_specs=pl.BlockSpec((seqs * tq, N_HEADS * V_DIM), lambda b, t: (b * qt + t, 0)),
        out_shape=jax.ShapeDtypeStruct((n_batch * n_own, N_HEADS * V_DIM), BF16),
        scratch_shapes=[
            pltpu.VMEM((seqs * n_keys, N_HEADS * HEAD_PAD), BF16),
            pltpu.VMEM((N_HEADS * VT_ROWS, seqs * n_keys), BF16),
            pltpu.VMEM((KV_RANK, N_HEADS * HEAD_PAD), BF16),
            pltpu.VMEM((N_HEADS * V_DIM, KV_RANK), BF16),
        ],
        compiler_params=pltpu.CompilerParams(
            dimension_semantics=("arbitrary", "arbitrary"), vmem_limit_bytes=VMEM_LIMIT),
        name="mla_attention_%d" % n_keys,
    )(*args)


TAIL_TM = 1024
TAIL_GROUP = 256
TAIL_SLABS = 4
GATE_ROW = ROPE_COL + QK_ROPE
GATE_CHUNK = 512
GATE_CHUNKS = 2 * D_MODEL // GATE_CHUNK


def _tail_kernel(x_ref, sh_ref, sc_ref, g_ref, nw_ref, frp_ref, frs_ref, op_ref, os_ref, wi_hbm,
                 wf_ref, wa_ref, wo_ref, out_ref, wg_s, wf_s, wa_s, wo_s, stage, sem, *, layer):
    s = pl.program_id(0)
    is_prompt = jnp.maximum(s - TAIL_SLABS, 0) < N_PROMPT // TAIL_TM

    def gate_copy(c):
        src = wi_hbm.at[layer, pl.ds(GATE_ROW + c * GATE_CHUNK, GATE_CHUNK), :]
        return pltpu.make_async_copy(src, stage.at[c % 2], sem.at[c % 2])

    @pl.when(s == 0)
    def _():
        gate_copy(0).start()
        for c in range(GATE_CHUNKS):
            if c + 1 < GATE_CHUNKS:
                gate_copy(c + 1).start()
            gate_copy(c).wait()
            wg_s[c * GATE_CHUNK:(c + 1) * GATE_CHUNK, :] = stage[c % 2].astype(BF16)

    @pl.when(s < TAIL_SLABS)
    def _():
        def put(dst, src):
            rows = src.shape[0]
            dst[pl.ds(pl.multiple_of(s * rows, rows), rows), :] = src.astype(BF16)

        put(wf_s, wf_ref[...])
        put(wa_s, wa_ref[...])
        put(wo_s, wo_ref[...])

    @pl.when(s >= TAIL_SLABS)
    def _():
        groups = [slice(r * TAIL_GROUP, (r + 1) * TAIL_GROUP) for r in range(TAIL_TM // TAIL_GROUP)]

        def branches(rows):
            x = x_ref[rows, :]
            h = _norm_mod(x, nw_ref[layer:layer + 1, :], sc_ref[...], sh_ref[...]).astype(BF16)
            gates = _dot_nt(h, wg_s[...])
            a_out = _dot(jnp.where(is_prompt, frp_ref[rows, :], frs_ref[rows, :]), wf_s[...])
            o_out = _dot(jnp.where(is_prompt, op_ref[rows, :], os_ref[rows, :]), wa_s[...])
            return x, gates, a_out, o_out

        def merge(rows, x, gates, a_out, o_out):
            merged = jax.nn.sigmoid(gates[:, :D_MODEL]) * a_out + jax.nn.sigmoid(gates[:, D_MODEL:]) * o_out
            m = _dot(merged.astype(BF16), wo_s[...])
            out_ref[rows, :] = x + g_ref[...] * m

        pending = branches(groups[0])
        for r in range(1, len(groups)):
            ready, pending = pending, branches(groups[r])
            merge(groups[r - 1], *ready)
        merge(groups[-1], *pending)


def _tail(x, mod, layer, nw, frs, os_, w_in_t, w_four, w_attn_proj, w_out):
    tm = TAIL_TM
    npt = N_PROMPT // tm
    tile = lambda s: jnp.maximum(s - TAIL_SLABS, 0)
    slab = lambda s: jnp.minimum(s, TAIL_SLABS - 1)
    tok = lambda s: (tile(s), 0)
    tok_p = lambda s: (jnp.minimum(tile(s), npt - 1), 0)
    tok_s = lambda s: (jnp.maximum(tile(s) - npt, 0), 0)
    n_attn = N_HEADS * V_DIM

    def slab_spec(rows, cols):
        return pl.BlockSpec((None, rows // TAIL_SLABS, cols), lambda s: (layer, slab(s), 0))

    return pl.pallas_call(
        functools.partial(_tail_kernel, layer=layer),
        grid=(TAIL_SLABS + N_TOK // tm,),
        in_specs=[
            pl.BlockSpec((tm, D_MODEL), tok),
            _mod_spec(layer, 3, tm, tile),
            _mod_spec(layer, 4, tm, tile),
            _mod_spec(layer, 5, tm, tile),
            pl.BlockSpec((DEPTH, D_MODEL), lambda s: (0, 0)),
            pl.BlockSpec((tm, D_FOUR), tok_p),
            pl.BlockSpec((tm, D_FOUR), tok_s),
            pl.BlockSpec((tm, n_attn), tok_p),
            pl.BlockSpec((tm, n_attn), tok_s),
            pl.BlockSpec(memory_space=pl.ANY),
            slab_spec(D_FOUR, D_MODEL),
            slab_spec(n_attn, D_MODEL),
            slab_spec(D_MODEL, D_MODEL),
        ],
        out_specs=pl.BlockSpec((tm, D_MODEL), tok),
        out_shape=jax.ShapeDtypeStruct((N_TOK, D_MODEL), F32),
        scratch_shapes=[
            pltpu.VMEM((2 * D_MODEL, D_MODEL), BF16),
            pltpu.VMEM((D_FOUR, D_MODEL), BF16),
            pltpu.VMEM((n_attn, D_MODEL), BF16),
            pltpu.VMEM((D_MODEL, D_MODEL), BF16),
            pltpu.VMEM((2, GATE_CHUNK, D_MODEL), F32),
            pltpu.SemaphoreType.DMA((2,)),
        ],
        compiler_params=pltpu.CompilerParams(
            dimension_semantics=("arbitrary",), vmem_limit_bytes=VMEM_LIMIT),
        name="mixer_tail",
    )(x, mod, mod, mod, nw, *frs, *os_, w_in_t, w_four, w_attn_proj, w_out)


def _rope_tables(tm):
    rows = DEC_SEQ // GRID_W
    row = np.repeat(np.arange(rows), GRID_W).astype(np.float64)
    col = np.tile(np.arange(GRID_W), rows).astype(np.float64)
    axis_dim = QK_ROPE // 2
    inv = ROPE_BASE ** (-np.arange(0, axis_dim, 2, dtype=np.float64) / axis_dim)
    ar = row[:, None] * inv
    ac = col[:, None] * inv
    cr, sr, cc, sc = np.cos(ar), np.sin(ar), np.cos(ac), np.sin(ac)
    cos32 = np.concatenate([cr, cr, cc, cc], axis=1)
    sin32 = np.concatenate([-sr, sr, -sc, sc], axis=1)
    scale = np.float32((QK_NOPE + QK_ROPE) ** -0.5 * np.log2(np.e))

    def table(rope_part, nope_val, ident_rope):
        t = np.zeros((DEC_SEQ + tm, HEAD_PAD), np.float32)
        t[:DEC_SEQ, :QK_ROPE] = rope_part
        t[:DEC_SEQ, QK_ROPE:QK_ROPE + QK_NOPE] = nope_val
        t[DEC_SEQ:, :QK_ROPE] = ident_rope
        t[DEC_SEQ:, QK_ROPE:QK_ROPE + QK_NOPE] = nope_val
        return t

    cq = (table(cos32, 1.0, 1.0) * scale).T
    sq = (table(sin32, 0.0, 0.0) * scale).T
    ck = table(cos32, 0.0, 1.0)
    sk = table(sin32, 0.0, 0.0)
    return tuple(jnp.asarray(np.ascontiguousarray(t), F32) for t in (cq, sq, ck, sk))


def kernel(x_prompt, x_sample, cache_ckv, cache_krope, c, c_ctx, w_mod, b_mod, norm_ffn1, w_ffn1_gate,
           w_ffn1_up, w_ffn1_down, norm_mix, w_in, w_four, q_norm, w_uq, kv_norm, w_ukv, w_attn_proj,
           w_out, norm_ffn2, w_ffn2_gate, w_ffn2_up, w_ffn2_down, final_norm):
    xs = (x_prompt.reshape(N_PROMPT, D_MODEL), x_sample.reshape(N_SAMPLE, D_MODEL))
    c_all = jnp.concatenate(
        [c_ctx[None, :], c, jnp.zeros((MOD_ROWS - 1 - DEC_BATCH, D_MODEL), F32)], axis=0)
    mod = _modulation(c_all, w_mod, b_mod).reshape(DEPTH * MOD_ROWS * N_MOD, 1, D_MODEL)
    cache_kr = jnp.pad(cache_krope, ((0, 0), (0, 0), (0, 0), (0, HEAD_PAD - QK_ROPE)))
    w_in_t = jnp.swapaxes(w_in, 1, 2)

    caches = ()
    for l in range(DEPTH):
        x = _ffn(xs, mod, l, 0, norm_ffn1, w_ffn1_gate, w_ffn1_up, w_ffn1_down, final_norm, False)
        f_in, q, ckv_n, kr, new_ckv, new_krope = _proj(
            x, mod, l, norm_mix, w_in_t, q_norm, w_uq, kv_norm, _rope_tables, caches)
        caches = (new_ckv, new_krope)
        frs = (_dft(f_in, BATCH, SEQ, 0), _dft(f_in, DEC_BATCH, DEC_SEQ, N_PROMPT))
        os_ = (_attention(q, ckv_n, kr, w_ukv, l, BATCH, SEQ, 0),
               _attention(q, ckv_n, kr, w_ukv, l, DEC_BATCH, DEC_SEQ, N_PROMPT, (cache_ckv, cache_kr)))
        x = _tail(x, mod, l, norm_mix, frs, os_, w_in_t, w_four, w_attn_proj, w_out)
        xs = _ffn((x,), mod, l, 6, norm_ffn2, w_ffn2_gate, w_ffn2_up, w_ffn2_down, final_norm, l == DEPTH - 1)
        if l < DEPTH - 1:
            xs = (xs,)

    y_prompt = xs[0].reshape(BATCH, SEQ, D_MODEL)
    y_sample = xs[1].reshape(DEC_BATCH, DEC_SEQ, D_MODEL)
    return y_prompt, y_sample, caches[0], caches[1]
```

```python
import functools

import numpy as np
import jax
import jax.numpy as jnp
from jax import lax
from jax.experimental import pallas as pl
from jax.experimental.pallas import tpu as pltpu

D_MODEL = 1024
BATCH = 16
SEQ = 256
DEPTH = 2
DEC_BATCH = 4
DEC_SEQ = 1024
PAST_LEN = 512
GRID_W = 64
D_FF = 2816
FOUR_GROUPS = 4
FOUR_GROUP_DIM = 64
D_FOUR = FOUR_GROUPS * FOUR_GROUP_DIM
N_HEADS = 8
QK_NOPE = 64
QK_ROPE = 32
V_DIM = 64
Q_RANK = 384
KV_RANK = 256
N_MOD = 9
ROPE_BASE = 10000.0
EPS = 1e-6

N_PROMPT = BATCH * SEQ
N_SAMPLE = DEC_BATCH * DEC_SEQ
N_TOK = N_PROMPT + N_SAMPLE
MOD_ROWS = 8
HEAD_PAD = 128
ROPE_AXIS = QK_ROPE // 2
ROPE_PAIR = ROPE_AXIS // 2

VMEM_LIMIT = 52 * 1024 * 1024

F32 = jnp.float32
BF16 = jnp.bfloat16


def _dot(a, b):
    return jnp.dot(a, b, preferred_element_type=F32)


def _dot_nt(a, b):
    return lax.dot_general(a, b, (((1,), (1,)), ((), ())), preferred_element_type=F32)


def _mod_row(i, tm):
    n_prompt_tiles = N_PROMPT // tm
    per_batch = DEC_SEQ // tm
    return jnp.where(i < n_prompt_tiles, 0, 1 + (i - n_prompt_tiles) // per_batch)


def _mod_spec(layer, which, tm, tile_of_step=lambda i: i):
    def idx(*g):
        return ((layer * MOD_ROWS + _mod_row(tile_of_step(g[0]), tm)) * N_MOD + which, 0, 0)
    return pl.BlockSpec((None, 1, D_MODEL), idx)


def _norm_mod(x, nw, sc, sh):
    ms = jnp.mean(x * x, axis=-1, keepdims=True)
    y = x * lax.rsqrt(ms + EPS) * nw
    return y * (1.0 + sc) + sh


def _mod_kernel(c_ref, w_ref, b_ref, o_ref):
    c = c_ref[...]
    a = (c * jax.nn.sigmoid(c)).astype(BF16)
    o_ref[...] = _dot(a, w_ref[...].astype(BF16)) + b_ref[pl.ds(pl.program_id(0), 1), :]


def _modulation(c_all, w_mod, b_mod):
    tn = 1024
    n_out = N_MOD * D_MODEL
    return pl.pallas_call(
        _mod_kernel,
        grid=(DEPTH, n_out // tn),
        in_specs=[
            pl.BlockSpec((MOD_ROWS, D_MODEL), lambda l, j: (0, 0)),
            pl.BlockSpec((None, D_MODEL, tn), lambda l, j: (l, 0, j)),
            pl.BlockSpec((DEPTH, tn), lambda l, j: (0, j)),
        ],
        out_specs=pl.BlockSpec((None, MOD_ROWS, tn), lambda l, j: (l, 0, j)),
        out_shape=jax.ShapeDtypeStruct((DEPTH, MOD_ROWS, n_out), F32),
        compiler_params=pltpu.CompilerParams(
            dimension_semantics=("arbitrary", "arbitrary"), vmem_limit_bytes=VMEM_LIMIT),
        name="modulation",
    )(c_all, w_mod, b_mod)


FFN_TM = 512
FFN_CHUNK = 256
FFN_NC = D_FF // FFN_CHUNK


def _ffn_kernel(*refs, layer, final, split_in):
    refs = list(refs)
    x_refs = [refs.pop(0) for _ in range(2 if split_in else 1)]
    sh_ref, sc_ref, g_ref, nw_ref, wg_ref, wu_ref, wd_ref, fn_ref = refs[:8]
    o_refs = refs[8:10] if final else refs[8:9]
    wg_s, wu_s, wd_s, h_scr, acc_scr = refs[-5:]
    s = pl.program_id(0)
    is_prompt = jnp.maximum(s - (FFN_NC - 1), 0) < N_PROMPT // FFN_TM

    def x_tile():
        if split_in:
            return jnp.where(is_prompt, x_refs[0][...], x_refs[1][...])
        return x_refs[0][...]

    def chunk_act(h, wg, wu):
        gate = _dot(h, wg)
        up = _dot(h, wu)
        return ((gate * jax.nn.sigmoid(gate)) * up).astype(BF16)

    def finish(acc):
        xn = x_tile() + 0.5 * g_ref[...] * acc
        if not final:
            o_refs[0][...] = xn
            return
        ms = jnp.mean(xn * xn, axis=-1, keepdims=True)
        y = xn * lax.rsqrt(ms + EPS) * fn_ref[...]

        @pl.when(is_prompt)
        def _():
            o_refs[0][...] = y

        @pl.when(jnp.logical_not(is_prompt))
        def _():
            o_refs[1][...] = y

    def hidden():
        return _norm_mod(x_tile(), nw_ref[layer:layer + 1, :], sc_ref[...], sh_ref[...]).astype(BF16)

    @pl.when(s == 0)
    def _():
        h_scr[...] = hidden()
        acc_scr[...] = jnp.zeros_like(acc_scr)

    @pl.when(s < FFN_NC)
    def _():
        wg = wg_ref[...].astype(BF16)
        wu = wu_ref[...].astype(BF16)
        wd = wd_ref[...].astype(BF16)
        wg_s[s] = wg
        wu_s[s] = wu
        wd_s[pl.ds(pl.multiple_of(s * FFN_CHUNK, FFN_CHUNK), FFN_CHUNK), :] = wd
        acc_scr[...] += _dot(chunk_act(h_scr[...], wg, wu), wd)

    @pl.when(s == FFN_NC - 1)
    def _():
        finish(acc_scr[...])

    @pl.when(s >= FFN_NC)
    def _():
        h = hidden()
        act = jnp.concatenate([chunk_act(h, wg_s[j], wu_s[j]) for j in range(FFN_NC)], axis=1)
        finish(_dot(act, wd_s[...]))


def _ffn(xs, mod, layer, mod_base, nw, wg, wu, wd, final_norm, final):
    tm = FFN_TM
    npt = N_PROMPT // tm
    split_in = len(xs) == 2
    tile = lambda s: jnp.maximum(s - (FFN_NC - 1), 0)
    chunk = lambda s: jnp.minimum(s, FFN_NC - 1)
    tok = pl.BlockSpec((tm, D_MODEL), lambda s: (tile(s), 0))
    tok_p = pl.BlockSpec((tm, D_MODEL), lambda s: (jnp.minimum(tile(s), npt - 1), 0))
    tok_s = pl.BlockSpec((tm, D_MODEL), lambda s: (jnp.maximum(tile(s) - npt, 0), 0))
    half = jax.ShapeDtypeStruct((N_PROMPT, D_MODEL), F32)
    return pl.pallas_call(
        functools.partial(_ffn_kernel, layer=layer, final=final, split_in=split_in),
        grid=(FFN_NC + N_TOK // tm - 1,),
        in_specs=([tok_p, tok_s] if split_in else [tok]) + [
            _mod_spec(layer, mod_base + 0, tm, tile),
            _mod_spec(layer, mod_base + 1, tm, tile),
            _mod_spec(layer, mod_base + 2, tm, tile),
            pl.BlockSpec((DEPTH, D_MODEL), lambda s: (0, 0)),
            pl.BlockSpec((None, D_MODEL, FFN_CHUNK), lambda s: (layer, 0, chunk(s))),
            pl.BlockSpec((None, D_MODEL, FFN_CHUNK), lambda s: (layer, 0, chunk(s))),
            pl.BlockSpec((None, FFN_CHUNK, D_MODEL), lambda s: (layer, chunk(s), 0)),
            pl.BlockSpec((1, D_MODEL), lambda s: (0, 0)),
        ],
        out_specs=[tok_p, tok_s] if final else tok,
        out_shape=[half, half] if final else jax.ShapeDtypeStruct((N_TOK, D_MODEL), F32),
        scratch_shapes=[
            pltpu.VMEM((FFN_NC, D_MODEL, FFN_CHUNK), BF16),
            pltpu.VMEM((FFN_NC, D_MODEL, FFN_CHUNK), BF16),
            pltpu.VMEM((D_FF, D_MODEL), BF16),
            pltpu.VMEM((tm, D_MODEL), BF16),
            pltpu.VMEM((tm, D_MODEL), F32),
        ],
        compiler_params=pltpu.CompilerParams(
            dimension_semantics=("arbitrary",), vmem_limit_bytes=VMEM_LIMIT),
        name="ffn_final" if final else ("ffn_first" if split_in else "ffn"),
    )(*xs, mod, mod, mod, nw, wg, wu, wd, final_norm.reshape(1, D_MODEL))


PROJ_TM = 1024
PROJ_SEQS = PROJ_TM // SEQ
PROJ_W = 1024
ROPE_COL = D_FOUR + Q_RANK + KV_RANK


def _pair_swap(x):
    lane = lax.broadcasted_iota(jnp.int32, x.shape, 1)
    first = lane % ROPE_AXIS < ROPE_PAIR
    return jnp.where(first, pltpu.roll(x, HEAD_PAD - ROPE_PAIR, 1), pltpu.roll(x, ROPE_PAIR, 1))


def _proj_kernel(*refs, layer, first_layer):
    (x_ref, sh_ref, sc_ref, nw_ref, w1_ref, qn_ref, wq_ref, pq_ref, kvn_ref,
     cq_ref, sq_ref, ck_ref, sk_ref) = refs[:13]
    f_ref, q_ref, ckv_ref, kr_ref, nckv_ref, nkr_ref, w1_s, wq_s = refs[-8:]
    is_prompt = pl.program_id(0) < N_PROMPT // PROJ_TM

    @pl.when(pl.program_id(0) == 0)
    def _():
        w1_s[...] = w1_ref[...].T.astype(BF16)
        wq_s[...] = _dot_nt(pq_ref[...], wq_ref[...].astype(BF16)).astype(BF16)

    n_qp = N_HEADS * HEAD_PAD
    c0 = D_FOUR + Q_RANK
    groups = [slice(r * SEQ, (r + 1) * SEQ) for r in range(PROJ_SEQS)]
    ps = []
    for rows in groups:
        h = _norm_mod(x_ref[rows, :], nw_ref[layer:layer + 1, :], sc_ref[...], sh_ref[...]).astype(BF16)
        ps.append(_dot(h, w1_s[...]))
    staged = []
    for rows, p in zip(groups, ps):
        f_ref[rows, :] = p[:, :D_FOUR]
        ql = p[:, D_FOUR:c0]
        qn = ql * lax.rsqrt(jnp.mean(ql * ql, axis=-1, keepdims=True) + EPS) * qn_ref[layer:layer + 1, :]
        qn = qn.astype(BF16)
        qq = _dot_nt(wq_s[:n_qp, :], qn)
        ckv = p[:, c0:c0 + KV_RANK]
        ckv_n = ckv * lax.rsqrt(jnp.mean(ckv * ckv, axis=-1, keepdims=True) + EPS) * kvn_ref[layer:layer + 1, :]
        ckv_ref[rows, :] = ckv_n
        kr = p[:, ROPE_COL:ROPE_COL + HEAD_PAD]
        staged.append((qn, qq, ckv_n, kr))

    @pl.when(is_prompt)
    def _():
        for r, (rows, (qn, qq, ckv_n, kr)) in enumerate(zip(groups, staged)):
            cq = cq_ref[:, rows]
            for hd in range(N_HEADS):
                lo = hd * HEAD_PAD
                q_ref[lo:lo + HEAD_PAD, rows] = (qq[lo:lo + HEAD_PAD, :] * cq).astype(BF16)
            kr_ref[rows, :] = (kr * ck_ref[rows, :]).astype(BF16)
            if first_layer:
                nckv_ref[r, 0] = ckv_n
                nkr_ref[r, 0] = kr.T[:QK_ROPE]
                for later in range(1, DEPTH):
                    nckv_ref[r, later] = jnp.zeros_like(ckv_n)
                    nkr_ref[r, later] = jnp.zeros((QK_ROPE, SEQ), F32)
            else:
                nckv_ref[r] = ckv_n
                nkr_ref[r] = kr.T[:QK_ROPE]

    @pl.when(jnp.logical_not(is_prompt))
    def _():
        for rows, (qn, qq, ckv_n, kr) in zip(groups, staged):
            cq, sq = cq_ref[:, rows], sq_ref[:, rows]
            qs = _dot_nt(wq_s[n_qp:, :], qn)
            for hd in range(N_HEADS):
                lo, mid = hd * HEAD_PAD, hd * HEAD_PAD + QK_ROPE
                rope = qq[lo:mid, :] * cq[:QK_ROPE] + qs[hd * QK_ROPE:(hd + 1) * QK_ROPE, :] * sq[:QK_ROPE]
                q_ref[lo:mid, rows] = rope.astype(BF16)
                q_ref[mid:lo + HEAD_PAD, rows] = (qq[mid:lo + HEAD_PAD, :] * cq[QK_ROPE:]).astype(BF16)
            kr_ref[rows, :] = (kr * ck_ref[rows, :] + _pair_swap(kr) * sk_ref[rows, :]).astype(BF16)


def _q_select():
    n_qp = N_HEADS * HEAD_PAD
    p = np.zeros((n_qp + N_HEADS * QK_ROPE, N_HEADS * (QK_NOPE + QK_ROPE)), np.float32)
    for hd in range(N_HEADS):
        src, dst = hd * (QK_NOPE + QK_ROPE), hd * HEAD_PAD
        for j in range(QK_ROPE):
            partner = j + ROPE_PAIR if j % ROPE_AXIS < ROPE_PAIR else j - ROPE_PAIR
            p[dst + j, src + QK_NOPE + j] = 1.0
            p[n_qp + hd * QK_ROPE + j, src + QK_NOPE + partner] = 1.0
        for j in range(QK_NOPE):
            p[dst + QK_ROPE + j, src + j] = 1.0
    return p.astype(BF16)


def _proj(x, mod, layer, nw, w_in_t, qn, w_uq, kvn, tabs, prev_caches):
    tm = PROJ_TM
    n_prompt_tiles = N_PROMPT // tm
    per_batch = DEC_SEQ // tm
    n_q = N_HEADS * (QK_NOPE + QK_ROPE)

    def tab_blk(i):
        return jnp.where(i < n_prompt_tiles, per_batch, (i - n_prompt_tiles) % per_batch)

    tok = lambda i: (i, 0)
    const = lambda i: (0, 0)
    first_layer = not prev_caches
    assert first_layer == (layer == 0)
    cache_layers = DEPTH if first_layer else None
    cache_idx = lambda i: (jnp.minimum(i, n_prompt_tiles - 1), 0 if first_layer else layer, 0, 0)
    tab_spec = pl.BlockSpec((tm, HEAD_PAD), lambda i: (tab_blk(i), 0))
    tab_t_spec = pl.BlockSpec((HEAD_PAD, tm), lambda i: (0, tab_blk(i)))
    return pl.pallas_call(
        functools.partial(_proj_kernel, layer=layer, first_layer=first_layer),
        grid=(N_TOK // tm,),
        in_specs=[
            pl.BlockSpec((tm, D_MODEL), tok),
            _mod_spec(layer, 3, tm),
            _mod_spec(layer, 4, tm),
            pl.BlockSpec((DEPTH, D_MODEL), const),
            pl.BlockSpec((None, PROJ_W, D_MODEL), lambda i: (layer, 0, 0)),
            pl.BlockSpec((DEPTH, Q_RANK), const),
            pl.BlockSpec((None, Q_RANK, n_q), lambda i: (layer, 0, 0)),
            pl.BlockSpec((N_HEADS * (HEAD_PAD + QK_ROPE), n_q), const),
            pl.BlockSpec((DEPTH, KV_RANK), const),
            tab_t_spec, tab_t_spec, tab_spec, tab_spec,
        ] + [pl.BlockSpec(memory_space=pl.ANY)] * len(prev_caches),
        out_specs=[
            pl.BlockSpec((tm, D_FOUR), tok),
            pl.BlockSpec((N_HEADS * HEAD_PAD, tm), lambda i: (0, i)),
            pl.BlockSpec((tm, KV_RANK), tok),
            pl.BlockSpec((tm, HEAD_PAD), tok),
            pl.BlockSpec((PROJ_SEQS, cache_layers, SEQ, KV_RANK), cache_idx),
            pl.BlockSpec((PROJ_SEQS, cache_layers, QK_ROPE, SEQ), cache_idx),
        ],
        out_shape=[
            jax.ShapeDtypeStruct((N_TOK, D_FOUR), F32),
            jax.ShapeDtypeStruct((N_HEADS * HEAD_PAD, N_TOK), BF16),
            jax.ShapeDtypeStruct((N_TOK, KV_RANK), F32),
            jax.ShapeDtypeStruct((N_TOK, HEAD_PAD), BF16),
            jax.ShapeDtypeStruct((BATCH, DEPTH, SEQ, KV_RANK), F32),
            jax.ShapeDtypeStruct((BATCH, DEPTH, QK_ROPE, SEQ), F32),
        ],
        input_output_aliases={13 + k: 4 + k for k in range(len(prev_caches))},
        scratch_shapes=[
            pltpu.VMEM((PROJ_W, D_MODEL), BF16),
            pltpu.VMEM((N_HEADS * (HEAD_PAD + QK_ROPE), Q_RANK), BF16),
        ],
        compiler_params=pltpu.CompilerParams(
            dimension_semantics=("arbitrary",), vmem_limit_bytes=VMEM_LIMIT),
        name="mixer_proj",
    )(x, mod, mod, nw, w_in_t, qn, w_uq, _q_select(), kvn, *tabs(tm), *prev_caches)


DFT_ROWS = 1024


def _split(x):
    hi = x.astype(BF16)
    lo = (x - hi.astype(F32)).astype(BF16)
    return hi, lo


def _dot3(ah, al, bh, bl):
    return _dot(ah, bh) + _dot(al, bh) + _dot(ah, bl)


def _dft_kernel(x_ref, gh_ref, gl_ref, ch_ref, cl_ref, sh_ref, sl_ref, rev_ref, o_ref, *, n):
    half = n // 2
    xh, xl = _split(x_ref[...])
    y = _dot3(xh, xl, gh_ref[...], gl_ref[...])
    yh, yl = _split(y)
    ch, cl, sh, sl = ch_ref[...], cl_ref[...], sh_ref[...], sl_ref[...]
    pos = lax.broadcasted_iota(jnp.int32, (n, D_FOUR), 0)
    alt = jnp.where(pos % 2 == 0, 1.0, -1.0).astype(F32) * (float(n) ** -0.5)
    first = lax.broadcasted_iota(jnp.int32, (half, D_FOUR), 0) == 0
    for b in range(x_ref.shape[0] // n):
        r = slice(b * n, (b + 1) * n)
        a = _dot3(ch, cl, yh[r, :D_FOUR], yl[r, :D_FOUR])
        bb = _dot3(sh, sl, yh[r, D_FOUR:], yl[r, D_FOUR:])
        o_ref[b * n:b * n + half, :] = (a - bb).astype(BF16)
        mirrored = _dot(rev_ref[...], (a + bb).astype(BF16))
        nyquist = jnp.sum(y[r, :D_FOUR] * alt, axis=0, keepdims=True)
        o_ref[b * n + half:(b + 1) * n, :] = jnp.where(first, nyquist, mirrored).astype(BF16)


def _split_np(a):
    a32 = np.asarray(a, np.float32)
    hi = a32.astype(BF16)
    lo = (a32 - hi.astype(np.float32)).astype(BF16)
    return hi, lo


def _dft_tables(n):
    half = n // 2
    k = np.arange(half, dtype=np.int64)
    pos = np.arange(n, dtype=np.int64)
    ang = 2.0 * np.pi * ((k[:, None] * pos[None, :]) % n).astype(np.float64) / n
    c = np.arange(FOUR_GROUP_DIM, dtype=np.int64)
    ang_c = 2.0 * np.pi * ((c[:, None] * c[None, :]) % FOUR_GROUP_DIM).astype(np.float64) / FOUR_GROUP_DIM
    eye = np.eye(FOUR_GROUPS)
    g = np.concatenate([np.kron(eye, np.cos(ang_c)), np.kron(eye, np.sin(ang_c))], axis=1)
    g = g / np.sqrt(FOUR_GROUP_DIM)
    rev = np.zeros((half, half), np.float32)
    rev[np.arange(1, half), half - np.arange(1, half)] = 1.0
    return (_split_np(g) + _split_np(np.cos(ang) / np.sqrt(n)) + _split_np(np.sin(ang) / np.sqrt(n))
            + (rev.astype(BF16),))


def _dft(f_in, n_batch, n, row0):
    tables = _dft_tables(n)
    rows = max(n, DFT_ROWS)
    blk0 = row0 // rows
    const = lambda b: (0, 0)
    half_spec = pl.BlockSpec((n // 2, n), const)
    return pl.pallas_call(
        functools.partial(_dft_kernel, n=n),
        grid=(n_batch * n // rows,),
        in_specs=[
            pl.BlockSpec((rows, D_FOUR), lambda b: (blk0 + b, 0)),
            pl.BlockSpec((D_FOUR, 2 * D_FOUR), const),
            pl.BlockSpec((D_FOUR, 2 * D_FOUR), const),
            half_spec, half_spec, half_spec, half_spec,
            pl.BlockSpec((n // 2, n // 2), const),
        ],
        out_specs=pl.BlockSpec((rows, D_FOUR), lambda b: (b, 0)),
        out_shape=jax.ShapeDtypeStruct((n_batch * n, D_FOUR), BF16),
        compiler_params=pltpu.CompilerParams(
            dimension_semantics=("arbitrary",), vmem_limit_bytes=VMEM_LIMIT),
        name="fnet_dft_%d" % n,
    )(f_in, *tables)


ATTN_TQ = 512
ATTN_SEQS = 4
ATTN_AHEAD = 3
VT_ROWS = V_DIM + 16


def _attn_kernel(*refs, n_own, n_ctx, seqs):
    if n_ctx:
        qt_ref, ckv_ref, kr_ref, cckv_ref, ckr_ref, wkv_ref, pk_ref, pvt_ref = refs[:8]
    else:
        qt_ref, ckv_ref, kr_ref, wkv_ref, pk_ref, pvt_ref = refs[:6]
    o_ref, k_scr, vt_scr, wk_s, wvt_s = refs[-5:]

    @pl.when(jnp.logical_and(pl.program_id(0) == 0, pl.program_id(1) == 0))
    def _():
        wkv = wkv_ref[...].astype(BF16)
        wk_s[...] = _dot(wkv, pk_ref[...]).astype(BF16)
        wvt_s[...] = _dot_nt(pvt_ref[...], wkv).astype(BF16)

    @pl.when(pl.program_id(1) == 0)
    def _():
        ckv = ckv_ref[...].astype(BF16)
        kr = kr_ref[...]
        if n_ctx:
            ckv = jnp.concatenate([ckv, cckv_ref[...].astype(BF16)], axis=0)
            ckr = jnp.concatenate([ckr_ref[...], jnp.zeros((HEAD_PAD - QK_ROPE, n_ctx), F32)], axis=0)
            kr = jnp.concatenate([kr, ckr.T.astype(BF16)], axis=0)
        k_nope = _dot(ckv, wk_s[...])
        kr = kr.astype(F32)
        for hd in range(N_HEADS):
            lo = hd * HEAD_PAD
            k_scr[:, lo:lo + HEAD_PAD] = (k_nope[:, lo:lo + HEAD_PAD] + kr).astype(BF16)
        vt = _dot_nt(wvt_s[...], ckv).astype(BF16)
        pad_rows = lax.broadcasted_iota(jnp.int32, (VT_ROWS - V_DIM, vt.shape[1]), 0)
        ones_row = (pad_rows == 0).astype(BF16)
        for hd in range(N_HEADS):
            vt_scr[hd * VT_ROWS:hd * VT_ROWS + V_DIM, :] = vt[hd * V_DIM:(hd + 1) * V_DIM]
            vt_scr[hd * VT_ROWS + V_DIM:(hd + 1) * VT_ROWS, :] = ones_row

    n_keys = n_own + n_ctx
    tq = qt_ref.shape[1] // seqs
    units = [(j, hd) for j in range(seqs) for hd in range(N_HEADS)]

    def scores(j, hd):
        lo = hd * HEAD_PAD
        return _dot(k_scr[j * n_keys:(j + 1) * n_keys, lo:lo + HEAD_PAD],
                    qt_ref[lo:lo + HEAD_PAD, j * tq:(j + 1) * tq])

    outs = []

    def values(j, hd, p):
        ov = _dot(vt_scr[hd * VT_ROWS:(hd + 1) * VT_ROWS, j * n_keys:(j + 1) * n_keys], p)
        outs.append(ov[:V_DIM] / ov[V_DIM:V_DIM + 1])
        if hd == N_HEADS - 1:
            o_ref[j * tq:(j + 1) * tq, :] = jnp.concatenate(outs, axis=0).T.astype(BF16)
            outs.clear()

    queue = [scores(*u) for u in units[:ATTN_AHEAD]]
    pending = None
    for i, unit in enumerate(units):
        s = queue.pop(0)
        if i + ATTN_AHEAD < len(units):
            queue.append(scores(*units[i + ATTN_AHEAD]))
        p = jnp.exp2(s - jnp.max(s, axis=0, keepdims=True)).astype(BF16)
        if pending is not None:
            values(*pending)
        pending = (*unit, p)
    values(*pending)


def _kv_select():
    n_kv = N_HEADS * (QK_NOPE + V_DIM)
    pk = np.zeros((n_kv, N_HEADS * HEAD_PAD), np.float32)
    pvt = np.zeros((N_HEADS * V_DIM, n_kv), np.float32)
    for hd in range(N_HEADS):
        src = hd * (QK_NOPE + V_DIM)
        for j in range(QK_NOPE):
            pk[src + j, hd * HEAD_PAD + QK_ROPE + j] = 1.0
        for j in range(V_DIM):
            pvt[hd * V_DIM + j, src + QK_NOPE + j] = 1.0
    return pk.astype(BF16), pvt.astype(BF16)


def _attention(q, ckv_n, kr, w_ukv, layer, n_batch, n_own, row0, ctx=None):
    tq = min(n_own, ATTN_TQ)
    n_kv = N_HEADS * (QK_NOPE + V_DIM)
    n_ctx = 0 if ctx is None else PAST_LEN
    n_keys = n_own + n_ctx
    qt = n_own // tq
    seqs = ATTN_SEQS if (ctx is None and qt == 1) else 1
    blk0 = row0 // (seqs * n_own)
    const = lambda b, t: (0, 0)
    in_specs = [
        pl.BlockSpec((N_HEADS * HEAD_PAD, seqs * tq), lambda b, t: (0, (row0 // (seqs * tq)) + b * qt + t)),
        pl.BlockSpec((seqs * n_own, KV_RANK), lambda b, t: (blk0 + b, 0)),
        pl.BlockSpec((seqs * n_own, HEAD_PAD), lambda b, t: (blk0 + b, 0)),
    ]
    args = [q, ckv_n, kr]
    if ctx is not None:
        cckv, ckr = ctx
        in_specs += [
            pl.BlockSpec((None, None, PAST_LEN, KV_RANK), lambda b, t: (b, layer, 0, 0)),
            pl.BlockSpec((None, None, QK_ROPE, PAST_LEN), lambda b, t: (b, layer, 0, 0)),
        ]
        args += [cckv, ckr]
    in_specs += [
        pl.BlockSpec((None, KV_RANK, n_kv), lambda b, t: (layer, 0, 0)),
        pl.BlockSpec((n_kv, N_HEADS * HEAD_PAD), const),
        pl.BlockSpec((N_HEADS * V_DIM, n_kv), const),
    ]
    args += [w_ukv, *_kv_select()]
    return pl.pallas_call(
        functools.partial(_attn_kernel, n_own=n_own, n_ctx=n_ctx, seqs=seqs),
        grid=(n_batch // seqs, qt),
        in_specs=in_specs,
        out_specs=pl.BlockSpec((seqs * tq, N_HEADS * V_DIM), lambda b, t: (b * qt + t, 0)),
        out_shape=jax.ShapeDtypeStruct((n_batch * n_own, N_HEADS * V_DIM), BF16),
        scratch_shapes=[
            pltpu.VMEM((seqs * n_keys, N_HEADS * HEAD_PAD), BF16),
            pltpu.VMEM((N_HEADS * VT_ROWS, seqs * n_keys), BF16),
            pltpu.VMEM((KV_RANK, N_HEADS * HEAD_PAD), BF16),
            pltpu.VMEM((N_HEADS * V_DIM, KV_RANK), BF16),
        ],
        compiler_params=pltpu.CompilerParams(
            dimension_semantics=("arbitrary", "arbitrary"), vmem_limit_bytes=VMEM_LIMIT),
        name="mla_attention_%d" % n_keys,
    )(*args)


TAIL_TM = 1024
TAIL_GROUP = 256
TAIL_SLABS = 4
GATE_ROW = ROPE_COL + QK_ROPE
GATE_CHUNK = 512
GATE_CHUNKS = 2 * D_MODEL // GATE_CHUNK


def _tail_kernel(x_ref, sh_ref, sc_ref, g_ref, nw_ref, frp_ref, frs_ref, op_ref, os_ref, wi_hbm,
                 wf_ref, wa_ref, wo_ref, out_ref, wg_s, wf_s, wa_s, wo_s, stage, sem, *, layer):
    s = pl.program_id(0)
    is_prompt = jnp.maximum(s - TAIL_SLABS, 0) < N_PROMPT // TAIL_TM

    def gate_copy(c):
        src = wi_hbm.at[layer, pl.ds(GATE_ROW + c * GATE_CHUNK, GATE_CHUNK), :]
        return pltpu.make_async_copy(src, stage.at[c % 2], sem.at[c % 2])

    @pl.when(s == 0)
    def _():
        gate_copy(0).start()
        for c in range(GATE_CHUNKS):
            if c + 1 < GATE_CHUNKS:
                gate_copy(c + 1).start()
            gate_copy(c).wait()
            wg_s[:, c * GATE_CHUNK:(c + 1) * GATE_CHUNK] = stage[c % 2].T.astype(BF16)

    @pl.when(s < TAIL_SLABS)
    def _():
        def put(dst, src):
            rows = src.shape[0]
            dst[pl.ds(pl.multiple_of(s * rows, rows), rows), :] = src.astype(BF16)

        put(wf_s, wf_ref[...])
        put(wa_s, wa_ref[...])
        put(wo_s, wo_ref[...])

    @pl.when(s >= TAIL_SLABS)
    def _():
        groups = [slice(r * TAIL_GROUP, (r + 1) * TAIL_GROUP) for r in range(TAIL_TM // TAIL_GROUP)]

        def branches(rows):
            x = x_ref[rows, :]
            h = _norm_mod(x, nw_ref[layer:layer + 1, :], sc_ref[...], sh_ref[...]).astype(BF16)
            gates = _dot(h, wg_s[...])
            a_out = _dot(jnp.where(is_prompt, frp_ref[rows, :], frs_ref[rows, :]), wf_s[...])
            o_out = _dot(jnp.where(is_prompt, op_ref[rows, :], os_ref[rows, :]), wa_s[...])
            return x, gates, a_out, o_out

        def merge(rows, x, gates, a_out, o_out):
            merged = jax.nn.sigmoid(gates[:, :D_MODEL]) * a_out + jax.nn.sigmoid(gates[:, D_MODEL:]) * o_out
            m = _dot(merged.astype(BF16), wo_s[...])
            out_ref[rows, :] = x + g_ref[...] * m

        pending = branches(groups[0])
        for r in range(1, len(groups)):
            ready, pending = pending, branches(groups[r])
            merge(groups[r - 1], *ready)
        merge(groups[-1], *pending)


def _tail(x, mod, layer, nw, frs, os_, w_in_t, w_four, w_attn_proj, w_out):
    tm = TAIL_TM
    npt = N_PROMPT // tm
    tile = lambda s: jnp.maximum(s - TAIL_SLABS, 0)
    slab = lambda s: jnp.minimum(s, TAIL_SLABS - 1)
    tok = lambda s: (tile(s), 0)
    tok_p = lambda s: (jnp.minimum(tile(s), npt - 1), 0)
    tok_s = lambda s: (jnp.maximum(tile(s) - npt, 0), 0)
    n_attn = N_HEADS * V_DIM

    def slab_spec(rows, cols):
        return pl.BlockSpec((None, rows // TAIL_SLABS, cols), lambda s: (layer, slab(s), 0))

    return pl.pallas_call(
        functools.partial(_tail_kernel, layer=layer),
        grid=(TAIL_SLABS + N_TOK // tm,),
        in_specs=[
            pl.BlockSpec((tm, D_MODEL), tok),
            _mod_spec(layer, 3, tm, tile),
            _mod_spec(layer, 4, tm, tile),
            _mod_spec(layer, 5, tm, tile),
            pl.BlockSpec((DEPTH, D_MODEL), lambda s: (0, 0)),
            pl.BlockSpec((tm, D_FOUR), tok_p),
            pl.BlockSpec((tm, D_FOUR), tok_s),
            pl.BlockSpec((tm, n_attn), tok_p),
            pl.BlockSpec((tm, n_attn), tok_s),
            pl.BlockSpec(memory_space=pl.ANY),
            slab_spec(D_FOUR, D_MODEL),
            slab_spec(n_attn, D_MODEL),
            slab_spec(D_MODEL, D_MODEL),
        ],
        out_specs=pl.BlockSpec((tm, D_MODEL), tok),
        out_shape=jax.ShapeDtypeStruct((N_TOK, D_MODEL), F32),
        scratch_shapes=[
            pltpu.VMEM((D_MODEL, 2 * D_MODEL), BF16),
            pltpu.VMEM((D_FOUR, D_MODEL), BF16),
            pltpu.VMEM((n_attn, D_MODEL), BF16),
            pltpu.VMEM((D_MODEL, D_MODEL), BF16),
            pltpu.VMEM((2, GATE_CHUNK, D_MODEL), F32),
            pltpu.SemaphoreType.DMA((2,)),
        ],
        compiler_params=pltpu.CompilerParams(
            dimension_semantics=("arbitrary",), vmem_limit_bytes=VMEM_LIMIT),
        name="mixer_tail",
    )(x, mod, mod, mod, nw, *frs, *os_, w_in_t, w_four, w_attn_proj, w_out)


def _rope_tables(tm):
    rows = DEC_SEQ // GRID_W
    row = np.repeat(np.arange(rows), GRID_W).astype(np.float64)
    col = np.tile(np.arange(GRID_W), rows).astype(np.float64)
    axis_dim = QK_ROPE // 2
    inv = ROPE_BASE ** (-np.arange(0, axis_dim, 2, dtype=np.float64) / axis_dim)
    ar = row[:, None] * inv
    ac = col[:, None] * inv
    cr, sr, cc, sc = np.cos(ar), np.sin(ar), np.cos(ac), np.sin(ac)
    cos32 = np.concatenate([cr, cr, cc, cc], axis=1)
    sin32 = np.concatenate([-sr, sr, -sc, sc], axis=1)
    scale = np.float32((QK_NOPE + QK_ROPE) ** -0.5 * np.log2(np.e))

    def table(rope_part, nope_val, ident_rope):
        t = np.zeros((DEC_SEQ + tm, HEAD_PAD), np.float32)
        t[:DEC_SEQ, :QK_ROPE] = rope_part
        t[:DEC_SEQ, QK_ROPE:QK_ROPE + QK_NOPE] = nope_val
        t[DEC_SEQ:, :QK_ROPE] = ident_rope
        t[DEC_SEQ:, QK_ROPE:QK_ROPE + QK_NOPE] = nope_val
        return t

    cq = (table(cos32, 1.0, 1.0) * scale).T
    sq = (table(sin32, 0.0, 0.0) * scale).T
    ck = table(cos32, 0.0, 1.0)
    sk = table(sin32, 0.0, 0.0)
    return tuple(jnp.asarray(np.ascontiguousarray(t), F32) for t in (cq, sq, ck, sk))


def kernel(x_prompt, x_sample, cache_ckv, cache_krope, c, c_ctx, w_mod, b_mod, norm_ffn1, w_ffn1_gate,
           w_ffn1_up, w_ffn1_down, norm_mix, w_in, w_four, q_norm, w_uq, kv_norm, w_ukv, w_attn_proj,
           w_out, norm_ffn2, w_ffn2_gate, w_ffn2_up, w_ffn2_down, final_norm):
    xs = (x_prompt.reshape(N_PROMPT, D_MODEL), x_sample.reshape(N_SAMPLE, D_MODEL))
    c_all = jnp.concatenate(
        [c_ctx[None, :], c, jnp.zeros((MOD_ROWS - 1 - DEC_BATCH, D_MODEL), F32)], axis=0)
    mod = _modulation(c_all, w_mod, b_mod).reshape(DEPTH * MOD_ROWS * N_MOD, 1, D_MODEL)
    cache_kr = jnp.swapaxes(cache_krope, 2, 3)
    w_in_t = jnp.swapaxes(w_in, 1, 2)

    caches = ()
    for l in range(DEPTH):
        x = _ffn(xs, mod, l, 0, norm_ffn1, w_ffn1_gate, w_ffn1_up, w_ffn1_down, final_norm, False)
        f_in, q, ckv_n, kr, new_ckv, new_krope = _proj(
            x, mod, l, norm_mix, w_in_t, q_norm, w_uq, kv_norm, _rope_tables, caches)
        caches = (new_ckv, new_krope)
        frs = (_dft(f_in, BATCH, SEQ, 0), _dft(f_in, DEC_BATCH, DEC_SEQ, N_PROMPT))
        os_ = (_attention(q, ckv_n, kr, w_ukv, l, BATCH, SEQ, 0),
               _attention(q, ckv_n, kr, w_ukv, l, DEC_BATCH, DEC_SEQ, N_PROMPT, (cache_ckv, cache_kr)))
        x = _tail(x, mod, l, norm_mix, frs, os_, w_in_t, w_four, w_attn_proj, w_out)
        xs = _ffn((x,), mod, l, 6, norm_ffn2, w_ffn2_gate, w_ffn2_up, w_ffn2_down, final_norm, l == DEPTH - 1)
        if l < DEPTH - 1:
            xs = (xs,)

    y_prompt = xs[0].reshape(BATCH, SEQ, D_MODEL)
    y_sample = xs[1].reshape(DEC_BATCH, DEC_SEQ, D_MODEL)
    return y_prompt, y_sample, caches[0], jnp.swapaxes(caches[1], 2, 3)
```

```python
import functools

import numpy as np
import jax
import jax.numpy as jnp
from jax import lax
from jax.experimental import pallas as pl
from jax.experimental.pallas import tpu as pltpu

D_MODEL = 1024
BATCH = 16
SEQ = 256
DEPTH = 2
DEC_BATCH = 4
DEC_SEQ = 1024
PAST_LEN = 512
GRID_W = 64
D_FF = 2816
FOUR_GROUPS = 4
FOUR_GROUP_DIM = 64
D_FOUR = FOUR_GROUPS * FOUR_GROUP_DIM
N_HEADS = 8
QK_NOPE = 64
QK_ROPE = 32
V_DIM = 64
Q_RANK = 384
KV_RANK = 256
N_MOD = 9
ROPE_BASE = 10000.0
EPS = 1e-6

N_PROMPT = BATCH * SEQ
N_SAMPLE = DEC_BATCH * DEC_SEQ
N_TOK = N_PROMPT + N_SAMPLE
MOD_ROWS = 8
HEAD_PAD = 128
ROPE_AXIS = QK_ROPE // 2
ROPE_PAIR = ROPE_AXIS // 2

VMEM_LIMIT = 52 * 1024 * 1024

F32 = jnp.float32
BF16 = jnp.bfloat16


def _dot(a, b):
    return jnp.dot(a, b, preferred_element_type=F32)


def _dot_nt(a, b):
    return lax.dot_general(a, b, (((1,), (1,)), ((), ())), preferred_element_type=F32)


def _mod_row(i, tm):
    n_prompt_tiles = N_PROMPT // tm
    per_batch = DEC_SEQ // tm
    return jnp.where(i < n_prompt_tiles, 0, 1 + (i - n_prompt_tiles) // per_batch)


def _mod_spec(layer, which, tm, tile_of_step=lambda i: i):
    def idx(*g):
        return ((layer * MOD_ROWS + _mod_row(tile_of_step(g[0]), tm)) * N_MOD + which, 0, 0)
    return pl.BlockSpec((None, 1, D_MODEL), idx)


def _norm_mod(x, nw, sc, sh):
    ms = jnp.mean(x * x, axis=-1, keepdims=True)
    y = x * lax.rsqrt(ms + EPS) * nw
    return y * (1.0 + sc) + sh


MOD_TN = 2304


def _mod_kernel(c_ref, w_ref, b_ref, o_ref):
    c = c_ref[...]
    a = (c * jax.nn.sigmoid(c)).astype(BF16)
    o_ref[...] = _dot(a, w_ref[...].astype(BF16)) + b_ref[pl.ds(pl.program_id(0), 1), :]


def _modulation(c_all, w_mod, b_mod):
    tn = MOD_TN
    n_out = N_MOD * D_MODEL
    return pl.pallas_call(
        _mod_kernel,
        grid=(DEPTH, n_out // tn),
        in_specs=[
            pl.BlockSpec((MOD_ROWS, D_MODEL), lambda l, j: (0, 0)),
            pl.BlockSpec((None, D_MODEL, tn), lambda l, j: (l, 0, j)),
            pl.BlockSpec((DEPTH, tn), lambda l, j: (0, j)),
        ],
        out_specs=pl.BlockSpec((None, MOD_ROWS, tn), lambda l, j: (l, 0, j)),
        out_shape=jax.ShapeDtypeStruct((DEPTH, MOD_ROWS, n_out), F32),
        compiler_params=pltpu.CompilerParams(
            dimension_semantics=("arbitrary", "arbitrary"), vmem_limit_bytes=VMEM_LIMIT),
        name="modulation",
    )(c_all, w_mod, b_mod)


FFN_TM = 512
FFN_CHUNK = 256
FFN_NC = D_FF // FFN_CHUNK


def _ffn_kernel(*refs, layer, final, split_in):
    refs = list(refs)
    x_refs = [refs.pop(0) for _ in range(2 if split_in else 1)]
    sh_ref, sc_ref, g_ref, nw_ref, wg_ref, wu_ref, wd_ref, fn_ref = refs[:8]
    o_refs = refs[8:10] if final else refs[8:9]
    wg_s, wu_s, wd_s, h_scr, acc_scr = refs[-5:]
    s = pl.program_id(0)
    is_prompt = jnp.maximum(s - (FFN_NC - 1), 0) < N_PROMPT // FFN_TM

    def x_tile():
        if split_in:
            return jnp.where(is_prompt, x_refs[0][...], x_refs[1][...])
        return x_refs[0][...]

    def chunk_act(h, wg, wu):
        gate = _dot(h, wg)
        up = _dot(h, wu)
        return ((gate * jax.nn.sigmoid(gate)) * up).astype(BF16)

    def finish(acc):
        xn = x_tile() + 0.5 * g_ref[...] * acc
        if not final:
            o_refs[0][...] = xn
            return
        ms = jnp.mean(xn * xn, axis=-1, keepdims=True)
        y = xn * lax.rsqrt(ms + EPS) * fn_ref[...]

        @pl.when(is_prompt)
        def _():
            o_refs[0][...] = y

        @pl.when(jnp.logical_not(is_prompt))
        def _():
            o_refs[1][...] = y

    def hidden():
        return _norm_mod(x_tile(), nw_ref[layer:layer + 1, :], sc_ref[...], sh_ref[...]).astype(BF16)

    @pl.when(s == 0)
    def _():
        h_scr[...] = hidden()
        acc_scr[...] = jnp.zeros_like(acc_scr)

    @pl.when(s < FFN_NC)
    def _():
        wg = wg_ref[...].astype(BF16)
        wu = wu_ref[...].astype(BF16)
        wd = wd_ref[...].astype(BF16)
        wg_s[s] = wg
        wu_s[s] = wu
        wd_s[pl.ds(pl.multiple_of(s * FFN_CHUNK, FFN_CHUNK), FFN_CHUNK), :] = wd
        acc_scr[...] += _dot(chunk_act(h_scr[...], wg, wu), wd)

    @pl.when(s == FFN_NC - 1)
    def _():
        finish(acc_scr[...])

    @pl.when(s >= FFN_NC)
    def _():
        h = hidden()
        act = jnp.concatenate([chunk_act(h, wg_s[j], wu_s[j]) for j in range(FFN_NC)], axis=1)
        finish(_dot(act, wd_s[...]))


def _ffn(xs, mod, layer, mod_base, nw, wg, wu, wd, final_norm, final):
    tm = FFN_TM
    npt = N_PROMPT // tm
    split_in = len(xs) == 2
    tile = lambda s: jnp.maximum(s - (FFN_NC - 1), 0)
    chunk = lambda s: jnp.minimum(s, FFN_NC - 1)
    tok = pl.BlockSpec((tm, D_MODEL), lambda s: (tile(s), 0))
    tok_p = pl.BlockSpec((tm, D_MODEL), lambda s: (jnp.minimum(tile(s), npt - 1), 0))
    tok_s = pl.BlockSpec((tm, D_MODEL), lambda s: (jnp.maximum(tile(s) - npt, 0), 0))
    half = jax.ShapeDtypeStruct((N_PROMPT, D_MODEL), F32)
    return pl.pallas_call(
        functools.partial(_ffn_kernel, layer=layer, final=final, split_in=split_in),
        grid=(FFN_NC + N_TOK // tm - 1,),
        in_specs=([tok_p, tok_s] if split_in else [tok]) + [
            _mod_spec(layer, mod_base + 0, tm, tile),
            _mod_spec(layer, mod_base + 1, tm, tile),
            _mod_spec(layer, mod_base + 2, tm, tile),
            pl.BlockSpec((DEPTH, D_MODEL), lambda s: (0, 0)),
            pl.BlockSpec((None, D_MODEL, FFN_CHUNK), lambda s: (layer, 0, chunk(s))),
            pl.BlockSpec((None, D_MODEL, FFN_CHUNK), lambda s: (layer, 0, chunk(s))),
            pl.BlockSpec((None, FFN_CHUNK, D_MODEL), lambda s: (layer, chunk(s), 0)),
            pl.BlockSpec((1, D_MODEL), lambda s: (0, 0)),
        ],
        out_specs=[tok_p, tok_s] if final else tok,
        out_shape=[half, half] if final else jax.ShapeDtypeStruct((N_TOK, D_MODEL), F32),
        scratch_shapes=[
            pltpu.VMEM((FFN_NC, D_MODEL, FFN_CHUNK), BF16),
            pltpu.VMEM((FFN_NC, D_MODEL, FFN_CHUNK), BF16),
            pltpu.VMEM((D_FF, D_MODEL), BF16),
            pltpu.VMEM((tm, D_MODEL), BF16),
            pltpu.VMEM((tm, D_MODEL), F32),
        ],
        compiler_params=pltpu.CompilerParams(
            dimension_semantics=("arbitrary",), vmem_limit_bytes=VMEM_LIMIT),
        name="ffn_final" if final else ("ffn_first" if split_in else "ffn"),
    )(*xs, mod, mod, mod, nw, wg, wu, wd, final_norm.reshape(1, D_MODEL))


PROJ_TM = 1024
PROJ_SEQS = PROJ_TM // SEQ
PROJ_W = 1024
ROPE_COL = D_FOUR + Q_RANK + KV_RANK


def _pair_swap(x):
    lane = lax.broadcasted_iota(jnp.int32, x.shape, 1)
    first = lane % ROPE_AXIS < ROPE_PAIR
    return jnp.where(first, pltpu.roll(x, HEAD_PAD - ROPE_PAIR, 1), pltpu.roll(x, ROPE_PAIR, 1))


def _proj_kernel(*refs, layer, first_layer):
    (x_ref, sh_ref, sc_ref, nw_ref, w1_ref, qn_ref, wq_ref, pq_ref, kvn_ref,
     cq_ref, sq_ref, ck_ref, sk_ref) = refs[:13]
    f_ref, q_ref, ckv_ref, kr_ref, nckv_ref, nkr_ref, w1_s, wq_s = refs[-8:]
    is_prompt = pl.program_id(0) < N_PROMPT // PROJ_TM

    @pl.when(pl.program_id(0) == 0)
    def _():
        w1_s[...] = w1_ref[...].T.astype(BF16)
        wq_s[...] = _dot_nt(pq_ref[...], wq_ref[...].astype(BF16)).astype(BF16)

    n_qp = N_HEADS * HEAD_PAD
    c0 = D_FOUR + Q_RANK
    groups = [slice(r * SEQ, (r + 1) * SEQ) for r in range(PROJ_SEQS)]
    ps = []
    for rows in groups:
        h = _norm_mod(x_ref[rows, :], nw_ref[layer:layer + 1, :], sc_ref[...], sh_ref[...]).astype(BF16)
        ps.append(_dot(h, w1_s[...]))
    staged = []
    for rows, p in zip(groups, ps):
        f_ref[rows, :] = p[:, :D_FOUR]
        ql = p[:, D_FOUR:c0]
        qn = ql * lax.rsqrt(jnp.mean(ql * ql, axis=-1, keepdims=True) + EPS) * qn_ref[layer:layer + 1, :]
        qn = qn.astype(BF16)
        qq = _dot_nt(wq_s[:n_qp, :], qn)
        ckv = p[:, c0:c0 + KV_RANK]
        ckv_n = ckv * lax.rsqrt(jnp.mean(ckv * ckv, axis=-1, keepdims=True) + EPS) * kvn_ref[layer:layer + 1, :]
        ckv_ref[rows, :] = ckv_n
        kr = p[:, ROPE_COL:ROPE_COL + HEAD_PAD]
        staged.append((qn, qq, ckv_n, kr))

    @pl.when(is_prompt)
    def _():
        for r, (rows, (qn, qq, ckv_n, kr)) in enumerate(zip(groups, staged)):
            cq = cq_ref[:, rows]
            for hd in range(N_HEADS):
                lo = hd * HEAD_PAD
                q_ref[lo:lo + HEAD_PAD, rows] = (qq[lo:lo + HEAD_PAD, :] * cq).astype(BF16)
            kr_ref[rows, :] = (kr * ck_ref[rows, :]).astype(BF16)
            if first_layer:
                nckv_ref[r, 0] = ckv_n
                nkr_ref[r, 0] = kr.T[:QK_ROPE]
                for later in range(1, DEPTH):
                    nckv_ref[r, later] = jnp.zeros_like(ckv_n)
                    nkr_ref[r, later] = jnp.zeros((QK_ROPE, SEQ), F32)
            else:
                nckv_ref[r] = ckv_n
                nkr_ref[r] = kr.T[:QK_ROPE]

    @pl.when(jnp.logical_not(is_prompt))
    def _():
        for rows, (qn, qq, ckv_n, kr) in zip(groups, staged):
            cq, sq = cq_ref[:, rows], sq_ref[:, rows]
            qs = _dot_nt(wq_s[n_qp:, :], qn)
            for hd in range(N_HEADS):
                lo, mid = hd * HEAD_PAD, hd * HEAD_PAD + QK_ROPE
                rope = qq[lo:mid, :] * cq[:QK_ROPE] + qs[hd * QK_ROPE:(hd + 1) * QK_ROPE, :] * sq[:QK_ROPE]
                q_ref[lo:mid, rows] = rope.astype(BF16)
                q_ref[mid:lo + HEAD_PAD, rows] = (qq[mid:lo + HEAD_PAD, :] * cq[QK_ROPE:]).astype(BF16)
            kr_ref[rows, :] = (kr * ck_ref[rows, :] + _pair_swap(kr) * sk_ref[rows, :]).astype(BF16)


def _q_select():
    n_qp = N_HEADS * HEAD_PAD
    p = np.zeros((n_qp + N_HEADS * QK_ROPE, N_HEADS * (QK_NOPE + QK_ROPE)), np.float32)
    for hd in range(N_HEADS):
        src, dst = hd * (QK_NOPE + QK_ROPE), hd * HEAD_PAD
        for j in range(QK_ROPE):
            partner = j + ROPE_PAIR if j % ROPE_AXIS < ROPE_PAIR else j - ROPE_PAIR
            p[dst + j, src + QK_NOPE + j] = 1.0
            p[n_qp + hd * QK_ROPE + j, src + QK_NOPE + partner] = 1.0
        for j in range(QK_NOPE):
            p[dst + QK_ROPE + j, src + j] = 1.0
    return p.astype(BF16)


def _proj(x, mod, layer, nw, w_in_t, qn, w_uq, kvn, tabs, prev_caches):
    tm = PROJ_TM
    n_prompt_tiles = N_PROMPT // tm
    per_batch = DEC_SEQ // tm
    n_q = N_HEADS * (QK_NOPE + QK_ROPE)

    def tab_blk(i):
        return jnp.where(i < n_prompt_tiles, per_batch, (i - n_prompt_tiles) % per_batch)

    tok = lambda i: (i, 0)
    const = lambda i: (0, 0)
    first_layer = not prev_caches
    assert first_layer == (layer == 0)
    cache_layers = DEPTH if first_layer else None
    cache_idx = lambda i: (jnp.minimum(i, n_prompt_tiles - 1), 0 if first_layer else layer, 0, 0)
    tab_spec = pl.BlockSpec((tm, HEAD_PAD), lambda i: (tab_blk(i), 0))
    tab_t_spec = pl.BlockSpec((HEAD_PAD, tm), lambda i: (0, tab_blk(i)))
    return pl.pallas_call(
        functools.partial(_proj_kernel, layer=layer, first_layer=first_layer),
        grid=(N_TOK // tm,),
        in_specs=[
            pl.BlockSpec((tm, D_MODEL), tok),
            _mod_spec(layer, 3, tm),
            _mod_spec(layer, 4, tm),
            pl.BlockSpec((DEPTH, D_MODEL), const),
            pl.BlockSpec((None, PROJ_W, D_MODEL), lambda i: (layer, 0, 0)),
            pl.BlockSpec((DEPTH, Q_RANK), const),
            pl.BlockSpec((None, Q_RANK, n_q), lambda i: (layer, 0, 0)),
            pl.BlockSpec((N_HEADS * (HEAD_PAD + QK_ROPE), n_q), const),
            pl.BlockSpec((DEPTH, KV_RANK), const),
            tab_t_spec, tab_t_spec, tab_spec, tab_spec,
        ] + [pl.BlockSpec(memory_space=pl.ANY)] * len(prev_caches),
        out_specs=[
            pl.BlockSpec((tm, D_FOUR), tok),
            pl.BlockSpec((N_HEADS * HEAD_PAD, tm), lambda i: (0, i)),
            pl.BlockSpec((tm, KV_RANK), tok),
            pl.BlockSpec((tm, HEAD_PAD), tok),
            pl.BlockSpec((PROJ_SEQS, cache_layers, SEQ, KV_RANK), cache_idx),
            pl.BlockSpec((PROJ_SEQS, cache_layers, QK_ROPE, SEQ), cache_idx),
        ],
        out_shape=[
            jax.ShapeDtypeStruct((N_TOK, D_FOUR), F32),
            jax.ShapeDtypeStruct((N_HEADS * HEAD_PAD, N_TOK), BF16),
            jax.ShapeDtypeStruct((N_TOK, KV_RANK), F32),
            jax.ShapeDtypeStruct((N_TOK, HEAD_PAD), BF16),
            jax.ShapeDtypeStruct((BATCH, DEPTH, SEQ, KV_RANK), F32),
            jax.ShapeDtypeStruct((BATCH, DEPTH, QK_ROPE, SEQ), F32),
        ],
        input_output_aliases={13 + k: 4 + k for k in range(len(prev_caches))},
        scratch_shapes=[
            pltpu.VMEM((PROJ_W, D_MODEL), BF16),
            pltpu.VMEM((N_HEADS * (HEAD_PAD + QK_ROPE), Q_RANK), BF16),
        ],
        compiler_params=pltpu.CompilerParams(
            dimension_semantics=("arbitrary",), vmem_limit_bytes=VMEM_LIMIT),
        name="mixer_proj",
    )(x, mod, mod, nw, w_in_t, qn, w_uq, _q_select(), kvn, *tabs(tm), *prev_caches)


DFT_ROWS = 1024


def _split(x):
    hi = x.astype(BF16)
    lo = (x - hi.astype(F32)).astype(BF16)
    return hi, lo


def _dot3(ah, al, bh, bl):
    return _dot(ah, bh) + _dot(al, bh) + _dot(ah, bl)


def _dft_kernel(x_ref, gh_ref, gl_ref, ch_ref, cl_ref, sh_ref, sl_ref, rev_ref, o_ref, *, n):
    half = n // 2
    xh, xl = _split(x_ref[...])
    y = _dot3(xh, xl, gh_ref[...], gl_ref[...])
    yh, yl = _split(y)
    ch, cl, sh, sl = ch_ref[...], cl_ref[...], sh_ref[...], sl_ref[...]
    pos = lax.broadcasted_iota(jnp.int32, (n, D_FOUR), 0)
    alt = jnp.where(pos % 2 == 0, 1.0, -1.0).astype(F32) * (float(n) ** -0.5)
    first = lax.broadcasted_iota(jnp.int32, (half, D_FOUR), 0) == 0
    for b in range(x_ref.shape[0] // n):
        r = slice(b * n, (b + 1) * n)
        a = _dot3(ch, cl, yh[r, :D_FOUR], yl[r, :D_FOUR])
        bb = _dot3(sh, sl, yh[r, D_FOUR:], yl[r, D_FOUR:])
        o_ref[b * n:b * n + half, :] = (a - bb).astype(BF16)
        mirrored = _dot(rev_ref[...], (a + bb).astype(BF16))
        nyquist = jnp.sum(y[r, :D_FOUR] * alt, axis=0, keepdims=True)
        o_ref[b * n + half:(b + 1) * n, :] = jnp.where(first, nyquist, mirrored).astype(BF16)


def _split_np(a):
    a32 = np.asarray(a, np.float32)
    hi = a32.astype(BF16)
    lo = (a32 - hi.astype(np.float32)).astype(BF16)
    return hi, lo


def _dft_tables(n):
    half = n // 2
    k = np.arange(half, dtype=np.int64)
    pos = np.arange(n, dtype=np.int64)
    ang = 2.0 * np.pi * ((k[:, None] * pos[None, :]) % n).astype(np.float64) / n
    c = np.arange(FOUR_GROUP_DIM, dtype=np.int64)
    ang_c = 2.0 * np.pi * ((c[:, None] * c[None, :]) % FOUR_GROUP_DIM).astype(np.float64) / FOUR_GROUP_DIM
    eye = np.eye(FOUR_GROUPS)
    g = np.concatenate([np.kron(eye, np.cos(ang_c)), np.kron(eye, np.sin(ang_c))], axis=1)
    g = g / np.sqrt(FOUR_GROUP_DIM)
    rev = np.zeros((half, half), np.float32)
    rev[np.arange(1, half), half - np.arange(1, half)] = 1.0
    return (_split_np(g) + _split_np(np.cos(ang) / np.sqrt(n)) + _split_np(np.sin(ang) / np.sqrt(n))
            + (rev.astype(BF16),))


def _dft(f_in, n_batch, n, row0):
    tables = _dft_tables(n)
    rows = max(n, DFT_ROWS)
    blk0 = row0 // rows
    const = lambda b: (0, 0)
    half_spec = pl.BlockSpec((n // 2, n), const)
    return pl.pallas_call(
        functools.partial(_dft_kernel, n=n),
        grid=(n_batch * n // rows,),
        in_specs=[
            pl.BlockSpec((rows, D_FOUR), lambda b: (blk0 + b, 0)),
            pl.BlockSpec((D_FOUR, 2 * D_FOUR), const),
            pl.BlockSpec((D_FOUR, 2 * D_FOUR), const),
            half_spec, half_spec, half_spec, half_spec,
            pl.BlockSpec((n // 2, n // 2), const),
        ],
        out_specs=pl.BlockSpec((rows, D_FOUR), lambda b: (b, 0)),
        out_shape=jax.ShapeDtypeStruct((n_batch * n, D_FOUR), BF16),
        compiler_params=pltpu.CompilerParams(
            dimension_semantics=("arbitrary",), vmem_limit_bytes=VMEM_LIMIT),
        name="fnet_dft_%d" % n,
    )(f_in, *tables)


ATTN_TQ = 256
ATTN_SEQS = 4
ATTN_AHEAD = 3
ATTN_VALUE_LAG = 1
VT_ROWS = V_DIM + 16


def _attn_kernel(*refs, n_own, n_ctx, seqs):
    if n_ctx:
        qt_ref, ckv_ref, kr_ref, cckv_ref, ckr_ref, wkv_ref, pk_ref, pvt_ref = refs[:8]
    else:
        qt_ref, ckv_ref, kr_ref, wkv_ref, pk_ref, pvt_ref = refs[:6]
    o_ref, k_scr, vt_scr, wk_s, wvt_s = refs[-5:]

    @pl.when(jnp.logical_and(pl.program_id(0) == 0, pl.program_id(1) == 0))
    def _():
        wkv = wkv_ref[...].astype(BF16)
        wk_s[...] = _dot(wkv, pk_ref[...]).astype(BF16)
        wvt_s[...] = _dot_nt(pvt_ref[...], wkv).astype(BF16)

    @pl.when(pl.program_id(1) == 0)
    def _():
        ckv = ckv_ref[...].astype(BF16)
        kr = kr_ref[...]
        if n_ctx:
            ckv = jnp.concatenate([ckv, cckv_ref[...].astype(BF16)], axis=0)
            ckr = jnp.concatenate([ckr_ref[...], jnp.zeros((HEAD_PAD - QK_ROPE, n_ctx), F32)], axis=0)
            kr = jnp.concatenate([kr, ckr.T.astype(BF16)], axis=0)
        k_nope = _dot(ckv, wk_s[...])
        kr = kr.astype(F32)
        for hd in range(N_HEADS):
            lo = hd * HEAD_PAD
            k_scr[:, lo:lo + HEAD_PAD] = (k_nope[:, lo:lo + HEAD_PAD] + kr).astype(BF16)
        vt = _dot_nt(wvt_s[...], ckv).astype(BF16)
        pad_rows = lax.broadcasted_iota(jnp.int32, (VT_ROWS - V_DIM, vt.shape[1]), 0)
        ones_row = (pad_rows == 0).astype(BF16)
        for hd in range(N_HEADS):
            vt_scr[hd * VT_ROWS:hd * VT_ROWS + V_DIM, :] = vt[hd * V_DIM:(hd + 1) * V_DIM]
            vt_scr[hd * VT_ROWS + V_DIM:(hd + 1) * VT_ROWS, :] = ones_row

    n_keys = n_own + n_ctx
    tq = qt_ref.shape[1] // seqs
    units = [(j, hd) for j in range(seqs) for hd in range(N_HEADS)]

    def scores(j, hd):
        lo = hd * HEAD_PAD
        return _dot(k_scr[j * n_keys:(j + 1) * n_keys, lo:lo + HEAD_PAD],
                    qt_ref[lo:lo + HEAD_PAD, j * tq:(j + 1) * tq])

    outs = []

    def values(j, hd, p):
        ov = _dot(vt_scr[hd * VT_ROWS:(hd + 1) * VT_ROWS, j * n_keys:(j + 1) * n_keys], p)
        outs.append(ov[:V_DIM] / ov[V_DIM:V_DIM + 1])
        if hd == N_HEADS - 1:
            o_ref[j * tq:(j + 1) * tq, :] = jnp.concatenate(outs, axis=0).T.astype(BF16)
            outs.clear()

    queue = [scores(*u) for u in units[:ATTN_AHEAD]]
    pending = []
    for i, unit in enumerate(units):
        s = queue.pop(0)
        if i + ATTN_AHEAD < len(units):
            queue.append(scores(*units[i + ATTN_AHEAD]))
        p = jnp.exp2(s - jnp.max(s, axis=0, keepdims=True)).astype(BF16)
        if len(pending) == ATTN_VALUE_LAG:
            values(*pending.pop(0))
        pending.append((*unit, p))
    for item in pending:
        values(*item)


def _kv_select():
    n_kv = N_HEADS * (QK_NOPE + V_DIM)
    pk = np.zeros((n_kv, N_HEADS * HEAD_PAD), np.float32)
    pvt = np.zeros((N_HEADS * V_DIM, n_kv), np.float32)
    for hd in range(N_HEADS):
        src = hd * (QK_NOPE + V_DIM)
        for j in range(QK_NOPE):
            pk[src + j, hd * HEAD_PAD + QK_ROPE + j] = 1.0
        for j in range(V_DIM):
            pvt[hd * V_DIM + j, src + QK_NOPE + j] = 1.0
    return pk.astype(BF16), pvt.astype(BF16)


def _attention(q, ckv_n, kr, w_ukv, layer, n_batch, n_own, row0, ctx=None):
    tq = min(n_own, ATTN_TQ)
    n_kv = N_HEADS * (QK_NOPE + V_DIM)
    n_ctx = 0 if ctx is None else PAST_LEN
    n_keys = n_own + n_ctx
    qt = n_own // tq
    seqs = ATTN_SEQS if (ctx is None and qt == 1) else 1
    blk0 = row0 // (seqs * n_own)
    const = lambda b, t: (0, 0)
    in_specs = [
        pl.BlockSpec((N_HEADS * HEAD_PAD, seqs * tq), lambda b, t: (0, (row0 // (seqs * tq)) + b * qt + t)),
        pl.BlockSpec((seqs * n_own, KV_RANK), lambda b, t: (blk0 + b, 0)),
        pl.BlockSpec((seqs * n_own, HEAD_PAD), lambda b, t: (blk0 + b, 0)),
    ]
    args = [q, ckv_n, kr]
    if ctx is not None:
        cckv, ckr = ctx
        in_specs += [
            pl.BlockSpec((None, None, PAST_LEN, KV_RANK), lambda b, t: (b, layer, 0, 0)),
            pl.BlockSpec((None, None, QK_ROPE, PAST_LEN), lambda b, t: (b, layer, 0, 0)),
        ]
        args += [cckv, ckr]
    in_specs += [
        pl.BlockSpec((None, KV_RANK, n_kv), lambda b, t: (layer, 0, 0)),
        pl.BlockSpec((n_kv, N_HEADS * HEAD_PAD), const),
        pl.BlockSpec((N_HEADS * V_DIM, n_kv), const),
    ]
    args += [w_ukv, *_kv_select()]
    return pl.pallas_call(
        functools.partial(_attn_kernel, n_own=n_own, n_ctx=n_ctx, seqs=seqs),
        grid=(n_batch // seqs, qt),
        in_specs=in_specs,
        out_specs=pl.BlockSpec((seqs * tq, N_HEADS * V_DIM), lambda b, t: (b * qt + t, 0)),
        out_shape=jax.ShapeDtypeStruct((n_batch * n_own, N_HEADS * V_DIM), BF16),
        scratch_shapes=[
            pltpu.VMEM((seqs * n_keys, N_HEADS * HEAD_PAD), BF16),
            pltpu.VMEM((N_HEADS * VT_ROWS, seqs * n_keys), BF16),
            pltpu.VMEM((KV_RANK, N_HEADS * HEAD_PAD), BF16),
            pltpu.VMEM((N_HEADS * V_DIM, KV_RANK), BF16),
        ],
        compiler_params=pltpu.CompilerParams(
            dimension_semantics=("arbitrary", "arbitrary"), vmem_limit_bytes=VMEM_LIMIT),
        name="mla_attention_%d" % n_keys,
    )(*args)


TAIL_TM = 1024
TAIL_GROUP = 256
TAIL_SLABS = 4
GATE_ROW = ROPE_COL + QK_ROPE
GATE_CHUNK = 512
GATE_CHUNKS = 2 * D_MODEL // GATE_CHUNK


def _tail_kernel(x_ref, sh_ref, sc_ref, g_ref, nw_ref, frp_ref, frs_ref, op_ref, os_ref, wi_hbm,
                 wf_ref, wa_ref, wo_ref, out_ref, wg_s, wf_s, wa_s, wo_s, stage, sem, *, layer):
    s = pl.program_id(0)
    is_prompt = jnp.maximum(s - TAIL_SLABS, 0) < N_PROMPT // TAIL_TM

    def gate_copy(c):
        src = wi_hbm.at[layer, pl.ds(GATE_ROW + c * GATE_CHUNK, GATE_CHUNK), :]
        return pltpu.make_async_copy(src, stage.at[c % 2], sem.at[c % 2])

    @pl.when(s == 0)
    def _():
        gate_copy(0).start()
        for c in range(GATE_CHUNKS):
            if c + 1 < GATE_CHUNKS:
                gate_copy(c + 1).start()
            gate_copy(c).wait()
            wg_s[:, c * GATE_CHUNK:(c + 1) * GATE_CHUNK] = stage[c % 2].T.astype(BF16)

    @pl.when(s < TAIL_SLABS)
    def _():
        def put(dst, src):
            rows = src.shape[0]
            dst[pl.ds(pl.multiple_of(s * rows, rows), rows), :] = src.astype(BF16)

        put(wf_s, wf_ref[...])
        put(wa_s, wa_ref[...])
        put(wo_s, wo_ref[...])

    @pl.when(s >= TAIL_SLABS)
    def _():
        groups = [slice(r * TAIL_GROUP, (r + 1) * TAIL_GROUP) for r in range(TAIL_TM // TAIL_GROUP)]

        def branches(rows):
            x = x_ref[rows, :]
            h = _norm_mod(x, nw_ref[layer:layer + 1, :], sc_ref[...], sh_ref[...]).astype(BF16)
            gates = _dot(h, wg_s[...])
            a_out = _dot(jnp.where(is_prompt, frp_ref[rows, :], frs_ref[rows, :]), wf_s[...])
            o_out = _dot(jnp.where(is_prompt, op_ref[rows, :], os_ref[rows, :]), wa_s[...])
            return x, gates, a_out, o_out

        def merge(rows, x, gates, a_out, o_out):
            merged = jax.nn.sigmoid(gates[:, :D_MODEL]) * a_out + jax.nn.sigmoid(gates[:, D_MODEL:]) * o_out
            m = _dot(merged.astype(BF16), wo_s[...])
            out_ref[rows, :] = x + g_ref[...] * m

        pending = branches(groups[0])
        for r in range(1, len(groups)):
            ready, pending = pending, branches(groups[r])
            merge(groups[r - 1], *ready)
        merge(groups[-1], *pending)


def _tail(x, mod, layer, nw, frs, os_, w_in_t, w_four, w_attn_proj, w_out):
    tm = TAIL_TM
    npt = N_PROMPT // tm
    tile = lambda s: jnp.maximum(s - TAIL_SLABS, 0)
    slab = lambda s: jnp.minimum(s, TAIL_SLABS - 1)
    tok = lambda s: (tile(s), 0)
    tok_p = lambda s: (jnp.minimum(tile(s), npt - 1), 0)
    tok_s = lambda s: (jnp.maximum(tile(s) - npt, 0), 0)
    n_attn = N_HEADS * V_DIM

    def slab_spec(rows, cols):
        return pl.BlockSpec((None, rows // TAIL_SLABS, cols), lambda s: (layer, slab(s), 0))

    return pl.pallas_call(
        functools.partial(_tail_kernel, layer=layer),
        grid=(TAIL_SLABS + N_TOK // tm,),
        in_specs=[
            pl.BlockSpec((tm, D_MODEL), tok),
            _mod_spec(layer, 3, tm, tile),
            _mod_spec(layer, 4, tm, tile),
            _mod_spec(layer, 5, tm, tile),
            pl.BlockSpec((DEPTH, D_MODEL), lambda s: (0, 0)),
            pl.BlockSpec((tm, D_FOUR), tok_p),
            pl.BlockSpec((tm, D_FOUR), tok_s),
            pl.BlockSpec((tm, n_attn), tok_p),
            pl.BlockSpec((tm, n_attn), tok_s),
            pl.BlockSpec(memory_space=pl.ANY),
            slab_spec(D_FOUR, D_MODEL),
            slab_spec(n_attn, D_MODEL),
            slab_spec(D_MODEL, D_MODEL),
        ],
        out_specs=pl.BlockSpec((tm, D_MODEL), tok),
        out_shape=jax.ShapeDtypeStruct((N_TOK, D_MODEL), F32),
        scratch_shapes=[
            pltpu.VMEM((D_MODEL, 2 * D_MODEL), BF16),
            pltpu.VMEM((D_FOUR, D_MODEL), BF16),
            pltpu.VMEM((n_attn, D_MODEL), BF16),
            pltpu.VMEM((D_MODEL, D_MODEL), BF16),
            pltpu.VMEM((2, GATE_CHUNK, D_MODEL), F32),
            pltpu.SemaphoreType.DMA((2,)),
        ],
        compiler_params=pltpu.CompilerParams(
            dimension_semantics=("arbitrary",), vmem_limit_bytes=VMEM_LIMIT),
        name="mixer_tail",
    )(x, mod, mod, mod, nw, *frs, *os_, w_in_t, w_four, w_attn_proj, w_out)


def _rope_tables(tm):
    rows = DEC_SEQ // GRID_W
    row = np.repeat(np.arange(rows), GRID_W).astype(np.float64)
    col = np.tile(np.arange(GRID_W), rows).astype(np.float64)
    axis_dim = QK_ROPE // 2
    inv = ROPE_BASE ** (-np.arange(0, axis_dim, 2, dtype=np.float64) / axis_dim)
    ar = row[:, None] * inv
    ac = col[:, None] * inv
    cr, sr, cc, sc = np.cos(ar), np.sin(ar), np.cos(ac), np.sin(ac)
    cos32 = np.concatenate([cr, cr, cc, cc], axis=1)
    sin32 = np.concatenate([-sr, sr, -sc, sc], axis=1)
    scale = np.float32((QK_NOPE + QK_ROPE) ** -0.5 * np.log2(np.e))

    def table(rope_part, nope_val, ident_rope):
        t = np.zeros((DEC_SEQ + tm, HEAD_PAD), np.float32)
        t[:DEC_SEQ, :QK_ROPE] = rope_part
        t[:DEC_SEQ, QK_ROPE:QK_ROPE + QK_NOPE] = nope_val
        t[DEC_SEQ:, :QK_ROPE] = ident_rope
        t[DEC_SEQ:, QK_ROPE:QK_ROPE + QK_NOPE] = nope_val
        return t

    cq = (table(cos32, 1.0, 1.0) * scale).T
    sq = (table(sin32, 0.0, 0.0) * scale).T
    ck = table(cos32, 0.0, 1.0)
    sk = table(sin32, 0.0, 0.0)
    return tuple(jnp.asarray(np.ascontiguousarray(t), F32) for t in (cq, sq, ck, sk))


def kernel(x_prompt, x_sample, cache_ckv, cache_krope, c, c_ctx, w_mod, b_mod, norm_ffn1, w_ffn1_gate,
           w_ffn1_up, w_ffn1_down, norm_mix, w_in, w_four, q_norm, w_uq, kv_norm, w_ukv, w_attn_proj,
           w_out, norm_ffn2, w_ffn2_gate, w_ffn2_up, w_ffn2_down, final_norm):
    xs = (x_prompt.reshape(N_PROMPT, D_MODEL), x_sample.reshape(N_SAMPLE, D_MODEL))
    c_all = jnp.concatenate(
        [c_ctx[None, :], c, jnp.zeros((MOD_ROWS - 1 - DEC_BATCH, D_MODEL), F32)], axis=0)
    mod = _modulation(c_all, w_mod, b_mod).reshape(DEPTH * MOD_ROWS * N_MOD, 1, D_MODEL)
    cache_kr = jnp.swapaxes(cache_krope, 2, 3)
    w_in_t = jnp.swapaxes(w_in, 1, 2)

    caches = ()
    for l in range(DEPTH):
        x = _ffn(xs, mod, l, 0, norm_ffn1, w_ffn1_gate, w_ffn1_up, w_ffn1_down, final_norm, False)
        f_in, q, ckv_n, kr, new_ckv, new_krope = _proj(
            x, mod, l, norm_mix, w_in_t, q_norm, w_uq, kv_norm, _rope_tables, caches)
        caches = (new_ckv, new_krope)
        frs = (_dft(f_in, BATCH, SEQ, 0), _dft(f_in, DEC_BATCH, DEC_SEQ, N_PROMPT))
        os_ = (_attention(q, ckv_n, kr, w_ukv, l, BATCH, SEQ, 0),
               _attention(q, ckv_n, kr, w_ukv, l, DEC_BATCH, DEC_SEQ, N_PROMPT, (cache_ckv, cache_kr)))
        x = _tail(x, mod, l, norm_mix, frs, os_, w_in_t, w_four, w_attn_proj, w_out)
        xs = _ffn((x,), mod, l, 6, norm_ffn2, w_ffn2_gate, w_ffn2_up, w_ffn2_down, final_norm, l == DEPTH - 1)
        if l < DEPTH - 1:
            xs = (xs,)

    y_prompt = xs[0].reshape(BATCH, SEQ, D_MODEL)
    y_sample = xs[1].reshape(DEC_BATCH, DEC_SEQ, D_MODEL)
    return y_prompt, y_sample, caches[0], jnp.swapaxes(caches[1], 2, 3)
```

```python
import functools

import numpy as np
import jax
import jax.numpy as jnp
from jax import lax
from jax.experimental import pallas as pl
from jax.experimental.pallas import tpu as pltpu

D_MODEL = 1024
BATCH = 16
SEQ = 256
DEPTH = 2
DEC_BATCH = 4
DEC_SEQ = 1024
PAST_LEN = 512
GRID_W = 64
D_FF = 2816
FOUR_GROUPS = 4
FOUR_GROUP_DIM = 64
D_FOUR = FOUR_GROUPS * FOUR_GROUP_DIM
N_HEADS = 8
QK_NOPE = 64
QK_ROPE = 32
V_DIM = 64
Q_RANK = 384
KV_RANK = 256
N_MOD = 9
ROPE_BASE = 10000.0
EPS = 1e-6

N_PROMPT = BATCH * SEQ
N_SAMPLE = DEC_BATCH * DEC_SEQ
N_TOK = N_PROMPT + N_SAMPLE
MOD_ROWS = 8
HEAD_PAD = 128
ROPE_AXIS = QK_ROPE // 2
ROPE_PAIR = ROPE_AXIS // 2

VMEM_LIMIT = 52 * 1024 * 1024

F32 = jnp.float32
BF16 = jnp.bfloat16


def _dot(a, b):
    return jnp.dot(a, b, preferred_element_type=F32)


def _dot_nt(a, b):
    return lax.dot_general(a, b, (((1,), (1,)), ((), ())), preferred_element_type=F32)


def _mod_row(i, tm):
    n_prompt_tiles = N_PROMPT // tm
    per_batch = DEC_SEQ // tm
    return jnp.where(i < n_prompt_tiles, 0, 1 + (i - n_prompt_tiles) // per_batch)


def _mod_spec(layer, which, tm, tile_of_step=lambda i: i):
    def idx(*g):
        return ((layer * MOD_ROWS + _mod_row(tile_of_step(g[0]), tm)) * N_MOD + which, 0, 0)
    return pl.BlockSpec((None, 1, D_MODEL), idx)


def _norm_mod(x, nw, sc, sh):
    ms = jnp.mean(x * x, axis=-1, keepdims=True)
    y = x * lax.rsqrt(ms + EPS) * nw
    return y * (1.0 + sc) + sh


MOD_TN = 2304


def _mod_kernel(c_ref, w_ref, b_ref, o_ref):
    c = c_ref[...]
    a = (c * jax.nn.sigmoid(c)).astype(BF16)
    o_ref[...] = _dot(a, w_ref[...].astype(BF16)) + b_ref[pl.ds(pl.program_id(0), 1), :]


def _modulation(c_all, w_mod, b_mod):
    tn = MOD_TN
    n_out = N_MOD * D_MODEL
    return pl.pallas_call(
        _mod_kernel,
        grid=(DEPTH, n_out // tn),
        in_specs=[
            pl.BlockSpec((MOD_ROWS, D_MODEL), lambda l, j: (0, 0)),
            pl.BlockSpec((None, D_MODEL, tn), lambda l, j: (l, 0, j)),
            pl.BlockSpec((DEPTH, tn), lambda l, j: (0, j)),
        ],
        out_specs=pl.BlockSpec((None, MOD_ROWS, tn), lambda l, j: (l, 0, j)),
        out_shape=jax.ShapeDtypeStruct((DEPTH, MOD_ROWS, n_out), F32),
        compiler_params=pltpu.CompilerParams(
            dimension_semantics=("arbitrary", "arbitrary"), vmem_limit_bytes=VMEM_LIMIT),
        name="modulation",
    )(c_all, w_mod, b_mod)


FFN_TM = 512
FFN_CHUNK = 256
FFN_NC = D_FF // FFN_CHUNK


def _ffn_kernel(*refs, layer, final, split_in):
    refs = list(refs)
    x_refs = [refs.pop(0) for _ in range(2 if split_in else 1)]
    sh_ref, sc_ref, g_ref, nw_ref, wg_ref, wu_ref, wd_ref, fn_ref = refs[:8]
    o_refs = refs[8:10] if final else refs[8:9]
    wg_s, wu_s, wd_s, h_scr, acc_scr = refs[-5:]
    s = pl.program_id(0)
    is_prompt = jnp.maximum(s - (FFN_NC - 1), 0) < N_PROMPT // FFN_TM

    def x_tile():
        if split_in:
            return jnp.where(is_prompt, x_refs[0][...], x_refs[1][...])
        return x_refs[0][...]

    def chunk_act(h, wg, wu):
        gate = _dot(h, wg)
        up = _dot(h, wu)
        return ((gate * jax.nn.sigmoid(gate)) * up).astype(BF16)

    def finish(acc):
        xn = x_tile() + 0.5 * g_ref[...] * acc
        if not final:
            o_refs[0][...] = xn
            return
        ms = jnp.mean(xn * xn, axis=-1, keepdims=True)
        y = xn * lax.rsqrt(ms + EPS) * fn_ref[...]

        @pl.when(is_prompt)
        def _():
            o_refs[0][...] = y

        @pl.when(jnp.logical_not(is_prompt))
        def _():
            o_refs[1][...] = y

    def hidden():
        return _norm_mod(x_tile(), nw_ref[layer:layer + 1, :], sc_ref[...], sh_ref[...]).astype(BF16)

    @pl.when(s == 0)
    def _():
        h_scr[...] = hidden()
        acc_scr[...] = jnp.zeros_like(acc_scr)

    @pl.when(s < FFN_NC)
    def _():
        wg = wg_ref[...].astype(BF16)
        wu = wu_ref[...].astype(BF16)
        wd = wd_ref[...].astype(BF16)
        wg_s[s] = wg
        wu_s[s] = wu
        wd_s[pl.ds(pl.multiple_of(s * FFN_CHUNK, FFN_CHUNK), FFN_CHUNK), :] = wd
        acc_scr[...] += _dot(chunk_act(h_scr[...], wg, wu), wd)

    @pl.when(s == FFN_NC - 1)
    def _():
        finish(acc_scr[...])

    @pl.when(s >= FFN_NC)
    def _():
        h = hidden()
        act = jnp.concatenate([chunk_act(h, wg_s[j], wu_s[j]) for j in range(FFN_NC)], axis=1)
        finish(_dot(act, wd_s[...]))


def _ffn(xs, mod, layer, mod_base, nw, wg, wu, wd, final_norm, final):
    tm = FFN_TM
    npt = N_PROMPT // tm
    split_in = len(xs) == 2
    tile = lambda s: jnp.maximum(s - (FFN_NC - 1), 0)
    chunk = lambda s: jnp.minimum(s, FFN_NC - 1)
    tok = pl.BlockSpec((tm, D_MODEL), lambda s: (tile(s), 0))
    tok_p = pl.BlockSpec((tm, D_MODEL), lambda s: (jnp.minimum(tile(s), npt - 1), 0))
    tok_s = pl.BlockSpec((tm, D_MODEL), lambda s: (jnp.maximum(tile(s) - npt, 0), 0))
    half = jax.ShapeDtypeStruct((N_PROMPT, D_MODEL), F32)
    return pl.pallas_call(
        functools.partial(_ffn_kernel, layer=layer, final=final, split_in=split_in),
        grid=(FFN_NC + N_TOK // tm - 1,),
        in_specs=([tok_p, tok_s] if split_in else [tok]) + [
            _mod_spec(layer, mod_base + 0, tm, tile),
            _mod_spec(layer, mod_base + 1, tm, tile),
            _mod_spec(layer, mod_base + 2, tm, tile),
            pl.BlockSpec((DEPTH, D_MODEL), lambda s: (0, 0)),
            pl.BlockSpec((None, D_MODEL, FFN_CHUNK), lambda s: (layer, 0, chunk(s))),
            pl.BlockSpec((None, D_MODEL, FFN_CHUNK), lambda s: (layer, 0, chunk(s))),
            pl.BlockSpec((None, FFN_CHUNK, D_MODEL), lambda s: (layer, chunk(s), 0)),
            pl.BlockSpec((1, D_MODEL), lambda s: (0, 0)),
        ],
        out_specs=[tok_p, tok_s] if final else tok,
        out_shape=[half, half] if final else jax.ShapeDtypeStruct((N_TOK, D_MODEL), F32),
        scratch_shapes=[
            pltpu.VMEM((FFN_NC, D_MODEL, FFN_CHUNK), BF16),
            pltpu.VMEM((FFN_NC, D_MODEL, FFN_CHUNK), BF16),
            pltpu.VMEM((D_FF, D_MODEL), BF16),
            pltpu.VMEM((tm, D_MODEL), BF16),
            pltpu.VMEM((tm, D_MODEL), F32),
        ],
        compiler_params=pltpu.CompilerParams(
            dimension_semantics=("arbitrary",), vmem_limit_bytes=VMEM_LIMIT),
        name="ffn_final" if final else ("ffn_first" if split_in else "ffn"),
    )(*xs, mod, mod, mod, nw, wg, wu, wd, final_norm.reshape(1, D_MODEL))


PROJ_TM = 1024
PROJ_SEQS = PROJ_TM // SEQ
PROJ_W = 1024
ROPE_COL = D_FOUR + Q_RANK + KV_RANK


def _pair_swap(x):
    lane = lax.broadcasted_iota(jnp.int32, x.shape, 1)
    first = lane % ROPE_AXIS < ROPE_PAIR
    return jnp.where(first, pltpu.roll(x, HEAD_PAD - ROPE_PAIR, 1), pltpu.roll(x, ROPE_PAIR, 1))


def _proj_kernel(*refs, layer, first_layer):
    (x_ref, sh_ref, sc_ref, nw_ref, w1_ref, qn_ref, wq_ref, pq_ref, kvn_ref,
     cq_ref, sq_ref, ck_ref, sk_ref) = refs[:13]
    f_ref, q_ref, ckv_ref, kr_ref, nckv_ref, nkr_ref, w1_s, wq_s = refs[-8:]
    is_prompt = pl.program_id(0) < N_PROMPT // PROJ_TM

    @pl.when(pl.program_id(0) == 0)
    def _():
        w1_s[...] = w1_ref[...].T.astype(BF16)
        wq_s[...] = _dot_nt(pq_ref[...], wq_ref[...].astype(BF16)).astype(BF16)

    n_qp = N_HEADS * HEAD_PAD
    c0 = D_FOUR + Q_RANK
    groups = [slice(r * SEQ, (r + 1) * SEQ) for r in range(PROJ_SEQS)]
    ps = []
    for rows in groups:
        h = _norm_mod(x_ref[rows, :], nw_ref[layer:layer + 1, :], sc_ref[...], sh_ref[...]).astype(BF16)
        ps.append(_dot(h, w1_s[...]))
    staged = []
    for rows, p in zip(groups, ps):
        f_ref[rows, :] = p[:, :D_FOUR]
        ql = p[:, D_FOUR:c0]
        qn = ql * lax.rsqrt(jnp.mean(ql * ql, axis=-1, keepdims=True) + EPS) * qn_ref[layer:layer + 1, :]
        qn = qn.astype(BF16)
        qq = _dot_nt(wq_s[:n_qp, :], qn)
        ckv = p[:, c0:c0 + KV_RANK]
        ckv_n = ckv * lax.rsqrt(jnp.mean(ckv * ckv, axis=-1, keepdims=True) + EPS) * kvn_ref[layer:layer + 1, :]
        ckv_ref[rows, :] = ckv_n
        kr = p[:, ROPE_COL:ROPE_COL + HEAD_PAD]
        kr = jnp.where(lax.broadcasted_iota(jnp.int32, kr.shape, 1) < QK_ROPE, kr, 0.0)
        staged.append((qn, qq, ckv_n, kr))

    @pl.when(is_prompt)
    def _():
        for r, (rows, (qn, qq, ckv_n, kr)) in enumerate(zip(groups, staged)):
            cq = cq_ref[:, rows]
            for hd in range(N_HEADS):
                lo = hd * HEAD_PAD
                q_ref[lo:lo + HEAD_PAD, rows] = (qq[lo:lo + HEAD_PAD, :] * cq).astype(BF16)
            kr_ref[rows, :] = (kr * ck_ref[rows, :]).astype(BF16)
            if first_layer:
                nckv_ref[r, 0] = ckv_n
                nkr_ref[r, 0] = kr.T[:QK_ROPE]
                for later in range(1, DEPTH):
                    nckv_ref[r, later] = jnp.zeros_like(ckv_n)
                    nkr_ref[r, later] = jnp.zeros((QK_ROPE, SEQ), F32)
            else:
                nckv_ref[r] = ckv_n
                nkr_ref[r] = kr.T[:QK_ROPE]

    @pl.when(jnp.logical_not(is_prompt))
    def _():
        for rows, (qn, qq, ckv_n, kr) in zip(groups, staged):
            cq, sq = cq_ref[:, rows], sq_ref[:, rows]
            qs = _dot_nt(wq_s[n_qp:, :], qn)
            for hd in range(N_HEADS):
                lo, mid = hd * HEAD_PAD, hd * HEAD_PAD + QK_ROPE
                rope = qq[lo:mid, :] * cq[:QK_ROPE] + qs[hd * QK_ROPE:(hd + 1) * QK_ROPE, :] * sq[:QK_ROPE]
                q_ref[lo:mid, rows] = rope.astype(BF16)
                q_ref[mid:lo + HEAD_PAD, rows] = (qq[mid:lo + HEAD_PAD, :] * cq[QK_ROPE:]).astype(BF16)
            kr_ref[rows, :] = (kr * ck_ref[rows, :] + _pair_swap(kr) * sk_ref[rows, :]).astype(BF16)


def _q_select():
    n_qp = N_HEADS * HEAD_PAD
    p = np.zeros((n_qp + N_HEADS * QK_ROPE, N_HEADS * (QK_NOPE + QK_ROPE)), np.float32)
    for hd in range(N_HEADS):
        src, dst = hd * (QK_NOPE + QK_ROPE), hd * HEAD_PAD
        for j in range(QK_ROPE):
            partner = j + ROPE_PAIR if j % ROPE_AXIS < ROPE_PAIR else j - ROPE_PAIR
            p[dst + j, src + QK_NOPE + j] = 1.0
            p[n_qp + hd * QK_ROPE + j, src + QK_NOPE + partner] = 1.0
        for j in range(QK_NOPE):
            p[dst + QK_ROPE + j, src + j] = 1.0
    return p.astype(BF16)


def _proj(x, mod, layer, nw, w_in_t, qn, w_uq, kvn, tabs, prev_caches):
    tm = PROJ_TM
    n_prompt_tiles = N_PROMPT // tm
    per_batch = DEC_SEQ // tm
    n_q = N_HEADS * (QK_NOPE + QK_ROPE)

    def tab_blk(i):
        return jnp.where(i < n_prompt_tiles, per_batch, (i - n_prompt_tiles) % per_batch)

    tok = lambda i: (i, 0)
    const = lambda i: (0, 0)
    first_layer = not prev_caches
    assert first_layer == (layer == 0)
    cache_layers = DEPTH if first_layer else None
    cache_idx = lambda i: (jnp.minimum(i, n_prompt_tiles - 1), 0 if first_layer else layer, 0, 0)
    tab_spec = pl.BlockSpec((tm, HEAD_PAD), lambda i: (tab_blk(i), 0))
    tab_t_spec = pl.BlockSpec((HEAD_PAD, tm), lambda i: (0, tab_blk(i)))
    return pl.pallas_call(
        functools.partial(_proj_kernel, layer=layer, first_layer=first_layer),
        grid=(N_TOK // tm,),
        in_specs=[
            pl.BlockSpec((tm, D_MODEL), tok),
            _mod_spec(layer, 3, tm),
            _mod_spec(layer, 4, tm),
            pl.BlockSpec((DEPTH, D_MODEL), const),
            pl.BlockSpec((None, PROJ_W, D_MODEL), lambda i: (layer, 0, 0)),
            pl.BlockSpec((DEPTH, Q_RANK), const),
            pl.BlockSpec((None, Q_RANK, n_q), lambda i: (layer, 0, 0)),
            pl.BlockSpec((N_HEADS * (HEAD_PAD + QK_ROPE), n_q), const),
            pl.BlockSpec((DEPTH, KV_RANK), const),
            tab_t_spec, tab_t_spec, tab_spec, tab_spec,
        ] + [pl.BlockSpec(memory_space=pl.ANY)] * len(prev_caches),
        out_specs=[
            pl.BlockSpec((tm, D_FOUR), tok),
            pl.BlockSpec((N_HEADS * HEAD_PAD, tm), lambda i: (0, i)),
            pl.BlockSpec((tm, KV_RANK), tok),
            pl.BlockSpec((tm, HEAD_PAD), tok),
            pl.BlockSpec((PROJ_SEQS, cache_layers, SEQ, KV_RANK), cache_idx),
            pl.BlockSpec((PROJ_SEQS, cache_layers, QK_ROPE, SEQ), cache_idx),
        ],
        out_shape=[
            jax.ShapeDtypeStruct((N_TOK, D_FOUR), F32),
            jax.ShapeDtypeStruct((N_HEADS * HEAD_PAD, N_TOK), BF16),
            jax.ShapeDtypeStruct((N_TOK, KV_RANK), F32),
            jax.ShapeDtypeStruct((N_TOK, HEAD_PAD), BF16),
            jax.ShapeDtypeStruct((BATCH, DEPTH, SEQ, KV_RANK), F32),
            jax.ShapeDtypeStruct((BATCH, DEPTH, QK_ROPE, SEQ), F32),
        ],
        input_output_aliases={13 + k: 4 + k for k in range(len(prev_caches))},
        scratch_shapes=[
            pltpu.VMEM((PROJ_W, D_MODEL), BF16),
            pltpu.VMEM((N_HEADS * (HEAD_PAD + QK_ROPE), Q_RANK), BF16),
        ],
        compiler_params=pltpu.CompilerParams(
            dimension_semantics=("arbitrary",), vmem_limit_bytes=VMEM_LIMIT),
        name="mixer_proj",
    )(x, mod, mod, nw, w_in_t, qn, w_uq, _q_select(), kvn, *tabs(tm), *prev_caches)


DFT_ROWS = 1024


def _split(x):
    hi = x.astype(BF16)
    lo = (x - hi.astype(F32)).astype(BF16)
    return hi, lo


def _dot3(ah, al, bh, bl):
    return _dot(ah, bh) + _dot(al, bh) + _dot(ah, bl)


def _dft_kernel(x_ref, gh_ref, gl_ref, ch_ref, cl_ref, sh_ref, sl_ref, rev_ref, o_ref, *, n):
    half = n // 2
    xh, xl = _split(x_ref[...])
    y = _dot3(xh, xl, gh_ref[...], gl_ref[...])
    yh, yl = _split(y)
    ch, cl, sh, sl = ch_ref[...], cl_ref[...], sh_ref[...], sl_ref[...]
    pos = lax.broadcasted_iota(jnp.int32, (n, D_FOUR), 0)
    alt = jnp.where(pos % 2 == 0, 1.0, -1.0).astype(F32) * (float(n) ** -0.5)
    first = lax.broadcasted_iota(jnp.int32, (half, D_FOUR), 0) == 0
    seqs = [slice(b * n, (b + 1) * n) for b in range(x_ref.shape[0] // n)]
    parts = [(_dot3(ch, cl, yh[r, :D_FOUR], yl[r, :D_FOUR]),
              _dot3(sh, sl, yh[r, D_FOUR:], yl[r, D_FOUR:]))
             for r in seqs]
    for r, (a, bb) in zip(seqs, parts):
        o_ref[r.start:r.start + half, :] = (a - bb).astype(BF16)
        mirrored = _dot(rev_ref[...], (a + bb).astype(BF16))
        nyquist = jnp.sum(y[r, :D_FOUR] * alt, axis=0, keepdims=True)
        o_ref[r.start + half:r.stop, :] = jnp.where(first, nyquist, mirrored).astype(BF16)


def _split_np(a):
    a32 = np.asarray(a, np.float32)
    hi = a32.astype(BF16)
    lo = (a32 - hi.astype(np.float32)).astype(BF16)
    return hi, lo


def _dft_tables(n):
    half = n // 2
    k = np.arange(half, dtype=np.int64)
    pos = np.arange(n, dtype=np.int64)
    ang = 2.0 * np.pi * ((k[:, None] * pos[None, :]) % n).astype(np.float64) / n
    c = np.arange(FOUR_GROUP_DIM, dtype=np.int64)
    ang_c = 2.0 * np.pi * ((c[:, None] * c[None, :]) % FOUR_GROUP_DIM).astype(np.float64) / FOUR_GROUP_DIM
    eye = np.eye(FOUR_GROUPS)
    g = np.concatenate([np.kron(eye, np.cos(ang_c)), np.kron(eye, np.sin(ang_c))], axis=1)
    g = g / np.sqrt(FOUR_GROUP_DIM)
    rev = np.zeros((half, half), np.float32)
    rev[np.arange(1, half), half - np.arange(1, half)] = 1.0
    return (_split_np(g) + _split_np(np.cos(ang) / np.sqrt(n)) + _split_np(np.sin(ang) / np.sqrt(n))
            + (rev.astype(BF16),))


def _dft(f_in, n_batch, n, row0):
    tables = _dft_tables(n)
    rows = max(n, DFT_ROWS)
    blk0 = row0 // rows
    const = lambda b: (0, 0)
    half_spec = pl.BlockSpec((n // 2, n), const)
    return pl.pallas_call(
        functools.partial(_dft_kernel, n=n),
        grid=(n_batch * n // rows,),
        in_specs=[
            pl.BlockSpec((rows, D_FOUR), lambda b: (blk0 + b, 0)),
            pl.BlockSpec((D_FOUR, 2 * D_FOUR), const),
            pl.BlockSpec((D_FOUR, 2 * D_FOUR), const),
            half_spec, half_spec, half_spec, half_spec,
            pl.BlockSpec((n // 2, n // 2), const),
        ],
        out_specs=pl.BlockSpec((rows, D_FOUR), lambda b: (b, 0)),
        out_shape=jax.ShapeDtypeStruct((n_batch * n, D_FOUR), BF16),
        compiler_params=pltpu.CompilerParams(
            dimension_semantics=("arbitrary",), vmem_limit_bytes=VMEM_LIMIT),
        name="fnet_dft_%d" % n,
    )(f_in, *tables)


ATTN_TQ = 256
ATTN_SEQS = 4
ATTN_AHEAD = 3
ATTN_VALUE_LAG = 1
VT_ROWS = V_DIM + 16


def _attn_kernel(*refs, n_own, n_ctx, seqs):
    if n_ctx:
        qt_ref, ckv_ref, kr_ref, cckv_ref, ckr_ref, wkv_ref, pk_ref, pvt_ref = refs[:8]
    else:
        qt_ref, ckv_ref, kr_ref, wkv_ref, pk_ref, pvt_ref = refs[:6]
    o_ref, k_scr, vt_scr, wk_s, wvt_s = refs[-5:]

    @pl.when(jnp.logical_and(pl.program_id(0) == 0, pl.program_id(1) == 0))
    def _():
        wkv = wkv_ref[...].astype(BF16)
        wk_s[...] = _dot(wkv, pk_ref[...]).astype(BF16)
        wvt_s[...] = _dot_nt(pvt_ref[...], wkv).astype(BF16)

    def build_keys_values():
        ckv = ckv_ref[...].astype(BF16)
        kr = kr_ref[...]
        if n_ctx:
            ckv = jnp.concatenate([ckv, cckv_ref[...].astype(BF16)], axis=0)
            ckr = jnp.concatenate([ckr_ref[...], jnp.zeros((HEAD_PAD - QK_ROPE, n_ctx), F32)], axis=0)
            kr = jnp.concatenate([kr, ckr.T.astype(BF16)], axis=0)
        k_nope = _dot(ckv, wk_s[...])
        kr = kr.astype(F32)
        for hd in range(N_HEADS):
            lo = hd * HEAD_PAD
            k_scr[:, lo:lo + HEAD_PAD] = (k_nope[:, lo:lo + HEAD_PAD] + kr).astype(BF16)
        vt = _dot_nt(wvt_s[...], ckv).astype(BF16)
        pad_rows = lax.broadcasted_iota(jnp.int32, (VT_ROWS - V_DIM, vt.shape[1]), 0)
        ones_row = (pad_rows == 0).astype(BF16)
        for hd in range(N_HEADS):
            vt_scr[hd * VT_ROWS:hd * VT_ROWS + V_DIM, :] = vt[hd * V_DIM:(hd + 1) * V_DIM]
            vt_scr[hd * VT_ROWS + V_DIM:(hd + 1) * VT_ROWS, :] = ones_row

    if seqs > 1:
        build_keys_values()
    else:
        pl.when(pl.program_id(1) == 0)(build_keys_values)

    n_keys = n_own + n_ctx
    tq = qt_ref.shape[1] // seqs
    units = [(j, hd) for j in range(seqs) for hd in range(N_HEADS)]

    def scores(j, hd):
        lo = hd * HEAD_PAD
        return _dot(k_scr[j * n_keys:(j + 1) * n_keys, lo:lo + HEAD_PAD],
                    qt_ref[lo:lo + HEAD_PAD, j * tq:(j + 1) * tq])

    outs = []

    def values(j, hd, p):
        ov = _dot(vt_scr[hd * VT_ROWS:(hd + 1) * VT_ROWS, j * n_keys:(j + 1) * n_keys], p)
        outs.append(ov[:V_DIM] / ov[V_DIM:V_DIM + 1])
        if hd == N_HEADS - 1:
            o_ref[j * tq:(j + 1) * tq, :] = jnp.concatenate(outs, axis=0).T.astype(BF16)
            outs.clear()

    queue = [scores(*u) for u in units[:ATTN_AHEAD]]
    pending = []
    for i, unit in enumerate(units):
        s = queue.pop(0)
        if i + ATTN_AHEAD < len(units):
            queue.append(scores(*units[i + ATTN_AHEAD]))
        p = jnp.exp2(s - jnp.max(s, axis=0, keepdims=True)).astype(BF16)
        if len(pending) == ATTN_VALUE_LAG:
            values(*pending.pop(0))
        pending.append((*unit, p))
    for item in pending:
        values(*item)


def _kv_select():
    n_kv = N_HEADS * (QK_NOPE + V_DIM)
    pk = np.zeros((n_kv, N_HEADS * HEAD_PAD), np.float32)
    pvt = np.zeros((N_HEADS * V_DIM, n_kv), np.float32)
    for hd in range(N_HEADS):
        src = hd * (QK_NOPE + V_DIM)
        for j in range(QK_NOPE):
            pk[src + j, hd * HEAD_PAD + QK_ROPE + j] = 1.0
        for j in range(V_DIM):
            pvt[hd * V_DIM + j, src + QK_NOPE + j] = 1.0
    return pk.astype(BF16), pvt.astype(BF16)


def _attention(q, ckv_n, kr, w_ukv, layer, n_batch, n_own, row0, ctx=None):
    tq = min(n_own, ATTN_TQ)
    n_kv = N_HEADS * (QK_NOPE + V_DIM)
    n_ctx = 0 if ctx is None else PAST_LEN
    n_keys = n_own + n_ctx
    qt = n_own // tq
    seqs = ATTN_SEQS if (ctx is None and qt == 1) else 1
    blk0 = row0 // (seqs * n_own)
    const = lambda b, t: (0, 0)
    in_specs = [
        pl.BlockSpec((N_HEADS * HEAD_PAD, seqs * tq), lambda b, t: (0, (row0 // (seqs * tq)) + b * qt + t)),
        pl.BlockSpec((seqs * n_own, KV_RANK), lambda b, t: (blk0 + b, 0)),
        pl.BlockSpec((seqs * n_own, HEAD_PAD), lambda b, t: (blk0 + b, 0)),
    ]
    args = [q, ckv_n, kr]
    if ctx is not None:
        cckv, ckr = ctx
        in_specs += [
            pl.BlockSpec((None, None, PAST_LEN, KV_RANK), lambda b, t: (b, layer, 0, 0)),
            pl.BlockSpec((None, None, QK_ROPE, PAST_LEN), lambda b, t: (b, layer, 0, 0)),
        ]
        args += [cckv, ckr]
    in_specs += [
        pl.BlockSpec((None, KV_RANK, n_kv), lambda b, t: (layer, 0, 0)),
        pl.BlockSpec((n_kv, N_HEADS * HEAD_PAD), const),
        pl.BlockSpec((N_HEADS * V_DIM, n_kv), const),
    ]
    args += [w_ukv, *_kv_select()]
    return pl.pallas_call(
        functools.partial(_attn_kernel, n_own=n_own, n_ctx=n_ctx, seqs=seqs),
        grid=(n_batch // seqs, qt),
        in_specs=in_specs,
        out_specs=pl.BlockSpec((seqs * tq, N_HEADS * V_DIM), lambda b, t: (b * qt + t, 0)),
        out_shape=jax.ShapeDtypeStruct((n_batch * n_own, N_HEADS * V_DIM), BF16),
        scratch_shapes=[
            pltpu.VMEM((seqs * n_keys, N_HEADS * HEAD_PAD), BF16),
            pltpu.VMEM((N_HEADS * VT_ROWS, seqs * n_keys), BF16),
            pltpu.VMEM((KV_RANK, N_HEADS * HEAD_PAD), BF16),
            pltpu.VMEM((N_HEADS * V_DIM, KV_RANK), BF16),
        ],
        compiler_params=pltpu.CompilerParams(
            dimension_semantics=("arbitrary", "arbitrary"), vmem_limit_bytes=VMEM_LIMIT),
        name="mla_attention_%d" % n_keys,
    )(*args)


TAIL_TM = 1024
TAIL_GROUP = 256
TAIL_SLABS = 4
GATE_ROW = ROPE_COL + QK_ROPE
GATE_CHUNK = 512
GATE_CHUNKS = 2 * D_MODEL // GATE_CHUNK


def _tail_kernel(x_ref, sh_ref, sc_ref, g_ref, nw_ref, frp_ref, frs_ref, op_ref, os_ref, wi_hbm,
                 wf_ref, wa_ref, wo_ref, out_ref, wg_s, wf_s, wa_s, wo_s, stage, sem, *, layer):
    s = pl.program_id(0)
    is_prompt = jnp.maximum(s - TAIL_SLABS, 0) < N_PROMPT // TAIL_TM

    def gate_copy(c):
        src = wi_hbm.at[layer, pl.ds(GATE_ROW + c * GATE_CHUNK, GATE_CHUNK), :]
        return pltpu.make_async_copy(src, stage.at[c % 2], sem.at[c % 2])

    @pl.when(s == 0)
    def _():
        gate_copy(0).start()
        for c in range(GATE_CHUNKS):
            if c + 1 < GATE_CHUNKS:
                gate_copy(c + 1).start()
            gate_copy(c).wait()
            wg_s[:, c * GATE_CHUNK:(c + 1) * GATE_CHUNK] = stage[c % 2].T.astype(BF16)

    @pl.when(s < TAIL_SLABS)
    def _():
        def put(dst, src):
            rows = src.shape[0]
            dst[pl.ds(pl.multiple_of(s * rows, rows), rows), :] = src.astype(BF16)

        put(wf_s, wf_ref[...])
        put(wa_s, wa_ref[...])
        put(wo_s, wo_ref[...])

    @pl.when(s >= TAIL_SLABS)
    def _():
        groups = [slice(r * TAIL_GROUP, (r + 1) * TAIL_GROUP) for r in range(TAIL_TM // TAIL_GROUP)]

        def branches(rows):
            x = x_ref[rows, :]
            h = _norm_mod(x, nw_ref[layer:layer + 1, :], sc_ref[...], sh_ref[...]).astype(BF16)
            gates = _dot(h, wg_s[...])
            a_out = _dot(jnp.where(is_prompt, frp_ref[rows, :], frs_ref[rows, :]), wf_s[...])
            o_out = _dot(jnp.where(is_prompt, op_ref[rows, :], os_ref[rows, :]), wa_s[...])
            return x, gates, a_out, o_out

        def merge(rows, x, gates, a_out, o_out):
            merged = jax.nn.sigmoid(gates[:, :D_MODEL]) * a_out + jax.nn.sigmoid(gates[:, D_MODEL:]) * o_out
            m = _dot(merged.astype(BF16), wo_s[...])
            out_ref[rows, :] = x + g_ref[...] * m

        pending = branches(groups[0])
        for r in range(1, len(groups)):
            ready, pending = pending, branches(groups[r])
            merge(groups[r - 1], *ready)
        merge(groups[-1], *pending)


def _tail(x, mod, layer, nw, frs, os_, w_in_t, w_four, w_attn_proj, w_out):
    tm = TAIL_TM
    npt = N_PROMPT // tm
    tile = lambda s: jnp.maximum(s - TAIL_SLABS, 0)
    slab = lambda s: jnp.minimum(s, TAIL_SLABS - 1)
    tok = lambda s: (tile(s), 0)
    tok_p = lambda s: (jnp.minimum(tile(s), npt - 1), 0)
    tok_s = lambda s: (jnp.maximum(tile(s) - npt, 0), 0)
    n_attn = N_HEADS * V_DIM

    def slab_spec(rows, cols):
        return pl.BlockSpec((None, rows // TAIL_SLABS, cols), lambda s: (layer, slab(s), 0))

    return pl.pallas_call(
        functools.partial(_tail_kernel, layer=layer),
        grid=(TAIL_SLABS + N_TOK // tm,),
        in_specs=[
            pl.BlockSpec((tm, D_MODEL), tok),
            _mod_spec(layer, 3, tm, tile),
            _mod_spec(layer, 4, tm, tile),
            _mod_spec(layer, 5, tm, tile),
            pl.BlockSpec((DEPTH, D_MODEL), lambda s: (0, 0)),
            pl.BlockSpec((tm, D_FOUR), tok_p),
            pl.BlockSpec((tm, D_FOUR), tok_s),
            pl.BlockSpec((tm, n_attn), tok_p),
            pl.BlockSpec((tm, n_attn), tok_s),
            pl.BlockSpec(memory_space=pl.ANY),
            slab_spec(D_FOUR, D_MODEL),
            slab_spec(n_attn, D_MODEL),
            slab_spec(D_MODEL, D_MODEL),
        ],
        out_specs=pl.BlockSpec((tm, D_MODEL), tok),
        out_shape=jax.ShapeDtypeStruct((N_TOK, D_MODEL), F32),
        scratch_shapes=[
            pltpu.VMEM((D_MODEL, 2 * D_MODEL), BF16),
            pltpu.VMEM((D_FOUR, D_MODEL), BF16),
            pltpu.VMEM((n_attn, D_MODEL), BF16),
            pltpu.VMEM((D_MODEL, D_MODEL), BF16),
            pltpu.VMEM((2, GATE_CHUNK, D_MODEL), F32),
            pltpu.SemaphoreType.DMA((2,)),
        ],
        compiler_params=pltpu.CompilerParams(
            dimension_semantics=("arbitrary",), vmem_limit_bytes=VMEM_LIMIT),
        name="mixer_tail",
    )(x, mod, mod, mod, nw, *frs, *os_, w_in_t, w_four, w_attn_proj, w_out)


def _rope_tables(tm):
    rows = DEC_SEQ // GRID_W
    row = np.repeat(np.arange(rows), GRID_W).astype(np.float64)
    col = np.tile(np.arange(GRID_W), rows).astype(np.float64)
    axis_dim = QK_ROPE // 2
    inv = ROPE_BASE ** (-np.arange(0, axis_dim, 2, dtype=np.float64) / axis_dim)
    ar = row[:, None] * inv
    ac = col[:, None] * inv
    cr, sr, cc, sc = np.cos(ar), np.sin(ar), np.cos(ac), np.sin(ac)
    cos32 = np.concatenate([cr, cr, cc, cc], axis=1)
    sin32 = np.concatenate([-sr, sr, -sc, sc], axis=1)
    scale = np.float32((QK_NOPE + QK_ROPE) ** -0.5 * np.log2(np.e))

    def table(rope_part, nope_val, ident_rope):
        t = np.zeros((DEC_SEQ + tm, HEAD_PAD), np.float32)
        t[:DEC_SEQ, :QK_ROPE] = rope_part
        t[:DEC_SEQ, QK_ROPE:QK_ROPE + QK_NOPE] = nope_val
        t[DEC_SEQ:, :QK_ROPE] = ident_rope
        t[DEC_SEQ:, QK_ROPE:QK_ROPE + QK_NOPE] = nope_val
        return t

    cq = (table(cos32, 1.0, 1.0) * scale).T
    sq = (table(sin32, 0.0, 0.0) * scale).T
    ck = table(cos32, 0.0, 1.0)
    sk = table(sin32, 0.0, 0.0)
    return tuple(jnp.asarray(np.ascontiguousarray(t), F32) for t in (cq, sq, ck, sk))


def kernel(x_prompt, x_sample, cache_ckv, cache_krope, c, c_ctx, w_mod, b_mod, norm_ffn1, w_ffn1_gate,
           w_ffn1_up, w_ffn1_down, norm_mix, w_in, w_four, q_norm, w_uq, kv_norm, w_ukv, w_attn_proj,
           w_out, norm_ffn2, w_ffn2_gate, w_ffn2_up, w_ffn2_down, final_norm):
    xs = (x_prompt.reshape(N_PROMPT, D_MODEL), x_sample.reshape(N_SAMPLE, D_MODEL))
    c_all = jnp.concatenate(
        [c_ctx[None, :], c, jnp.zeros((MOD_ROWS - 1 - DEC_BATCH, D_MODEL), F32)], axis=0)
    mod = _modulation(c_all, w_mod, b_mod).reshape(DEPTH * MOD_ROWS * N_MOD, 1, D_MODEL)
    cache_kr = jnp.swapaxes(cache_krope, 2, 3)
    w_in_t = jnp.swapaxes(w_in, 1, 2)

    caches = ()
    for l in range(DEPTH):
        x = _ffn(xs, mod, l, 0, norm_ffn1, w_ffn1_gate, w_ffn1_up, w_ffn1_down, final_norm, False)
        f_in, q, ckv_n, kr, new_ckv, new_krope = _proj(
            x, mod, l, norm_mix, w_in_t, q_norm, w_uq, kv_norm, _rope_tables, caches)
        caches = (new_ckv, new_krope)
        frs = (_dft(f_in, BATCH, SEQ, 0), _dft(f_in, DEC_BATCH, DEC_SEQ, N_PROMPT))
        os_ = (_attention(q, ckv_n, kr, w_ukv, l, BATCH, SEQ, 0),
               _attention(q, ckv_n, kr, w_ukv, l, DEC_BATCH, DEC_SEQ, N_PROMPT, (cache_ckv, cache_kr)))
        x = _tail(x, mod, l, norm_mix, frs, os_, w_in_t, w_four, w_attn_proj, w_out)
        xs = _ffn((x,), mod, l, 6, norm_ffn2, w_ffn2_gate, w_ffn2_up, w_ffn2_down, final_norm, l == DEPTH - 1)
        if l < DEPTH - 1:
            xs = (xs,)

    y_prompt = xs[0].reshape(BATCH, SEQ, D_MODEL)
    y_sample = xs[1].reshape(DEC_BATCH, DEC_SEQ, D_MODEL)
    return y_prompt, y_sample, caches[0], jnp.swapaxes(caches[1], 2, 3)
```

```python
import functools

import numpy as np
import jax
import jax.numpy as jnp
from jax import lax
from jax.experimental import pallas as pl
from jax.experimental.pallas import tpu as pltpu

D_MODEL = 1024
BATCH = 16
SEQ = 256
DEPTH = 2
DEC_BATCH = 4
DEC_SEQ = 1024
PAST_LEN = 512
GRID_W = 64
D_FF = 2816
FOUR_GROUPS = 4
FOUR_GROUP_DIM = 64
D_FOUR = FOUR_GROUPS * FOUR_GROUP_DIM
N_HEADS = 8
QK_NOPE = 64
QK_ROPE = 32
V_DIM = 64
Q_RANK = 384
KV_RANK = 256
N_MOD = 9
ROPE_BASE = 10000.0
EPS = 1e-6

N_PROMPT = BATCH * SEQ
N_SAMPLE = DEC_BATCH * DEC_SEQ
N_TOK = N_PROMPT + N_SAMPLE
MOD_ROWS = 8
MOD_FFN1, MOD_MIX, MOD_FFN2 = 0, 3, 6
HEAD_PAD = 128
ROPE_AXIS = QK_ROPE // 2
ROPE_PAIR = ROPE_AXIS // 2

VMEM_LIMIT = 52 * 1024 * 1024

F32 = jnp.float32
BF16 = jnp.bfloat16


def _dot(a, b):
    return jnp.dot(a, b, preferred_element_type=F32)


def _dot_nt(a, b):
    return lax.dot_general(a, b, (((1,), (1,)), ((), ())), preferred_element_type=F32)


def _mod_row(i, tm):
    n_prompt_tiles = N_PROMPT // tm
    per_batch = DEC_SEQ // tm
    return jnp.where(i < n_prompt_tiles, 0, 1 + (i - n_prompt_tiles) // per_batch)


def _mod_spec(layer, which, tm, tile_of_step=lambda i: i):
    def idx(*g):
        return ((layer * MOD_ROWS + _mod_row(tile_of_step(g[0]), tm)) * N_MOD + which, 0, 0)
    return pl.BlockSpec((None, 1, D_MODEL), idx)


def _norm_mod(x, nw, sc, sh):
    ms = jnp.mean(x * x, axis=-1, keepdims=True)
    y = x * lax.rsqrt(ms + EPS) * nw
    return y * (1.0 + sc) + sh


MOD_TN = 4608


def _mod_kernel(c_ref, w_ref, b_ref, o_ref):
    c = c_ref[...]
    a = (c * jax.nn.sigmoid(c)).astype(BF16)
    o_ref[...] = _dot(a, w_ref[...].astype(BF16)) + b_ref[pl.ds(pl.program_id(0), 1), :]


def _modulation(c_all, w_mod, b_mod):
    tn = MOD_TN
    n_out = N_MOD * D_MODEL
    return pl.pallas_call(
        _mod_kernel,
        grid=(DEPTH, n_out // tn),
        in_specs=[
            pl.BlockSpec((MOD_ROWS, D_MODEL), lambda l, j: (0, 0)),
            pl.BlockSpec((None, D_MODEL, tn), lambda l, j: (l, 0, j)),
            pl.BlockSpec((DEPTH, tn), lambda l, j: (0, j)),
        ],
        out_specs=pl.BlockSpec((None, MOD_ROWS, tn), lambda l, j: (l, 0, j)),
        out_shape=jax.ShapeDtypeStruct((DEPTH, MOD_ROWS, n_out), F32),
        compiler_params=pltpu.CompilerParams(
            dimension_semantics=("arbitrary", "arbitrary"), vmem_limit_bytes=VMEM_LIMIT),
        name="modulation",
    )(c_all, w_mod, b_mod)


FFN_TM = 512
FFN_CHUNK = 256
FFN_NC = D_FF // FFN_CHUNK


def _ffn_kernel(*refs, layer, final, split_in):
    refs = list(refs)
    x_refs = [refs.pop(0) for _ in range(2 if split_in else 1)]
    sh_ref, sc_ref, g_ref, nw_ref, wg_ref, wu_ref, wd_ref, fn_ref = refs[:8]
    o_refs = refs[8:10] if final else refs[8:9]
    wg_s, wu_s, wd_s, h_scr, acc_scr = refs[-5:]
    s = pl.program_id(0)
    is_prompt = jnp.maximum(s - (FFN_NC - 1), 0) < N_PROMPT // FFN_TM

    def x_tile():
        if split_in:
            return jnp.where(is_prompt, x_refs[0][...], x_refs[1][...])
        return x_refs[0][...]

    def chunk_act(h, wg, wu):
        gate = _dot(h, wg)
        up = _dot(h, wu)
        return ((gate * jax.nn.sigmoid(gate)) * up).astype(BF16)

    def finish(acc):
        xn = x_tile() + 0.5 * g_ref[...] * acc
        if not final:
            o_refs[0][...] = xn
            return
        ms = jnp.mean(xn * xn, axis=-1, keepdims=True)
        y = xn * lax.rsqrt(ms + EPS) * fn_ref[...]

        @pl.when(is_prompt)
        def _():
            o_refs[0][...] = y

        @pl.when(jnp.logical_not(is_prompt))
        def _():
            o_refs[1][...] = y

    def hidden():
        return _norm_mod(x_tile(), nw_ref[layer:layer + 1, :], sc_ref[...], sh_ref[...]).astype(BF16)

    @pl.when(s == 0)
    def _():
        h_scr[...] = hidden()
        acc_scr[...] = jnp.zeros_like(acc_scr)

    @pl.when(s < FFN_NC)
    def _():
        wg = wg_ref[...].astype(BF16)
        wu = wu_ref[...].astype(BF16)
        wd = wd_ref[...].astype(BF16)
        wg_s[s] = wg
        wu_s[s] = wu
        wd_s[pl.ds(pl.multiple_of(s * FFN_CHUNK, FFN_CHUNK), FFN_CHUNK), :] = wd
        acc_scr[...] += _dot(chunk_act(h_scr[...], wg, wu), wd)

    @pl.when(s == FFN_NC - 1)
    def _():
        finish(acc_scr[...])

    @pl.when(s >= FFN_NC)
    def _():
        h = hidden()
        act = jnp.concatenate([chunk_act(h, wg_s[j], wu_s[j]) for j in range(FFN_NC)], axis=1)
        finish(_dot(act, wd_s[...]))


def _ffn(xs, mod, layer, mod_base, nw, wg, wu, wd, final_norm, final):
    tm = FFN_TM
    npt = N_PROMPT // tm
    split_in = len(xs) == 2
    tile = lambda s: jnp.maximum(s - (FFN_NC - 1), 0)
    chunk = lambda s: jnp.minimum(s, FFN_NC - 1)
    tok = pl.BlockSpec((tm, D_MODEL), lambda s: (tile(s), 0))
    tok_p = pl.BlockSpec((tm, D_MODEL), lambda s: (jnp.minimum(tile(s), npt - 1), 0))
    tok_s = pl.BlockSpec((tm, D_MODEL), lambda s: (jnp.maximum(tile(s) - npt, 0), 0))
    half = jax.ShapeDtypeStruct((N_PROMPT, D_MODEL), F32)
    return pl.pallas_call(
        functools.partial(_ffn_kernel, layer=layer, final=final, split_in=split_in),
        grid=(FFN_NC + N_TOK // tm - 1,),
        in_specs=([tok_p, tok_s] if split_in else [tok]) + [
            _mod_spec(layer, mod_base + 0, tm, tile),
            _mod_spec(layer, mod_base + 1, tm, tile),
            _mod_spec(layer, mod_base + 2, tm, tile),
            pl.BlockSpec((DEPTH, D_MODEL), lambda s: (0, 0)),
            pl.BlockSpec((None, D_MODEL, FFN_CHUNK), lambda s: (layer, 0, chunk(s))),
            pl.BlockSpec((None, D_MODEL, FFN_CHUNK), lambda s: (layer, 0, chunk(s))),
            pl.BlockSpec((None, FFN_CHUNK, D_MODEL), lambda s: (layer, chunk(s), 0)),
            pl.BlockSpec((1, D_MODEL), lambda s: (0, 0)),
        ],
        out_specs=[tok_p, tok_s] if final else tok,
        out_shape=[half, half] if final else jax.ShapeDtypeStruct((N_TOK, D_MODEL), F32),
        scratch_shapes=[
            pltpu.VMEM((FFN_NC, D_MODEL, FFN_CHUNK), BF16),
            pltpu.VMEM((FFN_NC, D_MODEL, FFN_CHUNK), BF16),
            pltpu.VMEM((D_FF, D_MODEL), BF16),
            pltpu.VMEM((tm, D_MODEL), BF16),
            pltpu.VMEM((tm, D_MODEL), F32),
        ],
        compiler_params=pltpu.CompilerParams(
            dimension_semantics=("arbitrary",), vmem_limit_bytes=VMEM_LIMIT),
        name="ffn_final" if final else ("ffn_first" if split_in else "ffn"),
    )(*xs, mod, mod, mod, nw, wg, wu, wd, final_norm.reshape(1, D_MODEL))


PROJ_TM = 1024
PROJ_SEQS = PROJ_TM // SEQ
PROJ_W = 1024
ROPE_COL = D_FOUR + Q_RANK + KV_RANK


def _pair_swap(x):
    lane = lax.broadcasted_iota(jnp.int32, x.shape, 1)
    first = lane % ROPE_AXIS < ROPE_PAIR
    return jnp.where(first, pltpu.roll(x, HEAD_PAD - ROPE_PAIR, 1), pltpu.roll(x, ROPE_PAIR, 1))


def _proj_kernel(*refs, layer, first_layer):
    (x_ref, sh_ref, sc_ref, nw_ref, w1_ref, qn_ref, wq_ref, pq_ref, kvn_ref,
     cq_ref, sq_ref, ck_ref, sk_ref) = refs[:13]
    f_ref, q_ref, ckv_ref, kr_ref, nckv_ref, nkr_ref, w1_s, wq_s = refs[-8:]
    is_prompt = pl.program_id(0) < N_PROMPT // PROJ_TM

    @pl.when(pl.program_id(0) == 0)
    def _():
        w1_s[...] = w1_ref[...].T.astype(BF16)
        wq_s[...] = _dot_nt(pq_ref[...], wq_ref[...].astype(BF16)).astype(BF16)

    n_qp = N_HEADS * HEAD_PAD
    c0 = D_FOUR + Q_RANK
    groups = [slice(r * SEQ, (r + 1) * SEQ) for r in range(PROJ_SEQS)]
    ps = []
    for rows in groups:
        h = _norm_mod(x_ref[rows, :], nw_ref[layer:layer + 1, :], sc_ref[...], sh_ref[...]).astype(BF16)
        ps.append(_dot(h, w1_s[...]))
    staged = []
    for rows, p in zip(groups, ps):
        f_ref[rows, :] = p[:, :D_FOUR]
        ql = p[:, D_FOUR:c0]
        qn = ql * lax.rsqrt(jnp.mean(ql * ql, axis=-1, keepdims=True) + EPS) * qn_ref[layer:layer + 1, :]
        qn = qn.astype(BF16)
        qq = _dot_nt(wq_s[:n_qp, :], qn)
        ckv = p[:, c0:c0 + KV_RANK]
        ckv_n = ckv * lax.rsqrt(jnp.mean(ckv * ckv, axis=-1, keepdims=True) + EPS) * kvn_ref[layer:layer + 1, :]
        ckv_ref[rows, :] = ckv_n
        kr = p[:, ROPE_COL:ROPE_COL + HEAD_PAD]
        kr = jnp.where(lax.broadcasted_iota(jnp.int32, kr.shape, 1) < QK_ROPE, kr, 0.0)
        staged.append((qn, qq, ckv_n, kr))

    @pl.when(is_prompt)
    def _():
        for r, (rows, (qn, qq, ckv_n, kr)) in enumerate(zip(groups, staged)):
            cq = cq_ref[:, rows]
            for hd in range(N_HEADS):
                lo = hd * HEAD_PAD
                q_ref[lo:lo + HEAD_PAD, rows] = (qq[lo:lo + HEAD_PAD, :] * cq).astype(BF16)
            kr_ref[rows, :] = (kr * ck_ref[rows, :]).astype(BF16)
            if first_layer:
                nckv_ref[r, 0] = ckv_n
                nkr_ref[r, 0] = kr.T[:QK_ROPE]
                for later in range(1, DEPTH):
                    nckv_ref[r, later] = jnp.zeros_like(ckv_n)
                    nkr_ref[r, later] = jnp.zeros((QK_ROPE, SEQ), F32)
            else:
                nckv_ref[r] = ckv_n
                nkr_ref[r] = kr.T[:QK_ROPE]

    @pl.when(jnp.logical_not(is_prompt))
    def _():
        for rows, (qn, qq, ckv_n, kr) in zip(groups, staged):
            cq, sq = cq_ref[:, rows], sq_ref[:, rows]
            qs = _dot_nt(wq_s[n_qp:, :], qn)
            for hd in range(N_HEADS):
                lo, mid = hd * HEAD_PAD, hd * HEAD_PAD + QK_ROPE
                rope = qq[lo:mid, :] * cq[:QK_ROPE] + qs[hd * QK_ROPE:(hd + 1) * QK_ROPE, :] * sq[:QK_ROPE]
                q_ref[lo:mid, rows] = rope.astype(BF16)
                q_ref[mid:lo + HEAD_PAD, rows] = (qq[mid:lo + HEAD_PAD, :] * cq[QK_ROPE:]).astype(BF16)
            kr_ref[rows, :] = (kr * ck_ref[rows, :] + _pair_swap(kr) * sk_ref[rows, :]).astype(BF16)


def _q_select():
    n_qp = N_HEADS * HEAD_PAD
    p = np.zeros((n_qp + N_HEADS * QK_ROPE, N_HEADS * (QK_NOPE + QK_ROPE)), np.float32)
    for hd in range(N_HEADS):
        src, dst = hd * (QK_NOPE + QK_ROPE), hd * HEAD_PAD
        for j in range(QK_ROPE):
            partner = j + ROPE_PAIR if j % ROPE_AXIS < ROPE_PAIR else j - ROPE_PAIR
            p[dst + j, src + QK_NOPE + j] = 1.0
            p[n_qp + hd * QK_ROPE + j, src + QK_NOPE + partner] = 1.0
        for j in range(QK_NOPE):
            p[dst + QK_ROPE + j, src + j] = 1.0
    return p.astype(BF16)


def _proj(x, mod, layer, nw, w_in_t, qn, w_uq, kvn, tabs, prev_caches):
    tm = PROJ_TM
    n_prompt_tiles = N_PROMPT // tm
    per_batch = DEC_SEQ // tm
    n_q = N_HEADS * (QK_NOPE + QK_ROPE)

    def tab_blk(i):
        return jnp.where(i < n_prompt_tiles, per_batch, (i - n_prompt_tiles) % per_batch)

    tok = lambda i: (i, 0)
    const = lambda i: (0, 0)
    first_layer = not prev_caches
    assert first_layer == (layer == 0)
    cache_layers = DEPTH if first_layer else None
    cache_idx = lambda i: (jnp.minimum(i, n_prompt_tiles - 1), 0 if first_layer else layer, 0, 0)
    tab_spec = pl.BlockSpec((tm, HEAD_PAD), lambda i: (tab_blk(i), 0))
    tab_t_spec = pl.BlockSpec((HEAD_PAD, tm), lambda i: (0, tab_blk(i)))
    return pl.pallas_call(
        functools.partial(_proj_kernel, layer=layer, first_layer=first_layer),
        grid=(N_TOK // tm,),
        in_specs=[
            pl.BlockSpec((tm, D_MODEL), tok),
            _mod_spec(layer, MOD_MIX + 0, tm),
            _mod_spec(layer, MOD_MIX + 1, tm),
            pl.BlockSpec((DEPTH, D_MODEL), const),
            pl.BlockSpec((None, PROJ_W, D_MODEL), lambda i: (layer, 0, 0)),
            pl.BlockSpec((DEPTH, Q_RANK), const),
            pl.BlockSpec((None, Q_RANK, n_q), lambda i: (layer, 0, 0)),
            pl.BlockSpec((N_HEADS * (HEAD_PAD + QK_ROPE), n_q), const),
            pl.BlockSpec((DEPTH, KV_RANK), const),
            tab_t_spec, tab_t_spec, tab_spec, tab_spec,
        ] + [pl.BlockSpec(memory_space=pl.ANY)] * len(prev_caches),
        out_specs=[
            pl.BlockSpec((tm, D_FOUR), tok),
            pl.BlockSpec((N_HEADS * HEAD_PAD, tm), lambda i: (0, i)),
            pl.BlockSpec((tm, KV_RANK), tok),
            pl.BlockSpec((tm, HEAD_PAD), tok),
            pl.BlockSpec((PROJ_SEQS, cache_layers, SEQ, KV_RANK), cache_idx),
            pl.BlockSpec((PROJ_SEQS, cache_layers, QK_ROPE, SEQ), cache_idx),
        ],
        out_shape=[
            jax.ShapeDtypeStruct((N_TOK, D_FOUR), F32),
            jax.ShapeDtypeStruct((N_HEADS * HEAD_PAD, N_TOK), BF16),
            jax.ShapeDtypeStruct((N_TOK, KV_RANK), F32),
            jax.ShapeDtypeStruct((N_TOK, HEAD_PAD), BF16),
            jax.ShapeDtypeStruct((BATCH, DEPTH, SEQ, KV_RANK), F32),
            jax.ShapeDtypeStruct((BATCH, DEPTH, QK_ROPE, SEQ), F32),
        ],
        input_output_aliases={13 + k: 4 + k for k in range(len(prev_caches))},
        scratch_shapes=[
            pltpu.VMEM((PROJ_W, D_MODEL), BF16),
            pltpu.VMEM((N_HEADS * (HEAD_PAD + QK_ROPE), Q_RANK), BF16),
        ],
        compiler_params=pltpu.CompilerParams(
            dimension_semantics=("arbitrary",), vmem_limit_bytes=VMEM_LIMIT),
        name="mixer_proj",
    )(x, mod, mod, nw, w_in_t, qn, w_uq, _q_select(), kvn, *tabs(tm), *prev_caches)


DFT_ROWS = 1024


def _split(x):
    hi = x.astype(BF16)
    lo = (x - hi.astype(F32)).astype(BF16)
    return hi, lo


def _dot3(ah, al, bh, bl):
    return _dot(ah, bh) + _dot(al, bh) + _dot(ah, bl)


def _dft_kernel(x_ref, gh_ref, gl_ref, ch_ref, cl_ref, sh_ref, sl_ref, rev_ref, o_ref, *, n):
    half = n // 2
    xh, xl = _split(x_ref[...])
    y = _dot3(xh, xl, gh_ref[...], gl_ref[...])
    yh, yl = _split(y)
    ch, cl, sh, sl = ch_ref[...], cl_ref[...], sh_ref[...], sl_ref[...]
    pos = lax.broadcasted_iota(jnp.int32, (n, D_FOUR), 0)
    alt = jnp.where(pos % 2 == 0, 1.0, -1.0).astype(F32) * (float(n) ** -0.5)
    first = lax.broadcasted_iota(jnp.int32, (half, D_FOUR), 0) == 0
    seqs = [slice(b * n, (b + 1) * n) for b in range(x_ref.shape[0] // n)]
    parts = [(_dot3(ch, cl, yh[r, :D_FOUR], yl[r, :D_FOUR]),
              _dot3(sh, sl, yh[r, D_FOUR:], yl[r, D_FOUR:]))
             for r in seqs]
    for r, (a, bb) in zip(seqs, parts):
        o_ref[r.start:r.start + half, :] = (a - bb).astype(BF16)
        mirrored = _dot(rev_ref[...], (a + bb).astype(BF16))
        nyquist = jnp.sum(y[r, :D_FOUR] * alt, axis=0, keepdims=True)
        o_ref[r.start + half:r.stop, :] = jnp.where(first, nyquist, mirrored).astype(BF16)


def _split_np(a):
    a32 = np.asarray(a, np.float32)
    hi = a32.astype(BF16)
    lo = (a32 - hi.astype(np.float32)).astype(BF16)
    return hi, lo


def _dft_tables(n):
    half = n // 2
    k = np.arange(half, dtype=np.int64)
    pos = np.arange(n, dtype=np.int64)
    ang = 2.0 * np.pi * ((k[:, None] * pos[None, :]) % n).astype(np.float64) / n
    c = np.arange(FOUR_GROUP_DIM, dtype=np.int64)
    ang_c = 2.0 * np.pi * ((c[:, None] * c[None, :]) % FOUR_GROUP_DIM).astype(np.float64) / FOUR_GROUP_DIM
    eye = np.eye(FOUR_GROUPS)
    g = np.concatenate([np.kron(eye, np.cos(ang_c)), np.kron(eye, np.sin(ang_c))], axis=1)
    g = g / np.sqrt(FOUR_GROUP_DIM)
    rev = np.zeros((half, half), np.float32)
    rev[np.arange(1, half), half - np.arange(1, half)] = 1.0
    return (_split_np(g) + _split_np(np.cos(ang) / np.sqrt(n)) + _split_np(np.sin(ang) / np.sqrt(n))
            + (rev.astype(BF16),))


def _dft(f_in, n_batch, n, row0):
    tables = _dft_tables(n)
    rows = max(n, DFT_ROWS)
    blk0 = row0 // rows
    const = lambda b: (0, 0)
    half_spec = pl.BlockSpec((n // 2, n), const)
    return pl.pallas_call(
        functools.partial(_dft_kernel, n=n),
        grid=(n_batch * n // rows,),
        in_specs=[
            pl.BlockSpec((rows, D_FOUR), lambda b: (blk0 + b, 0)),
            pl.BlockSpec((D_FOUR, 2 * D_FOUR), const),
            pl.BlockSpec((D_FOUR, 2 * D_FOUR), const),
            half_spec, half_spec, half_spec, half_spec,
            pl.BlockSpec((n // 2, n // 2), const),
        ],
        out_specs=pl.BlockSpec((rows, D_FOUR), lambda b: (b, 0)),
        out_shape=jax.ShapeDtypeStruct((n_batch * n, D_FOUR), BF16),
        compiler_params=pltpu.CompilerParams(
            dimension_semantics=("arbitrary",), vmem_limit_bytes=VMEM_LIMIT),
        name="fnet_dft_%d" % n,
    )(f_in, *tables)


ATTN_TQ = 256
ATTN_SEQS = 4
ATTN_AHEAD = 3
ATTN_VALUE_LAG = 1
VT_ROWS = V_DIM + 16


def _attn_kernel(*refs, n_own, n_ctx, seqs):
    if n_ctx:
        qt_ref, ckv_ref, kr_ref, cckv_ref, ckr_ref, wkv_ref, pk_ref, pvt_ref = refs[:8]
    else:
        qt_ref, ckv_ref, kr_ref, wkv_ref, pk_ref, pvt_ref = refs[:6]
    o_ref, k_scr, vt_scr, wk_s, wvt_s = refs[-5:]

    @pl.when(jnp.logical_and(pl.program_id(0) == 0, pl.program_id(1) == 0))
    def _():
        wkv = wkv_ref[...].astype(BF16)
        wk_s[...] = _dot(wkv, pk_ref[...]).astype(BF16)
        wvt_s[...] = _dot_nt(pvt_ref[...], wkv).astype(BF16)

    def build_keys_values():
        ckv = ckv_ref[...].astype(BF16)
        kr = kr_ref[...]
        if n_ctx:
            ckv = jnp.concatenate([ckv, cckv_ref[...].astype(BF16)], axis=0)
            ckr = jnp.concatenate([ckr_ref[...], jnp.zeros((HEAD_PAD - QK_ROPE, n_ctx), F32)], axis=0)
            kr = jnp.concatenate([kr, ckr.T.astype(BF16)], axis=0)
        k_nope = _dot(ckv, wk_s[...])
        kr = kr.astype(F32)
        for hd in range(N_HEADS):
            lo = hd * HEAD_PAD
            k_scr[:, lo:lo + HEAD_PAD] = (k_nope[:, lo:lo + HEAD_PAD] + kr).astype(BF16)
        vt = _dot_nt(wvt_s[...], ckv).astype(BF16)
        pad_rows = lax.broadcasted_iota(jnp.int32, (VT_ROWS - V_DIM, vt.shape[1]), 0)
        ones_row = (pad_rows == 0).astype(BF16)
        for hd in range(N_HEADS):
            vt_scr[hd * VT_ROWS:hd * VT_ROWS + V_DIM, :] = vt[hd * V_DIM:(hd + 1) * V_DIM]
            vt_scr[hd * VT_ROWS + V_DIM:(hd + 1) * VT_ROWS, :] = ones_row

    if seqs > 1:
        build_keys_values()
    else:
        pl.when(pl.program_id(1) == 0)(build_keys_values)

    n_keys = n_own + n_ctx
    tq = qt_ref.shape[1] // seqs
    units = [(j, hd) for j in range(seqs) for hd in range(N_HEADS)]

    def scores(j, hd):
        lo = hd * HEAD_PAD
        return _dot(k_scr[j * n_keys:(j + 1) * n_keys, lo:lo + HEAD_PAD],
                    qt_ref[lo:lo + HEAD_PAD, j * tq:(j + 1) * tq])

    outs = []

    def values(j, hd, p):
        ov = _dot(vt_scr[hd * VT_ROWS:(hd + 1) * VT_ROWS, j * n_keys:(j + 1) * n_keys], p)
        outs.append(ov[:V_DIM] / ov[V_DIM:V_DIM + 1])
        if hd == N_HEADS - 1:
            o_ref[j * tq:(j + 1) * tq, :] = jnp.concatenate(outs, axis=0).T.astype(BF16)
            outs.clear()

    queue = [scores(*u) for u in units[:ATTN_AHEAD]]
    pending = []
    for i, unit in enumerate(units):
        s = queue.pop(0)
        if i + ATTN_AHEAD < len(units):
            queue.append(scores(*units[i + ATTN_AHEAD]))
        p = jnp.exp2(s - jnp.max(s, axis=0, keepdims=True)).astype(BF16)
        if len(pending) == ATTN_VALUE_LAG:
            values(*pending.pop(0))
        pending.append((*unit, p))
    for item in pending:
        values(*item)


def _kv_select():
    n_kv = N_HEADS * (QK_NOPE + V_DIM)
    pk = np.zeros((n_kv, N_HEADS * HEAD_PAD), np.float32)
    pvt = np.zeros((N_HEADS * V_DIM, n_kv), np.float32)
    for hd in range(N_HEADS):
        src = hd * (QK_NOPE + V_DIM)
        for j in range(QK_NOPE):
            pk[src + j, hd * HEAD_PAD + QK_ROPE + j] = 1.0
        for j in range(V_DIM):
            pvt[hd * V_DIM + j, src + QK_NOPE + j] = 1.0
    return pk.astype(BF16), pvt.astype(BF16)


def _attention(q, ckv_n, kr, w_ukv, layer, n_batch, n_own, row0, ctx=None):
    tq = min(n_own, ATTN_TQ)
    n_kv = N_HEADS * (QK_NOPE + V_DIM)
    n_ctx = 0 if ctx is None else PAST_LEN
    n_keys = n_own + n_ctx
    qt = n_own // tq
    seqs = ATTN_SEQS if (ctx is None and qt == 1) else 1
    blk0 = row0 // (seqs * n_own)
    const = lambda b, t: (0, 0)
    in_specs = [
        pl.BlockSpec((N_HEADS * HEAD_PAD, seqs * tq), lambda b, t: (0, (row0 // (seqs * tq)) + b * qt + t)),
        pl.BlockSpec((seqs * n_own, KV_RANK), lambda b, t: (blk0 + b, 0)),
        pl.BlockSpec((seqs * n_own, HEAD_PAD), lambda b, t: (blk0 + b, 0)),
    ]
    args = [q, ckv_n, kr]
    if ctx is not None:
        cckv, ckr = ctx
        in_specs += [
            pl.BlockSpec((None, None, PAST_LEN, KV_RANK), lambda b, t: (b, layer, 0, 0)),
            pl.BlockSpec((None, None, QK_ROPE, PAST_LEN), lambda b, t: (b, layer, 0, 0)),
        ]
        args += [cckv, ckr]
    in_specs += [
        pl.BlockSpec((None, KV_RANK, n_kv), lambda b, t: (layer, 0, 0)),
        pl.BlockSpec((n_kv, N_HEADS * HEAD_PAD), const),
        pl.BlockSpec((N_HEADS * V_DIM, n_kv), const),
    ]
    args += [w_ukv, *_kv_select()]
    return pl.pallas_call(
        functools.partial(_attn_kernel, n_own=n_own, n_ctx=n_ctx, seqs=seqs),
        grid=(n_batch // seqs, qt),
        in_specs=in_specs,
        out_specs=pl.BlockSpec((seqs * tq, N_HEADS * V_DIM), lambda b, t: (b * qt + t, 0)),
        out_shape=jax.ShapeDtypeStruct((n_batch * n_own, N_HEADS * V_DIM), BF16),
        scratch_shapes=[
            pltpu.VMEM((seqs * n_keys, N_HEADS * HEAD_PAD), BF16),
            pltpu.VMEM((N_HEADS * VT_ROWS, seqs * n_keys), BF16),
            pltpu.VMEM((KV_RANK, N_HEADS * HEAD_PAD), BF16),
            pltpu.VMEM((N_HEADS * V_DIM, KV_RANK), BF16),
        ],
        compiler_params=pltpu.CompilerParams(
            dimension_semantics=("arbitrary", "arbitrary"), vmem_limit_bytes=VMEM_LIMIT),
        name="mla_attention_%d" % n_keys,
    )(*args)


TAIL_TM = 1024
TAIL_GROUP = 256
TAIL_SLABS = 4
GATE_ROW = ROPE_COL + QK_ROPE
GATE_CHUNK = 512
GATE_CHUNKS = 2 * D_MODEL // GATE_CHUNK


def _tail_kernel(x_ref, sh_ref, sc_ref, g_ref, nw_ref, frp_ref, frs_ref, op_ref, os_ref, wi_hbm,
                 wf_ref, wa_ref, wo_ref, out_ref, wg_s, wf_s, wa_s, wo_s, stage, sem, *, layer):
    s = pl.program_id(0)
    is_prompt = jnp.maximum(s - TAIL_SLABS, 0) < N_PROMPT // TAIL_TM

    def gate_copy(c):
        src = wi_hbm.at[layer, pl.ds(GATE_ROW + c * GATE_CHUNK, GATE_CHUNK), :]
        return pltpu.make_async_copy(src, stage.at[c % 2], sem.at[c % 2])

    @pl.when(s == 0)
    def _():
        gate_copy(0).start()
        for c in range(GATE_CHUNKS):
            if c + 1 < GATE_CHUNKS:
                gate_copy(c + 1).start()
            gate_copy(c).wait()
            wg_s[:, c * GATE_CHUNK:(c + 1) * GATE_CHUNK] = stage[c % 2].T.astype(BF16)

    @pl.when(s < TAIL_SLABS)
    def _():
        def put(dst, src):
            rows = src.shape[0]
            dst[pl.ds(pl.multiple_of(s * rows, rows), rows), :] = src.astype(BF16)

        put(wf_s, wf_ref[...])
        put(wa_s, wa_ref[...])
        put(wo_s, wo_ref[...])

    @pl.when(s >= TAIL_SLABS)
    def _():
        groups = [slice(r * TAIL_GROUP, (r + 1) * TAIL_GROUP) for r in range(TAIL_TM // TAIL_GROUP)]

        def branches(rows):
            x = x_ref[rows, :]
            h = _norm_mod(x, nw_ref[layer:layer + 1, :], sc_ref[...], sh_ref[...]).astype(BF16)
            gates = _dot(h, wg_s[...])
            a_out = _dot(jnp.where(is_prompt, frp_ref[rows, :], frs_ref[rows, :]), wf_s[...])
            o_out = _dot(jnp.where(is_prompt, op_ref[rows, :], os_ref[rows, :]), wa_s[...])
            return x, gates, a_out, o_out

        def merge(rows, x, gates, a_out, o_out):
            merged = jax.nn.sigmoid(gates[:, :D_MODEL]) * a_out + jax.nn.sigmoid(gates[:, D_MODEL:]) * o_out
            m = _dot(merged.astype(BF16), wo_s[...])
            out_ref[rows, :] = x + g_ref[...] * m

        pending = branches(groups[0])
        for r in range(1, len(groups)):
            ready, pending = pending, branches(groups[r])
            merge(groups[r - 1], *ready)
        merge(groups[-1], *pending)


def _tail(x, mod, layer, nw, frs, os_, w_in_t, w_four, w_attn_proj, w_out):
    tm = TAIL_TM
    npt = N_PROMPT // tm
    tile = lambda s: jnp.maximum(s - TAIL_SLABS, 0)
    slab = lambda s: jnp.minimum(s, TAIL_SLABS - 1)
    tok = lambda s: (tile(s), 0)
    tok_p = lambda s: (jnp.minimum(tile(s), npt - 1), 0)
    tok_s = lambda s: (jnp.maximum(tile(s) - npt, 0), 0)
    n_attn = N_HEADS * V_DIM

    def slab_spec(rows, cols):
        return pl.BlockSpec((None, rows // TAIL_SLABS, cols), lambda s: (layer, slab(s), 0))

    return pl.pallas_call(
        functools.partial(_tail_kernel, layer=layer),
        grid=(TAIL_SLABS + N_TOK // tm,),
        in_specs=[
            pl.BlockSpec((tm, D_MODEL), tok),
            _mod_spec(layer, MOD_MIX + 0, tm, tile),
            _mod_spec(layer, MOD_MIX + 1, tm, tile),
            _mod_spec(layer, MOD_MIX + 2, tm, tile),
            pl.BlockSpec((DEPTH, D_MODEL), lambda s: (0, 0)),
            pl.BlockSpec((tm, D_FOUR), tok_p),
            pl.BlockSpec((tm, D_FOUR), tok_s),
            pl.BlockSpec((tm, n_attn), tok_p),
            pl.BlockSpec((tm, n_attn), tok_s),
            pl.BlockSpec(memory_space=pl.ANY),
            slab_spec(D_FOUR, D_MODEL),
            slab_spec(n_attn, D_MODEL),
            slab_spec(D_MODEL, D_MODEL),
        ],
        out_specs=pl.BlockSpec((tm, D_MODEL), tok),
        out_shape=jax.ShapeDtypeStruct((N_TOK, D_MODEL), F32),
        scratch_shapes=[
            pltpu.VMEM((D_MODEL, 2 * D_MODEL), BF16),
            pltpu.VMEM((D_FOUR, D_MODEL), BF16),
            pltpu.VMEM((n_attn, D_MODEL), BF16),
            pltpu.VMEM((D_MODEL, D_MODEL), BF16),
            pltpu.VMEM((2, GATE_CHUNK, D_MODEL), F32),
            pltpu.SemaphoreType.DMA((2,)),
        ],
        compiler_params=pltpu.CompilerParams(
            dimension_semantics=("arbitrary",), vmem_limit_bytes=VMEM_LIMIT),
        name="mixer_tail",
    )(x, mod, mod, mod, nw, *frs, *os_, w_in_t, w_four, w_attn_proj, w_out)


def _rope_tables(tm):
    rows = DEC_SEQ // GRID_W
    row = np.repeat(np.arange(rows), GRID_W).astype(np.float64)
    col = np.tile(np.arange(GRID_W), rows).astype(np.float64)
    inv = ROPE_BASE ** (-np.arange(0, ROPE_AXIS, 2, dtype=np.float64) / ROPE_AXIS)
    ar = row[:, None] * inv
    ac = col[:, None] * inv
    cr, sr, cc, sc = np.cos(ar), np.sin(ar), np.cos(ac), np.sin(ac)
    cos32 = np.concatenate([cr, cr, cc, cc], axis=1)
    sin32 = np.concatenate([-sr, sr, -sc, sc], axis=1)
    scale = np.float32((QK_NOPE + QK_ROPE) ** -0.5 * np.log2(np.e))

    def table(rope_part, nope_val, ident_rope):
        t = np.zeros((DEC_SEQ + tm, HEAD_PAD), np.float32)
        t[:DEC_SEQ, :QK_ROPE] = rope_part
        t[:DEC_SEQ, QK_ROPE:QK_ROPE + QK_NOPE] = nope_val
        t[DEC_SEQ:, :QK_ROPE] = ident_rope
        t[DEC_SEQ:, QK_ROPE:QK_ROPE + QK_NOPE] = nope_val
        return t

    cq = (table(cos32, 1.0, 1.0) * scale).T
    sq = (table(sin32, 0.0, 0.0) * scale).T
    ck = table(cos32, 0.0, 1.0)
    sk = table(sin32, 0.0, 0.0)
    return tuple(jnp.asarray(np.ascontiguousarray(t), F32) for t in (cq, sq, ck, sk))


def kernel(x_prompt, x_sample, cache_ckv, cache_krope, c, c_ctx, w_mod, b_mod, norm_ffn1, w_ffn1_gate,
           w_ffn1_up, w_ffn1_down, norm_mix, w_in, w_four, q_norm, w_uq, kv_norm, w_ukv, w_attn_proj,
           w_out, norm_ffn2, w_ffn2_gate, w_ffn2_up, w_ffn2_down, final_norm):
    xs = (x_prompt.reshape(N_PROMPT, D_MODEL), x_sample.reshape(N_SAMPLE, D_MODEL))
    c_all = jnp.concatenate(
        [c_ctx[None, :], c, jnp.zeros((MOD_ROWS - 1 - DEC_BATCH, D_MODEL), F32)], axis=0)
    mod = _modulation(c_all, w_mod, b_mod).reshape(DEPTH * MOD_ROWS * N_MOD, 1, D_MODEL)
    cache_kr = jnp.swapaxes(cache_krope, 2, 3)
    w_in_t = jnp.swapaxes(w_in, 1, 2)

    caches = ()
    for l in range(DEPTH):
        x = _ffn(xs, mod, l, MOD_FFN1, norm_ffn1, w_ffn1_gate, w_ffn1_up, w_ffn1_down, final_norm, False)
        f_in, q, ckv_n, kr, new_ckv, new_krope = _proj(
            x, mod, l, norm_mix, w_in_t, q_norm, w_uq, kv_norm, _rope_tables, caches)
        caches = (new_ckv, new_krope)
        frs = (_dft(f_in, BATCH, SEQ, 0), _dft(f_in, DEC_BATCH, DEC_SEQ, N_PROMPT))
        os_ = (_attention(q, ckv_n, kr, w_ukv, l, BATCH, SEQ, 0),
               _attention(q, ckv_n, kr, w_ukv, l, DEC_BATCH, DEC_SEQ, N_PROMPT, (cache_ckv, cache_kr)))
        x = _tail(x, mod, l, norm_mix, frs, os_, w_in_t, w_four, w_attn_proj, w_out)
        xs = _ffn((x,), mod, l, MOD_FFN2, norm_ffn2, w_ffn2_gate, w_ffn2_up, w_ffn2_down, final_norm,
                  l == DEPTH - 1)
        if l < DEPTH - 1:
            xs = (xs,)

    y_prompt = xs[0].reshape(BATCH, SEQ, D_MODEL)
    y_sample = xs[1].reshape(DEC_BATCH, DEC_SEQ, D_MODEL)
    return y_prompt, y_sample, caches[0], jnp.swapaxes(caches[1], 2, 3)
```

```python
import functools

import numpy as np
import jax
import jax.numpy as jnp
from jax import lax
from jax.experimental import pallas as pl
from jax.experimental.pallas import tpu as pltpu

D_MODEL = 1024
BATCH = 16
SEQ = 256
DEPTH = 2
DEC_BATCH = 4
DEC_SEQ = 1024
PAST_LEN = 512
GRID_W = 64
D_FF = 2816
FOUR_GROUPS = 4
FOUR_GROUP_DIM = 64
D_FOUR = FOUR_GROUPS * FOUR_GROUP_DIM
N_HEADS = 8
QK_NOPE = 64
QK_ROPE = 32
V_DIM = 64
Q_RANK = 384
KV_RANK = 256
N_MOD = 9
ROPE_BASE = 10000.0
EPS = 1e-6

N_PROMPT = BATCH * SEQ
N_SAMPLE = DEC_BATCH * DEC_SEQ
N_TOK = N_PROMPT + N_SAMPLE
MOD_ROWS = 8
MOD_FFN1, MOD_MIX, MOD_FFN2 = 0, 3, 6
HEAD_PAD = 128
ROPE_AXIS = QK_ROPE // 2
ROPE_PAIR = ROPE_AXIS // 2

VMEM_LIMIT = 52 * 1024 * 1024

F32 = jnp.float32
BF16 = jnp.bfloat16


def _dot(a, b):
    return jnp.dot(a, b, preferred_element_type=F32)


def _dot_nt(a, b):
    return lax.dot_general(a, b, (((1,), (1,)), ((), ())), preferred_element_type=F32)


def _mod_row(i, tm):
    n_prompt_tiles = N_PROMPT // tm
    per_batch = DEC_SEQ // tm
    return jnp.where(i < n_prompt_tiles, 0, 1 + (i - n_prompt_tiles) // per_batch)


def _mod_spec(layer, which, tm, tile_of_step=lambda i: i):
    def idx(*g):
        return ((layer * MOD_ROWS + _mod_row(tile_of_step(g[0]), tm)) * N_MOD + which, 0, 0)
    return pl.BlockSpec((None, 1, D_MODEL), idx)


def _norm_mod(x, nw, sc, sh):
    ms = jnp.mean(x * x, axis=-1, keepdims=True)
    y = x * lax.rsqrt(ms + EPS) * nw
    return y * (1.0 + sc) + sh


MOD_TN = 2304


def _mod_kernel(c_ref, w_ref, b_ref, o_ref):
    c = c_ref[...]
    a = (c * jax.nn.sigmoid(c)).astype(BF16)
    o_ref[...] = _dot(a, w_ref[...].astype(BF16)) + b_ref[pl.ds(pl.program_id(0), 1), :]


def _modulation(c_all, w_mod, b_mod):
    tn = MOD_TN
    n_out = N_MOD * D_MODEL
    return pl.pallas_call(
        _mod_kernel,
        grid=(DEPTH, n_out // tn),
        in_specs=[
            pl.BlockSpec((MOD_ROWS, D_MODEL), lambda l, j: (0, 0)),
            pl.BlockSpec((None, D_MODEL, tn), lambda l, j: (l, 0, j)),
            pl.BlockSpec((DEPTH, tn), lambda l, j: (0, j)),
        ],
        out_specs=pl.BlockSpec((None, MOD_ROWS, tn), lambda l, j: (l, 0, j)),
        out_shape=jax.ShapeDtypeStruct((DEPTH, MOD_ROWS, n_out), F32),
        compiler_params=pltpu.CompilerParams(
            dimension_semantics=("arbitrary", "arbitrary"), vmem_limit_bytes=VMEM_LIMIT),
        name="modulation",
    )(c_all, w_mod, b_mod)


FFN_TM = 512
FFN_CHUNK = 256
FFN_NC = D_FF // FFN_CHUNK


def _ffn_kernel(*refs, layer, final, split_in):
    refs = list(refs)
    x_refs = [refs.pop(0) for _ in range(2 if split_in else 1)]
    sh_ref, sc_ref, g_ref, nw_ref, wg_ref, wu_ref, wd_ref, fn_ref = refs[:8]
    o_refs = refs[8:10] if final else refs[8:9]
    wg_s, wu_s, wd_s, h_scr, acc_scr = refs[-5:]
    s = pl.program_id(0)
    is_prompt = jnp.maximum(s - (FFN_NC - 1), 0) < N_PROMPT // FFN_TM

    def x_tile():
        if split_in:
            return jnp.where(is_prompt, x_refs[0][...], x_refs[1][...])
        return x_refs[0][...]

    def chunk_act(h, wg, wu):
        gate = _dot(h, wg)
        up = _dot(h, wu)
        return ((gate * jax.nn.sigmoid(gate)) * up).astype(BF16)

    def finish(acc):
        xn = x_tile() + 0.5 * g_ref[...] * acc
        if not final:
            o_refs[0][...] = xn
            return
        ms = jnp.mean(xn * xn, axis=-1, keepdims=True)
        y = xn * lax.rsqrt(ms + EPS) * fn_ref[...]

        @pl.when(is_prompt)
        def _():
            o_refs[0][...] = y

        @pl.when(jnp.logical_not(is_prompt))
        def _():
            o_refs[1][...] = y

    def hidden():
        return _norm_mod(x_tile(), nw_ref[layer:layer + 1, :], sc_ref[...], sh_ref[...]).astype(BF16)

    @pl.when(s == 0)
    def _():
        h_scr[...] = hidden()
        acc_scr[...] = jnp.zeros_like(acc_scr)

    @pl.when(s < FFN_NC)
    def _():
        wg = wg_ref[...].astype(BF16)
        wu = wu_ref[...].astype(BF16)
        wd = wd_ref[...].astype(BF16)
        wg_s[s] = wg
        wu_s[s] = wu
        wd_s[pl.ds(pl.multiple_of(s * FFN_CHUNK, FFN_CHUNK), FFN_CHUNK), :] = wd
        acc_scr[...] += _dot(chunk_act(h_scr[...], wg, wu), wd)

    @pl.when(s == FFN_NC - 1)
    def _():
        finish(acc_scr[...])

    @pl.when(s >= FFN_NC)
    def _():
        h = hidden()
        act = jnp.concatenate([chunk_act(h, wg_s[j], wu_s[j]) for j in range(FFN_NC)], axis=1)
        finish(_dot(act, wd_s[...]))


def _ffn(xs, mod, layer, mod_base, nw, wg, wu, wd, final_norm, final):
    tm = FFN_TM
    npt = N_PROMPT // tm
    split_in = len(xs) == 2
    tile = lambda s: jnp.maximum(s - (FFN_NC - 1), 0)
    chunk = lambda s: jnp.minimum(s, FFN_NC - 1)
    tok = pl.BlockSpec((tm, D_MODEL), lambda s: (tile(s), 0))
    tok_p = pl.BlockSpec((tm, D_MODEL), lambda s: (jnp.minimum(tile(s), npt - 1), 0))
    tok_s = pl.BlockSpec((tm, D_MODEL), lambda s: (jnp.maximum(tile(s) - npt, 0), 0))
    half = jax.ShapeDtypeStruct((N_PROMPT, D_MODEL), F32)
    return pl.pallas_call(
        functools.partial(_ffn_kernel, layer=layer, final=final, split_in=split_in),
        grid=(FFN_NC + N_TOK // tm - 1,),
        in_specs=([tok_p, tok_s] if split_in else [tok]) + [
            _mod_spec(layer, mod_base + 0, tm, tile),
            _mod_spec(layer, mod_base + 1, tm, tile),
            _mod_spec(layer, mod_base + 2, tm, tile),
            pl.BlockSpec((DEPTH, D_MODEL), lambda s: (0, 0)),
            pl.BlockSpec((None, D_MODEL, FFN_CHUNK), lambda s: (layer, 0, chunk(s))),
            pl.BlockSpec((None, D_MODEL, FFN_CHUNK), lambda s: (layer, 0, chunk(s))),
            pl.BlockSpec((None, FFN_CHUNK, D_MODEL), lambda s: (layer, chunk(s), 0)),
            pl.BlockSpec((1, D_MODEL), lambda s: (0, 0)),
        ],
        out_specs=[tok_p, tok_s] if final else tok,
        out_shape=[half, half] if final else jax.ShapeDtypeStruct((N_TOK, D_MODEL), F32),
        scratch_shapes=[
            pltpu.VMEM((FFN_NC, D_MODEL, FFN_CHUNK), BF16),
            pltpu.VMEM((FFN_NC, D_MODEL, FFN_CHUNK), BF16),
            pltpu.VMEM((D_FF, D_MODEL), BF16),
            pltpu.VMEM((tm, D_MODEL), BF16),
            pltpu.VMEM((tm, D_MODEL), F32),
        ],
        compiler_params=pltpu.CompilerParams(
            dimension_semantics=("arbitrary",), vmem_limit_bytes=VMEM_LIMIT),
        name="ffn_final" if final else ("ffn_first" if split_in else "ffn"),
    )(*xs, mod, mod, mod, nw, wg, wu, wd, final_norm.reshape(1, D_MODEL))


PROJ_TM = 1024
PROJ_SEQS = PROJ_TM // SEQ
PROJ_W = 1024
ROPE_COL = D_FOUR + Q_RANK + KV_RANK


def _pair_swap(x):
    lane = lax.broadcasted_iota(jnp.int32, x.shape, 1)
    first = lane % ROPE_AXIS < ROPE_PAIR
    return jnp.where(first, pltpu.roll(x, HEAD_PAD - ROPE_PAIR, 1), pltpu.roll(x, ROPE_PAIR, 1))


def _proj_kernel(*refs, layer, first_layer):
    (x_ref, sh_ref, sc_ref, nw_ref, w1_ref, qn_ref, wq_ref, pq_ref, kvn_ref,
     cq_ref, sq_ref, ck_ref, sk_ref) = refs[:13]
    f_ref, q_ref, ckv_ref, kr_ref, nckv_ref, nkr_ref, w1_s, wq_s = refs[-8:]
    is_prompt = pl.program_id(0) < N_PROMPT // PROJ_TM

    @pl.when(pl.program_id(0) == 0)
    def _():
        w1_s[...] = w1_ref[...].T.astype(BF16)
        wq_s[...] = _dot_nt(pq_ref[...], wq_ref[...].astype(BF16)).astype(BF16)

    n_qp = N_HEADS * HEAD_PAD
    c0 = D_FOUR + Q_RANK
    groups = [slice(r * SEQ, (r + 1) * SEQ) for r in range(PROJ_SEQS)]
    ps = []
    for rows in groups:
        h = _norm_mod(x_ref[rows, :], nw_ref[layer:layer + 1, :], sc_ref[...], sh_ref[...]).astype(BF16)
        ps.append(_dot(h, w1_s[...]))
    staged = []
    for rows, p in zip(groups, ps):
        f_ref[rows, :] = p[:, :D_FOUR]
        ql = p[:, D_FOUR:c0]
        qn = ql * lax.rsqrt(jnp.mean(ql * ql, axis=-1, keepdims=True) + EPS) * qn_ref[layer:layer + 1, :]
        qn = qn.astype(BF16)
        qq = _dot_nt(wq_s[:n_qp, :], qn)
        ckv = p[:, c0:c0 + KV_RANK]
        ckv_n = ckv * lax.rsqrt(jnp.mean(ckv * ckv, axis=-1, keepdims=True) + EPS) * kvn_ref[layer:layer + 1, :]
        ckv_ref[rows, :] = ckv_n
        kr = p[:, ROPE_COL:ROPE_COL + HEAD_PAD]
        kr = jnp.where(lax.broadcasted_iota(jnp.int32, kr.shape, 1) < QK_ROPE, kr, 0.0)
        staged.append((qn, qq, ckv_n, kr))

    @pl.when(is_prompt)
    def _():
        for r, (rows, (qn, qq, ckv_n, kr)) in enumerate(zip(groups, staged)):
            cq = cq_ref[:, rows]
            for hd in range(N_HEADS):
                lo = hd * HEAD_PAD
                q_ref[lo:lo + HEAD_PAD, rows] = (qq[lo:lo + HEAD_PAD, :] * cq).astype(BF16)
            kr_ref[rows, :] = (kr * ck_ref[rows, :]).astype(BF16)
            if first_layer:
                nckv_ref[r, 0] = ckv_n
                nkr_ref[r, 0] = kr.T[:QK_ROPE]
                for later in range(1, DEPTH):
                    nckv_ref[r, later] = jnp.zeros_like(ckv_n)
                    nkr_ref[r, later] = jnp.zeros((QK_ROPE, SEQ), F32)
            else:
                nckv_ref[r] = ckv_n
                nkr_ref[r] = kr.T[:QK_ROPE]

    @pl.when(jnp.logical_not(is_prompt))
    def _():
        for rows, (qn, qq, ckv_n, kr) in zip(groups, staged):
            cq, sq = cq_ref[:, rows], sq_ref[:, rows]
            qs = _dot_nt(wq_s[n_qp:, :], qn)
            for hd in range(N_HEADS):
                lo, mid = hd * HEAD_PAD, hd * HEAD_PAD + QK_ROPE
                rope = qq[lo:mid, :] * cq[:QK_ROPE] + qs[hd * QK_ROPE:(hd + 1) * QK_ROPE, :] * sq[:QK_ROPE]
                q_ref[lo:mid, rows] = rope.astype(BF16)
                q_ref[mid:lo + HEAD_PAD, rows] = (qq[mid:lo + HEAD_PAD, :] * cq[QK_ROPE:]).astype(BF16)
            kr_ref[rows, :] = (kr * ck_ref[rows, :] + _pair_swap(kr) * sk_ref[rows, :]).astype(BF16)


def _q_select():
    n_qp = N_HEADS * HEAD_PAD
    p = np.zeros((n_qp + N_HEADS * QK_ROPE, N_HEADS * (QK_NOPE + QK_ROPE)), np.float32)
    for hd in range(N_HEADS):
        src, dst = hd * (QK_NOPE + QK_ROPE), hd * HEAD_PAD
        for j in range(QK_ROPE):
            partner = j + ROPE_PAIR if j % ROPE_AXIS < ROPE_PAIR else j - ROPE_PAIR
            p[dst + j, src + QK_NOPE + j] = 1.0
            p[n_qp + hd * QK_ROPE + j, src + QK_NOPE + partner] = 1.0
        for j in range(QK_NOPE):
            p[dst + QK_ROPE + j, src + j] = 1.0
    return p.astype(BF16)


def _proj(x, mod, layer, nw, w_in_t, qn, w_uq, kvn, tabs, prev_caches):
    tm = PROJ_TM
    n_prompt_tiles = N_PROMPT // tm
    per_batch = DEC_SEQ // tm
    n_q = N_HEADS * (QK_NOPE + QK_ROPE)

    def tab_blk(i):
        return jnp.where(i < n_prompt_tiles, per_batch, (i - n_prompt_tiles) % per_batch)

    tok = lambda i: (i, 0)
    const = lambda i: (0, 0)
    first_layer = not prev_caches
    assert first_layer == (layer == 0)
    cache_layers = DEPTH if first_layer else None
    cache_idx = lambda i: (jnp.minimum(i, n_prompt_tiles - 1), 0 if first_layer else layer, 0, 0)
    tab_spec = pl.BlockSpec((tm, HEAD_PAD), lambda i: (tab_blk(i), 0))
    tab_t_spec = pl.BlockSpec((HEAD_PAD, tm), lambda i: (0, tab_blk(i)))
    return pl.pallas_call(
        functools.partial(_proj_kernel, layer=layer, first_layer=first_layer),
        grid=(N_TOK // tm,),
        in_specs=[
            pl.BlockSpec((tm, D_MODEL), tok),
            _mod_spec(layer, MOD_MIX + 0, tm),
            _mod_spec(layer, MOD_MIX + 1, tm),
            pl.BlockSpec((DEPTH, D_MODEL), const),
            pl.BlockSpec((None, PROJ_W, D_MODEL), lambda i: (layer, 0, 0)),
            pl.BlockSpec((DEPTH, Q_RANK), const),
            pl.BlockSpec((None, Q_RANK, n_q), lambda i: (layer, 0, 0)),
            pl.BlockSpec((N_HEADS * (HEAD_PAD + QK_ROPE), n_q), const),
            pl.BlockSpec((DEPTH, KV_RANK), const),
            tab_t_spec, tab_t_spec, tab_spec, tab_spec,
        ] + [pl.BlockSpec(memory_space=pl.ANY)] * len(prev_caches),
        out_specs=[
            pl.BlockSpec((tm, D_FOUR), tok),
            pl.BlockSpec((N_HEADS * HEAD_PAD, tm), lambda i: (0, i)),
            pl.BlockSpec((tm, KV_RANK), tok),
            pl.BlockSpec((tm, HEAD_PAD), tok),
            pl.BlockSpec((PROJ_SEQS, cache_layers, SEQ, KV_RANK), cache_idx),
            pl.BlockSpec((PROJ_SEQS, cache_layers, QK_ROPE, SEQ), cache_idx),
        ],
        out_shape=[
            jax.ShapeDtypeStruct((N_TOK, D_FOUR), F32),
            jax.ShapeDtypeStruct((N_HEADS * HEAD_PAD, N_TOK), BF16),
            jax.ShapeDtypeStruct((N_TOK, KV_RANK), F32),
            jax.ShapeDtypeStruct((N_TOK, HEAD_PAD), BF16),
            jax.ShapeDtypeStruct((BATCH, DEPTH, SEQ, KV_RANK), F32),
            jax.ShapeDtypeStruct((BATCH, DEPTH, QK_ROPE, SEQ), F32),
        ],
        input_output_aliases={13 + k: 4 + k for k in range(len(prev_caches))},
        scratch_shapes=[
            pltpu.VMEM((PROJ_W, D_MODEL), BF16),
            pltpu.VMEM((N_HEADS * (HEAD_PAD + QK_ROPE), Q_RANK), BF16),
        ],
        compiler_params=pltpu.CompilerParams(
            dimension_semantics=("arbitrary",), vmem_limit_bytes=VMEM_LIMIT),
        name="mixer_proj",
    )(x, mod, mod, nw, w_in_t, qn, w_uq, _q_select(), kvn, *tabs(tm), *prev_caches)


DFT_ROWS = 1024


def _split(x):
    hi = x.astype(BF16)
    lo = (x - hi.astype(F32)).astype(BF16)
    return hi, lo


def _dot3(ah, al, bh, bl):
    return _dot(ah, bh) + _dot(al, bh) + _dot(ah, bl)


def _dft_kernel(x_ref, gh_ref, gl_ref, ch_ref, cl_ref, sh_ref, sl_ref, rev_ref, o_ref, *, n):
    half = n // 2
    xh, xl = _split(x_ref[...])
    y = _dot3(xh, xl, gh_ref[...], gl_ref[...])
    yh, yl = _split(y)
    ch, cl, sh, sl = ch_ref[...], cl_ref[...], sh_ref[...], sl_ref[...]
    pos = lax.broadcasted_iota(jnp.int32, (n, D_FOUR), 0)
    alt = jnp.where(pos % 2 == 0, 1.0, -1.0).astype(F32) * (float(n) ** -0.5)
    first = lax.broadcasted_iota(jnp.int32, (half, D_FOUR), 0) == 0
    seqs = [slice(b * n, (b + 1) * n) for b in range(x_ref.shape[0] // n)]
    parts = [(_dot3(ch, cl, yh[r, :D_FOUR], yl[r, :D_FOUR]),
              _dot3(sh, sl, yh[r, D_FOUR:], yl[r, D_FOUR:]))
             for r in seqs]
    for r, (a, bb) in zip(seqs, parts):
        o_ref[r.start:r.start + half, :] = (a - bb).astype(BF16)
        mirrored = _dot(rev_ref[...], (a + bb).astype(BF16))
        nyquist = jnp.sum(y[r, :D_FOUR] * alt, axis=0, keepdims=True)
        o_ref[r.start + half:r.stop, :] = jnp.where(first, nyquist, mirrored).astype(BF16)


def _split_np(a):
    a32 = np.asarray(a, np.float32)
    hi = a32.astype(BF16)
    lo = (a32 - hi.astype(np.float32)).astype(BF16)
    return hi, lo


def _dft_tables(n):
    half = n // 2
    k = np.arange(half, dtype=np.int64)
    pos = np.arange(n, dtype=np.int64)
    ang = 2.0 * np.pi * ((k[:, None] * pos[None, :]) % n).astype(np.float64) / n
    c = np.arange(FOUR_GROUP_DIM, dtype=np.int64)
    ang_c = 2.0 * np.pi * ((c[:, None] * c[None, :]) % FOUR_GROUP_DIM).astype(np.float64) / FOUR_GROUP_DIM
    eye = np.eye(FOUR_GROUPS)
    g = np.concatenate([np.kron(eye, np.cos(ang_c)), np.kron(eye, np.sin(ang_c))], axis=1)
    g = g / np.sqrt(FOUR_GROUP_DIM)
    rev = np.zeros((half, half), np.float32)
    rev[np.arange(1, half), half - np.arange(1, half)] = 1.0
    return (_split_np(g) + _split_np(np.cos(ang) / np.sqrt(n)) + _split_np(np.sin(ang) / np.sqrt(n))
            + (rev.astype(BF16),))


def _dft(f_in, n_batch, n, row0):
    tables = _dft_tables(n)
    rows = max(n, DFT_ROWS)
    blk0 = row0 // rows
    const = lambda b: (0, 0)
    half_spec = pl.BlockSpec((n // 2, n), const)
    return pl.pallas_call(
        functools.partial(_dft_kernel, n=n),
        grid=(n_batch * n // rows,),
        in_specs=[
            pl.BlockSpec((rows, D_FOUR), lambda b: (blk0 + b, 0)),
            pl.BlockSpec((D_FOUR, 2 * D_FOUR), const),
            pl.BlockSpec((D_FOUR, 2 * D_FOUR), const),
            half_spec, half_spec, half_spec, half_spec,
            pl.BlockSpec((n // 2, n // 2), const),
        ],
        out_specs=pl.BlockSpec((rows, D_FOUR), lambda b: (b, 0)),
        out_shape=jax.ShapeDtypeStruct((n_batch * n, D_FOUR), BF16),
        compiler_params=pltpu.CompilerParams(
            dimension_semantics=("arbitrary",), vmem_limit_bytes=VMEM_LIMIT),
        name="fnet_dft_%d" % n,
    )(f_in, *tables)


ATTN_TQ = 256
ATTN_SEQS = 4
ATTN_AHEAD = 3
ATTN_VALUE_LAG = 1
VT_ROWS = V_DIM + 16


def _attn_kernel(*refs, n_own, n_ctx, seqs):
    if n_ctx:
        qt_ref, ckv_ref, kr_ref, cckv_ref, ckr_ref, wkv_ref, pk_ref, pvt_ref = refs[:8]
    else:
        qt_ref, ckv_ref, kr_ref, wkv_ref, pk_ref, pvt_ref = refs[:6]
    o_ref, k_scr, vt_scr, wk_s, wvt_s = refs[-5:]

    @pl.when(jnp.logical_and(pl.program_id(0) == 0, pl.program_id(1) == 0))
    def _():
        wkv = wkv_ref[...].astype(BF16)
        wk_s[...] = _dot(wkv, pk_ref[...]).astype(BF16)
        wvt_s[...] = _dot_nt(pvt_ref[...], wkv).astype(BF16)

    def build_keys_values():
        ckv = ckv_ref[...].astype(BF16)
        kr = kr_ref[...]
        if n_ctx:
            ckv = jnp.concatenate([ckv, cckv_ref[...].astype(BF16)], axis=0)
            ckr = jnp.concatenate([ckr_ref[...], jnp.zeros((HEAD_PAD - QK_ROPE, n_ctx), F32)], axis=0)
            kr = jnp.concatenate([kr, ckr.T.astype(BF16)], axis=0)
        k_nope = _dot(ckv, wk_s[...])
        kr = kr.astype(F32)
        for hd in range(N_HEADS):
            lo = hd * HEAD_PAD
            k_scr[:, lo:lo + HEAD_PAD] = (k_nope[:, lo:lo + HEAD_PAD] + kr).astype(BF16)
        vt = _dot_nt(wvt_s[...], ckv).astype(BF16)
        pad_rows = lax.broadcasted_iota(jnp.int32, (VT_ROWS - V_DIM, vt.shape[1]), 0)
        ones_row = (pad_rows == 0).astype(BF16)
        for hd in range(N_HEADS):
            vt_scr[hd * VT_ROWS:hd * VT_ROWS + V_DIM, :] = vt[hd * V_DIM:(hd + 1) * V_DIM]
            vt_scr[hd * VT_ROWS + V_DIM:(hd + 1) * VT_ROWS, :] = ones_row

    if seqs > 1:
        build_keys_values()
    else:
        pl.when(pl.program_id(1) == 0)(build_keys_values)

    n_keys = n_own + n_ctx
    tq = qt_ref.shape[1] // seqs
    units = [(j, hd) for j in range(seqs) for hd in range(N_HEADS)]

    def scores(j, hd):
        lo = hd * HEAD_PAD
        return _dot(k_scr[j * n_keys:(j + 1) * n_keys, lo:lo + HEAD_PAD],
                    qt_ref[lo:lo + HEAD_PAD, j * tq:(j + 1) * tq])

    outs = []

    def values(j, hd, p):
        ov = _dot(vt_scr[hd * VT_ROWS:(hd + 1) * VT_ROWS, j * n_keys:(j + 1) * n_keys], p)
        outs.append(ov[:V_DIM] / ov[V_DIM:V_DIM + 1])
        if hd == N_HEADS - 1:
            o_ref[j * tq:(j + 1) * tq, :] = jnp.concatenate(outs, axis=0).T.astype(BF16)
            outs.clear()

    queue = [scores(*u) for u in units[:ATTN_AHEAD]]
    pending = []
    for i, unit in enumerate(units):
        s = queue.pop(0)
        if i + ATTN_AHEAD < len(units):
            queue.append(scores(*units[i + ATTN_AHEAD]))
        p = jnp.exp2(s - jnp.max(s, axis=0, keepdims=True)).astype(BF16)
        if len(pending) == ATTN_VALUE_LAG:
            values(*pending.pop(0))
        pending.append((*unit, p))
    for item in pending:
        values(*item)


def _kv_select():
    n_kv = N_HEADS * (QK_NOPE + V_DIM)
    pk = np.zeros((n_kv, N_HEADS * HEAD_PAD), np.float32)
    pvt = np.zeros((N_HEADS * V_DIM, n_kv), np.float32)
    for hd in range(N_HEADS):
        src = hd * (QK_NOPE + V_DIM)
        for j in range(QK_NOPE):
            pk[src + j, hd * HEAD_PAD + QK_ROPE + j] = 1.0
        for j in range(V_DIM):
            pvt[hd * V_DIM + j, src + QK_NOPE + j] = 1.0
    return pk.astype(BF16), pvt.astype(BF16)


def _attention(q, ckv_n, kr, w_ukv, layer, n_batch, n_own, row0, ctx=None):
    tq = min(n_own, ATTN_TQ)
    n_kv = N_HEADS * (QK_NOPE + V_DIM)
    n_ctx = 0 if ctx is None else PAST_LEN
    n_keys = n_own + n_ctx
    qt = n_own // tq
    seqs = ATTN_SEQS if (ctx is None and qt == 1) else 1
    blk0 = row0 // (seqs * n_own)
    const = lambda b, t: (0, 0)
    in_specs = [
        pl.BlockSpec((N_HEADS * HEAD_PAD, seqs * tq), lambda b, t: (0, (row0 // (seqs * tq)) + b * qt + t)),
        pl.BlockSpec((seqs * n_own, KV_RANK), lambda b, t: (blk0 + b, 0)),
        pl.BlockSpec((seqs * n_own, HEAD_PAD), lambda b, t: (blk0 + b, 0)),
    ]
    args = [q, ckv_n, kr]
    if ctx is not None:
        cckv, ckr = ctx
        in_specs += [
            pl.BlockSpec((None, None, PAST_LEN, KV_RANK), lambda b, t: (b, layer, 0, 0)),
            pl.BlockSpec((None, None, QK_ROPE, PAST_LEN), lambda b, t: (b, layer, 0, 0)),
        ]
        args += [cckv, ckr]
    in_specs += [
        pl.BlockSpec((None, KV_RANK, n_kv), lambda b, t: (layer, 0, 0)),
        pl.BlockSpec((n_kv, N_HEADS * HEAD_PAD), const),
        pl.BlockSpec((N_HEADS * V_DIM, n_kv), const),
    ]
    args += [w_ukv, *_kv_select()]
    return pl.pallas_call(
        functools.partial(_attn_kernel, n_own=n_own, n_ctx=n_ctx, seqs=seqs),
        grid=(n_batch // seqs, qt),
        in_specs=in_specs,
        out_specs=pl.BlockSpec((seqs * tq, N_HEADS * V_DIM), lambda b, t: (b * qt + t, 0)),
        out_shape=jax.ShapeDtypeStruct((n_batch * n_own, N_HEADS * V_DIM), BF16),
        scratch_shapes=[
            pltpu.VMEM((seqs * n_keys, N_HEADS * HEAD_PAD), BF16),
            pltpu.VMEM((N_HEADS * VT_ROWS, seqs * n_keys), BF16),
            pltpu.VMEM((KV_RANK, N_HEADS * HEAD_PAD), BF16),
            pltpu.VMEM((N_HEADS * V_DIM, KV_RANK), BF16),
        ],
        compiler_params=pltpu.CompilerParams(
            dimension_semantics=("arbitrary", "arbitrary"), vmem_limit_bytes=VMEM_LIMIT),
        name="mla_attention_%d" % n_keys,
    )(*args)


TAIL_TM = 1024
TAIL_GROUP = 256
TAIL_SLABS = 4
GATE_ROW = ROPE_COL + QK_ROPE
GATE_CHUNK = 512
GATE_CHUNKS = 2 * D_MODEL // GATE_CHUNK


def _tail_kernel(x_ref, sh_ref, sc_ref, g_ref, nw_ref, frp_ref, frs_ref, op_ref, os_ref, wi_hbm,
                 wf_ref, wa_ref, wo_ref, out_ref, wg_s, wf_s, wa_s, wo_s, stage, sem, *, layer):
    s = pl.program_id(0)
    is_prompt = jnp.maximum(s - TAIL_SLABS, 0) < N_PROMPT // TAIL_TM

    def gate_copy(c):
        src = wi_hbm.at[layer, pl.ds(GATE_ROW + c * GATE_CHUNK, GATE_CHUNK), :]
        return pltpu.make_async_copy(src, stage.at[c % 2], sem.at[c % 2])

    @pl.when(s == 0)
    def _():
        gate_copy(0).start()
        for c in range(GATE_CHUNKS):
            if c + 1 < GATE_CHUNKS:
                gate_copy(c + 1).start()
            gate_copy(c).wait()
            wg_s[:, c * GATE_CHUNK:(c + 1) * GATE_CHUNK] = stage[c % 2].T.astype(BF16)

    @pl.when(s < TAIL_SLABS)
    def _():
        def put(dst, src):
            rows = src.shape[0]
            dst[pl.ds(pl.multiple_of(s * rows, rows), rows), :] = src.astype(BF16)

        put(wf_s, wf_ref[...])
        put(wa_s, wa_ref[...])
        put(wo_s, wo_ref[...])

    @pl.when(s >= TAIL_SLABS)
    def _():
        groups = [slice(r * TAIL_GROUP, (r + 1) * TAIL_GROUP) for r in range(TAIL_TM // TAIL_GROUP)]

        def branches(rows):
            x = x_ref[rows, :]
            h = _norm_mod(x, nw_ref[layer:layer + 1, :], sc_ref[...], sh_ref[...]).astype(BF16)
            gates = _dot(h, wg_s[...])
            a_out = _dot(jnp.where(is_prompt, frp_ref[rows, :], frs_ref[rows, :]), wf_s[...])
            o_out = _dot(jnp.where(is_prompt, op_ref[rows, :], os_ref[rows, :]), wa_s[...])
            return x, gates, a_out, o_out

        def merge(rows, x, gates, a_out, o_out):
            merged = jax.nn.sigmoid(gates[:, :D_MODEL]) * a_out + jax.nn.sigmoid(gates[:, D_MODEL:]) * o_out
            m = _dot(merged.astype(BF16), wo_s[...])
            out_ref[rows, :] = x + g_ref[...] * m

        pending = branches(groups[0])
        for r in range(1, len(groups)):
            ready, pending = pending, branches(groups[r])
            merge(groups[r - 1], *ready)
        merge(groups[-1], *pending)


def _tail(x, mod, layer, nw, frs, os_, w_in_t, w_four, w_attn_proj, w_out):
    tm = TAIL_TM
    npt = N_PROMPT // tm
    tile = lambda s: jnp.maximum(s - TAIL_SLABS, 0)
    slab = lambda s: jnp.minimum(s, TAIL_SLABS - 1)
    tok = lambda s: (tile(s), 0)
    tok_p = lambda s: (jnp.minimum(tile(s), npt - 1), 0)
    tok_s = lambda s: (jnp.maximum(tile(s) - npt, 0), 0)
    n_attn = N_HEADS * V_DIM

    def slab_spec(rows, cols):
        return pl.BlockSpec((None, rows // TAIL_SLABS, cols), lambda s: (layer, slab(s), 0))

    return pl.pallas_call(
        functools.partial(_tail_kernel, layer=layer),
        grid=(TAIL_SLABS + N_TOK // tm,),
        in_specs=[
            pl.BlockSpec((tm, D_MODEL), tok),
            _mod_spec(layer, MOD_MIX + 0, tm, tile),
            _mod_spec(layer, MOD_MIX + 1, tm, tile),
            _mod_spec(layer, MOD_MIX + 2, tm, tile),
            pl.BlockSpec((DEPTH, D_MODEL), lambda s: (0, 0)),
            pl.BlockSpec((tm, D_FOUR), tok_p),
            pl.BlockSpec((tm, D_FOUR), tok_s),
            pl.BlockSpec((tm, n_attn), tok_p),
            pl.BlockSpec((tm, n_attn), tok_s),
            pl.BlockSpec(memory_space=pl.ANY),
            slab_spec(D_FOUR, D_MODEL),
            slab_spec(n_attn, D_MODEL),
            slab_spec(D_MODEL, D_MODEL),
        ],
        out_specs=pl.BlockSpec((tm, D_MODEL), tok),
        out_shape=jax.ShapeDtypeStruct((N_TOK, D_MODEL), F32),
        scratch_shapes=[
            pltpu.VMEM((D_MODEL, 2 * D_MODEL), BF16),
            pltpu.VMEM((D_FOUR, D_MODEL), BF16),
            pltpu.VMEM((n_attn, D_MODEL), BF16),
            pltpu.VMEM((D_MODEL, D_MODEL), BF16),
            pltpu.VMEM((2, GATE_CHUNK, D_MODEL), F32),
            pltpu.SemaphoreType.DMA((2,)),
        ],
        compiler_params=pltpu.CompilerParams(
            dimension_semantics=("arbitrary",), vmem_limit_bytes=VMEM_LIMIT),
        name="mixer_tail",
    )(x, mod, mod, mod, nw, *frs, *os_, w_in_t, w_four, w_attn_proj, w_out)


def _rope_tables(tm):
    rows = DEC_SEQ // GRID_W
    row = np.repeat(np.arange(rows), GRID_W).astype(np.float64)
    col = np.tile(np.arange(GRID_W), rows).astype(np.float64)
    inv = ROPE_BASE ** (-np.arange(0, ROPE_AXIS, 2, dtype=np.float64) / ROPE_AXIS)
    ar = row[:, None] * inv
    ac = col[:, None] * inv
    cr, sr, cc, sc = np.cos(ar), np.sin(ar), np.cos(ac), np.sin(ac)
    cos32 = np.concatenate([cr, cr, cc, cc], axis=1)
    sin32 = np.concatenate([-sr, sr, -sc, sc], axis=1)
    scale = np.float32((QK_NOPE + QK_ROPE) ** -0.5 * np.log2(np.e))

    def table(rope_part, nope_val, ident_rope):
        t = np.zeros((DEC_SEQ + tm, HEAD_PAD), np.float32)
        t[:DEC_SEQ, :QK_ROPE] = rope_part
        t[:DEC_SEQ, QK_ROPE:QK_ROPE + QK_NOPE] = nope_val
        t[DEC_SEQ:, :QK_ROPE] = ident_rope
        t[DEC_SEQ:, QK_ROPE:QK_ROPE + QK_NOPE] = nope_val
        return t

    cq = (table(cos32, 1.0, 1.0) * scale).T
    sq = (table(sin32, 0.0, 0.0) * scale).T
    ck = table(cos32, 0.0, 1.0)
    sk = table(sin32, 0.0, 0.0)
    return tuple(jnp.asarray(np.ascontiguousarray(t), F32) for t in (cq, sq, ck, sk))


def kernel(x_prompt, x_sample, cache_ckv, cache_krope, c, c_ctx, w_mod, b_mod, norm_ffn1, w_ffn1_gate,
           w_ffn1_up, w_ffn1_down, norm_mix, w_in, w_four, q_norm, w_uq, kv_norm, w_ukv, w_attn_proj,
           w_out, norm_ffn2, w_ffn2_gate, w_ffn2_up, w_ffn2_down, final_norm):
    xs = (x_prompt.reshape(N_PROMPT, D_MODEL), x_sample.reshape(N_SAMPLE, D_MODEL))
    c_all = jnp.concatenate(
        [c_ctx[None, :], c, jnp.zeros((MOD_ROWS - 1 - DEC_BATCH, D_MODEL), F32)], axis=0)
    mod = _modulation(c_all, w_mod, b_mod).reshape(DEPTH * MOD_ROWS * N_MOD, 1, D_MODEL)
    cache_kr = jnp.swapaxes(cache_krope, 2, 3)
    w_in_t = jnp.swapaxes(w_in, 1, 2)

    caches = ()
    for l in range(DEPTH):
        x = _ffn(xs, mod, l, MOD_FFN1, norm_ffn1, w_ffn1_gate, w_ffn1_up, w_ffn1_down, final_norm, False)
        f_in, q, ckv_n, kr, new_ckv, new_krope = _proj(
            x, mod, l, norm_mix, w_in_t, q_norm, w_uq, kv_norm, _rope_tables, caches)
        caches = (new_ckv, new_krope)
        frs = (_dft(f_in, BATCH, SEQ, 0), _dft(f_in, DEC_BATCH, DEC_SEQ, N_PROMPT))
        os_ = (_attention(q, ckv_n, kr, w_ukv, l, BATCH, SEQ, 0),
               _attention(q, ckv_n, kr, w_ukv, l, DEC_BATCH, DEC_SEQ, N_PROMPT, (cache_ckv, cache_kr)))
        x = _tail(x, mod, l, norm_mix, frs, os_, w_in_t, w_four, w_attn_proj, w_out)
        xs = _ffn((x,), mod, l, MOD_FFN2, norm_ffn2, w_ffn2_gate, w_ffn2_up, w_ffn2_down, final_norm,
                  l == DEPTH - 1)
        if l < DEPTH - 1:
            xs = (xs,)

    y_prompt = xs[0].reshape(BATCH, SEQ, D_MODEL)
    y_sample = xs[1].reshape(DEC_BATCH, DEC_SEQ, D_MODEL)
    return y_prompt, y_sample, caches[0], jnp.swapaxes(caches[1], 2, 3)
```

```python
import functools

import numpy as np
import jax
import jax.numpy as jnp
from jax import lax
from jax.experimental import pallas as pl
from jax.experimental.pallas import tpu as pltpu

D_MODEL = 1024
BATCH = 16
SEQ = 256
DEPTH = 2
DEC_BATCH = 4
DEC_SEQ = 1024
PAST_LEN = 512
GRID_W = 64
D_FF = 2816
FOUR_GROUPS = 4
FOUR_GROUP_DIM = 64
D_FOUR = FOUR_GROUPS * FOUR_GROUP_DIM
N_HEADS = 8
QK_NOPE = 64
QK_ROPE = 32
V_DIM = 64
Q_RANK = 384
KV_RANK = 256
N_MOD = 9
ROPE_BASE = 10000.0
EPS = 1e-6

N_PROMPT = BATCH * SEQ
N_SAMPLE = DEC_BATCH * DEC_SEQ
N_TOK = N_PROMPT + N_SAMPLE
MOD_ROWS = 8
MOD_FFN1, MOD_MIX, MOD_FFN2 = 0, 3, 6
HEAD_PAD = 128
ROPE_AXIS = QK_ROPE // 2
ROPE_PAIR = ROPE_AXIS // 2

VMEM_LIMIT = 52 * 1024 * 1024

F32 = jnp.float32
BF16 = jnp.bfloat16


def _dot(a, b):
    return jnp.dot(a, b, preferred_element_type=F32)


def _dot_nt(a, b):
    return lax.dot_general(a, b, (((1,), (1,)), ((), ())), preferred_element_type=F32)


def _mod_row(i, tm):
    n_prompt_tiles = N_PROMPT // tm
    per_batch = DEC_SEQ // tm
    return jnp.where(i < n_prompt_tiles, 0, 1 + (i - n_prompt_tiles) // per_batch)


def _mod_spec(layer, which, tm, tile_of_step=lambda i: i):
    def idx(*g):
        return ((layer * MOD_ROWS + _mod_row(tile_of_step(g[0]), tm)) * N_MOD + which, 0, 0)
    return pl.BlockSpec((None, 1, D_MODEL), idx)


def _norm_mod(x, nw, sc, sh):
    ms = jnp.mean(x * x, axis=-1, keepdims=True)
    y = x * lax.rsqrt(ms + EPS) * nw
    return y * (1.0 + sc) + sh


MOD_TN = 2304


def _mod_kernel(c_ref, w_ref, b_ref, o_ref):
    c = c_ref[...]
    a = (c * jax.nn.sigmoid(c)).astype(BF16)
    o_ref[...] = _dot(a, w_ref[...].astype(BF16)) + b_ref[pl.ds(pl.program_id(0), 1), :]


def _modulation(c_all, w_mod, b_mod):
    tn = MOD_TN
    n_out = N_MOD * D_MODEL
    return pl.pallas_call(
        _mod_kernel,
        grid=(DEPTH, n_out // tn),
        in_specs=[
            pl.BlockSpec((MOD_ROWS, D_MODEL), lambda l, j: (0, 0)),
            pl.BlockSpec((None, D_MODEL, tn), lambda l, j: (l, 0, j)),
            pl.BlockSpec((DEPTH, tn), lambda l, j: (0, j)),
        ],
        out_specs=pl.BlockSpec((None, MOD_ROWS, tn), lambda l, j: (l, 0, j)),
        out_shape=jax.ShapeDtypeStruct((DEPTH, MOD_ROWS, n_out), F32),
        compiler_params=pltpu.CompilerParams(
            dimension_semantics=("arbitrary", "arbitrary"), vmem_limit_bytes=VMEM_LIMIT),
        name="modulation",
    )(c_all, w_mod, b_mod)


FFN_TM = 512
FFN_CHUNK = 256
FFN_NC = D_FF // FFN_CHUNK


def _ffn_kernel(*refs, layer, final, split_in):
    refs = list(refs)
    x_refs = [refs.pop(0) for _ in range(2 if split_in else 1)]
    sh_ref, sc_ref, g_ref, nw_ref, wg_ref, wu_ref, wd_ref, fn_ref = refs[:8]
    o_refs = refs[8:10] if final else refs[8:9]
    wg_s, wu_s, wd_s, h_scr, acc_scr = refs[-5:]
    s = pl.program_id(0)
    is_prompt = jnp.maximum(s - (FFN_NC - 1), 0) < N_PROMPT // FFN_TM

    def x_tile():
        if split_in:
            return jnp.where(is_prompt, x_refs[0][...], x_refs[1][...])
        return x_refs[0][...]

    def chunk_act(h, wg, wu):
        gate = _dot(h, wg)
        up = _dot(h, wu)
        return ((gate * jax.nn.sigmoid(gate)) * up).astype(BF16)

    def finish(acc):
        xn = x_tile() + 0.5 * g_ref[...] * acc
        if not final:
            o_refs[0][...] = xn
            return
        ms = jnp.mean(xn * xn, axis=-1, keepdims=True)
        y = xn * lax.rsqrt(ms + EPS) * fn_ref[...]

        @pl.when(is_prompt)
        def _():
            o_refs[0][...] = y

        @pl.when(jnp.logical_not(is_prompt))
        def _():
            o_refs[1][...] = y

    def hidden():
        return _norm_mod(x_tile(), nw_ref[layer:layer + 1, :], sc_ref[...], sh_ref[...]).astype(BF16)

    @pl.when(s == 0)
    def _():
        h_scr[...] = hidden()
        acc_scr[...] = jnp.zeros_like(acc_scr)

    @pl.when(s < FFN_NC)
    def _():
        wg = wg_ref[...].astype(BF16)
        wu = wu_ref[...].astype(BF16)
        wd = wd_ref[...].astype(BF16)
        wg_s[s] = wg
        wu_s[s] = wu
        wd_s[pl.ds(pl.multiple_of(s * FFN_CHUNK, FFN_CHUNK), FFN_CHUNK), :] = wd
        acc_scr[...] += _dot(chunk_act(h_scr[...], wg, wu), wd)

    @pl.when(s == FFN_NC - 1)
    def _():
        finish(acc_scr[...])

    @pl.when(s >= FFN_NC)
    def _():
        h = hidden()
        act = jnp.concatenate([chunk_act(h, wg_s[j], wu_s[j]) for j in range(FFN_NC)], axis=1)
        finish(_dot(act, wd_s[...]))


def _ffn(xs, mod, layer, mod_base, nw, wg, wu, wd, final_norm, final):
    tm = FFN_TM
    npt = N_PROMPT // tm
    split_in = len(xs) == 2
    tile = lambda s: jnp.maximum(s - (FFN_NC - 1), 0)
    chunk = lambda s: jnp.minimum(s, FFN_NC - 1)
    tok = pl.BlockSpec((tm, D_MODEL), lambda s: (tile(s), 0))
    tok_p = pl.BlockSpec((tm, D_MODEL), lambda s: (jnp.minimum(tile(s), npt - 1), 0))
    tok_s = pl.BlockSpec((tm, D_MODEL), lambda s: (jnp.maximum(tile(s) - npt, 0), 0))
    half = jax.ShapeDtypeStruct((N_PROMPT, D_MODEL), F32)
    return pl.pallas_call(
        functools.partial(_ffn_kernel, layer=layer, final=final, split_in=split_in),
        grid=(FFN_NC + N_TOK // tm - 1,),
        in_specs=([tok_p, tok_s] if split_in else [tok]) + [
            _mod_spec(layer, mod_base + 0, tm, tile),
            _mod_spec(layer, mod_base + 1, tm, tile),
            _mod_spec(layer, mod_base + 2, tm, tile),
            pl.BlockSpec((DEPTH, D_MODEL), lambda s: (0, 0)),
            pl.BlockSpec((None, D_MODEL, FFN_CHUNK), lambda s: (layer, 0, chunk(s))),
            pl.BlockSpec((None, D_MODEL, FFN_CHUNK), lambda s: (layer, 0, chunk(s))),
            pl.BlockSpec((None, FFN_CHUNK, D_MODEL), lambda s: (layer, chunk(s), 0)),
            pl.BlockSpec((1, D_MODEL), lambda s: (0, 0)),
        ],
        out_specs=[tok_p, tok_s] if final else tok,
        out_shape=[half, half] if final else jax.ShapeDtypeStruct((N_TOK, D_MODEL), F32),
        scratch_shapes=[
            pltpu.VMEM((FFN_NC, D_MODEL, FFN_CHUNK), BF16),
            pltpu.VMEM((FFN_NC, D_MODEL, FFN_CHUNK), BF16),
            pltpu.VMEM((D_FF, D_MODEL), BF16),
            pltpu.VMEM((tm, D_MODEL), BF16),
            pltpu.VMEM((tm, D_MODEL), F32),
        ],
        compiler_params=pltpu.CompilerParams(
            dimension_semantics=("arbitrary",), vmem_limit_bytes=VMEM_LIMIT),
        name="ffn_final" if final else ("ffn_first" if split_in else "ffn"),
    )(*xs, mod, mod, mod, nw, wg, wu, wd, final_norm.reshape(1, D_MODEL))


PROJ_TM = 1024
PROJ_SEQS = PROJ_TM // SEQ
PROJ_W = 1024
ROPE_COL = D_FOUR + Q_RANK + KV_RANK


def _pair_swap(x):
    lane = lax.broadcasted_iota(jnp.int32, x.shape, 1)
    first = lane % ROPE_AXIS < ROPE_PAIR
    return jnp.where(first, pltpu.roll(x, HEAD_PAD - ROPE_PAIR, 1), pltpu.roll(x, ROPE_PAIR, 1))


def _proj_kernel(*refs, layer, first_layer):
    (x_ref, sh_ref, sc_ref, nw_ref, w1_ref, qn_ref, wq_ref, pq_ref, kvn_ref,
     cq_ref, sq_ref, ck_ref, sk_ref) = refs[:13]
    f_ref, q_ref, ckv_ref, kr_ref, nckv_ref, nkr_ref, w1_s, wq_s = refs[-8:]
    is_prompt = pl.program_id(0) < N_PROMPT // PROJ_TM

    @pl.when(pl.program_id(0) == 0)
    def _():
        w1_s[...] = w1_ref[...].T.astype(BF16)
        wq_s[...] = _dot_nt(pq_ref[...], wq_ref[...].astype(BF16)).astype(BF16)

    n_qp = N_HEADS * HEAD_PAD
    c0 = D_FOUR + Q_RANK
    groups = [slice(r * SEQ, (r + 1) * SEQ) for r in range(PROJ_SEQS)]
    ps = []
    for rows in groups:
        h = _norm_mod(x_ref[rows, :], nw_ref[layer:layer + 1, :], sc_ref[...], sh_ref[...]).astype(BF16)
        ps.append(_dot(h, w1_s[...]))
    staged = []
    for rows, p in zip(groups, ps):
        f_ref[rows, :] = p[:, :D_FOUR]
        ql = p[:, D_FOUR:c0]
        qn = ql * lax.rsqrt(jnp.mean(ql * ql, axis=-1, keepdims=True) + EPS) * qn_ref[layer:layer + 1, :]
        qn = qn.astype(BF16)
        qq = _dot_nt(wq_s[:n_qp, :], qn)
        ckv = p[:, c0:c0 + KV_RANK]
        ckv_n = ckv * lax.rsqrt(jnp.mean(ckv * ckv, axis=-1, keepdims=True) + EPS) * kvn_ref[layer:layer + 1, :]
        ckv_ref[rows, :] = ckv_n
        kr = p[:, ROPE_COL:ROPE_COL + HEAD_PAD]
        kr = jnp.where(lax.broadcasted_iota(jnp.int32, kr.shape, 1) < QK_ROPE, kr, 0.0)
        staged.append((qn, qq, ckv_n, kr))

    @pl.when(is_prompt)
    def _():
        for r, (rows, (qn, qq, ckv_n, kr)) in enumerate(zip(groups, staged)):
            cq = cq_ref[:, rows]
            for hd in range(N_HEADS):
                lo = hd * HEAD_PAD
                q_ref[lo:lo + HEAD_PAD, rows] = (qq[lo:lo + HEAD_PAD, :] * cq).astype(BF16)
            kr_ref[rows, :] = (kr * ck_ref[rows, :]).astype(BF16)
            if first_layer:
                nckv_ref[r, 0] = ckv_n
                nkr_ref[r, 0] = kr.T[:QK_ROPE]
                for later in range(1, DEPTH):
                    nckv_ref[r, later] = jnp.zeros_like(ckv_n)
                    nkr_ref[r, later] = jnp.zeros((QK_ROPE, SEQ), F32)
            else:
                nckv_ref[r] = ckv_n
                nkr_ref[r] = kr.T[:QK_ROPE]

    @pl.when(jnp.logical_not(is_prompt))
    def _():
        for rows, (qn, qq, ckv_n, kr) in zip(groups, staged):
            cq, sq = cq_ref[:, rows], sq_ref[:, rows]
            qs = _dot_nt(wq_s[n_qp:, :], qn)
            for hd in range(N_HEADS):
                lo, mid = hd * HEAD_PAD, hd * HEAD_PAD + QK_ROPE
                rope = qq[lo:mid, :] * cq[:QK_ROPE] + qs[hd * QK_ROPE:(hd + 1) * QK_ROPE, :] * sq[:QK_ROPE]
                q_ref[lo:mid, rows] = rope.astype(BF16)
                q_ref[mid:lo + HEAD_PAD, rows] = (qq[mid:lo + HEAD_PAD, :] * cq[QK_ROPE:]).astype(BF16)
            kr_ref[rows, :] = (kr * ck_ref[rows, :] + _pair_swap(kr) * sk_ref[rows, :]).astype(BF16)


def _q_select():
    n_qp = N_HEADS * HEAD_PAD
    p = np.zeros((n_qp + N_HEADS * QK_ROPE, N_HEADS * (QK_NOPE + QK_ROPE)), np.float32)
    for hd in range(N_HEADS):
        src, dst = hd * (QK_NOPE + QK_ROPE), hd * HEAD_PAD
        for j in range(QK_ROPE):
            partner = j + ROPE_PAIR if j % ROPE_AXIS < ROPE_PAIR else j - ROPE_PAIR
            p[dst + j, src + QK_NOPE + j] = 1.0
            p[n_qp + hd * QK_ROPE + j, src + QK_NOPE + partner] = 1.0
        for j in range(QK_NOPE):
            p[dst + QK_ROPE + j, src + j] = 1.0
    return p.astype(BF16)


def _proj(x, mod, layer, nw, w_in_t, qn, w_uq, kvn, tabs, prev_caches):
    tm = PROJ_TM
    n_prompt_tiles = N_PROMPT // tm
    per_batch = DEC_SEQ // tm
    n_q = N_HEADS * (QK_NOPE + QK_ROPE)

    def tab_blk(i):
        return jnp.where(i < n_prompt_tiles, per_batch, (i - n_prompt_tiles) % per_batch)

    tok = lambda i: (i, 0)
    const = lambda i: (0, 0)
    first_layer = not prev_caches
    assert first_layer == (layer == 0)
    cache_layers = DEPTH if first_layer else None
    cache_idx = lambda i: (jnp.minimum(i, n_prompt_tiles - 1), 0 if first_layer else layer, 0, 0)
    tab_spec = pl.BlockSpec((tm, HEAD_PAD), lambda i: (tab_blk(i), 0))
    tab_t_spec = pl.BlockSpec((HEAD_PAD, tm), lambda i: (0, tab_blk(i)))
    return pl.pallas_call(
        functools.partial(_proj_kernel, layer=layer, first_layer=first_layer),
        grid=(N_TOK // tm,),
        in_specs=[
            pl.BlockSpec((tm, D_MODEL), tok),
            _mod_spec(layer, MOD_MIX + 0, tm),
            _mod_spec(layer, MOD_MIX + 1, tm),
            pl.BlockSpec((DEPTH, D_MODEL), const),
            pl.BlockSpec((None, PROJ_W, D_MODEL), lambda i: (layer, 0, 0)),
            pl.BlockSpec((DEPTH, Q_RANK), const),
            pl.BlockSpec((None, Q_RANK, n_q), lambda i: (layer, 0, 0)),
            pl.BlockSpec((N_HEADS * (HEAD_PAD + QK_ROPE), n_q), const),
            pl.BlockSpec((DEPTH, KV_RANK), const),
            tab_t_spec, tab_t_spec, tab_spec, tab_spec,
        ] + [pl.BlockSpec(memory_space=pl.ANY)] * len(prev_caches),
        out_specs=[
            pl.BlockSpec((tm, D_FOUR), tok),
            pl.BlockSpec((N_HEADS * HEAD_PAD, tm), lambda i: (0, i)),
            pl.BlockSpec((tm, KV_RANK), tok),
            pl.BlockSpec((tm, HEAD_PAD), tok),
            pl.BlockSpec((PROJ_SEQS, cache_layers, SEQ, KV_RANK), cache_idx),
            pl.BlockSpec((PROJ_SEQS, cache_layers, QK_ROPE, SEQ), cache_idx),
        ],
        out_shape=[
            jax.ShapeDtypeStruct((N_TOK, D_FOUR), F32),
            jax.ShapeDtypeStruct((N_HEADS * HEAD_PAD, N_TOK), BF16),
            jax.ShapeDtypeStruct((N_TOK, KV_RANK), F32),
            jax.ShapeDtypeStruct((N_TOK, HEAD_PAD), BF16),
            jax.ShapeDtypeStruct((BATCH, DEPTH, SEQ, KV_RANK), F32),
            jax.ShapeDtypeStruct((BATCH, DEPTH, QK_ROPE, SEQ), F32),
        ],
        input_output_aliases={13 + k: 4 + k for k in range(len(prev_caches))},
        scratch_shapes=[
            pltpu.VMEM((PROJ_W, D_MODEL), BF16),
            pltpu.VMEM((N_HEADS * (HEAD_PAD + QK_ROPE), Q_RANK), BF16),
        ],
        compiler_params=pltpu.CompilerParams(
            dimension_semantics=("arbitrary",), vmem_limit_bytes=VMEM_LIMIT),
        name="mixer_proj",
    )(x, mod, mod, nw, w_in_t, qn, w_uq, _q_select(), kvn, *tabs(tm), *prev_caches)


DFT_ROWS = 1024


def _split(x):
    hi = x.astype(BF16)
    lo = (x - hi.astype(F32)).astype(BF16)
    return hi, lo


def _dot3(ah, al, bh, bl):
    return _dot(ah, bh) + _dot(al, bh) + _dot(ah, bl)


def _dft_kernel(x_ref, gh_ref, gl_ref, ch_ref, cl_ref, sh_ref, sl_ref, rev_ref, o_ref, *, n):
    half = n // 2
    xh, xl = _split(x_ref[...])
    y = _dot3(xh, xl, gh_ref[...], gl_ref[...])
    yh, yl = _split(y)
    ch, cl, sh, sl = ch_ref[...], cl_ref[...], sh_ref[...], sl_ref[...]
    pos = lax.broadcasted_iota(jnp.int32, (n, D_FOUR), 0)
    alt = jnp.where(pos % 2 == 0, 1.0, -1.0).astype(F32) * (float(n) ** -0.5)
    first = lax.broadcasted_iota(jnp.int32, (half, D_FOUR), 0) == 0
    seqs = [slice(b * n, (b + 1) * n) for b in range(x_ref.shape[0] // n)]
    parts = [(_dot3(ch, cl, yh[r, :D_FOUR], yl[r, :D_FOUR]),
              _dot3(sh, sl, yh[r, D_FOUR:], yl[r, D_FOUR:]))
             for r in seqs]
    for r, (a, bb) in zip(seqs, parts):
        o_ref[r.start:r.start + half, :] = (a - bb).astype(BF16)
        mirrored = _dot(rev_ref[...], (a + bb).astype(BF16))
        nyquist = jnp.sum(y[r, :D_FOUR] * alt, axis=0, keepdims=True)
        o_ref[r.start + half:r.stop, :] = jnp.where(first, nyquist, mirrored).astype(BF16)


def _split_np(a):
    a32 = np.asarray(a, np.float32)
    hi = a32.astype(BF16)
    lo = (a32 - hi.astype(np.float32)).astype(BF16)
    return hi, lo


def _dft_tables(n):
    half = n // 2
    k = np.arange(half, dtype=np.int64)
    pos = np.arange(n, dtype=np.int64)
    ang = 2.0 * np.pi * ((k[:, None] * pos[None, :]) % n).astype(np.float64) / n
    c = np.arange(FOUR_GROUP_DIM, dtype=np.int64)
    ang_c = 2.0 * np.pi * ((c[:, None] * c[None, :]) % FOUR_GROUP_DIM).astype(np.float64) / FOUR_GROUP_DIM
    eye = np.eye(FOUR_GROUPS)
    g = np.concatenate([np.kron(eye, np.cos(ang_c)), np.kron(eye, np.sin(ang_c))], axis=1)
    g = g / np.sqrt(FOUR_GROUP_DIM)
    rev = np.zeros((half, half), np.float32)
    rev[np.arange(1, half), half - np.arange(1, half)] = 1.0
    return (_split_np(g) + _split_np(np.cos(ang) / np.sqrt(n)) + _split_np(np.sin(ang) / np.sqrt(n))
            + (rev.astype(BF16),))


def _dft(f_in, n_batch, n, row0):
    tables = _dft_tables(n)
    rows = max(n, DFT_ROWS)
    blk0 = row0 // rows
    const = lambda b: (0, 0)
    half_spec = pl.BlockSpec((n // 2, n), const)
    return pl.pallas_call(
        functools.partial(_dft_kernel, n=n),
        grid=(n_batch * n // rows,),
        in_specs=[
            pl.BlockSpec((rows, D_FOUR), lambda b: (blk0 + b, 0)),
            pl.BlockSpec((D_FOUR, 2 * D_FOUR), const),
            pl.BlockSpec((D_FOUR, 2 * D_FOUR), const),
            half_spec, half_spec, half_spec, half_spec,
            pl.BlockSpec((n // 2, n // 2), const),
        ],
        out_specs=pl.BlockSpec((rows, D_FOUR), lambda b: (b, 0)),
        out_shape=jax.ShapeDtypeStruct((n_batch * n, D_FOUR), BF16),
        compiler_params=pltpu.CompilerParams(
            dimension_semantics=("arbitrary",), vmem_limit_bytes=VMEM_LIMIT),
        name="fnet_dft_%d" % n,
    )(f_in, *tables)


ATTN_TQ = 256
ATTN_SEQS = 4
ATTN_AHEAD = 3
ATTN_VALUE_LAG = 1
VT_ROWS = V_DIM + 16


def _attn_kernel(*refs, n_own, n_ctx, tiles, shared):
    if n_ctx:
        qt_ref, ckv_ref, kr_ref, cckv_ref, ckr_ref, wkv_ref, pk_ref, pvt_ref = refs[:8]
    else:
        qt_ref, ckv_ref, kr_ref, wkv_ref, pk_ref, pvt_ref = refs[:6]
    o_ref, k_scr, vt_scr, wk_s, wvt_s = refs[-5:]

    @pl.when(pl.program_id(0) == 0)
    def _():
        wkv = wkv_ref[...].astype(BF16)
        wk_s[...] = _dot(wkv, pk_ref[...]).astype(BF16)
        wvt_s[...] = _dot_nt(pvt_ref[...], wkv).astype(BF16)

    def build_keys_values():
        ckv = ckv_ref[...].astype(BF16)
        kr = kr_ref[...]
        if n_ctx:
            ckv = jnp.concatenate([ckv, cckv_ref[...].astype(BF16)], axis=0)
            ckr = jnp.concatenate([ckr_ref[...], jnp.zeros((HEAD_PAD - QK_ROPE, n_ctx), F32)], axis=0)
            kr = jnp.concatenate([kr, ckr.T.astype(BF16)], axis=0)
        k_nope = _dot(ckv, wk_s[...])
        kr = kr.astype(F32)
        for hd in range(N_HEADS):
            lo = hd * HEAD_PAD
            k_scr[:, lo:lo + HEAD_PAD] = (k_nope[:, lo:lo + HEAD_PAD] + kr).astype(BF16)
        vt = _dot_nt(wvt_s[...], ckv).astype(BF16)
        pad_rows = lax.broadcasted_iota(jnp.int32, (VT_ROWS - V_DIM, vt.shape[1]), 0)
        ones_row = (pad_rows == 0).astype(BF16)
        for hd in range(N_HEADS):
            vt_scr[hd * VT_ROWS:hd * VT_ROWS + V_DIM, :] = vt[hd * V_DIM:(hd + 1) * V_DIM]
            vt_scr[hd * VT_ROWS + V_DIM:(hd + 1) * VT_ROWS, :] = ones_row

    build_keys_values()

    n_keys = n_own + n_ctx
    tq = qt_ref.shape[1] // tiles
    units = [(j, hd) for j in range(tiles) for hd in range(N_HEADS)]

    def keys_of(j):
        return slice(0, n_keys) if shared else slice(j * n_keys, (j + 1) * n_keys)

    def scores(j, hd):
        lo = hd * HEAD_PAD
        return _dot(k_scr[keys_of(j), lo:lo + HEAD_PAD],
                    qt_ref[lo:lo + HEAD_PAD, j * tq:(j + 1) * tq])

    outs = []

    def values(j, hd, p):
        ov = _dot(vt_scr[hd * VT_ROWS:(hd + 1) * VT_ROWS, keys_of(j)], p)
        outs.append(ov[:V_DIM] / ov[V_DIM:V_DIM + 1])
        if hd == N_HEADS - 1:
            o_ref[j * tq:(j + 1) * tq, :] = jnp.concatenate(outs, axis=0).T.astype(BF16)
            outs.clear()

    queue = [scores(*u) for u in units[:ATTN_AHEAD]]
    pending = []
    for i, unit in enumerate(units):
        s = queue.pop(0)
        if i + ATTN_AHEAD < len(units):
            queue.append(scores(*units[i + ATTN_AHEAD]))
        p = jnp.exp2(s - jnp.max(s, axis=0, keepdims=True)).astype(BF16)
        if len(pending) == ATTN_VALUE_LAG:
            values(*pending.pop(0))
        pending.append((*unit, p))
    for item in pending:
        values(*item)


def _kv_select():
    n_kv = N_HEADS * (QK_NOPE + V_DIM)
    pk = np.zeros((n_kv, N_HEADS * HEAD_PAD), np.float32)
    pvt = np.zeros((N_HEADS * V_DIM, n_kv), np.float32)
    for hd in range(N_HEADS):
        src = hd * (QK_NOPE + V_DIM)
        for j in range(QK_NOPE):
            pk[src + j, hd * HEAD_PAD + QK_ROPE + j] = 1.0
        for j in range(V_DIM):
            pvt[hd * V_DIM + j, src + QK_NOPE + j] = 1.0
    return pk.astype(BF16), pvt.astype(BF16)


def _attention(q, ckv_n, kr, w_ukv, layer, n_batch, n_own, row0, ctx=None):
    tq = min(n_own, ATTN_TQ)
    n_kv = N_HEADS * (QK_NOPE + V_DIM)
    n_ctx = 0 if ctx is None else PAST_LEN
    n_keys = n_own + n_ctx
    shared = n_own > tq
    assert shared or ctx is None
    seqs = 1 if shared else ATTN_SEQS
    tiles = n_own // tq if shared else seqs
    rows = seqs * n_own
    blk0 = row0 // rows
    const = lambda b: (0, 0)
    in_specs = [
        pl.BlockSpec((N_HEADS * HEAD_PAD, rows), lambda b: (0, blk0 + b)),
        pl.BlockSpec((rows, KV_RANK), lambda b: (blk0 + b, 0)),
        pl.BlockSpec((rows, HEAD_PAD), lambda b: (blk0 + b, 0)),
    ]
    args = [q, ckv_n, kr]
    if ctx is not None:
        cckv, ckr = ctx
        in_specs += [
            pl.BlockSpec((None, None, PAST_LEN, KV_RANK), lambda b: (b, layer, 0, 0)),
            pl.BlockSpec((None, None, QK_ROPE, PAST_LEN), lambda b: (b, layer, 0, 0)),
        ]
        args += [cckv, ckr]
    in_specs += [
        pl.BlockSpec((None, KV_RANK, n_kv), lambda b: (layer, 0, 0)),
        pl.BlockSpec((n_kv, N_HEADS * HEAD_PAD), const),
        pl.BlockSpec((N_HEADS * V_DIM, n_kv), const),
    ]
    args += [w_ukv, *_kv_select()]
    return pl.pallas_call(
        functools.partial(_attn_kernel, n_own=n_own, n_ctx=n_ctx, tiles=tiles, shared=shared),
        grid=(n_batch // seqs,),
        in_specs=in_specs,
        out_specs=pl.BlockSpec((rows, N_HEADS * V_DIM), lambda b: (b, 0)),
        out_shape=jax.ShapeDtypeStruct((n_batch * n_own, N_HEADS * V_DIM), BF16),
        scratch_shapes=[
            pltpu.VMEM((seqs * n_keys, N_HEADS * HEAD_PAD), BF16),
            pltpu.VMEM((N_HEADS * VT_ROWS, seqs * n_keys), BF16),
            pltpu.VMEM((KV_RANK, N_HEADS * HEAD_PAD), BF16),
            pltpu.VMEM((N_HEADS * V_DIM, KV_RANK), BF16),
        ],
        compiler_params=pltpu.CompilerParams(
            dimension_semantics=("arbitrary",), vmem_limit_bytes=VMEM_LIMIT),
        name="mla_attention_%d" % n_keys,
    )(*args)


TAIL_TM = 1024
TAIL_GROUP = 256
TAIL_SLABS = 4
GATE_ROW = ROPE_COL + QK_ROPE
GATE_CHUNK = 512
GATE_CHUNKS = 2 * D_MODEL // GATE_CHUNK


def _tail_kernel(x_ref, sh_ref, sc_ref, g_ref, nw_ref, frp_ref, frs_ref, op_ref, os_ref, wi_hbm,
                 wf_ref, wa_ref, wo_ref, out_ref, wg_s, wf_s, wa_s, wo_s, stage, sem, *, layer):
    s = pl.program_id(0)
    is_prompt = jnp.maximum(s - TAIL_SLABS, 0) < N_PROMPT // TAIL_TM

    def gate_copy(c):
        src = wi_hbm.at[layer, pl.ds(GATE_ROW + c * GATE_CHUNK, GATE_CHUNK), :]
        return pltpu.make_async_copy(src, stage.at[c % 2], sem.at[c % 2])

    @pl.when(s == 0)
    def _():
        gate_copy(0).start()
        for c in range(GATE_CHUNKS):
            if c + 1 < GATE_CHUNKS:
                gate_copy(c + 1).start()
            gate_copy(c).wait()
            wg_s[:, c * GATE_CHUNK:(c + 1) * GATE_CHUNK] = stage[c % 2].T.astype(BF16)

    @pl.when(s < TAIL_SLABS)
    def _():
        def put(dst, src):
            rows = src.shape[0]
            dst[pl.ds(pl.multiple_of(s * rows, rows), rows), :] = src.astype(BF16)

        put(wf_s, wf_ref[...])
        put(wa_s, wa_ref[...])
        put(wo_s, wo_ref[...])

    @pl.when(s >= TAIL_SLABS)
    def _():
        groups = [slice(r * TAIL_GROUP, (r + 1) * TAIL_GROUP) for r in range(TAIL_TM // TAIL_GROUP)]

        def branches(rows):
            x = x_ref[rows, :]
            h = _norm_mod(x, nw_ref[layer:layer + 1, :], sc_ref[...], sh_ref[...]).astype(BF16)
            gates = _dot(h, wg_s[...])
            a_out = _dot(jnp.where(is_prompt, frp_ref[rows, :], frs_ref[rows, :]), wf_s[...])
            o_out = _dot(jnp.where(is_prompt, op_ref[rows, :], os_ref[rows, :]), wa_s[...])
            return x, gates, a_out, o_out

        def merge(rows, x, gates, a_out, o_out):
            merged = jax.nn.sigmoid(gates[:, :D_MODEL]) * a_out + jax.nn.sigmoid(gates[:, D_MODEL:]) * o_out
            m = _dot(merged.astype(BF16), wo_s[...])
            out_ref[rows, :] = x + g_ref[...] * m

        pending = branches(groups[0])
        for r in range(1, len(groups)):
            ready, pending = pending, branches(groups[r])
            merge(groups[r - 1], *ready)
        merge(groups[-1], *pending)


def _tail(x, mod, layer, nw, frs, os_, w_in_t, w_four, w_attn_proj, w_out):
    tm = TAIL_TM
    npt = N_PROMPT // tm
    tile = lambda s: jnp.maximum(s - TAIL_SLABS, 0)
    slab = lambda s: jnp.minimum(s, TAIL_SLABS - 1)
    tok = lambda s: (tile(s), 0)
    tok_p = lambda s: (jnp.minimum(tile(s), npt - 1), 0)
    tok_s = lambda s: (jnp.maximum(tile(s) - npt, 0), 0)
    n_attn = N_HEADS * V_DIM

    def slab_spec(rows, cols):
        return pl.BlockSpec((None, rows // TAIL_SLABS, cols), lambda s: (layer, slab(s), 0))

    return pl.pallas_call(
        functools.partial(_tail_kernel, layer=layer),
        grid=(TAIL_SLABS + N_TOK // tm,),
        in_specs=[
            pl.BlockSpec((tm, D_MODEL), tok),
            _mod_spec(layer, MOD_MIX + 0, tm, tile),
            _mod_spec(layer, MOD_MIX + 1, tm, tile),
            _mod_spec(layer, MOD_MIX + 2, tm, tile),
            pl.BlockSpec((DEPTH, D_MODEL), lambda s: (0, 0)),
            pl.BlockSpec((tm, D_FOUR), tok_p),
            pl.BlockSpec((tm, D_FOUR), tok_s),
            pl.BlockSpec((tm, n_attn), tok_p),
            pl.BlockSpec((tm, n_attn), tok_s),
            pl.BlockSpec(memory_space=pl.ANY),
            slab_spec(D_FOUR, D_MODEL),
            slab_spec(n_attn, D_MODEL),
            slab_spec(D_MODEL, D_MODEL),
        ],
        out_specs=pl.BlockSpec((tm, D_MODEL), tok),
        out_shape=jax.ShapeDtypeStruct((N_TOK, D_MODEL), F32),
        scratch_shapes=[
            pltpu.VMEM((D_MODEL, 2 * D_MODEL), BF16),
            pltpu.VMEM((D_FOUR, D_MODEL), BF16),
            pltpu.VMEM((n_attn, D_MODEL), BF16),
            pltpu.VMEM((D_MODEL, D_MODEL), BF16),
            pltpu.VMEM((2, GATE_CHUNK, D_MODEL), F32),
            pltpu.SemaphoreType.DMA((2,)),
        ],
        compiler_params=pltpu.CompilerParams(
            dimension_semantics=("arbitrary",), vmem_limit_bytes=VMEM_LIMIT),
        name="mixer_tail",
    )(x, mod, mod, mod, nw, *frs, *os_, w_in_t, w_four, w_attn_proj, w_out)


def _rope_tables(tm):
    rows = DEC_SEQ // GRID_W
    row = np.repeat(np.arange(rows), GRID_W).astype(np.float64)
    col = np.tile(np.arange(GRID_W), rows).astype(np.float64)
    inv = ROPE_BASE ** (-np.arange(0, ROPE_AXIS, 2, dtype=np.float64) / ROPE_AXIS)
    ar = row[:, None] * inv
    ac = col[:, None] * inv
    cr, sr, cc, sc = np.cos(ar), np.sin(ar), np.cos(ac), np.sin(ac)
    cos32 = np.concatenate([cr, cr, cc, cc], axis=1)
    sin32 = np.concatenate([-sr, sr, -sc, sc], axis=1)
    scale = np.float32((QK_NOPE + QK_ROPE) ** -0.5 * np.log2(np.e))

    def table(rope_part, nope_val, ident_rope):
        t = np.zeros((DEC_SEQ + tm, HEAD_PAD), np.float32)
        t[:DEC_SEQ, :QK_ROPE] = rope_part
        t[:DEC_SEQ, QK_ROPE:QK_ROPE + QK_NOPE] = nope_val
        t[DEC_SEQ:, :QK_ROPE] = ident_rope
        t[DEC_SEQ:, QK_ROPE:QK_ROPE + QK_NOPE] = nope_val
        return t

    cq = (table(cos32, 1.0, 1.0) * scale).T
    sq = (table(sin32, 0.0, 0.0) * scale).T
    ck = table(cos32, 0.0, 1.0)
    sk = table(sin32, 0.0, 0.0)
    return tuple(jnp.asarray(np.ascontiguousarray(t), F32) for t in (cq, sq, ck, sk))


def kernel(x_prompt, x_sample, cache_ckv, cache_krope, c, c_ctx, w_mod, b_mod, norm_ffn1, w_ffn1_gate,
           w_ffn1_up, w_ffn1_down, norm_mix, w_in, w_four, q_norm, w_uq, kv_norm, w_ukv, w_attn_proj,
           w_out, norm_ffn2, w_ffn2_gate, w_ffn2_up, w_ffn2_down, final_norm):
    xs = (x_prompt.reshape(N_PROMPT, D_MODEL), x_sample.reshape(N_SAMPLE, D_MODEL))
    c_all = jnp.concatenate(
        [c_ctx[None, :], c, jnp.zeros((MOD_ROWS - 1 - DEC_BATCH, D_MODEL), F32)], axis=0)
    mod = _modulation(c_all, w_mod, b_mod).reshape(DEPTH * MOD_ROWS * N_MOD, 1, D_MODEL)
    cache_kr = jnp.swapaxes(cache_krope, 2, 3)
    w_in_t = jnp.swapaxes(w_in, 1, 2)

    caches = ()
    for l in range(DEPTH):
        x = _ffn(xs, mod, l, MOD_FFN1, norm_ffn1, w_ffn1_gate, w_ffn1_up, w_ffn1_down, final_norm, False)
        f_in, q, ckv_n, kr, new_ckv, new_krope = _proj(
            x, mod, l, norm_mix, w_in_t, q_norm, w_uq, kv_norm, _rope_tables, caches)
        caches = (new_ckv, new_krope)
        frs = (_dft(f_in, BATCH, SEQ, 0), _dft(f_in, DEC_BATCH, DEC_SEQ, N_PROMPT))
        os_ = (_attention(q, ckv_n, kr, w_ukv, l, BATCH, SEQ, 0),
               _attention(q, ckv_n, kr, w_ukv, l, DEC_BATCH, DEC_SEQ, N_PROMPT, (cache_ckv, cache_kr)))
        x = _tail(x, mod, l, norm_mix, frs, os_, w_in_t, w_four, w_attn_proj, w_out)
        xs = _ffn((x,), mod, l, MOD_FFN2, norm_ffn2, w_ffn2_gate, w_ffn2_up, w_ffn2_down, final_norm,
                  l == DEPTH - 1)
        if l < DEPTH - 1:
            xs = (xs,)

    y_prompt = xs[0].reshape(BATCH, SEQ, D_MODEL)
    y_sample = xs[1].reshape(DEC_BATCH, DEC_SEQ, D_MODEL)
    return y_prompt, y_sample, caches[0], jnp.swapaxes(caches[1], 2, 3)
```

```python
import functools

import numpy as np
import jax
import jax.numpy as jnp
from jax import lax
from jax.experimental import pallas as pl
from jax.experimental.pallas import tpu as pltpu

D_MODEL = 1024
BATCH = 16
SEQ = 256
DEPTH = 2
DEC_BATCH = 4
DEC_SEQ = 1024
PAST_LEN = 512
GRID_W = 64
D_FF = 2816
FOUR_GROUPS = 4
FOUR_GROUP_DIM = 64
D_FOUR = FOUR_GROUPS * FOUR_GROUP_DIM
N_HEADS = 8
QK_NOPE = 64
QK_ROPE = 32
V_DIM = 64
Q_RANK = 384
KV_RANK = 256
N_MOD = 9
ROPE_BASE = 10000.0
EPS = 1e-6

N_PROMPT = BATCH * SEQ
N_SAMPLE = DEC_BATCH * DEC_SEQ
N_TOK = N_PROMPT + N_SAMPLE
MOD_ROWS = 8
MOD_FFN1, MOD_MIX, MOD_FFN2 = 0, 3, 6
HEAD_PAD = 128
ROPE_AXIS = QK_ROPE // 2
ROPE_PAIR = ROPE_AXIS // 2

VMEM_LIMIT = 52 * 1024 * 1024

F32 = jnp.float32
BF16 = jnp.bfloat16


def _dot(a, b):
    return jnp.dot(a, b, preferred_element_type=F32)


def _dot_nt(a, b):
    return lax.dot_general(a, b, (((1,), (1,)), ((), ())), preferred_element_type=F32)


def _mod_row(i, tm):
    n_prompt_tiles = N_PROMPT // tm
    per_batch = DEC_SEQ // tm
    return jnp.where(i < n_prompt_tiles, 0, 1 + (i - n_prompt_tiles) // per_batch)


def _mod_spec(layer, which, tm, tile_of_step=lambda i: i):
    def idx(*g):
        return ((layer * MOD_ROWS + _mod_row(tile_of_step(g[0]), tm)) * N_MOD + which, 0, 0)
    return pl.BlockSpec((None, 1, D_MODEL), idx)


def _norm_mod(x, nw, sc, sh):
    ms = jnp.mean(x * x, axis=-1, keepdims=True)
    y = x * lax.rsqrt(ms + EPS) * nw
    return y * (1.0 + sc) + sh


MOD_TN = 2304


def _mod_kernel(c_ref, w_ref, b_ref, o_ref):
    c = c_ref[...]
    a = (c * jax.nn.sigmoid(c)).astype(BF16)
    o_ref[...] = _dot(a, w_ref[...].astype(BF16)) + b_ref[pl.ds(pl.program_id(0), 1), :]


def _modulation(c_all, w_mod, b_mod):
    tn = MOD_TN
    n_out = N_MOD * D_MODEL
    return pl.pallas_call(
        _mod_kernel,
        grid=(DEPTH, n_out // tn),
        in_specs=[
            pl.BlockSpec((MOD_ROWS, D_MODEL), lambda l, j: (0, 0)),
            pl.BlockSpec((None, D_MODEL, tn), lambda l, j: (l, 0, j)),
            pl.BlockSpec((DEPTH, tn), lambda l, j: (0, j)),
        ],
        out_specs=pl.BlockSpec((None, MOD_ROWS, tn), lambda l, j: (l, 0, j)),
        out_shape=jax.ShapeDtypeStruct((DEPTH, MOD_ROWS, n_out), F32),
        compiler_params=pltpu.CompilerParams(
            dimension_semantics=("arbitrary", "arbitrary"), vmem_limit_bytes=VMEM_LIMIT),
        name="modulation",
    )(c_all, w_mod, b_mod)


FFN_TM = 512
FFN_CHUNK = 256
FFN_NC = D_FF // FFN_CHUNK


def _ffn_kernel(*refs, layer, final, split_in):
    refs = list(refs)
    x_refs = [refs.pop(0) for _ in range(2 if split_in else 1)]
    sh_ref, sc_ref, g_ref, nw_ref, wg_hbm, wu_hbm, wd_hbm, fn_ref = refs[:8]
    o_refs = refs[8:10] if final else refs[8:9]
    wg_s, wu_s, wd_s, stage_g, stage_u, stage_d, sems = refs[-7:]
    s = pl.program_id(0)
    is_prompt = s < N_PROMPT // FFN_TM

    def chunk_copies(j):
        slot, span = j % 2, pl.ds(j * FFN_CHUNK, FFN_CHUNK)
        return (pltpu.make_async_copy(wg_hbm.at[layer, :, span], stage_g.at[slot], sems.at[0, slot]),
                pltpu.make_async_copy(wu_hbm.at[layer, :, span], stage_u.at[slot], sems.at[1, slot]),
                pltpu.make_async_copy(wd_hbm.at[layer, span, :], stage_d.at[slot], sems.at[2, slot]))

    def x_tile():
        if split_in:
            return jnp.where(is_prompt, x_refs[0][...], x_refs[1][...])
        return x_refs[0][...]

    def chunk_act(h, wg, wu):
        gate = _dot(h, wg)
        up = _dot(h, wu)
        return ((gate * jax.nn.sigmoid(gate)) * up).astype(BF16)

    def finish(acc):
        xn = x_tile() + 0.5 * g_ref[...] * acc
        if not final:
            o_refs[0][...] = xn
            return
        ms = jnp.mean(xn * xn, axis=-1, keepdims=True)
        y = xn * lax.rsqrt(ms + EPS) * fn_ref[...]

        @pl.when(is_prompt)
        def _():
            o_refs[0][...] = y

        @pl.when(jnp.logical_not(is_prompt))
        def _():
            o_refs[1][...] = y

    def hidden():
        return _norm_mod(x_tile(), nw_ref[layer:layer + 1, :], sc_ref[...], sh_ref[...]).astype(BF16)

    @pl.when(s == 0)
    def _():
        for copy in chunk_copies(0):
            copy.start()
        h = hidden()
        acc = None
        for j in range(FFN_NC):
            if j + 1 < FFN_NC:
                for copy in chunk_copies(j + 1):
                    copy.start()
            for copy in chunk_copies(j):
                copy.wait()
            wg = stage_g[j % 2].astype(BF16)
            wu = stage_u[j % 2].astype(BF16)
            wd = stage_d[j % 2].astype(BF16)
            wg_s[j] = wg
            wu_s[j] = wu
            wd_s[j * FFN_CHUNK:(j + 1) * FFN_CHUNK, :] = wd
            part = _dot(chunk_act(h, wg, wu), wd)
            acc = part if acc is None else acc + part
        finish(acc)

    @pl.when(s > 0)
    def _():
        h = hidden()
        act = jnp.concatenate([chunk_act(h, wg_s[j], wu_s[j]) for j in range(FFN_NC)], axis=1)
        finish(_dot(act, wd_s[...]))


def _ffn(xs, mod, layer, mod_base, nw, wg, wu, wd, final_norm, final):
    tm = FFN_TM
    npt = N_PROMPT // tm
    split_in = len(xs) == 2
    tok = pl.BlockSpec((tm, D_MODEL), lambda s: (s, 0))
    tok_p = pl.BlockSpec((tm, D_MODEL), lambda s: (jnp.minimum(s, npt - 1), 0))
    tok_s = pl.BlockSpec((tm, D_MODEL), lambda s: (jnp.maximum(s - npt, 0), 0))
    half = jax.ShapeDtypeStruct((N_PROMPT, D_MODEL), F32)
    hbm = pl.BlockSpec(memory_space=pl.ANY)
    return pl.pallas_call(
        functools.partial(_ffn_kernel, layer=layer, final=final, split_in=split_in),
        grid=(N_TOK // tm,),
        in_specs=([tok_p, tok_s] if split_in else [tok]) + [
            _mod_spec(layer, mod_base + 0, tm),
            _mod_spec(layer, mod_base + 1, tm),
            _mod_spec(layer, mod_base + 2, tm),
            pl.BlockSpec((DEPTH, D_MODEL), lambda s: (0, 0)),
            hbm, hbm, hbm,
            pl.BlockSpec((1, D_MODEL), lambda s: (0, 0)),
        ],
        out_specs=[tok_p, tok_s] if final else tok,
        out_shape=[half, half] if final else jax.ShapeDtypeStruct((N_TOK, D_MODEL), F32),
        scratch_shapes=[
            pltpu.VMEM((FFN_NC, D_MODEL, FFN_CHUNK), BF16),
            pltpu.VMEM((FFN_NC, D_MODEL, FFN_CHUNK), BF16),
            pltpu.VMEM((D_FF, D_MODEL), BF16),
            pltpu.VMEM((2, D_MODEL, FFN_CHUNK), F32),
            pltpu.VMEM((2, D_MODEL, FFN_CHUNK), F32),
            pltpu.VMEM((2, FFN_CHUNK, D_MODEL), F32),
            pltpu.SemaphoreType.DMA((3, 2)),
        ],
        compiler_params=pltpu.CompilerParams(
            dimension_semantics=("arbitrary",), vmem_limit_bytes=VMEM_LIMIT),
        name="ffn_final" if final else ("ffn_first" if split_in else "ffn"),
    )(*xs, mod, mod, mod, nw, wg, wu, wd, final_norm.reshape(1, D_MODEL))


PROJ_TM = 1024
PROJ_SEQS = PROJ_TM // SEQ
PROJ_W = 1024
ROPE_COL = D_FOUR + Q_RANK + KV_RANK


def _pair_swap(x):
    lane = lax.broadcasted_iota(jnp.int32, x.shape, 1)
    first = lane % ROPE_AXIS < ROPE_PAIR
    return jnp.where(first, pltpu.roll(x, HEAD_PAD - ROPE_PAIR, 1), pltpu.roll(x, ROPE_PAIR, 1))


def _proj_kernel(*refs, layer, first_layer):
    (x_ref, sh_ref, sc_ref, nw_ref, w1_ref, qn_ref, wq_ref, pq_ref, kvn_ref,
     cq_ref, sq_ref, ck_ref, sk_ref) = refs[:13]
    f_ref, q_ref, ckv_ref, kr_ref, nckv_ref, nkr_ref, w1_s, wq_s = refs[-8:]
    is_prompt = pl.program_id(0) < N_PROMPT // PROJ_TM

    @pl.when(pl.program_id(0) == 0)
    def _():
        w1_s[...] = w1_ref[...].T.astype(BF16)
        wq_s[...] = _dot_nt(pq_ref[...], wq_ref[...].astype(BF16)).astype(BF16)

    n_qp = N_HEADS * HEAD_PAD
    c0 = D_FOUR + Q_RANK
    groups = [slice(r * SEQ, (r + 1) * SEQ) for r in range(PROJ_SEQS)]
    ps = []
    for rows in groups:
        h = _norm_mod(x_ref[rows, :], nw_ref[layer:layer + 1, :], sc_ref[...], sh_ref[...]).astype(BF16)
        ps.append(_dot(h, w1_s[...]))
    staged = []
    for rows, p in zip(groups, ps):
        f_ref[rows, :] = p[:, :D_FOUR]
        ql = p[:, D_FOUR:c0]
        qn = ql * lax.rsqrt(jnp.mean(ql * ql, axis=-1, keepdims=True) + EPS) * qn_ref[layer:layer + 1, :]
        qn = qn.astype(BF16)
        qq = _dot_nt(wq_s[:n_qp, :], qn)
        ckv = p[:, c0:c0 + KV_RANK]
        ckv_n = ckv * lax.rsqrt(jnp.mean(ckv * ckv, axis=-1, keepdims=True) + EPS) * kvn_ref[layer:layer + 1, :]
        ckv_ref[rows, :] = ckv_n
        kr = p[:, ROPE_COL:ROPE_COL + HEAD_PAD]
        kr = jnp.where(lax.broadcasted_iota(jnp.int32, kr.shape, 1) < QK_ROPE, kr, 0.0)
        cq, sq = cq_ref[:, rows], sq_ref[:, rows]
        qs = _dot_nt(wq_s[n_qp:, :], qn)
        for hd in range(N_HEADS):
            lo, mid = hd * HEAD_PAD, hd * HEAD_PAD + QK_ROPE
            rope = qq[lo:mid, :] * cq[:QK_ROPE] + qs[hd * QK_ROPE:(hd + 1) * QK_ROPE, :] * sq[:QK_ROPE]
            q_ref[lo:mid, rows] = rope.astype(BF16)
            q_ref[mid:lo + HEAD_PAD, rows] = (qq[mid:lo + HEAD_PAD, :] * cq[QK_ROPE:]).astype(BF16)
        kr_ref[rows, :] = (kr * ck_ref[rows, :] + _pair_swap(kr) * sk_ref[rows, :]).astype(BF16)
        staged.append((ckv_n, kr))

    @pl.when(is_prompt)
    def _():
        for r, (ckv_n, kr) in enumerate(staged):
            if first_layer:
                nckv_ref[r, 0] = ckv_n
                nkr_ref[r, 0] = kr.T[:QK_ROPE]
                for later in range(1, DEPTH):
                    nckv_ref[r, later] = jnp.zeros_like(ckv_n)
                    nkr_ref[r, later] = jnp.zeros((QK_ROPE, SEQ), F32)
            else:
                nckv_ref[r] = ckv_n
                nkr_ref[r] = kr.T[:QK_ROPE]


def _q_select():
    n_qp = N_HEADS * HEAD_PAD
    p = np.zeros((n_qp + N_HEADS * QK_ROPE, N_HEADS * (QK_NOPE + QK_ROPE)), np.float32)
    for hd in range(N_HEADS):
        src, dst = hd * (QK_NOPE + QK_ROPE), hd * HEAD_PAD
        for j in range(QK_ROPE):
            partner = j + ROPE_PAIR if j % ROPE_AXIS < ROPE_PAIR else j - ROPE_PAIR
            p[dst + j, src + QK_NOPE + j] = 1.0
            p[n_qp + hd * QK_ROPE + j, src + QK_NOPE + partner] = 1.0
        for j in range(QK_NOPE):
            p[dst + QK_ROPE + j, src + j] = 1.0
    return p.astype(BF16)


def _proj(x, mod, layer, nw, w_in_t, qn, w_uq, kvn, tabs, prev_caches):
    tm = PROJ_TM
    n_prompt_tiles = N_PROMPT // tm
    per_batch = DEC_SEQ // tm
    n_q = N_HEADS * (QK_NOPE + QK_ROPE)

    def tab_blk(i):
        return jnp.where(i < n_prompt_tiles, per_batch, (i - n_prompt_tiles) % per_batch)

    tok = lambda i: (i, 0)
    const = lambda i: (0, 0)
    first_layer = not prev_caches
    assert first_layer == (layer == 0)
    cache_layers = DEPTH if first_layer else None
    cache_idx = lambda i: (jnp.minimum(i, n_prompt_tiles - 1), 0 if first_layer else layer, 0, 0)
    tab_spec = pl.BlockSpec((tm, HEAD_PAD), lambda i: (tab_blk(i), 0))
    tab_t_spec = pl.BlockSpec((HEAD_PAD, tm), lambda i: (0, tab_blk(i)))
    return pl.pallas_call(
        functools.partial(_proj_kernel, layer=layer, first_layer=first_layer),
        grid=(N_TOK // tm,),
        in_specs=[
            pl.BlockSpec((tm, D_MODEL), tok),
            _mod_spec(layer, MOD_MIX + 0, tm),
            _mod_spec(layer, MOD_MIX + 1, tm),
            pl.BlockSpec((DEPTH, D_MODEL), const),
            pl.BlockSpec((None, PROJ_W, D_MODEL), lambda i: (layer, 0, 0)),
            pl.BlockSpec((DEPTH, Q_RANK), const),
            pl.BlockSpec((None, Q_RANK, n_q), lambda i: (layer, 0, 0)),
            pl.BlockSpec((N_HEADS * (HEAD_PAD + QK_ROPE), n_q), const),
            pl.BlockSpec((DEPTH, KV_RANK), const),
            tab_t_spec, tab_t_spec, tab_spec, tab_spec,
        ] + [pl.BlockSpec(memory_space=pl.ANY)] * len(prev_caches),
        out_specs=[
            pl.BlockSpec((tm, D_FOUR), tok),
            pl.BlockSpec((N_HEADS * HEAD_PAD, tm), lambda i: (0, i)),
            pl.BlockSpec((tm, KV_RANK), tok),
            pl.BlockSpec((tm, HEAD_PAD), tok),
            pl.BlockSpec((PROJ_SEQS, cache_layers, SEQ, KV_RANK), cache_idx),
            pl.BlockSpec((PROJ_SEQS, cache_layers, QK_ROPE, SEQ), cache_idx),
        ],
        out_shape=[
            jax.ShapeDtypeStruct((N_TOK, D_FOUR), F32),
            jax.ShapeDtypeStruct((N_HEADS * HEAD_PAD, N_TOK), BF16),
            jax.ShapeDtypeStruct((N_TOK, KV_RANK), F32),
            jax.ShapeDtypeStruct((N_TOK, HEAD_PAD), BF16),
            jax.ShapeDtypeStruct((BATCH, DEPTH, SEQ, KV_RANK), F32),
            jax.ShapeDtypeStruct((BATCH, DEPTH, QK_ROPE, SEQ), F32),
        ],
        input_output_aliases={13 + k: 4 + k for k in range(len(prev_caches))},
        scratch_shapes=[
            pltpu.VMEM((PROJ_W, D_MODEL), BF16),
            pltpu.VMEM((N_HEADS * (HEAD_PAD + QK_ROPE), Q_RANK), BF16),
        ],
        compiler_params=pltpu.CompilerParams(
            dimension_semantics=("arbitrary",), vmem_limit_bytes=VMEM_LIMIT),
        name="mixer_proj",
    )(x, mod, mod, nw, w_in_t, qn, w_uq, _q_select(), kvn, *tabs(tm), *prev_caches)


DFT_ROWS = 1024


def _split(x):
    hi = x.astype(BF16)
    lo = (x - hi.astype(F32)).astype(BF16)
    return hi, lo


def _dot3(ah, al, bh, bl):
    return _dot(ah, bh) + _dot(al, bh) + _dot(ah, bl)


def _dft_kernel(x_ref, gh_ref, gl_ref, ch_ref, cl_ref, sh_ref, sl_ref, rev_ref, o_ref, *, n):
    half = n // 2
    xh, xl = _split(x_ref[...])
    y = _dot3(xh, xl, gh_ref[...], gl_ref[...])
    yh, yl = _split(y)
    ch, cl, sh, sl = ch_ref[...], cl_ref[...], sh_ref[...], sl_ref[...]
    pos = lax.broadcasted_iota(jnp.int32, (n, D_FOUR), 0)
    alt = jnp.where(pos % 2 == 0, 1.0, -1.0).astype(F32) * (float(n) ** -0.5)
    first = lax.broadcasted_iota(jnp.int32, (half, D_FOUR), 0) == 0
    seqs = [slice(b * n, (b + 1) * n) for b in range(x_ref.shape[0] // n)]
    parts = [(_dot3(ch, cl, yh[r, :D_FOUR], yl[r, :D_FOUR]),
              _dot3(sh, sl, yh[r, D_FOUR:], yl[r, D_FOUR:]))
             for r in seqs]
    for r, (a, bb) in zip(seqs, parts):
        o_ref[r.start:r.start + half, :] = (a - bb).astype(BF16)
        mirrored = _dot(rev_ref[...], (a + bb).astype(BF16))
        nyquist = jnp.sum(y[r, :D_FOUR] * alt, axis=0, keepdims=True)
        o_ref[r.start + half:r.stop, :] = jnp.where(first, nyquist, mirrored).astype(BF16)


def _split_np(a):
    a32 = np.asarray(a, np.float32)
    hi = a32.astype(BF16)
    lo = (a32 - hi.astype(np.float32)).astype(BF16)
    return hi, lo


def _dft_tables(n):
    half = n // 2
    k = np.arange(half, dtype=np.int64)
    pos = np.arange(n, dtype=np.int64)
    ang = 2.0 * np.pi * ((k[:, None] * pos[None, :]) % n).astype(np.float64) / n
    c = np.arange(FOUR_GROUP_DIM, dtype=np.int64)
    ang_c = 2.0 * np.pi * ((c[:, None] * c[None, :]) % FOUR_GROUP_DIM).astype(np.float64) / FOUR_GROUP_DIM
    eye = np.eye(FOUR_GROUPS)
    g = np.concatenate([np.kron(eye, np.cos(ang_c)), np.kron(eye, np.sin(ang_c))], axis=1)
    g = g / np.sqrt(FOUR_GROUP_DIM)
    rev = np.zeros((half, half), np.float32)
    rev[np.arange(1, half), half - np.arange(1, half)] = 1.0
    return (_split_np(g) + _split_np(np.cos(ang) / np.sqrt(n)) + _split_np(np.sin(ang) / np.sqrt(n))
            + (rev.astype(BF16),))


def _dft(f_in, n_batch, n, row0):
    tables = _dft_tables(n)
    rows = max(n, DFT_ROWS)
    blk0 = row0 // rows
    const = lambda b: (0, 0)
    half_spec = pl.BlockSpec((n // 2, n), const)
    return pl.pallas_call(
        functools.partial(_dft_kernel, n=n),
        grid=(n_batch * n // rows,),
        in_specs=[
            pl.BlockSpec((rows, D_FOUR), lambda b: (blk0 + b, 0)),
            pl.BlockSpec((D_FOUR, 2 * D_FOUR), const),
            pl.BlockSpec((D_FOUR, 2 * D_FOUR), const),
            half_spec, half_spec, half_spec, half_spec,
            pl.BlockSpec((n // 2, n // 2), const),
        ],
        out_specs=pl.BlockSpec((rows, D_FOUR), lambda b: (b, 0)),
        out_shape=jax.ShapeDtypeStruct((n_batch * n, D_FOUR), BF16),
        compiler_params=pltpu.CompilerParams(
            dimension_semantics=("arbitrary",), vmem_limit_bytes=VMEM_LIMIT),
        name="fnet_dft_%d" % n,
    )(f_in, *tables)


ATTN_TQ = 256
ATTN_SEQS = 4
ATTN_AHEAD = 4
ATTN_VALUE_LAG = 1
VT_ROWS = V_DIM + 16


def _attn_kernel(*refs, n_own, n_ctx, tiles, shared):
    if n_ctx:
        qt_ref, ckv_ref, kr_ref, cckv_ref, ckr_ref, wkv_ref, pk_ref, pvt_ref = refs[:8]
    else:
        qt_ref, ckv_ref, kr_ref, wkv_ref, pk_ref, pvt_ref = refs[:6]
    o_ref, k_scr, vt_scr, wk_s, wvt_s = refs[-5:]

    @pl.when(pl.program_id(0) == 0)
    def _():
        wkv = wkv_ref[...].astype(BF16)
        wk_s[...] = _dot(wkv, pk_ref[...]).astype(BF16)
        wvt_s[...] = _dot_nt(pvt_ref[...], wkv).astype(BF16)

    def build_keys_values():
        ckv = ckv_ref[...].astype(BF16)
        kr = kr_ref[...]
        if n_ctx:
            ckv = jnp.concatenate([ckv, cckv_ref[...].astype(BF16)], axis=0)
            ckr = jnp.concatenate([ckr_ref[...], jnp.zeros((HEAD_PAD - QK_ROPE, n_ctx), F32)], axis=0)
            kr = jnp.concatenate([kr, ckr.T.astype(BF16)], axis=0)
        k_nope = _dot(ckv, wk_s[...])
        kr = kr.astype(F32)
        for hd in range(N_HEADS):
            lo = hd * HEAD_PAD
            k_scr[:, lo:lo + HEAD_PAD] = (k_nope[:, lo:lo + HEAD_PAD] + kr).astype(BF16)
        vt = _dot_nt(wvt_s[...], ckv).astype(BF16)
        pad_rows = lax.broadcasted_iota(jnp.int32, (VT_ROWS - V_DIM, vt.shape[1]), 0)
        ones_row = (pad_rows == 0).astype(BF16)
        for hd in range(N_HEADS):
            vt_scr[hd * VT_ROWS:hd * VT_ROWS + V_DIM, :] = vt[hd * V_DIM:(hd + 1) * V_DIM]
            vt_scr[hd * VT_ROWS + V_DIM:(hd + 1) * VT_ROWS, :] = ones_row

    build_keys_values()

    n_keys = n_own + n_ctx
    tq = qt_ref.shape[1] // tiles
    units = [(j, hd) for j in range(tiles) for hd in range(N_HEADS)]

    def keys_of(j):
        return slice(0, n_keys) if shared else slice(j * n_keys, (j + 1) * n_keys)

    def scores(j, hd):
        lo = hd * HEAD_PAD
        return _dot(k_scr[keys_of(j), lo:lo + HEAD_PAD],
                    qt_ref[lo:lo + HEAD_PAD, j * tq:(j + 1) * tq])

    outs = []

    def values(j, hd, p):
        ov = _dot(vt_scr[hd * VT_ROWS:(hd + 1) * VT_ROWS, keys_of(j)], p)
        outs.append(ov[:V_DIM] / ov[V_DIM:V_DIM + 1])
        if hd == N_HEADS - 1:
            o_ref[j * tq:(j + 1) * tq, :] = jnp.concatenate(outs, axis=0).T.astype(BF16)
            outs.clear()

    queue = [scores(*u) for u in units[:ATTN_AHEAD]]
    pending = []
    for i, unit in enumerate(units):
        s = queue.pop(0)
        if i + ATTN_AHEAD < len(units):
            queue.append(scores(*units[i + ATTN_AHEAD]))
        p = jnp.exp2(s - jnp.max(s, axis=0, keepdims=True)).astype(BF16)
        if len(pending) == ATTN_VALUE_LAG:
            values(*pending.pop(0))
        pending.append((*unit, p))
    for item in pending:
        values(*item)


def _kv_select():
    n_kv = N_HEADS * (QK_NOPE + V_DIM)
    pk = np.zeros((n_kv, N_HEADS * HEAD_PAD), np.float32)
    pvt = np.zeros((N_HEADS * V_DIM, n_kv), np.float32)
    for hd in range(N_HEADS):
        src = hd * (QK_NOPE + V_DIM)
        for j in range(QK_NOPE):
            pk[src + j, hd * HEAD_PAD + QK_ROPE + j] = 1.0
        for j in range(V_DIM):
            pvt[hd * V_DIM + j, src + QK_NOPE + j] = 1.0
    return pk.astype(BF16), pvt.astype(BF16)


def _attention(q, ckv_n, kr, w_ukv, layer, n_batch, n_own, row0, ctx=None):
    tq = min(n_own, ATTN_TQ)
    n_kv = N_HEADS * (QK_NOPE + V_DIM)
    n_ctx = 0 if ctx is None else PAST_LEN
    n_keys = n_own + n_ctx
    shared = n_own > tq
    assert shared or ctx is None
    seqs = 1 if shared else ATTN_SEQS
    tiles = n_own // tq if shared else seqs
    rows = seqs * n_own
    blk0 = row0 // rows
    const = lambda b: (0, 0)
    in_specs = [
        pl.BlockSpec((N_HEADS * HEAD_PAD, rows), lambda b: (0, blk0 + b)),
        pl.BlockSpec((rows, KV_RANK), lambda b: (blk0 + b, 0)),
        pl.BlockSpec((rows, HEAD_PAD), lambda b: (blk0 + b, 0)),
    ]
    args = [q, ckv_n, kr]
    if ctx is not None:
        cckv, ckr = ctx
        in_specs += [
            pl.BlockSpec((None, None, PAST_LEN, KV_RANK), lambda b: (b, layer, 0, 0)),
            pl.BlockSpec((None, None, QK_ROPE, PAST_LEN), lambda b: (b, layer, 0, 0)),
        ]
        args += [cckv, ckr]
    in_specs += [
        pl.BlockSpec((None, KV_RANK, n_kv), lambda b: (layer, 0, 0)),
        pl.BlockSpec((n_kv, N_HEADS * HEAD_PAD), const),
        pl.BlockSpec((N_HEADS * V_DIM, n_kv), const),
    ]
    args += [w_ukv, *_kv_select()]
    return pl.pallas_call(
        functools.partial(_attn_kernel, n_own=n_own, n_ctx=n_ctx, tiles=tiles, shared=shared),
        grid=(n_batch // seqs,),
        in_specs=in_specs,
        out_specs=pl.BlockSpec((rows, N_HEADS * V_DIM), lambda b: (b, 0)),
        out_shape=jax.ShapeDtypeStruct((n_batch * n_own, N_HEADS * V_DIM), BF16),
        scratch_shapes=[
            pltpu.VMEM((seqs * n_keys, N_HEADS * HEAD_PAD), BF16),
            pltpu.VMEM((N_HEADS * VT_ROWS, seqs * n_keys), BF16),
            pltpu.VMEM((KV_RANK, N_HEADS * HEAD_PAD), BF16),
            pltpu.VMEM((N_HEADS * V_DIM, KV_RANK), BF16),
        ],
        compiler_params=pltpu.CompilerParams(
            dimension_semantics=("arbitrary",), vmem_limit_bytes=VMEM_LIMIT),
        name="mla_attention_%d" % n_keys,
    )(*args)


TAIL_TM = 1024
TAIL_GROUP = 256
TAIL_SLABS = 2
GATE_ROW = ROPE_COL + QK_ROPE
GATE_CHUNK = 512
GATE_CHUNKS = 2 * D_MODEL // GATE_CHUNK


def _tail_kernel(x_ref, sh_ref, sc_ref, g_ref, nw_ref, frp_ref, frs_ref, op_ref, os_ref, wi_hbm,
                 wf_ref, wa_ref, wo_ref, out_ref, wg_s, wf_s, wa_s, wo_s, stage, sem, *, layer):
    s = pl.program_id(0)
    is_prompt = jnp.maximum(s - TAIL_SLABS, 0) < N_PROMPT // TAIL_TM

    def gate_copy(c):
        src = wi_hbm.at[layer, pl.ds(GATE_ROW + c * GATE_CHUNK, GATE_CHUNK), :]
        return pltpu.make_async_copy(src, stage.at[c % 2], sem.at[c % 2])

    @pl.when(s == 0)
    def _():
        gate_copy(0).start()
        for c in range(GATE_CHUNKS):
            if c + 1 < GATE_CHUNKS:
                gate_copy(c + 1).start()
            gate_copy(c).wait()
            wg_s[:, c * GATE_CHUNK:(c + 1) * GATE_CHUNK] = stage[c % 2].T.astype(BF16)

    @pl.when(s < TAIL_SLABS)
    def _():
        def put(dst, src):
            rows = src.shape[0]
            dst[pl.ds(pl.multiple_of(s * rows, rows), rows), :] = src.astype(BF16)

        put(wf_s, wf_ref[...])
        put(wa_s, wa_ref[...])
        put(wo_s, wo_ref[...])

    @pl.when(s >= TAIL_SLABS)
    def _():
        groups = [slice(r * TAIL_GROUP, (r + 1) * TAIL_GROUP) for r in range(TAIL_TM // TAIL_GROUP)]

        def branches(rows):
            x = x_ref[rows, :]
            h = _norm_mod(x, nw_ref[layer:layer + 1, :], sc_ref[...], sh_ref[...]).astype(BF16)
            gates = _dot(h, wg_s[...])
            a_out = _dot(jnp.where(is_prompt, frp_ref[rows, :], frs_ref[rows, :]), wf_s[...])
            o_out = _dot(jnp.where(is_prompt, op_ref[rows, :], os_ref[rows, :]), wa_s[...])
            return x, gates, a_out, o_out

        def merge(rows, x, gates, a_out, o_out):
            merged = jax.nn.sigmoid(gates[:, :D_MODEL]) * a_out + jax.nn.sigmoid(gates[:, D_MODEL:]) * o_out
            m = _dot(merged.astype(BF16), wo_s[...])
            out_ref[rows, :] = x + g_ref[...] * m

        pending = branches(groups[0])
        for r in range(1, len(groups)):
            ready, pending = pending, branches(groups[r])
            merge(groups[r - 1], *ready)
        merge(groups[-1], *pending)


def _tail(x, mod, layer, nw, frs, os_, w_in_t, w_four, w_attn_proj, w_out):
    tm = TAIL_TM
    npt = N_PROMPT // tm
    tile = lambda s: jnp.maximum(s - TAIL_SLABS, 0)
    slab = lambda s: jnp.minimum(s, TAIL_SLABS - 1)
    tok = lambda s: (tile(s), 0)
    tok_p = lambda s: (jnp.minimum(tile(s), npt - 1), 0)
    tok_s = lambda s: (jnp.maximum(tile(s) - npt, 0), 0)
    n_attn = N_HEADS * V_DIM

    def slab_spec(rows, cols):
        return pl.BlockSpec((None, rows // TAIL_SLABS, cols), lambda s: (layer, slab(s), 0))

    return pl.pallas_call(
        functools.partial(_tail_kernel, layer=layer),
        grid=(TAIL_SLABS + N_TOK // tm,),
        in_specs=[
            pl.BlockSpec((tm, D_MODEL), tok),
            _mod_spec(layer, MOD_MIX + 0, tm, tile),
            _mod_spec(layer, MOD_MIX + 1, tm, tile),
            _mod_spec(layer, MOD_MIX + 2, tm, tile),
            pl.BlockSpec((DEPTH, D_MODEL), lambda s: (0, 0)),
            pl.BlockSpec((tm, D_FOUR), tok_p),
            pl.BlockSpec((tm, D_FOUR), tok_s),
            pl.BlockSpec((tm, n_attn), tok_p),
            pl.BlockSpec((tm, n_attn), tok_s),
            pl.BlockSpec(memory_space=pl.ANY),
            slab_spec(D_FOUR, D_MODEL),
            slab_spec(n_attn, D_MODEL),
            slab_spec(D_MODEL, D_MODEL),
        ],
        out_specs=pl.BlockSpec((tm, D_MODEL), tok),
        out_shape=jax.ShapeDtypeStruct((N_TOK, D_MODEL), F32),
        scratch_shapes=[
            pltpu.VMEM((D_MODEL, 2 * D_MODEL), BF16),
            pltpu.VMEM((D_FOUR, D_MODEL), BF16),
            pltpu.VMEM((n_attn, D_MODEL), BF16),
            pltpu.VMEM((D_MODEL, D_MODEL), BF16),
            pltpu.VMEM((2, GATE_CHUNK, D_MODEL), F32),
            pltpu.SemaphoreType.DMA((2,)),
        ],
        compiler_params=pltpu.CompilerParams(
            dimension_semantics=("arbitrary",), vmem_limit_bytes=VMEM_LIMIT),
        name="mixer_tail",
    )(x, mod, mod, mod, nw, *frs, *os_, w_in_t, w_four, w_attn_proj, w_out)


def _rope_tables(tm):
    rows = DEC_SEQ // GRID_W
    row = np.repeat(np.arange(rows), GRID_W).astype(np.float64)
    col = np.tile(np.arange(GRID_W), rows).astype(np.float64)
    inv = ROPE_BASE ** (-np.arange(0, ROPE_AXIS, 2, dtype=np.float64) / ROPE_AXIS)
    ar = row[:, None] * inv
    ac = col[:, None] * inv
    cr, sr, cc, sc = np.cos(ar), np.sin(ar), np.cos(ac), np.sin(ac)
    cos32 = np.concatenate([cr, cr, cc, cc], axis=1)
    sin32 = np.concatenate([-sr, sr, -sc, sc], axis=1)
    scale = np.float32((QK_NOPE + QK_ROPE) ** -0.5 * np.log2(np.e))

    def table(rope_part, nope_val, ident_rope):
        t = np.zeros((DEC_SEQ + tm, HEAD_PAD), np.float32)
        t[:DEC_SEQ, :QK_ROPE] = rope_part
        t[:DEC_SEQ, QK_ROPE:QK_ROPE + QK_NOPE] = nope_val
        t[DEC_SEQ:, :QK_ROPE] = ident_rope
        t[DEC_SEQ:, QK_ROPE:QK_ROPE + QK_NOPE] = nope_val
        return t

    cq = (table(cos32, 1.0, 1.0) * scale).T
    sq = (table(sin32, 0.0, 0.0) * scale).T
    ck = table(cos32, 0.0, 1.0)
    sk = table(sin32, 0.0, 0.0)
    return tuple(jnp.asarray(np.ascontiguousarray(t), F32) for t in (cq, sq, ck, sk))


def kernel(x_prompt, x_sample, cache_ckv, cache_krope, c, c_ctx, w_mod, b_mod, norm_ffn1, w_ffn1_gate,
           w_ffn1_up, w_ffn1_down, norm_mix, w_in, w_four, q_norm, w_uq, kv_norm, w_ukv, w_attn_proj,
           w_out, norm_ffn2, w_ffn2_gate, w_ffn2_up, w_ffn2_down, final_norm):
    xs = (x_prompt.reshape(N_PROMPT, D_MODEL), x_sample.reshape(N_SAMPLE, D_MODEL))
    c_all = jnp.concatenate(
        [c_ctx[None, :], c, jnp.zeros((MOD_ROWS - 1 - DEC_BATCH, D_MODEL), F32)], axis=0)
    mod = _modulation(c_all, w_mod, b_mod).reshape(DEPTH * MOD_ROWS * N_MOD, 1, D_MODEL)
    cache_kr = jnp.swapaxes(cache_krope, 2, 3)
    w_in_t = jnp.swapaxes(w_in, 1, 2)

    caches = ()
    for l in range(DEPTH):
        x = _ffn(xs, mod, l, MOD_FFN1, norm_ffn1, w_ffn1_gate, w_ffn1_up, w_ffn1_down, final_norm, False)
        f_in, q, ckv_n, kr, new_ckv, new_krope = _proj(
            x, mod, l, norm_mix, w_in_t, q_norm, w_uq, kv_norm, _rope_tables, caches)
        caches = (new_ckv, new_krope)
        frs = (_dft(f_in, BATCH, SEQ, 0), _dft(f_in, DEC_BATCH, DEC_SEQ, N_PROMPT))
        os_ = (_attention(q, ckv_n, kr, w_ukv, l, BATCH, SEQ, 0),
               _attention(q, ckv_n, kr, w_ukv, l, DEC_BATCH, DEC_SEQ, N_PROMPT, (cache_ckv, cache_kr)))
        x = _tail(x, mod, l, norm_mix, frs, os_, w_in_t, w_four, w_attn_proj, w_out)
        xs = _ffn((x,), mod, l, MOD_FFN2, norm_ffn2, w_ffn2_gate, w_ffn2_up, w_ffn2_down, final_norm,
                  l == DEPTH - 1)
        if l < DEPTH - 1:
            xs = (xs,)

    y_prompt = xs[0].reshape(BATCH, SEQ, D_MODEL)
    y_sample = xs[1].reshape(DEC_BATCH, DEC_SEQ, D_MODEL)
    return y_prompt, y_sample, caches[0], jnp.swapaxes(caches[1], 2, 3)
```

```python
import functools

import numpy as np
import jax
import jax.numpy as jnp
from jax import lax
from jax.experimental import pallas as pl
from jax.experimental.pallas import tpu as pltpu

D_MODEL = 1024
BATCH = 16
SEQ = 256
DEPTH = 2
DEC_BATCH = 4
DEC_SEQ = 1024
PAST_LEN = 512
GRID_W = 64
D_FF = 2816
FOUR_GROUPS = 4
FOUR_GROUP_DIM = 64
D_FOUR = FOUR_GROUPS * FOUR_GROUP_DIM
N_HEADS = 8
QK_NOPE = 64
QK_ROPE = 32
V_DIM = 64
Q_RANK = 384
KV_RANK = 256
N_MOD = 9
ROPE_BASE = 10000.0
EPS = 1e-6

N_PROMPT = BATCH * SEQ
N_SAMPLE = DEC_BATCH * DEC_SEQ
N_TOK = N_PROMPT + N_SAMPLE
MOD_ROWS = 8
MOD_FFN1, MOD_MIX, MOD_FFN2 = 0, 3, 6
HEAD_PAD = 128
ROPE_AXIS = QK_ROPE // 2
ROPE_PAIR = ROPE_AXIS // 2

VMEM_LIMIT = 52 * 1024 * 1024

F32 = jnp.float32
BF16 = jnp.bfloat16


def _dot(a, b):
    return jnp.dot(a, b, preferred_element_type=F32)


def _dot_nt(a, b):
    return lax.dot_general(a, b, (((1,), (1,)), ((), ())), preferred_element_type=F32)


def _mod_row(i, tm):
    n_prompt_tiles = N_PROMPT // tm
    per_batch = DEC_SEQ // tm
    return jnp.where(i < n_prompt_tiles, 0, 1 + (i - n_prompt_tiles) // per_batch)


def _mod_spec(layer, which, tm, tile_of_step=lambda i: i):
    def idx(*g):
        return ((layer * MOD_ROWS + _mod_row(tile_of_step(g[0]), tm)) * N_MOD + which, 0, 0)
    return pl.BlockSpec((None, 1, D_MODEL), idx)


def _norm_mod(x, nw, sc, sh):
    ms = jnp.mean(x * x, axis=-1, keepdims=True)
    y = x * lax.rsqrt(ms + EPS) * nw
    return y * (1.0 + sc) + sh


MOD_TN = 2304


def _mod_kernel(c_ref, w_ref, b_ref, o_ref):
    c = c_ref[...]
    a = (c * jax.nn.sigmoid(c)).astype(BF16)
    o_ref[...] = _dot(a, w_ref[...].astype(BF16)) + b_ref[pl.ds(pl.program_id(0), 1), :]


def _modulation(c_all, w_mod, b_mod):
    tn = MOD_TN
    n_out = N_MOD * D_MODEL
    return pl.pallas_call(
        _mod_kernel,
        grid=(DEPTH, n_out // tn),
        in_specs=[
            pl.BlockSpec((MOD_ROWS, D_MODEL), lambda l, j: (0, 0)),
            pl.BlockSpec((None, D_MODEL, tn), lambda l, j: (l, 0, j)),
            pl.BlockSpec((DEPTH, tn), lambda l, j: (0, j)),
        ],
        out_specs=pl.BlockSpec((None, MOD_ROWS, tn), lambda l, j: (l, 0, j)),
        out_shape=jax.ShapeDtypeStruct((DEPTH, MOD_ROWS, n_out), F32),
        compiler_params=pltpu.CompilerParams(
            dimension_semantics=("arbitrary", "arbitrary"), vmem_limit_bytes=VMEM_LIMIT),
        name="modulation",
    )(c_all, w_mod, b_mod)


FFN_TM = 512
FFN_CHUNK = 256
FFN_NC = D_FF // FFN_CHUNK


def _ffn_kernel(*refs, layer, final, split_in):
    refs = list(refs)
    x_refs = [refs.pop(0) for _ in range(2 if split_in else 1)]
    sh_ref, sc_ref, g_ref, nw_ref, wg_hbm, wu_hbm, wd_hbm, fn_ref = refs[:8]
    o_refs = refs[8:10] if final else refs[8:9]
    wg_s, wu_s, wd_s, stage_g, stage_u, stage_d, sems = refs[-7:]
    s = pl.program_id(0)
    is_prompt = s < N_PROMPT // FFN_TM

    def chunk_copies(j):
        slot, span = j % 2, pl.ds(j * FFN_CHUNK, FFN_CHUNK)
        return (pltpu.make_async_copy(wg_hbm.at[layer, :, span], stage_g.at[slot], sems.at[0, slot]),
                pltpu.make_async_copy(wu_hbm.at[layer, :, span], stage_u.at[slot], sems.at[1, slot]),
                pltpu.make_async_copy(wd_hbm.at[layer, span, :], stage_d.at[slot], sems.at[2, slot]))

    def x_tile():
        if split_in:
            return jnp.where(is_prompt, x_refs[0][...], x_refs[1][...])
        return x_refs[0][...]

    def chunk_act(h, wg, wu):
        gate = _dot(h, wg)
        up = _dot(h, wu)
        return ((gate * jax.nn.sigmoid(gate)) * up).astype(BF16)

    def finish(acc):
        xn = x_tile() + 0.5 * g_ref[...] * acc
        if not final:
            o_refs[0][...] = xn
            return
        ms = jnp.mean(xn * xn, axis=-1, keepdims=True)
        y = xn * lax.rsqrt(ms + EPS) * fn_ref[...]

        @pl.when(is_prompt)
        def _():
            o_refs[0][...] = y

        @pl.when(jnp.logical_not(is_prompt))
        def _():
            o_refs[1][...] = y

    def hidden():
        return _norm_mod(x_tile(), nw_ref[layer:layer + 1, :], sc_ref[...], sh_ref[...]).astype(BF16)

    @pl.when(s == 0)
    def _():
        for copy in chunk_copies(0):
            copy.start()
        h = hidden()
        acc = None
        for j in range(FFN_NC):
            if j + 1 < FFN_NC:
                for copy in chunk_copies(j + 1):
                    copy.start()
            for copy in chunk_copies(j):
                copy.wait()
            wg = stage_g[j % 2].astype(BF16)
            wu = stage_u[j % 2].astype(BF16)
            wd = stage_d[j % 2].astype(BF16)
            wg_s[j] = wg
            wu_s[j] = wu
            wd_s[j * FFN_CHUNK:(j + 1) * FFN_CHUNK, :] = wd
            part = _dot(chunk_act(h, wg, wu), wd)
            acc = part if acc is None else acc + part
        finish(acc)

    @pl.when(s > 0)
    def _():
        h = hidden()
        act = jnp.concatenate([chunk_act(h, wg_s[j], wu_s[j]) for j in range(FFN_NC)], axis=1)
        finish(_dot(act, wd_s[...]))


def _ffn(xs, mod, layer, mod_base, nw, wg, wu, wd, final_norm, final):
    tm = FFN_TM
    npt = N_PROMPT // tm
    split_in = len(xs) == 2
    tok = pl.BlockSpec((tm, D_MODEL), lambda s: (s, 0))
    tok_p = pl.BlockSpec((tm, D_MODEL), lambda s: (jnp.minimum(s, npt - 1), 0))
    tok_s = pl.BlockSpec((tm, D_MODEL), lambda s: (jnp.maximum(s - npt, 0), 0))
    half = jax.ShapeDtypeStruct((N_PROMPT, D_MODEL), F32)
    hbm = pl.BlockSpec(memory_space=pl.ANY)
    return pl.pallas_call(
        functools.partial(_ffn_kernel, layer=layer, final=final, split_in=split_in),
        grid=(N_TOK // tm,),
        in_specs=([tok_p, tok_s] if split_in else [tok]) + [
            _mod_spec(layer, mod_base + 0, tm),
            _mod_spec(layer, mod_base + 1, tm),
            _mod_spec(layer, mod_base + 2, tm),
            pl.BlockSpec((DEPTH, D_MODEL), lambda s: (0, 0)),
            hbm, hbm, hbm,
            pl.BlockSpec((1, D_MODEL), lambda s: (0, 0)),
        ],
        out_specs=[tok_p, tok_s] if final else tok,
        out_shape=[half, half] if final else jax.ShapeDtypeStruct((N_TOK, D_MODEL), F32),
        scratch_shapes=[
            pltpu.VMEM((FFN_NC, D_MODEL, FFN_CHUNK), BF16),
            pltpu.VMEM((FFN_NC, D_MODEL, FFN_CHUNK), BF16),
            pltpu.VMEM((D_FF, D_MODEL), BF16),
            pltpu.VMEM((2, D_MODEL, FFN_CHUNK), F32),
            pltpu.VMEM((2, D_MODEL, FFN_CHUNK), F32),
            pltpu.VMEM((2, FFN_CHUNK, D_MODEL), F32),
            pltpu.SemaphoreType.DMA((3, 2)),
        ],
        compiler_params=pltpu.CompilerParams(
            dimension_semantics=("arbitrary",), vmem_limit_bytes=VMEM_LIMIT),
        name="ffn_final" if final else ("ffn_first" if split_in else "ffn"),
    )(*xs, mod, mod, mod, nw, wg, wu, wd, final_norm.reshape(1, D_MODEL))


PROJ_TM = 1024
PROJ_SEQS = PROJ_TM // SEQ
PROJ_W = 1024
ROPE_COL = D_FOUR + Q_RANK + KV_RANK


def _pair_swap(x):
    lane = lax.broadcasted_iota(jnp.int32, x.shape, 1)
    first = lane % ROPE_AXIS < ROPE_PAIR
    return jnp.where(first, pltpu.roll(x, HEAD_PAD - ROPE_PAIR, 1), pltpu.roll(x, ROPE_PAIR, 1))


def _proj_kernel(*refs, layer, first_layer):
    (x_ref, sh_ref, sc_ref, nw_ref, w1_ref, qn_ref, wq_ref, pq_ref, kvn_ref,
     cq_ref, sq_ref, ck_ref, sk_ref) = refs[:13]
    f_ref, q_ref, ckv_ref, kr_ref, nckv_ref, nkr_ref, w1_s, wq_s = refs[-8:]
    is_prompt = pl.program_id(0) < N_PROMPT // PROJ_TM

    @pl.when(pl.program_id(0) == 0)
    def _():
        w1_s[...] = w1_ref[...].T.astype(BF16)
        wq_s[...] = _dot_nt(pq_ref[...], wq_ref[...].astype(BF16)).astype(BF16)

    n_qp = N_HEADS * HEAD_PAD
    c0 = D_FOUR + Q_RANK
    groups = [slice(r * SEQ, (r + 1) * SEQ) for r in range(PROJ_SEQS)]
    ps = []
    for rows in groups:
        h = _norm_mod(x_ref[rows, :], nw_ref[layer:layer + 1, :], sc_ref[...], sh_ref[...]).astype(BF16)
        ps.append(_dot(h, w1_s[...]))
    staged = []
    for rows, p in zip(groups, ps):
        f_ref[rows, :] = p[:, :D_FOUR]
        ql = p[:, D_FOUR:c0]
        qn = ql * lax.rsqrt(jnp.mean(ql * ql, axis=-1, keepdims=True) + EPS) * qn_ref[layer:layer + 1, :]
        qn = qn.astype(BF16)
        qq = _dot_nt(wq_s[:n_qp, :], qn)
        ckv = p[:, c0:c0 + KV_RANK]
        ckv_n = ckv * lax.rsqrt(jnp.mean(ckv * ckv, axis=-1, keepdims=True) + EPS) * kvn_ref[layer:layer + 1, :]
        ckv_ref[rows, :] = ckv_n
        kr = p[:, ROPE_COL:ROPE_COL + HEAD_PAD]
        kr = jnp.where(lax.broadcasted_iota(jnp.int32, kr.shape, 1) < QK_ROPE, kr, 0.0)
        cq, sq = cq_ref[:, rows], sq_ref[:, rows]
        qs = _dot_nt(wq_s[n_qp:, :], qn)
        for hd in range(N_HEADS):
            lo, mid = hd * HEAD_PAD, hd * HEAD_PAD + QK_ROPE
            rope = qq[lo:mid, :] * cq[:QK_ROPE] + qs[hd * QK_ROPE:(hd + 1) * QK_ROPE, :] * sq[:QK_ROPE]
            q_ref[lo:mid, rows] = rope.astype(BF16)
            q_ref[mid:lo + HEAD_PAD, rows] = (qq[mid:lo + HEAD_PAD, :] * cq[QK_ROPE:]).astype(BF16)
        kr_ref[rows, :] = (kr * ck_ref[rows, :] + _pair_swap(kr) * sk_ref[rows, :]).astype(BF16)
        staged.append((ckv_n, kr))

    @pl.when(is_prompt)
    def _():
        for r, (ckv_n, kr) in enumerate(staged):
            if first_layer:
                nckv_ref[r, 0] = ckv_n
                nkr_ref[r, 0] = kr.T[:QK_ROPE]
                for later in range(1, DEPTH):
                    nckv_ref[r, later] = jnp.zeros_like(ckv_n)
                    nkr_ref[r, later] = jnp.zeros((QK_ROPE, SEQ), F32)
            else:
                nckv_ref[r] = ckv_n
                nkr_ref[r] = kr.T[:QK_ROPE]


def _q_select():
    n_qp = N_HEADS * HEAD_PAD
    p = np.zeros((n_qp + N_HEADS * QK_ROPE, N_HEADS * (QK_NOPE + QK_ROPE)), np.float32)
    for hd in range(N_HEADS):
        src, dst = hd * (QK_NOPE + QK_ROPE), hd * HEAD_PAD
        for j in range(QK_ROPE):
            partner = j + ROPE_PAIR if j % ROPE_AXIS < ROPE_PAIR else j - ROPE_PAIR
            p[dst + j, src + QK_NOPE + j] = 1.0
            p[n_qp + hd * QK_ROPE + j, src + QK_NOPE + partner] = 1.0
        for j in range(QK_NOPE):
            p[dst + QK_ROPE + j, src + j] = 1.0
    return p.astype(BF16)


def _proj(x, mod, layer, nw, w_in_t, qn, w_uq, kvn, tabs, prev_caches):
    tm = PROJ_TM
    n_prompt_tiles = N_PROMPT // tm
    per_batch = DEC_SEQ // tm
    n_q = N_HEADS * (QK_NOPE + QK_ROPE)

    def tab_blk(i):
        return jnp.where(i < n_prompt_tiles, per_batch, (i - n_prompt_tiles) % per_batch)

    tok = lambda i: (i, 0)
    const = lambda i: (0, 0)
    first_layer = not prev_caches
    assert first_layer == (layer == 0)
    cache_layers = DEPTH if first_layer else None
    cache_idx = lambda i: (jnp.minimum(i, n_prompt_tiles - 1), 0 if first_layer else layer, 0, 0)
    tab_spec = pl.BlockSpec((tm, HEAD_PAD), lambda i: (tab_blk(i), 0))
    tab_t_spec = pl.BlockSpec((HEAD_PAD, tm), lambda i: (0, tab_blk(i)))
    return pl.pallas_call(
        functools.partial(_proj_kernel, layer=layer, first_layer=first_layer),
        grid=(N_TOK // tm,),
        in_specs=[
            pl.BlockSpec((tm, D_MODEL), tok),
            _mod_spec(layer, MOD_MIX + 0, tm),
            _mod_spec(layer, MOD_MIX + 1, tm),
            pl.BlockSpec((DEPTH, D_MODEL), const),
            pl.BlockSpec((None, PROJ_W, D_MODEL), lambda i: (layer, 0, 0)),
            pl.BlockSpec((DEPTH, Q_RANK), const),
            pl.BlockSpec((None, Q_RANK, n_q), lambda i: (layer, 0, 0)),
            pl.BlockSpec((N_HEADS * (HEAD_PAD + QK_ROPE), n_q), const),
            pl.BlockSpec((DEPTH, KV_RANK), const),
            tab_t_spec, tab_t_spec, tab_spec, tab_spec,
        ] + [pl.BlockSpec(memory_space=pl.ANY)] * len(prev_caches),
        out_specs=[
            pl.BlockSpec((tm, D_FOUR), tok),
            pl.BlockSpec((N_HEADS * HEAD_PAD, tm), lambda i: (0, i)),
            pl.BlockSpec((tm, KV_RANK), tok),
            pl.BlockSpec((tm, HEAD_PAD), tok),
            pl.BlockSpec((PROJ_SEQS, cache_layers, SEQ, KV_RANK), cache_idx),
            pl.BlockSpec((PROJ_SEQS, cache_layers, QK_ROPE, SEQ), cache_idx),
        ],
        out_shape=[
            jax.ShapeDtypeStruct((N_TOK, D_FOUR), F32),
            jax.ShapeDtypeStruct((N_HEADS * HEAD_PAD, N_TOK), BF16),
            jax.ShapeDtypeStruct((N_TOK, KV_RANK), F32),
            jax.ShapeDtypeStruct((N_TOK, HEAD_PAD), BF16),
            jax.ShapeDtypeStruct((BATCH, DEPTH, SEQ, KV_RANK), F32),
            jax.ShapeDtypeStruct((BATCH, DEPTH, QK_ROPE, SEQ), F32),
        ],
        input_output_aliases={13 + k: 4 + k for k in range(len(prev_caches))},
        scratch_shapes=[
            pltpu.VMEM((PROJ_W, D_MODEL), BF16),
            pltpu.VMEM((N_HEADS * (HEAD_PAD + QK_ROPE), Q_RANK), BF16),
        ],
        compiler_params=pltpu.CompilerParams(
            dimension_semantics=("arbitrary",), vmem_limit_bytes=VMEM_LIMIT),
        name="mixer_proj",
    )(x, mod, mod, nw, w_in_t, qn, w_uq, _q_select(), kvn, *tabs(tm), *prev_caches)


DFT_ROWS = 2048


def _split(x):
    hi = x.astype(BF16)
    lo = (x - hi.astype(F32)).astype(BF16)
    return hi, lo


def _dot3(ah, al, bh, bl):
    return _dot(ah, bh) + _dot(al, bh) + _dot(ah, bl)


def _dft_kernel(x_ref, gh_ref, gl_ref, ch_ref, cl_ref, sh_ref, sl_ref, rev_ref, o_ref, *, n):
    half = n // 2
    xh, xl = _split(x_ref[...])
    y = _dot3(xh, xl, gh_ref[...], gl_ref[...])
    yh, yl = _split(y)
    ch, cl, sh, sl = ch_ref[...], cl_ref[...], sh_ref[...], sl_ref[...]
    pos = lax.broadcasted_iota(jnp.int32, (n, D_FOUR), 0)
    alt = jnp.where(pos % 2 == 0, 1.0, -1.0).astype(F32) * (float(n) ** -0.5)
    first = lax.broadcasted_iota(jnp.int32, (half, D_FOUR), 0) == 0
    seqs = [slice(b * n, (b + 1) * n) for b in range(x_ref.shape[0] // n)]
    parts = [(_dot3(ch, cl, yh[r, :D_FOUR], yl[r, :D_FOUR]),
              _dot3(sh, sl, yh[r, D_FOUR:], yl[r, D_FOUR:]))
             for r in seqs]
    for r, (a, bb) in zip(seqs, parts):
        o_ref[r.start:r.start + half, :] = (a - bb).astype(BF16)
        mirrored = _dot(rev_ref[...], (a + bb).astype(BF16))
        nyquist = jnp.sum(y[r, :D_FOUR] * alt, axis=0, keepdims=True)
        o_ref[r.start + half:r.stop, :] = jnp.where(first, nyquist, mirrored).astype(BF16)


def _split_np(a):
    a32 = np.asarray(a, np.float32)
    hi = a32.astype(BF16)
    lo = (a32 - hi.astype(np.float32)).astype(BF16)
    return hi, lo


def _dft_tables(n):
    half = n // 2
    k = np.arange(half, dtype=np.int64)
    pos = np.arange(n, dtype=np.int64)
    ang = 2.0 * np.pi * ((k[:, None] * pos[None, :]) % n).astype(np.float64) / n
    c = np.arange(FOUR_GROUP_DIM, dtype=np.int64)
    ang_c = 2.0 * np.pi * ((c[:, None] * c[None, :]) % FOUR_GROUP_DIM).astype(np.float64) / FOUR_GROUP_DIM
    eye = np.eye(FOUR_GROUPS)
    g = np.concatenate([np.kron(eye, np.cos(ang_c)), np.kron(eye, np.sin(ang_c))], axis=1)
    g = g / np.sqrt(FOUR_GROUP_DIM)
    rev = np.zeros((half, half), np.float32)
    rev[np.arange(1, half), half - np.arange(1, half)] = 1.0
    return (_split_np(g) + _split_np(np.cos(ang) / np.sqrt(n)) + _split_np(np.sin(ang) / np.sqrt(n))
            + (rev.astype(BF16),))


def _dft(f_in, n_batch, n, row0):
    tables = _dft_tables(n)
    rows = max(n, DFT_ROWS)
    blk0 = row0 // rows
    const = lambda b: (0, 0)
    half_spec = pl.BlockSpec((n // 2, n), const)
    return pl.pallas_call(
        functools.partial(_dft_kernel, n=n),
        grid=(n_batch * n // rows,),
        in_specs=[
            pl.BlockSpec((rows, D_FOUR), lambda b: (blk0 + b, 0)),
            pl.BlockSpec((D_FOUR, 2 * D_FOUR), const),
            pl.BlockSpec((D_FOUR, 2 * D_FOUR), const),
            half_spec, half_spec, half_spec, half_spec,
            pl.BlockSpec((n // 2, n // 2), const),
        ],
        out_specs=pl.BlockSpec((rows, D_FOUR), lambda b: (b, 0)),
        out_shape=jax.ShapeDtypeStruct((n_batch * n, D_FOUR), BF16),
        compiler_params=pltpu.CompilerParams(
            dimension_semantics=("arbitrary",), vmem_limit_bytes=VMEM_LIMIT),
        name="fnet_dft_%d" % n,
    )(f_in, *tables)


ATTN_TQ = 256
ATTN_SEQS = 4
ATTN_AHEAD = 4
ATTN_VALUE_LAG = 1
VT_ROWS = V_DIM + 16


def _attn_kernel(*refs, n_own, n_ctx, tiles, shared):
    if n_ctx:
        qt_ref, ckv_ref, kr_ref, cckv_ref, ckr_ref, wkv_ref, pk_ref, pvt_ref = refs[:8]
    else:
        qt_ref, ckv_ref, kr_ref, wkv_ref, pk_ref, pvt_ref = refs[:6]
    o_ref, k_scr, vt_scr, wk_s, wvt_s = refs[-5:]

    @pl.when(pl.program_id(0) == 0)
    def _():
        wkv = wkv_ref[...].astype(BF16)
        wk_s[...] = _dot(wkv, pk_ref[...]).astype(BF16)
        wvt_s[...] = _dot_nt(pvt_ref[...], wkv).astype(BF16)

    def build_keys_values():
        ckv = ckv_ref[...].astype(BF16)
        kr = kr_ref[...]
        if n_ctx:
            ckv = jnp.concatenate([ckv, cckv_ref[...].astype(BF16)], axis=0)
            ckr = jnp.concatenate([ckr_ref[...], jnp.zeros((HEAD_PAD - QK_ROPE, n_ctx), F32)], axis=0)
            kr = jnp.concatenate([kr, ckr.T.astype(BF16)], axis=0)
        k_nope = _dot(ckv, wk_s[...])
        kr = kr.astype(F32)
        for hd in range(N_HEADS):
            lo = hd * HEAD_PAD
            k_scr[:, lo:lo + HEAD_PAD] = (k_nope[:, lo:lo + HEAD_PAD] + kr).astype(BF16)
        vt = _dot_nt(wvt_s[...], ckv).astype(BF16)
        pad_rows = lax.broadcasted_iota(jnp.int32, (VT_ROWS - V_DIM, vt.shape[1]), 0)
        ones_row = (pad_rows == 0).astype(BF16)
        for hd in range(N_HEADS):
            vt_scr[hd * VT_ROWS:hd * VT_ROWS + V_DIM, :] = vt[hd * V_DIM:(hd + 1) * V_DIM]
            vt_scr[hd * VT_ROWS + V_DIM:(hd + 1) * VT_ROWS, :] = ones_row

    build_keys_values()

    n_keys = n_own + n_ctx
    tq = qt_ref.shape[1] // tiles
    units = [(j, hd) for j in range(tiles) for hd in range(N_HEADS)]

    def keys_of(j):
        return slice(0, n_keys) if shared else slice(j * n_keys, (j + 1) * n_keys)

    def scores(j, hd):
        lo = hd * HEAD_PAD
        return _dot(k_scr[keys_of(j), lo:lo + HEAD_PAD],
                    qt_ref[lo:lo + HEAD_PAD, j * tq:(j + 1) * tq])

    outs = []

    def values(j, hd, p):
        ov = _dot(vt_scr[hd * VT_ROWS:(hd + 1) * VT_ROWS, keys_of(j)], p)
        outs.append(ov[:V_DIM] / ov[V_DIM:V_DIM + 1])
        if hd == N_HEADS - 1:
            o_ref[j * tq:(j + 1) * tq, :] = jnp.concatenate(outs, axis=0).T.astype(BF16)
            outs.clear()

    queue = [scores(*u) for u in units[:ATTN_AHEAD]]
    pending = []
    for i, unit in enumerate(units):
        s = queue.pop(0)
        if i + ATTN_AHEAD < len(units):
            queue.append(scores(*units[i + ATTN_AHEAD]))
        p = jnp.exp2(s - jnp.max(s, axis=0, keepdims=True)).astype(BF16)
        if len(pending) == ATTN_VALUE_LAG:
            values(*pending.pop(0))
        pending.append((*unit, p))
    for item in pending:
        values(*item)


def _kv_select():
    n_kv = N_HEADS * (QK_NOPE + V_DIM)
    pk = np.zeros((n_kv, N_HEADS * HEAD_PAD), np.float32)
    pvt = np.zeros((N_HEADS * V_DIM, n_kv), np.float32)
    for hd in range(N_HEADS):
        src = hd * (QK_NOPE + V_DIM)
        for j in range(QK_NOPE):
            pk[src + j, hd * HEAD_PAD + QK_ROPE + j] = 1.0
        for j in range(V_DIM):
            pvt[hd * V_DIM + j, src + QK_NOPE + j] = 1.0
    return pk.astype(BF16), pvt.astype(BF16)


def _attention(q, ckv_n, kr, w_ukv, layer, n_batch, n_own, row0, ctx=None):
    tq = min(n_own, ATTN_TQ)
    n_kv = N_HEADS * (QK_NOPE + V_DIM)
    n_ctx = 0 if ctx is None else PAST_LEN
    n_keys = n_own + n_ctx
    shared = n_own > tq
    assert shared or ctx is None
    seqs = 1 if shared else ATTN_SEQS
    tiles = n_own // tq if shared else seqs
    rows = seqs * n_own
    blk0 = row0 // rows
    const = lambda b: (0, 0)
    in_specs = [
        pl.BlockSpec((N_HEADS * HEAD_PAD, rows), lambda b: (0, blk0 + b)),
        pl.BlockSpec((rows, KV_RANK), lambda b: (blk0 + b, 0)),
        pl.BlockSpec((rows, HEAD_PAD), lambda b: (blk0 + b, 0)),
    ]
    args = [q, ckv_n, kr]
    if ctx is not None:
        cckv, ckr = ctx
        in_specs += [
            pl.BlockSpec((None, None, PAST_LEN, KV_RANK), lambda b: (b, layer, 0, 0)),
            pl.BlockSpec((None, None, QK_ROPE, PAST_LEN), lambda b: (b, layer, 0, 0)),
        ]
        args += [cckv, ckr]
    in_specs += [
        pl.BlockSpec((None, KV_RANK, n_kv), lambda b: (layer, 0, 0)),
        pl.BlockSpec((n_kv, N_HEADS * HEAD_PAD), const),
        pl.BlockSpec((N_HEADS * V_DIM, n_kv), const),
    ]
    args += [w_ukv, *_kv_select()]
    return pl.pallas_call(
        functools.partial(_attn_kernel, n_own=n_own, n_ctx=n_ctx, tiles=tiles, shared=shared),
        grid=(n_batch // seqs,),
        in_specs=in_specs,
        out_specs=pl.BlockSpec((rows, N_HEADS * V_DIM), lambda b: (b, 0)),
        out_shape=jax.ShapeDtypeStruct((n_batch * n_own, N_HEADS * V_DIM), BF16),
        scratch_shapes=[
            pltpu.VMEM((seqs * n_keys, N_HEADS * HEAD_PAD), BF16),
            pltpu.VMEM((N_HEADS * VT_ROWS, seqs * n_keys), BF16),
            pltpu.VMEM((KV_RANK, N_HEADS * HEAD_PAD), BF16),
            pltpu.VMEM((N_HEADS * V_DIM, KV_RANK), BF16),
        ],
        compiler_params=pltpu.CompilerParams(
            dimension_semantics=("arbitrary",), vmem_limit_bytes=VMEM_LIMIT),
        name="mla_attention_%d" % n_keys,
    )(*args)


TAIL_TM = 1024
TAIL_GROUP = 256
TAIL_SLABS = 2
GATE_ROW = ROPE_COL + QK_ROPE
GATE_CHUNK = 512
GATE_CHUNKS = 2 * D_MODEL // GATE_CHUNK


def _tail_kernel(x_ref, sh_ref, sc_ref, g_ref, nw_ref, frp_ref, frs_ref, op_ref, os_ref, wi_hbm,
                 wf_ref, wa_ref, wo_ref, out_ref, wg_s, wf_s, wa_s, wo_s, stage, sem, *, layer):
    s = pl.program_id(0)
    is_prompt = jnp.maximum(s - TAIL_SLABS, 0) < N_PROMPT // TAIL_TM

    def gate_copy(c):
        src = wi_hbm.at[layer, pl.ds(GATE_ROW + c * GATE_CHUNK, GATE_CHUNK), :]
        return pltpu.make_async_copy(src, stage.at[c % 2], sem.at[c % 2])

    @pl.when(s == 0)
    def _():
        gate_copy(0).start()
        for c in range(GATE_CHUNKS):
            if c + 1 < GATE_CHUNKS:
                gate_copy(c + 1).start()
            gate_copy(c).wait()
            wg_s[:, c * GATE_CHUNK:(c + 1) * GATE_CHUNK] = stage[c % 2].T.astype(BF16)

    @pl.when(s < TAIL_SLABS)
    def _():
        def put(dst, src):
            rows = src.shape[0]
            dst[pl.ds(pl.multiple_of(s * rows, rows), rows), :] = src.astype(BF16)

        put(wf_s, wf_ref[...])
        put(wa_s, wa_ref[...])
        put(wo_s, wo_ref[...])

    @pl.when(s >= TAIL_SLABS)
    def _():
        groups = [slice(r * TAIL_GROUP, (r + 1) * TAIL_GROUP) for r in range(TAIL_TM // TAIL_GROUP)]

        def branches(rows):
            x = x_ref[rows, :]
            h = _norm_mod(x, nw_ref[layer:layer + 1, :], sc_ref[...], sh_ref[...]).astype(BF16)
            gates = _dot(h, wg_s[...])
            a_out = _dot(jnp.where(is_prompt, frp_ref[rows, :], frs_ref[rows, :]), wf_s[...])
            o_out = _dot(jnp.where(is_prompt, op_ref[rows, :], os_ref[rows, :]), wa_s[...])
            return x, gates, a_out, o_out

        def merge(rows, x, gates, a_out, o_out):
            merged = jax.nn.sigmoid(gates[:, :D_MODEL]) * a_out + jax.nn.sigmoid(gates[:, D_MODEL:]) * o_out
            m = _dot(merged.astype(BF16), wo_s[...])
            out_ref[rows, :] = x + g_ref[...] * m

        pending = branches(groups[0])
        for r in range(1, len(groups)):
            ready, pending = pending, branches(groups[r])
            merge(groups[r - 1], *ready)
        merge(groups[-1], *pending)


def _tail(x, mod, layer, nw, frs, os_, w_in_t, w_four, w_attn_proj, w_out):
    tm = TAIL_TM
    npt = N_PROMPT // tm
    tile = lambda s: jnp.maximum(s - TAIL_SLABS, 0)
    slab = lambda s: jnp.minimum(s, TAIL_SLABS - 1)
    tok = lambda s: (tile(s), 0)
    tok_p = lambda s: (jnp.minimum(tile(s), npt - 1), 0)
    tok_s = lambda s: (jnp.maximum(tile(s) - npt, 0), 0)
    n_attn = N_HEADS * V_DIM

    def slab_spec(rows, cols):
        return pl.BlockSpec((None, rows // TAIL_SLABS, cols), lambda s: (layer, slab(s), 0))

    return pl.pallas_call(
        functools.partial(_tail_kernel, layer=layer),
        grid=(TAIL_SLABS + N_TOK // tm,),
        in_specs=[
            pl.BlockSpec((tm, D_MODEL), tok),
            _mod_spec(layer, MOD_MIX + 0, tm, tile),
            _mod_spec(layer, MOD_MIX + 1, tm, tile),
            _mod_spec(layer, MOD_MIX + 2, tm, tile),
            pl.BlockSpec((DEPTH, D_MODEL), lambda s: (0, 0)),
            pl.BlockSpec((tm, D_FOUR), tok_p),
            pl.BlockSpec((tm, D_FOUR), tok_s),
            pl.BlockSpec((tm, n_attn), tok_p),
            pl.BlockSpec((tm, n_attn), tok_s),
            pl.BlockSpec(memory_space=pl.ANY),
            slab_spec(D_FOUR, D_MODEL),
            slab_spec(n_attn, D_MODEL),
            slab_spec(D_MODEL, D_MODEL),
        ],
        out_specs=pl.BlockSpec((tm, D_MODEL), tok),
        out_shape=jax.ShapeDtypeStruct((N_TOK, D_MODEL), F32),
        scratch_shapes=[
            pltpu.VMEM((D_MODEL, 2 * D_MODEL), BF16),
            pltpu.VMEM((D_FOUR, D_MODEL), BF16),
            pltpu.VMEM((n_attn, D_MODEL), BF16),
            pltpu.VMEM((D_MODEL, D_MODEL), BF16),
            pltpu.VMEM((2, GATE_CHUNK, D_MODEL), F32),
            pltpu.SemaphoreType.DMA((2,)),
        ],
        compiler_params=pltpu.CompilerParams(
            dimension_semantics=("arbitrary",), vmem_limit_bytes=VMEM_LIMIT),
        name="mixer_tail",
    )(x, mod, mod, mod, nw, *frs, *os_, w_in_t, w_four, w_attn_proj, w_out)


def _rope_tables(tm):
    rows = DEC_SEQ // GRID_W
    row = np.repeat(np.arange(rows), GRID_W).astype(np.float64)
    col = np.tile(np.arange(GRID_W), rows).astype(np.float64)
    inv = ROPE_BASE ** (-np.arange(0, ROPE_AXIS, 2, dtype=np.float64) / ROPE_AXIS)
    ar = row[:, None] * inv
    ac = col[:, None] * inv
    cr, sr, cc, sc = np.cos(ar), np.sin(ar), np.cos(ac), np.sin(ac)
    cos32 = np.concatenate([cr, cr, cc, cc], axis=1)
    sin32 = np.concatenate([-sr, sr, -sc, sc], axis=1)
    scale = np.float32((QK_NOPE + QK_ROPE) ** -0.5 * np.log2(np.e))

    def table(rope_part, nope_val, ident_rope):
        t = np.zeros((DEC_SEQ + tm, HEAD_PAD), np.float32)
        t[:DEC_SEQ, :QK_ROPE] = rope_part
        t[:DEC_SEQ, QK_ROPE:QK_ROPE + QK_NOPE] = nope_val
        t[DEC_SEQ:, :QK_ROPE] = ident_rope
        t[DEC_SEQ:, QK_ROPE:QK_ROPE + QK_NOPE] = nope_val
        return t

    cq = (table(cos32, 1.0, 1.0) * scale).T
    sq = (table(sin32, 0.0, 0.0) * scale).T
    ck = table(cos32, 0.0, 1.0)
    sk = table(sin32, 0.0, 0.0)
    return tuple(jnp.asarray(np.ascontiguousarray(t), F32) for t in (cq, sq, ck, sk))


def kernel(x_prompt, x_sample, cache_ckv, cache_krope, c, c_ctx, w_mod, b_mod, norm_ffn1, w_ffn1_gate,
           w_ffn1_up, w_ffn1_down, norm_mix, w_in, w_four, q_norm, w_uq, kv_norm, w_ukv, w_attn_proj,
           w_out, norm_ffn2, w_ffn2_gate, w_ffn2_up, w_ffn2_down, final_norm):
    xs = (x_prompt.reshape(N_PROMPT, D_MODEL), x_sample.reshape(N_SAMPLE, D_MODEL))
    c_all = jnp.concatenate(
        [c_ctx[None, :], c, jnp.zeros((MOD_ROWS - 1 - DEC_BATCH, D_MODEL), F32)], axis=0)
    mod = _modulation(c_all, w_mod, b_mod).reshape(DEPTH * MOD_ROWS * N_MOD, 1, D_MODEL)
    cache_kr = jnp.swapaxes(cache_krope, 2, 3)
    w_in_t = jnp.swapaxes(w_in, 1, 2)

    caches = ()
    for l in range(DEPTH):
        x = _ffn(xs, mod, l, MOD_FFN1, norm_ffn1, w_ffn1_gate, w_ffn1_up, w_ffn1_down, final_norm, False)
        f_in, q, ckv_n, kr, new_ckv, new_krope = _proj(
            x, mod, l, norm_mix, w_in_t, q_norm, w_uq, kv_norm, _rope_tables, caches)
        caches = (new_ckv, new_krope)
        frs = (_dft(f_in, BATCH, SEQ, 0), _dft(f_in, DEC_BATCH, DEC_SEQ, N_PROMPT))
        os_ = (_attention(q, ckv_n, kr, w_ukv, l, BATCH, SEQ, 0),
               _attention(q, ckv_n, kr, w_ukv, l, DEC_BATCH, DEC_SEQ, N_PROMPT, (cache_ckv, cache_kr)))
        x = _tail(x, mod, l, norm_mix, frs, os_, w_in_t, w_four, w_attn_proj, w_out)
        xs = _ffn((x,), mod, l, MOD_FFN2, norm_ffn2, w_ffn2_gate, w_ffn2_up, w_ffn2_down, final_norm,
                  l == DEPTH - 1)
        if l < DEPTH - 1:
            xs = (xs,)

    y_prompt = xs[0].reshape(BATCH, SEQ, D_MODEL)
    y_sample = xs[1].reshape(DEC_BATCH, DEC_SEQ, D_MODEL)
    return y_prompt, y_sample, caches[0], jnp.swapaxes(caches[1], 2, 3)
```

```python
import functools

import numpy as np
import jax
import jax.numpy as jnp
from jax import lax
from jax.experimental import pallas as pl
from jax.experimental.pallas import tpu as pltpu

D_MODEL = 1024
BATCH = 16
SEQ = 256
DEPTH = 2
DEC_BATCH = 4
DEC_SEQ = 1024
PAST_LEN = 512
GRID_W = 64
D_FF = 2816
FOUR_GROUPS = 4
FOUR_GROUP_DIM = 64
D_FOUR = FOUR_GROUPS * FOUR_GROUP_DIM
N_HEADS = 8
QK_NOPE = 64
QK_ROPE = 32
V_DIM = 64
Q_RANK = 384
KV_RANK = 256
N_MOD = 9
ROPE_BASE = 10000.0
EPS = 1e-6

N_PROMPT = BATCH * SEQ
N_SAMPLE = DEC_BATCH * DEC_SEQ
N_TOK = N_PROMPT + N_SAMPLE
MOD_ROWS = 8
MOD_FFN1, MOD_MIX, MOD_FFN2 = 0, 3, 6
HEAD_PAD = 128
ROPE_AXIS = QK_ROPE // 2
ROPE_PAIR = ROPE_AXIS // 2

VMEM_LIMIT = 52 * 1024 * 1024

F32 = jnp.float32
BF16 = jnp.bfloat16


def _dot(a, b):
    return jnp.dot(a, b, preferred_element_type=F32)


def _dot_nt(a, b):
    return lax.dot_general(a, b, (((1,), (1,)), ((), ())), preferred_element_type=F32)


def _mod_row(i, tm):
    n_prompt_tiles = N_PROMPT // tm
    per_batch = DEC_SEQ // tm
    return jnp.where(i < n_prompt_tiles, 0, 1 + (i - n_prompt_tiles) // per_batch)


def _mod_spec(layer, which, tm, tile_of_step=lambda i: i):
    def idx(*g):
        return ((layer * MOD_ROWS + _mod_row(tile_of_step(g[0]), tm)) * N_MOD + which, 0, 0)
    return pl.BlockSpec((None, 1, D_MODEL), idx)


def _norm_mod(x, nw, sc, sh):
    ms = jnp.mean(x * x, axis=-1, keepdims=True)
    y = x * lax.rsqrt(ms + EPS) * nw
    return y * (1.0 + sc) + sh


MOD_TN = 2304


def _mod_kernel(c_ref, w_ref, b_ref, o_ref):
    c = c_ref[...]
    a = (c * jax.nn.sigmoid(c)).astype(BF16)
    o_ref[...] = _dot(a, w_ref[...].astype(BF16)) + b_ref[pl.ds(pl.program_id(0), 1), :]


def _modulation(c_all, w_mod, b_mod):
    tn = MOD_TN
    n_out = N_MOD * D_MODEL
    return pl.pallas_call(
        _mod_kernel,
        grid=(DEPTH, n_out // tn),
        in_specs=[
            pl.BlockSpec((MOD_ROWS, D_MODEL), lambda l, j: (0, 0)),
            pl.BlockSpec((None, D_MODEL, tn), lambda l, j: (l, 0, j)),
            pl.BlockSpec((DEPTH, tn), lambda l, j: (0, j)),
        ],
        out_specs=pl.BlockSpec((None, MOD_ROWS, tn), lambda l, j: (l, 0, j)),
        out_shape=jax.ShapeDtypeStruct((DEPTH, MOD_ROWS, n_out), F32),
        compiler_params=pltpu.CompilerParams(
            dimension_semantics=("arbitrary", "arbitrary"), vmem_limit_bytes=VMEM_LIMIT),
        name="modulation",
    )(c_all, w_mod, b_mod)


FFN_TM = 512
FFN_CHUNK = 256
FFN_NC = D_FF // FFN_CHUNK


def _ffn_kernel(*refs, layer, final, split_in):
    refs = list(refs)
    x_refs = [refs.pop(0) for _ in range(2 if split_in else 1)]
    sh_ref, sc_ref, g_ref, nw_ref, wg_hbm, wu_hbm, wd_hbm, fn_ref = refs[:8]
    o_refs = refs[8:10] if final else refs[8:9]
    wg_s, wu_s, wd_s, stage_g, stage_u, stage_d, sems = refs[-7:]
    s = pl.program_id(0)
    is_prompt = s < N_PROMPT // FFN_TM

    def chunk_copies(j):
        slot, span = j % 2, pl.ds(j * FFN_CHUNK, FFN_CHUNK)
        return (pltpu.make_async_copy(wg_hbm.at[layer, :, span], stage_g.at[slot], sems.at[0, slot]),
                pltpu.make_async_copy(wu_hbm.at[layer, :, span], stage_u.at[slot], sems.at[1, slot]),
                pltpu.make_async_copy(wd_hbm.at[layer, span, :], stage_d.at[slot], sems.at[2, slot]))

    def x_tile():
        if split_in:
            return jnp.where(is_prompt, x_refs[0][...], x_refs[1][...])
        return x_refs[0][...]

    def chunk_act(h, wg, wu):
        gate = _dot(h, wg)
        up = _dot(h, wu)
        return ((gate * jax.nn.sigmoid(gate)) * up).astype(BF16)

    def finish(acc):
        xn = x_tile() + 0.5 * g_ref[...] * acc
        if not final:
            o_refs[0][...] = xn
            return
        ms = jnp.mean(xn * xn, axis=-1, keepdims=True)
        y = xn * lax.rsqrt(ms + EPS) * fn_ref[...]

        @pl.when(is_prompt)
        def _():
            o_refs[0][...] = y

        @pl.when(jnp.logical_not(is_prompt))
        def _():
            o_refs[1][...] = y

    def hidden():
        return _norm_mod(x_tile(), nw_ref[layer:layer + 1, :], sc_ref[...], sh_ref[...]).astype(BF16)

    @pl.when(s == 0)
    def _():
        for copy in chunk_copies(0):
            copy.start()
        h = hidden()
        acc = None
        for j in range(FFN_NC):
            if j + 1 < FFN_NC:
                for copy in chunk_copies(j + 1):
                    copy.start()
            for copy in chunk_copies(j):
                copy.wait()
            wg = stage_g[j % 2].astype(BF16)
            wu = stage_u[j % 2].astype(BF16)
            wd = stage_d[j % 2].astype(BF16)
            wg_s[j] = wg
            wu_s[j] = wu
            wd_s[j * FFN_CHUNK:(j + 1) * FFN_CHUNK, :] = wd
            part = _dot(chunk_act(h, wg, wu), wd)
            acc = part if acc is None else acc + part
        finish(acc)

    @pl.when(s > 0)
    def _():
        h = hidden()
        act = jnp.concatenate([chunk_act(h, wg_s[j], wu_s[j]) for j in range(FFN_NC)], axis=1)
        finish(_dot(act, wd_s[...]))


def _ffn(xs, mod, layer, mod_base, nw, wg, wu, wd, final_norm, final):
    tm = FFN_TM
    npt = N_PROMPT // tm
    split_in = len(xs) == 2
    tok = pl.BlockSpec((tm, D_MODEL), lambda s: (s, 0))
    tok_p = pl.BlockSpec((tm, D_MODEL), lambda s: (jnp.minimum(s, npt - 1), 0))
    tok_s = pl.BlockSpec((tm, D_MODEL), lambda s: (jnp.maximum(s - npt, 0), 0))
    half = jax.ShapeDtypeStruct((N_PROMPT, D_MODEL), F32)
    hbm = pl.BlockSpec(memory_space=pl.ANY)
    return pl.pallas_call(
        functools.partial(_ffn_kernel, layer=layer, final=final, split_in=split_in),
        grid=(N_TOK // tm,),
        in_specs=([tok_p, tok_s] if split_in else [tok]) + [
            _mod_spec(layer, mod_base + 0, tm),
            _mod_spec(layer, mod_base + 1, tm),
            _mod_spec(layer, mod_base + 2, tm),
            pl.BlockSpec((DEPTH, D_MODEL), lambda s: (0, 0)),
            hbm, hbm, hbm,
            pl.BlockSpec((1, D_MODEL), lambda s: (0, 0)),
        ],
        out_specs=[tok_p, tok_s] if final else tok,
        out_shape=[half, half] if final else jax.ShapeDtypeStruct((N_TOK, D_MODEL), F32),
        scratch_shapes=[
            pltpu.VMEM((FFN_NC, D_MODEL, FFN_CHUNK), BF16),
            pltpu.VMEM((FFN_NC, D_MODEL, FFN_CHUNK), BF16),
            pltpu.VMEM((D_FF, D_MODEL), BF16),
            pltpu.VMEM((2, D_MODEL, FFN_CHUNK), F32),
            pltpu.VMEM((2, D_MODEL, FFN_CHUNK), F32),
            pltpu.VMEM((2, FFN_CHUNK, D_MODEL), F32),
            pltpu.SemaphoreType.DMA((3, 2)),
        ],
        compiler_params=pltpu.CompilerParams(
            dimension_semantics=("arbitrary",), vmem_limit_bytes=VMEM_LIMIT),
        name="ffn_final" if final else ("ffn_first" if split_in else "ffn"),
    )(*xs, mod, mod, mod, nw, wg, wu, wd, final_norm.reshape(1, D_MODEL))


PROJ_TM = 1024
PROJ_SEQS = PROJ_TM // SEQ
PROJ_W = 1024
ROPE_COL = D_FOUR + Q_RANK + KV_RANK


def _pair_swap(x):
    lane = lax.broadcasted_iota(jnp.int32, x.shape, 1)
    first = lane % ROPE_AXIS < ROPE_PAIR
    return jnp.where(first, pltpu.roll(x, HEAD_PAD - ROPE_PAIR, 1), pltpu.roll(x, ROPE_PAIR, 1))


def _proj_kernel(*refs, layer, first_layer):
    (x_ref, sh_ref, sc_ref, nw_ref, w1_ref, qn_ref, wq_ref, pq_ref, kvn_ref,
     cq_ref, sq_ref, ck_ref, sk_ref) = refs[:13]
    f_ref, q_ref, ckv_ref, kr_ref, nckv_ref, nkr_ref, w1_s, wq_s = refs[-8:]
    is_prompt = pl.program_id(0) < N_PROMPT // PROJ_TM

    @pl.when(pl.program_id(0) == 0)
    def _():
        w1_s[...] = w1_ref[...].T.astype(BF16)
        wq_s[...] = _dot_nt(pq_ref[...], wq_ref[...].astype(BF16)).astype(BF16)

    n_qp = N_HEADS * HEAD_PAD
    c0 = D_FOUR + Q_RANK
    groups = [slice(r * SEQ, (r + 1) * SEQ) for r in range(PROJ_SEQS)]
    ps = []
    for rows in groups:
        h = _norm_mod(x_ref[rows, :], nw_ref[layer:layer + 1, :], sc_ref[...], sh_ref[...]).astype(BF16)
        ps.append(_dot(h, w1_s[...]))
    staged = []
    for rows, p in zip(groups, ps):
        f_ref[rows, :] = p[:, :D_FOUR]
        ql = p[:, D_FOUR:c0]
        qn = ql * lax.rsqrt(jnp.mean(ql * ql, axis=-1, keepdims=True) + EPS) * qn_ref[layer:layer + 1, :]
        qn = qn.astype(BF16)
        qq = _dot_nt(wq_s[:n_qp, :], qn)
        ckv = p[:, c0:c0 + KV_RANK]
        ckv_n = ckv * lax.rsqrt(jnp.mean(ckv * ckv, axis=-1, keepdims=True) + EPS) * kvn_ref[layer:layer + 1, :]
        ckv_ref[rows, :] = ckv_n
        kr = p[:, ROPE_COL:ROPE_COL + HEAD_PAD]
        kr = jnp.where(lax.broadcasted_iota(jnp.int32, kr.shape, 1) < QK_ROPE, kr, 0.0)
        cq, sq = cq_ref[:, rows], sq_ref[:, rows]
        qs = _dot_nt(wq_s[n_qp:, :], qn)
        for hd in range(N_HEADS):
            lo, mid = hd * HEAD_PAD, hd * HEAD_PAD + QK_ROPE
            rope = qq[lo:mid, :] * cq[:QK_ROPE] + qs[hd * QK_ROPE:(hd + 1) * QK_ROPE, :] * sq[:QK_ROPE]
            q_ref[lo:mid, rows] = rope.astype(BF16)
            q_ref[mid:lo + HEAD_PAD, rows] = (qq[mid:lo + HEAD_PAD, :] * cq[QK_ROPE:]).astype(BF16)
        kr_ref[rows, :] = (kr * ck_ref[rows, :] + _pair_swap(kr) * sk_ref[rows, :]).astype(BF16)
        staged.append((ckv_n, kr))

    @pl.when(is_prompt)
    def _():
        for r, (ckv_n, kr) in enumerate(staged):
            if first_layer:
                nckv_ref[r, 0] = ckv_n
                nkr_ref[r, 0] = kr.T[:QK_ROPE]
                for later in range(1, DEPTH):
                    nckv_ref[r, later] = jnp.zeros_like(ckv_n)
                    nkr_ref[r, later] = jnp.zeros((QK_ROPE, SEQ), F32)
            else:
                nckv_ref[r] = ckv_n
                nkr_ref[r] = kr.T[:QK_ROPE]


def _q_select():
    n_qp = N_HEADS * HEAD_PAD
    p = np.zeros((n_qp + N_HEADS * QK_ROPE, N_HEADS * (QK_NOPE + QK_ROPE)), np.float32)
    for hd in range(N_HEADS):
        src, dst = hd * (QK_NOPE + QK_ROPE), hd * HEAD_PAD
        for j in range(QK_ROPE):
            partner = j + ROPE_PAIR if j % ROPE_AXIS < ROPE_PAIR else j - ROPE_PAIR
            p[dst + j, src + QK_NOPE + j] = 1.0
            p[n_qp + hd * QK_ROPE + j, src + QK_NOPE + partner] = 1.0
        for j in range(QK_NOPE):
            p[dst + QK_ROPE + j, src + j] = 1.0
    return p.astype(BF16)


def _proj(x, mod, layer, nw, w_in_t, qn, w_uq, kvn, tabs, prev_caches):
    tm = PROJ_TM
    n_prompt_tiles = N_PROMPT // tm
    per_batch = DEC_SEQ // tm
    n_q = N_HEADS * (QK_NOPE + QK_ROPE)

    def tab_blk(i):
        return jnp.where(i < n_prompt_tiles, per_batch, (i - n_prompt_tiles) % per_batch)

    tok = lambda i: (i, 0)
    const = lambda i: (0, 0)
    first_layer = not prev_caches
    assert first_layer == (layer == 0)
    cache_layers = DEPTH if first_layer else None
    cache_idx = lambda i: (jnp.minimum(i, n_prompt_tiles - 1), 0 if first_layer else layer, 0, 0)
    tab_spec = pl.BlockSpec((tm, HEAD_PAD), lambda i: (tab_blk(i), 0))
    tab_t_spec = pl.BlockSpec((HEAD_PAD, tm), lambda i: (0, tab_blk(i)))
    return pl.pallas_call(
        functools.partial(_proj_kernel, layer=layer, first_layer=first_layer),
        grid=(N_TOK // tm,),
        in_specs=[
            pl.BlockSpec((tm, D_MODEL), tok),
            _mod_spec(layer, MOD_MIX + 0, tm),
            _mod_spec(layer, MOD_MIX + 1, tm),
            pl.BlockSpec((DEPTH, D_MODEL), const),
            pl.BlockSpec((None, PROJ_W, D_MODEL), lambda i: (layer, 0, 0)),
            pl.BlockSpec((DEPTH, Q_RANK), const),
            pl.BlockSpec((None, Q_RANK, n_q), lambda i: (layer, 0, 0)),
            pl.BlockSpec((N_HEADS * (HEAD_PAD + QK_ROPE), n_q), const),
            pl.BlockSpec((DEPTH, KV_RANK), const),
            tab_t_spec, tab_t_spec, tab_spec, tab_spec,
        ] + [pl.BlockSpec(memory_space=pl.ANY)] * len(prev_caches),
        out_specs=[
            pl.BlockSpec((tm, D_FOUR), tok),
            pl.BlockSpec((N_HEADS * HEAD_PAD, tm), lambda i: (0, i)),
            pl.BlockSpec((tm, KV_RANK), tok),
            pl.BlockSpec((tm, HEAD_PAD), tok),
            pl.BlockSpec((PROJ_SEQS, cache_layers, SEQ, KV_RANK), cache_idx),
            pl.BlockSpec((PROJ_SEQS, cache_layers, QK_ROPE, SEQ), cache_idx),
        ],
        out_shape=[
            jax.ShapeDtypeStruct((N_TOK, D_FOUR), F32),
            jax.ShapeDtypeStruct((N_HEADS * HEAD_PAD, N_TOK), BF16),
            jax.ShapeDtypeStruct((N_TOK, KV_RANK), F32),
            jax.ShapeDtypeStruct((N_TOK, HEAD_PAD), BF16),
            jax.ShapeDtypeStruct((BATCH, DEPTH, SEQ, KV_RANK), F32),
            jax.ShapeDtypeStruct((BATCH, DEPTH, QK_ROPE, SEQ), F32),
        ],
        input_output_aliases={13 + k: 4 + k for k in range(len(prev_caches))},
        scratch_shapes=[
            pltpu.VMEM((PROJ_W, D_MODEL), BF16),
            pltpu.VMEM((N_HEADS * (HEAD_PAD + QK_ROPE), Q_RANK), BF16),
        ],
        compiler_params=pltpu.CompilerParams(
            dimension_semantics=("arbitrary",), vmem_limit_bytes=VMEM_LIMIT),
        name="mixer_proj",
    )(x, mod, mod, nw, w_in_t, qn, w_uq, _q_select(), kvn, *tabs(tm), *prev_caches)


DFT_ROWS = 4096


def _split(x):
    hi = x.astype(BF16)
    lo = (x - hi.astype(F32)).astype(BF16)
    return hi, lo


def _dot3(ah, al, bh, bl):
    return _dot(ah, bh) + _dot(al, bh) + _dot(ah, bl)


def _dft_kernel(x_ref, gh_ref, gl_ref, ch_ref, cl_ref, sh_ref, sl_ref, rev_ref, o_ref, *, n):
    half = n // 2
    xh, xl = _split(x_ref[...])
    y = _dot3(xh, xl, gh_ref[...], gl_ref[...])
    yh, yl = _split(y)
    ch, cl, sh, sl = ch_ref[...], cl_ref[...], sh_ref[...], sl_ref[...]
    pos = lax.broadcasted_iota(jnp.int32, (n, D_FOUR), 0)
    alt = jnp.where(pos % 2 == 0, 1.0, -1.0).astype(F32) * (float(n) ** -0.5)
    first = lax.broadcasted_iota(jnp.int32, (half, D_FOUR), 0) == 0
    seqs = [slice(b * n, (b + 1) * n) for b in range(x_ref.shape[0] // n)]
    parts = [(_dot3(ch, cl, yh[r, :D_FOUR], yl[r, :D_FOUR]),
              _dot3(sh, sl, yh[r, D_FOUR:], yl[r, D_FOUR:]))
             for r in seqs]
    for r, (a, bb) in zip(seqs, parts):
        o_ref[r.start:r.start + half, :] = (a - bb).astype(BF16)
        mirrored = _dot(rev_ref[...], (a + bb).astype(BF16))
        nyquist = jnp.sum(y[r, :D_FOUR] * alt, axis=0, keepdims=True)
        o_ref[r.start + half:r.stop, :] = jnp.where(first, nyquist, mirrored).astype(BF16)


def _split_np(a):
    a32 = np.asarray(a, np.float32)
    hi = a32.astype(BF16)
    lo = (a32 - hi.astype(np.float32)).astype(BF16)
    return hi, lo


def _dft_tables(n):
    half = n // 2
    k = np.arange(half, dtype=np.int64)
    pos = np.arange(n, dtype=np.int64)
    ang = 2.0 * np.pi * ((k[:, None] * pos[None, :]) % n).astype(np.float64) / n
    c = np.arange(FOUR_GROUP_DIM, dtype=np.int64)
    ang_c = 2.0 * np.pi * ((c[:, None] * c[None, :]) % FOUR_GROUP_DIM).astype(np.float64) / FOUR_GROUP_DIM
    eye = np.eye(FOUR_GROUPS)
    g = np.concatenate([np.kron(eye, np.cos(ang_c)), np.kron(eye, np.sin(ang_c))], axis=1)
    g = g / np.sqrt(FOUR_GROUP_DIM)
    rev = np.zeros((half, half), np.float32)
    rev[np.arange(1, half), half - np.arange(1, half)] = 1.0
    return (_split_np(g) + _split_np(np.cos(ang) / np.sqrt(n)) + _split_np(np.sin(ang) / np.sqrt(n))
            + (rev.astype(BF16),))


def _dft(f_in, n_batch, n, row0):
    tables = _dft_tables(n)
    rows = max(n, DFT_ROWS)
    blk0 = row0 // rows
    const = lambda b: (0, 0)
    half_spec = pl.BlockSpec((n // 2, n), const)
    return pl.pallas_call(
        functools.partial(_dft_kernel, n=n),
        grid=(n_batch * n // rows,),
        in_specs=[
            pl.BlockSpec((rows, D_FOUR), lambda b: (blk0 + b, 0)),
            pl.BlockSpec((D_FOUR, 2 * D_FOUR), const),
            pl.BlockSpec((D_FOUR, 2 * D_FOUR), const),
            half_spec, half_spec, half_spec, half_spec,
            pl.BlockSpec((n // 2, n // 2), const),
        ],
        out_specs=pl.BlockSpec((rows, D_FOUR), lambda b: (b, 0)),
        out_shape=jax.ShapeDtypeStruct((n_batch * n, D_FOUR), BF16),
        compiler_params=pltpu.CompilerParams(
            dimension_semantics=("arbitrary",), vmem_limit_bytes=VMEM_LIMIT),
        name="fnet_dft_%d" % n,
    )(f_in, *tables)


ATTN_TQ = 256
ATTN_SEQS = 4
ATTN_AHEAD = 4
ATTN_VALUE_LAG = 1
VT_ROWS = V_DIM + 16


def _attn_kernel(*refs, n_own, n_ctx, tiles, shared):
    if n_ctx:
        qt_ref, ckv_ref, kr_ref, cckv_ref, ckr_ref, wkv_ref, pk_ref, pvt_ref = refs[:8]
    else:
        qt_ref, ckv_ref, kr_ref, wkv_ref, pk_ref, pvt_ref = refs[:6]
    o_ref, k_scr, vt_scr, wk_s, wvt_s = refs[-5:]

    @pl.when(pl.program_id(0) == 0)
    def _():
        wkv = wkv_ref[...].astype(BF16)
        wk_s[...] = _dot(wkv, pk_ref[...]).astype(BF16)
        wvt_s[...] = _dot_nt(pvt_ref[...], wkv).astype(BF16)

    def build_keys_values():
        ckv = ckv_ref[...].astype(BF16)
        kr = kr_ref[...]
        if n_ctx:
            ckv = jnp.concatenate([ckv, cckv_ref[...].astype(BF16)], axis=0)
            ckr = jnp.concatenate([ckr_ref[...], jnp.zeros((HEAD_PAD - QK_ROPE, n_ctx), F32)], axis=0)
            kr = jnp.concatenate([kr, ckr.T.astype(BF16)], axis=0)
        k_nope = _dot(ckv, wk_s[...])
        kr = kr.astype(F32)
        for hd in range(N_HEADS):
            lo = hd * HEAD_PAD
            k_scr[:, lo:lo + HEAD_PAD] = (k_nope[:, lo:lo + HEAD_PAD] + kr).astype(BF16)
        vt = _dot_nt(wvt_s[...], ckv).astype(BF16)
        pad_rows = lax.broadcasted_iota(jnp.int32, (VT_ROWS - V_DIM, vt.shape[1]), 0)
        ones_row = (pad_rows == 0).astype(BF16)
        for hd in range(N_HEADS):
            vt_scr[hd * VT_ROWS:hd * VT_ROWS + V_DIM, :] = vt[hd * V_DIM:(hd + 1) * V_DIM]
            vt_scr[hd * VT_ROWS + V_DIM:(hd + 1) * VT_ROWS, :] = ones_row

    build_keys_values()

    n_keys = n_own + n_ctx
    tq = qt_ref.shape[1] // tiles
    units = [(j, hd) for j in range(tiles) for hd in range(N_HEADS)]

    def keys_of(j):
        return slice(0, n_keys) if shared else slice(j * n_keys, (j + 1) * n_keys)

    def scores(j, hd):
        lo = hd * HEAD_PAD
        return _dot(k_scr[keys_of(j), lo:lo + HEAD_PAD],
                    qt_ref[lo:lo + HEAD_PAD, j * tq:(j + 1) * tq])

    outs = []

    def values(j, hd, p):
        ov = _dot(vt_scr[hd * VT_ROWS:(hd + 1) * VT_ROWS, keys_of(j)], p)
        outs.append(ov[:V_DIM] / ov[V_DIM:V_DIM + 1])
        if hd == N_HEADS - 1:
            o_ref[j * tq:(j + 1) * tq, :] = jnp.concatenate(outs, axis=0).T.astype(BF16)
            outs.clear()

    queue = [scores(*u) for u in units[:ATTN_AHEAD]]
    pending = []
    for i, unit in enumerate(units):
        s = queue.pop(0)
        if i + ATTN_AHEAD < len(units):
            queue.append(scores(*units[i + ATTN_AHEAD]))
        p = jnp.exp2(s - jnp.max(s, axis=0, keepdims=True)).astype(BF16)
        if len(pending) == ATTN_VALUE_LAG:
            values(*pending.pop(0))
        pending.append((*unit, p))
    for item in pending:
        values(*item)


def _kv_select():
    n_kv = N_HEADS * (QK_NOPE + V_DIM)
    pk = np.zeros((n_kv, N_HEADS * HEAD_PAD), np.float32)
    pvt = np.zeros((N_HEADS * V_DIM, n_kv), np.float32)
    for hd in range(N_HEADS):
        src = hd * (QK_NOPE + V_DIM)
        for j in range(QK_NOPE):
            pk[src + j, hd * HEAD_PAD + QK_ROPE + j] = 1.0
        for j in range(V_DIM):
            pvt[hd * V_DIM + j, src + QK_NOPE + j] = 1.0
    return pk.astype(BF16), pvt.astype(BF16)


def _attention(q, ckv_n, kr, w_ukv, layer, n_batch, n_own, row0, ctx=None):
    tq = min(n_own, ATTN_TQ)
    n_kv = N_HEADS * (QK_NOPE + V_DIM)
    n_ctx = 0 if ctx is None else PAST_LEN
    n_keys = n_own + n_ctx
    shared = n_own > tq
    assert shared or ctx is None
    seqs = 1 if shared else ATTN_SEQS
    tiles = n_own // tq if shared else seqs
    rows = seqs * n_own
    blk0 = row0 // rows
    const = lambda b: (0, 0)
    in_specs = [
        pl.BlockSpec((N_HEADS * HEAD_PAD, rows), lambda b: (0, blk0 + b)),
        pl.BlockSpec((rows, KV_RANK), lambda b: (blk0 + b, 0)),
        pl.BlockSpec((rows, HEAD_PAD), lambda b: (blk0 + b, 0)),
    ]
    args = [q, ckv_n, kr]
    if ctx is not None:
        cckv, ckr = ctx
        in_specs += [
            pl.BlockSpec((None, None, PAST_LEN, KV_RANK), lambda b: (b, layer, 0, 0)),
            pl.BlockSpec((None, None, QK_ROPE, PAST_LEN), lambda b: (b, layer, 0, 0)),
        ]
        args += [cckv, ckr]
    in_specs += [
        pl.BlockSpec((None, KV_RANK, n_kv), lambda b: (layer, 0, 0)),
        pl.BlockSpec((n_kv, N_HEADS * HEAD_PAD), const),
        pl.BlockSpec((N_HEADS * V_DIM, n_kv), const),
    ]
    args += [w_ukv, *_kv_select()]
    return pl.pallas_call(
        functools.partial(_attn_kernel, n_own=n_own, n_ctx=n_ctx, tiles=tiles, shared=shared),
        grid=(n_batch // seqs,),
        in_specs=in_specs,
        out_specs=pl.BlockSpec((rows, N_HEADS * V_DIM), lambda b: (b, 0)),
        out_shape=jax.ShapeDtypeStruct((n_batch * n_own, N_HEADS * V_DIM), BF16),
        scratch_shapes=[
            pltpu.VMEM((seqs * n_keys, N_HEADS * HEAD_PAD), BF16),
            pltpu.VMEM((N_HEADS * VT_ROWS, seqs * n_keys), BF16),
            pltpu.VMEM((KV_RANK, N_HEADS * HEAD_PAD), BF16),
            pltpu.VMEM((N_HEADS * V_DIM, KV_RANK), BF16),
        ],
        compiler_params=pltpu.CompilerParams(
            dimension_semantics=("arbitrary",), vmem_limit_bytes=VMEM_LIMIT),
        name="mla_attention_%d" % n_keys,
    )(*args)


TAIL_TM = 1024
TAIL_GROUP = 256
TAIL_SLABS = 2
GATE_ROW = ROPE_COL + QK_ROPE
GATE_CHUNK = 512
GATE_CHUNKS = 2 * D_MODEL // GATE_CHUNK


def _tail_kernel(x_ref, sh_ref, sc_ref, g_ref, nw_ref, frp_ref, frs_ref, op_ref, os_ref, wi_hbm,
                 wf_ref, wa_ref, wo_ref, out_ref, wg_s, wf_s, wa_s, wo_s, stage, sem, *, layer):
    s = pl.program_id(0)
    is_prompt = jnp.maximum(s - TAIL_SLABS, 0) < N_PROMPT // TAIL_TM

    def gate_copy(c):
        src = wi_hbm.at[layer, pl.ds(GATE_ROW + c * GATE_CHUNK, GATE_CHUNK), :]
        return pltpu.make_async_copy(src, stage.at[c % 2], sem.at[c % 2])

    @pl.when(s == 0)
    def _():
        gate_copy(0).start()
        for c in range(GATE_CHUNKS):
            if c + 1 < GATE_CHUNKS:
                gate_copy(c + 1).start()
            gate_copy(c).wait()
            wg_s[:, c * GATE_CHUNK:(c + 1) * GATE_CHUNK] = stage[c % 2].T.astype(BF16)

    @pl.when(s < TAIL_SLABS)
    def _():
        def put(dst, src):
            rows = src.shape[0]
            dst[pl.ds(pl.multiple_of(s * rows, rows), rows), :] = src.astype(BF16)

        put(wf_s, wf_ref[...])
        put(wa_s, wa_ref[...])
        put(wo_s, wo_ref[...])

    @pl.when(s >= TAIL_SLABS)
    def _():
        groups = [slice(r * TAIL_GROUP, (r + 1) * TAIL_GROUP) for r in range(TAIL_TM // TAIL_GROUP)]

        def branches(rows):
            x = x_ref[rows, :]
            h = _norm_mod(x, nw_ref[layer:layer + 1, :], sc_ref[...], sh_ref[...]).astype(BF16)
            gates = _dot(h, wg_s[...])
            a_out = _dot(jnp.where(is_prompt, frp_ref[rows, :], frs_ref[rows, :]), wf_s[...])
            o_out = _dot(jnp.where(is_prompt, op_ref[rows, :], os_ref[rows, :]), wa_s[...])
            return x, gates, a_out, o_out

        def merge(rows, x, gates, a_out, o_out):
            merged = jax.nn.sigmoid(gates[:, :D_MODEL]) * a_out + jax.nn.sigmoid(gates[:, D_MODEL:]) * o_out
            m = _dot(merged.astype(BF16), wo_s[...])
            out_ref[rows, :] = x + g_ref[...] * m

        pending = branches(groups[0])
        for r in range(1, len(groups)):
            ready, pending = pending, branches(groups[r])
            merge(groups[r - 1], *ready)
        merge(groups[-1], *pending)


def _tail(x, mod, layer, nw, frs, os_, w_in_t, w_four, w_attn_proj, w_out):
    tm = TAIL_TM
    npt = N_PROMPT // tm
    tile = lambda s: jnp.maximum(s - TAIL_SLABS, 0)
    slab = lambda s: jnp.minimum(s, TAIL_SLABS - 1)
    tok = lambda s: (tile(s), 0)
    tok_p = lambda s: (jnp.minimum(tile(s), npt - 1), 0)
    tok_s = lambda s: (jnp.maximum(tile(s) - npt, 0), 0)
    n_attn = N_HEADS * V_DIM

    def slab_spec(rows, cols):
        return pl.BlockSpec((None, rows // TAIL_SLABS, cols), lambda s: (layer, slab(s), 0))

    return pl.pallas_call(
        functools.partial(_tail_kernel, layer=layer),
        grid=(TAIL_SLABS + N_TOK // tm,),
        in_specs=[
            pl.BlockSpec((tm, D_MODEL), tok),
            _mod_spec(layer, MOD_MIX + 0, tm, tile),
            _mod_spec(layer, MOD_MIX + 1, tm, tile),
            _mod_spec(layer, MOD_MIX + 2, tm, tile),
            pl.BlockSpec((DEPTH, D_MODEL), lambda s: (0, 0)),
            pl.BlockSpec((tm, D_FOUR), tok_p),
            pl.BlockSpec((tm, D_FOUR), tok_s),
            pl.BlockSpec((tm, n_attn), tok_p),
            pl.BlockSpec((tm, n_attn), tok_s),
            pl.BlockSpec(memory_space=pl.ANY),
            slab_spec(D_FOUR, D_MODEL),
            slab_spec(n_attn, D_MODEL),
            slab_spec(D_MODEL, D_MODEL),
        ],
        out_specs=pl.BlockSpec((tm, D_MODEL), tok),
        out_shape=jax.ShapeDtypeStruct((N_TOK, D_MODEL), F32),
        scratch_shapes=[
            pltpu.VMEM((D_MODEL, 2 * D_MODEL), BF16),
            pltpu.VMEM((D_FOUR, D_MODEL), BF16),
            pltpu.VMEM((n_attn, D_MODEL), BF16),
            pltpu.VMEM((D_MODEL, D_MODEL), BF16),
            pltpu.VMEM((2, GATE_CHUNK, D_MODEL), F32),
            pltpu.SemaphoreType.DMA((2,)),
        ],
        compiler_params=pltpu.CompilerParams(
            dimension_semantics=("arbitrary",), vmem_limit_bytes=VMEM_LIMIT),
        name="mixer_tail",
    )(x, mod, mod, mod, nw, *frs, *os_, w_in_t, w_four, w_attn_proj, w_out)


def _rope_tables(tm):
    rows = DEC_SEQ // GRID_W
    row = np.repeat(np.arange(rows), GRID_W).astype(np.float64)
    col = np.tile(np.arange(GRID_W), rows).astype(np.float64)
    inv = ROPE_BASE ** (-np.arange(0, ROPE_AXIS, 2, dtype=np.float64) / ROPE_AXIS)
    ar = row[:, None] * inv
    ac = col[:, None] * inv
    cr, sr, cc, sc = np.cos(ar), np.sin(ar), np.cos(ac), np.sin(ac)
    cos32 = np.concatenate([cr, cr, cc, cc], axis=1)
    sin32 = np.concatenate([-sr, sr, -sc, sc], axis=1)
    scale = np.float32((QK_NOPE + QK_ROPE) ** -0.5 * np.log2(np.e))

    def table(rope_part, nope_val, ident_rope):
        t = np.zeros((DEC_SEQ + tm, HEAD_PAD), np.float32)
        t[:DEC_SEQ, :QK_ROPE] = rope_part
        t[:DEC_SEQ, QK_ROPE:QK_ROPE + QK_NOPE] = nope_val
        t[DEC_SEQ:, :QK_ROPE] = ident_rope
        t[DEC_SEQ:, QK_ROPE:QK_ROPE + QK_NOPE] = nope_val
        return t

    cq = (table(cos32, 1.0, 1.0) * scale).T
    sq = (table(sin32, 0.0, 0.0) * scale).T
    ck = table(cos32, 0.0, 1.0)
    sk = table(sin32, 0.0, 0.0)
    return tuple(jnp.asarray(np.ascontiguousarray(t), F32) for t in (cq, sq, ck, sk))


def kernel(x_prompt, x_sample, cache_ckv, cache_krope, c, c_ctx, w_mod, b_mod, norm_ffn1, w_ffn1_gate,
           w_ffn1_up, w_ffn1_down, norm_mix, w_in, w_four, q_norm, w_uq, kv_norm, w_ukv, w_attn_proj,
           w_out, norm_ffn2, w_ffn2_gate, w_ffn2_up, w_ffn2_down, final_norm):
    xs = (x_prompt.reshape(N_PROMPT, D_MODEL), x_sample.reshape(N_SAMPLE, D_MODEL))
    c_all = jnp.concatenate(
        [c_ctx[None, :], c, jnp.zeros((MOD_ROWS - 1 - DEC_BATCH, D_MODEL), F32)], axis=0)
    mod = _modulation(c_all, w_mod, b_mod).reshape(DEPTH * MOD_ROWS * N_MOD, 1, D_MODEL)
    cache_kr = jnp.swapaxes(cache_krope, 2, 3)
    w_in_t = jnp.swapaxes(w_in, 1, 2)

    caches = ()
    for l in range(DEPTH):
        x = _ffn(xs, mod, l, MOD_FFN1, norm_ffn1, w_ffn1_gate, w_ffn1_up, w_ffn1_down, final_norm, False)
        f_in, q, ckv_n, kr, new_ckv, new_krope = _proj(
            x, mod, l, norm_mix, w_in_t, q_norm, w_uq, kv_norm, _rope_tables, caches)
        caches = (new_ckv, new_krope)
        frs = (_dft(f_in, BATCH, SEQ, 0), _dft(f_in, DEC_BATCH, DEC_SEQ, N_PROMPT))
        os_ = (_attention(q, ckv_n, kr, w_ukv, l, BATCH, SEQ, 0),
               _attention(q, ckv_n, kr, w_ukv, l, DEC_BATCH, DEC_SEQ, N_PROMPT, (cache_ckv, cache_kr)))
        x = _tail(x, mod, l, norm_mix, frs, os_, w_in_t, w_four, w_attn_proj, w_out)
        xs = _ffn((x,), mod, l, MOD_FFN2, norm_ffn2, w_ffn2_gate, w_ffn2_up, w_ffn2_down, final_norm,
                  l == DEPTH - 1)
        if l < DEPTH - 1:
            xs = (xs,)

    y_prompt = xs[0].reshape(BATCH, SEQ, D_MODEL)
    y_sample = xs[1].reshape(DEC_BATCH, DEC_SEQ, D_MODEL)
    return y_prompt, y_sample, caches[0], jnp.swapaxes(caches[1], 2, 3)
```
